```python
import jax, jax.numpy as jnp
from jax import lax
import numpy as np

D_MODEL = 1024
BATCH = 1
SEQ = 16384
DEPTH = 1
DEC_BATCH = 128
DEC_SEQ = 1
PAST_LEN = 16384
PAGE_SIZE = 128

N_HEADS = 8
N_KV_HEADS = 2
HEAD_DIM = 128
GQA_GROUP = N_HEADS // N_KV_HEADS
ATTN_WIDTH = N_HEADS * HEAD_DIM
KV_WIDTH = N_KV_HEADS * HEAD_DIM
WINDOW = 128
BLOCK = WINDOW
D_RNN = D_MODEL
RNN_BLOCKS = 8
RNN_BLOCK_W = D_RNN // RNN_BLOCKS
RG_C = 8.0
RNN_CONV_W = 4
MIX_WIDTH = ATTN_WIDTH + D_RNN
IN_COLS = ATTN_WIDTH + 2 * KV_WIDTH + 2 * D_RNN
D_FF = 2816
FFN_CONV_W = 3
RMS_EPS = 1e-6

kernel_name = "hymba_swa_sink_rglru_convffn_step"


def rms_norm(x, g):
    x32 = x.astype(jnp.float32)
    y = x32 * lax.rsqrt(jnp.mean(x32 * x32, axis=-1, keepdims=True) + RMS_EPS)
    return (y * g.astype(jnp.float32)).astype(x.dtype)


def alibi_slopes():
    h = jnp.arange(1, N_HEADS + 1, dtype=jnp.float32)
    return jnp.exp2(-8.0 * h / N_HEADS).reshape(N_KV_HEADS, GQA_GROUP)


def sink_alibi_window_attention(q, k, v, q_pos, k_pos, sinks):
    s = jnp.einsum('...qkgd,...skd->...kgqs', q, k,
                   preferred_element_type=jnp.float32) * (HEAD_DIM ** -0.5)
    dist = q_pos[..., :, None] - k_pos[..., None, :]
    valid = (dist >= 0) & (dist < WINDOW) & (k_pos[..., None, :] >= 0)
    dist_f = dist.astype(jnp.float32)[..., None, None, :, :]
    s = jnp.where(valid[..., None, None, :, :],
                  s - alibi_slopes()[:, :, None, None] * dist_f, -jnp.inf)
    sink = sinks.astype(jnp.float32).reshape(N_KV_HEADS, GQA_GROUP)[:, :, None, None]
    m = jnp.maximum(jnp.max(s, axis=-1, keepdims=True), sink)
    p = jnp.exp(s - m)
    p = p / (jnp.sum(p, axis=-1, keepdims=True) + jnp.exp(sink - m))
    return jnp.einsum('...kgqs,...skd->...qkgd', p, v.astype(jnp.float32))


def prompt_window_attention(q, k, v, sinks):
    B, T = q.shape[:2]
    nb = T // BLOCK
    qb = q.reshape(B, nb, BLOCK, N_KV_HEADS, GQA_GROUP, HEAD_DIM)

    def band(t):
        tp = jnp.pad(t, ((0, 0), (BLOCK, 0), (0, 0), (0, 0)))
        tb = tp.reshape(B, nb + 1, BLOCK, N_KV_HEADS, HEAD_DIM)
        return jnp.concatenate([tb[:, :-1], tb[:, 1:]], axis=2)

    q_pos = jnp.arange(T, dtype=jnp.int32).reshape(nb, BLOCK)
    k_pos = q_pos[:, :1] - BLOCK + jnp.arange(2 * BLOCK, dtype=jnp.int32)[None, :]
    o = sink_alibi_window_attention(qb, band(k), band(v), q_pos, k_pos, sinks)
    return o.reshape(B, T, ATTN_WIDTH)


def sample_window_attention(q, k, v, cache_k, cache_v, sinks):
    B, T = q.shape[:2]
    W = cache_k.shape[1]
    keys = jnp.concatenate([cache_k.astype(k.dtype), k], axis=1)
    vals = jnp.concatenate([cache_v.astype(v.dtype), v], axis=1)
    q_pos = PAST_LEN + jnp.arange(T, dtype=jnp.int32)
    k_pos = PAST_LEN - W + jnp.arange(W + T, dtype=jnp.int32)
    o = sink_alibi_window_attention(q.reshape(B, T, N_KV_HEADS, GQA_GROUP, HEAD_DIM),
                                    keys, vals, q_pos, k_pos, sinks)
    return o.reshape(B, T, ATTN_WIDTH), keys[:, -W:], vals[:, -W:]


def causal_depthwise_conv(x, buf, w, b):
    width = w.shape[0]
    T = x.shape[1]
    xp = jnp.concatenate([buf.astype(x.dtype), x], axis=1)
    y = xp[:, 0:T] * w[0]
    for j in range(1, width):
        y = y + xp[:, j:j + T] * w[j]
    return (y + b).astype(x.dtype), xp[:, T:]


def rg_lru(x, w_a, b_a, w_i, b_i, lam, h0):
    B, T, _ = x.shape
    xb = x.reshape(B, T, RNN_BLOCKS, RNN_BLOCK_W)
    r = jax.nn.sigmoid(jnp.einsum('btnc,ncd->btnd', xb, w_a, preferred_element_type=jnp.float32)
                       + b_a.astype(jnp.float32)).reshape(B, T, D_RNN)
    i = jax.nn.sigmoid(jnp.einsum('btnc,ncd->btnd', xb, w_i, preferred_element_type=jnp.float32)
                       + b_i.astype(jnp.float32)).reshape(B, T, D_RNN)
    log_a = -RG_C * r * jax.nn.softplus(-lam.astype(jnp.float32))
    a = jnp.exp(log_a)
    u = jnp.sqrt(-jnp.expm1(2.0 * log_a)) * (i * x.astype(jnp.float32))
    u = u.at[:, 0].add(a[:, 0] * h0.astype(jnp.float32))

    def combine(left, right):
        a_l, u_l = left
        a_r, u_r = right
        return a_l * a_r, a_r * u_l + u_r

    _, h = lax.associative_scan(combine, (a, u), axis=1)
    return h, h[:, -1]


def decoder_layer(x, c, win_k, win_v, h0, conv_buf, ffn_buf, p, is_prompt, w_keep):
    B, T = x.shape[:2]
    dt = x.dtype
    mod = jax.nn.silu(c.astype(jnp.float32)) @ p['w_ada'].astype(jnp.float32) + p['b_ada'].astype(jnp.float32)
    sh1, sc1, g1, sh2, sc2, g2 = jnp.split(mod.astype(dt)[:, None, :], 6, axis=-1)

    hmix = rms_norm(x, p['g_pre_mix']) * (1 + sc1) + sh1
    z = hmix @ p['w_in']
    c1 = ATTN_WIDTH
    c2 = c1 + KV_WIDTH
    c3 = c2 + KV_WIDTH
    c4 = c3 + D_RNN
    q, k, v, xr, yr = jnp.split(z, [c1, c2, c3, c4], axis=-1)
    k = k.reshape(B, T, N_KV_HEADS, HEAD_DIM)
    v = v.reshape(B, T, N_KV_HEADS, HEAD_DIM)
    if is_prompt:
        attn = prompt_window_attention(q, k, v, p['sinks'])
        new_k, new_v = k[:, -w_keep:], v[:, -w_keep:]
    else:
        attn, new_k, new_v = sample_window_attention(q, k, v, win_k, win_v, p['sinks'])

    xc, new_conv = causal_depthwise_conv(xr, conv_buf, p['conv_w'], p['conv_b'])
    hseq, h_last = rg_lru(xc, p['w_a'], p['b_a'], p['w_i'], p['b_i'], p['lam'], h0)
    rnn = hseq * jax.nn.gelu(yr.astype(jnp.float32))

    mix = jnp.concatenate([rms_norm(attn.astype(dt), p['g_attn_out']),
                           rms_norm(rnn.astype(dt), p['g_rnn_out'])], axis=-1)
    x = x + g1 * rms_norm(mix @ p['w_out'], p['g_post_mix'])

    hff = rms_norm(x, p['g_pre_ffn']) * (1 + sc2) + sh2
    up, new_ffn = causal_depthwise_conv(hff @ p['w_up'], ffn_buf, p['ffn_conv_w'], p['ffn_conv_b'])
    gate, val = jnp.split(up, 2, axis=-1)
    f = (jax.nn.gelu(gate) * val) @ p['w_down']
    x = x + g2 * rms_norm(f, p['g_post_ffn'])
    return x, (new_k, new_v, h_last.astype(dt), new_conv, new_ffn)


def setup_inputs(seed: int = 0) -> dict:
    key = jax.random.key(seed)
    ks = jax.random.split(key, 40)
    f32 = jnp.float32
    w_keep = min(WINDOW, PAST_LEN)
    nrm = lambda k, shape, s: jax.random.normal(k, shape, f32) * s
    gain = lambda k, shape: 1.0 + 0.05 * jax.random.normal(k, shape, f32)
    a0 = jax.random.uniform(ks[20], (DEPTH, D_RNN), f32, 0.9, 0.999)
    a_base = a0 ** (1.0 / RG_C)
    lam = jnp.log(a_base) - jnp.log1p(-a_base)
    return {
        "x_prompt": nrm(ks[0], (BATCH, SEQ, D_MODEL), 1.0),
        "x_sample": nrm(ks[1], (DEC_BATCH, DEC_SEQ, D_MODEL), 1.0),
        "cache_k": nrm(ks[2], (DEPTH, DEC_BATCH, w_keep, N_KV_HEADS, HEAD_DIM), 1.0),
        "cache_v": nrm(ks[3], (DEPTH, DEC_BATCH, w_keep, N_KV_HEADS, HEAD_DIM), 1.0),
        "state_h": nrm(ks[4], (DEPTH, DEC_BATCH, D_RNN), 0.5),
        "state_conv": nrm(ks[5], (DEPTH, DEC_BATCH, RNN_CONV_W - 1, D_RNN), 1.0),
        "state_ffn_conv": nrm(ks[6], (DEPTH, DEC_BATCH, FFN_CONV_W - 1, 2 * D_FF), 1.0),
        "c_prompt": nrm(ks[7], (BATCH, D_MODEL), 1.0),
        "c_sample": nrm(ks[8], (DEC_BATCH, D_MODEL), 1.0),
        "w_ada": nrm(ks[9], (DEPTH, D_MODEL, 6 * D_MODEL), 0.5 * D_MODEL ** -0.5),
        "b_ada": nrm(ks[10], (DEPTH, 6 * D_MODEL), 0.02),
        "g_pre_mix": gain(ks[11], (DEPTH, D_MODEL)),
        "w_in": nrm(ks[12], (DEPTH, D_MODEL, IN_COLS), D_MODEL ** -0.5),
        "conv_w": nrm(ks[13], (DEPTH, RNN_CONV_W, D_RNN), RNN_CONV_W ** -0.5),
        "conv_b": nrm(ks[14], (DEPTH, D_RNN), 0.02),
        "w_a": nrm(ks[15], (DEPTH, RNN_BLOCKS, RNN_BLOCK_W, RNN_BLOCK_W), RNN_BLOCK_W ** -0.5),
        "b_a": nrm(ks[16], (DEPTH, RNN_BLOCKS, RNN_BLOCK_W), 0.02),
        "w_i": nrm(ks[17], (DEPTH, RNN_BLOCKS, RNN_BLOCK_W, RNN_BLOCK_W), RNN_BLOCK_W ** -0.5),
        "b_i": nrm(ks[18], (DEPTH, RNN_BLOCKS, RNN_BLOCK_W), 0.02),
        "lam": lam,
        "sinks": nrm(ks[19], (DEPTH, N_HEADS), 1.0),
        "g_attn_out": gain(ks[21], (DEPTH, ATTN_WIDTH)),
        "g_rnn_out": gain(ks[22], (DEPTH, D_RNN)),
        "w_out": nrm(ks[23], (DEPTH, MIX_WIDTH, D_MODEL), MIX_WIDTH ** -0.5),
        "g_post_mix": gain(ks[24], (DEPTH, D_MODEL)),
        "g_pre_ffn": gain(ks[25], (DEPTH, D_MODEL)),
        "w_up": nrm(ks[26], (DEPTH, D_MODEL, 2 * D_FF), D_MODEL ** -0.5),
        "ffn_conv_w": nrm(ks[27], (DEPTH, FFN_CONV_W, 2 * D_FF), FFN_CONV_W ** -0.5),
        "ffn_conv_b": nrm(ks[28], (DEPTH, 2 * D_FF), 0.02),
        "w_down": nrm(ks[29], (DEPTH, D_FF, D_MODEL), D_FF ** -0.5),
        "g_post_ffn": gain(ks[30], (DEPTH, D_MODEL)),
    }


def reference(x_prompt, x_sample, cache_k, cache_v, state_h, state_conv, state_ffn_conv,
              c_prompt, c_sample, w_ada, b_ada, g_pre_mix, w_in, conv_w, conv_b, w_a, b_a,
              w_i, b_i, lam, sinks, g_attn_out, g_rnn_out, w_out, g_post_mix, g_pre_ffn,
              w_up, ffn_conv_w, ffn_conv_b, w_down, g_post_ffn):
    w_keep = cache_k.shape[2]
    yp, ys = x_prompt, x_sample
    Bp = x_prompt.shape[0]
    dt = x_prompt.dtype
    p_states, s_states = [], []
    for l in range(DEPTH):
        p = dict(w_ada=w_ada[l], b_ada=b_ada[l], g_pre_mix=g_pre_mix[l], w_in=w_in[l],
                 conv_w=conv_w[l], conv_b=conv_b[l], w_a=w_a[l], b_a=b_a[l], w_i=w_i[l],
                 b_i=b_i[l], lam=lam[l], sinks=sinks[l], g_attn_out=g_attn_out[l],
                 g_rnn_out=g_rnn_out[l], w_out=w_out[l], g_post_mix=g_post_mix[l],
                 g_pre_ffn=g_pre_ffn[l], w_up=w_up[l], ffn_conv_w=ffn_conv_w[l],
                 ffn_conv_b=ffn_conv_b[l], w_down=w_down[l], g_post_ffn=g_post_ffn[l])
        yp, sp = decoder_layer(
            yp, c_prompt, None, None,
            jnp.zeros((Bp, D_RNN), jnp.float32),
            jnp.zeros((Bp, RNN_CONV_W - 1, D_RNN), dt),
            jnp.zeros((Bp, FFN_CONV_W - 1, 2 * D_FF), dt),
            p, True, w_keep)
        ys, ss = decoder_layer(
            ys, c_sample, cache_k[l], cache_v[l], state_h[l], state_conv[l],
            state_ffn_conv[l], p, False, w_keep)
        p_states.append(sp)
        s_states.append(ss)
    kp, vp, hp, convp, ffnp = [jnp.stack(t) for t in zip(*p_states)]
    ks_, vs_, hs_, convs_, ffns_ = [jnp.stack(t) for t in zip(*s_states)]
    return (yp, ys, kp, vp, hp, convp, ffnp, ks_, vs_, hs_, convs_, ffns_)
```

```python
import functools
import math

import jax
import jax.numpy as jnp
from jax import lax
from jax.experimental import pallas as pl
from jax.experimental.pallas import tpu as pltpu

D_MODEL = 1024
N_HEADS = 8
N_KV_HEADS = 2
HEAD_DIM = 128
GQA_GROUP = N_HEADS // N_KV_HEADS
ATTN_WIDTH = N_HEADS * HEAD_DIM
KV_WIDTH = N_KV_HEADS * HEAD_DIM
WINDOW = 128
D_RNN = D_MODEL
RNN_BLOCKS = 8
RNN_BLOCK_W = D_RNN // RNN_BLOCKS
RG_C = 8.0
RNN_CONV_W = 4
D_FF = 2816
FFN_CONV_W = 3
RMS_EPS = 1e-6

C_K = ATTN_WIDTH
C_V = C_K + KV_WIDTH
C_XR = C_V + KV_WIDTH
C_YR = C_XR + D_RNN
IN_COLS = C_YR + D_RNN

SUBLANES = 8
TOKEN_BLOCK = 256
SAMPLE_CHUNK = 16
MOD_PAD_ROWS = 8
VMEM_LIMIT_BYTES = 56 * 1024 * 1024

ALIBI_SLOPES = tuple(2.0 ** (-8.0 * (h + 1) / N_HEADS) for h in range(N_HEADS))
Q_SCALE = HEAD_DIM ** -0.5
SQRT_2_OVER_PI = math.sqrt(2.0 / math.pi)

bf16 = jnp.bfloat16
f32 = jnp.float32


def _rms_norm(x, g):
    ms = jnp.mean(x * x, axis=-1, keepdims=True)
    return x * lax.rsqrt(ms + RMS_EPS) * g


def _gelu_tanh(x):
    return x * (0.5 * (1.0 + jnp.tanh(SQRT_2_OVER_PI * (x + 0.044715 * (x * x * x)))))


def _sigmoid(x):
    return 1.0 / (1.0 + jnp.exp(-x))


def _softplus_neg(lam):
    return jnp.maximum(-lam, 0.0) + jnp.log1p(jnp.exp(-jnp.abs(lam)))


def _dot(a, b):
    return jnp.dot(a.astype(bf16), b, preferred_element_type=f32)


def _dot_nt(a, b):
    return lax.dot_general(a, b, (((1,), (1,)), ((), ())), preferred_element_type=f32)


def _rglru_gates(xc, wa_ref, ba, wi_ref, bi, sp):
    a_parts, u_parts = [], []
    for n in range(RNN_BLOCKS):
        sl = slice(n * RNN_BLOCK_W, (n + 1) * RNN_BLOCK_W)
        xn = xc[:, sl]
        xb = xn.astype(bf16)
        r = _sigmoid(jnp.dot(xb, wa_ref[n], preferred_element_type=f32) + ba[:, sl])
        i = _sigmoid(jnp.dot(xb, wi_ref[n], preferred_element_type=f32) + bi[:, sl])
        log_a = (-RG_C) * r * sp[:, sl]
        t = jnp.tanh(log_a)
        one_minus_a2 = (-2.0 * t) / (1.0 - t)
        a_parts.append(jnp.exp(log_a))
        u_parts.append(jnp.sqrt(one_minus_a2) * (i * xn))
    return a_parts, u_parts


def _mod_kernel(c_ref, w_ref, b_ref, o_ref):
    c = c_ref[...]
    s = c * _sigmoid(c)
    o_ref[...] = _dot(s, w_ref[...].astype(bf16)) + b_ref[...]


def _mod_call(c_all, w_ada, b_ada):
    rows = c_all.shape[0]
    ncol = w_ada.shape[1]
    bn = D_MODEL
    return pl.pallas_call(
        _mod_kernel,
        grid=(ncol // bn,),
        in_specs=[
            pl.BlockSpec((rows, D_MODEL), lambda j: (0, 0)),
            pl.BlockSpec((D_MODEL, bn), lambda j: (0, j)),
            pl.BlockSpec((1, bn), lambda j: (0, j)),
        ],
        out_specs=pl.BlockSpec((rows, bn), lambda j: (0, j)),
        out_shape=jax.ShapeDtypeStruct((rows, ncol), f32),
        compiler_params=pltpu.CompilerParams(dimension_semantics=("arbitrary",)),
        name="adaln_mod",
    )(c_all, w_ada, b_ada)


def _prompt_kernel(
    x_ref, mod_ref, sinks_ref,
    g_pre_mix_ref, w_in_ref, conv_w_ref, conv_b_ref, wa_ref, ba_ref, wi_ref, bi_ref, lam_ref,
    g_attn_ref, g_rnn_ref, w_out_ref, g_post_mix_ref, g_pre_ffn_ref, w_up_ref,
    fconv_w_ref, fconv_b_ref, w_down_ref, g_post_ffn_ref,
    y_ref, kwin_ref, vwin_ref, hlast_ref, convst_ref, ffnst_ref,
    kbuf, vbuf, xrbuf, a_s, u_s, hcar, upbuf, attn_s, act_s,
):
    TB = TOKEN_BLOCK
    step = pl.program_id(0)
    last = pl.num_programs(0) - 1

    @pl.when(step == 0)
    def _init():
        kbuf[0:WINDOW, :] = jnp.zeros((WINDOW, KV_WIDTH), bf16)
        vbuf[0:WINDOW, :] = jnp.zeros((WINDOW, KV_WIDTH), bf16)
        xrbuf[0:SUBLANES, :] = jnp.zeros((SUBLANES, D_RNN), f32)
        upbuf[0:SUBLANES, :] = jnp.zeros((SUBLANES, 2 * D_FF), f32)
        hcar[...] = jnp.zeros((SUBLANES, D_RNN), f32)

    sh1 = mod_ref[0:1, 0 * D_MODEL:1 * D_MODEL]
    sc1 = mod_ref[0:1, 1 * D_MODEL:2 * D_MODEL]
    g1 = mod_ref[0:1, 2 * D_MODEL:3 * D_MODEL]
    sh2 = mod_ref[0:1, 3 * D_MODEL:4 * D_MODEL]
    sc2 = mod_ref[0:1, 4 * D_MODEL:5 * D_MODEL]
    g2 = mod_ref[0:1, 5 * D_MODEL:6 * D_MODEL]

    x = x_ref[...]
    hmix = (_rms_norm(x, g_pre_mix_ref[...]) * (1.0 + sc1) + sh1).astype(bf16)

    q = jnp.dot(hmix, w_in_ref[:, 0:C_K], preferred_element_type=f32) * Q_SCALE
    kv = jnp.dot(hmix, w_in_ref[:, C_K:C_XR], preferred_element_type=f32)
    k = kv[:, 0:KV_WIDTH]
    v = kv[:, KV_WIDTH:2 * KV_WIDTH]
    kbuf[WINDOW:WINDOW + TB, :] = k.astype(bf16)
    vbuf[WINDOW:WINDOW + TB, :] = v.astype(bf16)
    xr = jnp.dot(hmix, w_in_ref[:, C_XR:C_YR], preferred_element_type=f32)
    xrbuf[SUBLANES:SUBLANES + TB, :] = xr

    @pl.when(step == last)
    def _win_out():
        kwin_ref[...] = k[TB - WINDOW:TB, :]
        vwin_ref[...] = v[TB - WINDOW:TB, :]
        convst_ref[...] = xr[TB - (RNN_CONV_W - 1):TB, :]

    qi = lax.broadcasted_iota(jnp.int32, (WINDOW, 2 * WINDOW), 0)
    si = lax.broadcasted_iota(jnp.int32, (WINDOW, 2 * WINDOW), 1)
    dist = qi + WINDOW - si
    valid = (dist >= 0) & (dist < WINDOW)
    base = jnp.where(valid, dist.astype(f32), jnp.inf)
    first_pen = jnp.where(step == 0, jnp.inf, 0.0)
    base_first = jnp.where(si < WINDOW, base + first_pen, base)
    qb = q.astype(bf16)
    for j in range(TB // WINDOW):
        r0 = j * WINDOW
        bj = base_first if j == 0 else base
        for h in range(N_HEADS):
            c = h // GQA_GROUP
            qh = qb[r0:r0 + WINDOW, h * HEAD_DIM:(h + 1) * HEAD_DIM]
            kw = kbuf[r0:r0 + 2 * WINDOW, c * HEAD_DIM:(c + 1) * HEAD_DIM]
            vw = vbuf[r0:r0 + 2 * WINDOW, c * HEAD_DIM:(c + 1) * HEAD_DIM]
            s = _dot_nt(qh, kw) - ALIBI_SLOPES[h] * bj
            sink = sinks_ref[h]
            m = jnp.maximum(jnp.max(s, axis=-1, keepdims=True), sink)
            p = jnp.exp(s - m)
            denom = jnp.sum(p, axis=-1, keepdims=True) + jnp.exp(sink - m)
            o = jnp.dot(p.astype(bf16), vw, preferred_element_type=f32) / denom
            attn_s[r0:r0 + WINDOW, h * HEAD_DIM:(h + 1) * HEAD_DIM] = o

    kbuf[0:WINDOW, :] = kbuf[TB:TB + WINDOW, :]
    vbuf[0:WINDOW, :] = vbuf[TB:TB + WINDOW, :]

    xc = conv_b_ref[...] + conv_w_ref[RNN_CONV_W - 1:RNN_CONV_W, :] * xr
    for jj in range(1, RNN_CONV_W):
        xc = xc + (conv_w_ref[RNN_CONV_W - 1 - jj:RNN_CONV_W - jj, :]
                   * xrbuf[SUBLANES - jj:SUBLANES - jj + TB, :])
    xrbuf[0:SUBLANES, :] = xrbuf[TB:TB + SUBLANES, :]

    sp = _softplus_neg(lam_ref[...])
    a_parts, u_parts = _rglru_gates(xc, wa_ref, ba_ref[...], wi_ref, bi_ref[...], sp)
    for n in range(RNN_BLOCKS):
        sl = slice(n * RNN_BLOCK_W, (n + 1) * RNN_BLOCK_W)
        a_s[:, sl] = a_parts[n]
        u_s[:, sl] = u_parts[n]

    row = lax.broadcasted_iota(jnp.int32, (SUBLANES, D_RNN), 0)

    def scan_body(g, hprev):
        r = pl.multiple_of(g * SUBLANES, SUBLANES)
        a = a_s[pl.ds(r, SUBLANES), :]
        u = u_s[pl.ds(r, SUBLANES), :]
        for s_ in (1, 2, 4):
            ok = row >= s_
            a_sh = pltpu.roll(a, s_, axis=0)
            u_sh = pltpu.roll(u, s_, axis=0)
            u = jnp.where(ok, a * u_sh + u, u)
            a = jnp.where(ok, a * a_sh, a)
        hh = a * hprev + u
        u_s[pl.ds(r, SUBLANES), :] = hh
        return jnp.broadcast_to(hh[SUBLANES - 1:SUBLANES, :], (SUBLANES, D_RNN))

    hfin = lax.fori_loop(0, TB // SUBLANES, scan_body, hcar[...])
    hcar[...] = hfin

    @pl.when(step == last)
    def _h_out():
        hlast_ref[...] = hfin[0:1, :]

    yr = jnp.dot(hmix, w_in_ref[:, C_YR:IN_COLS], preferred_element_type=f32)
    rnn = u_s[...] * _gelu_tanh(yr)

    attn_n = _rms_norm(attn_s[...], g_attn_ref[...])
    rnn_n = _rms_norm(rnn, g_rnn_ref[...])
    mo = (_dot(attn_n, w_out_ref[0:ATTN_WIDTH, :])
          + _dot(rnn_n, w_out_ref[ATTN_WIDTH:ATTN_WIDTH + D_RNN, :]))
    x1 = x + g1 * _rms_norm(mo, g_post_mix_ref[...])

    hff = (_rms_norm(x1, g_pre_ffn_ref[...]) * (1.0 + sc2) + sh2).astype(bf16)
    upbuf[SUBLANES:SUBLANES + TB, :] = jnp.dot(hff, w_up_ref[...], preferred_element_type=f32)

    @pl.when(step == last)
    def _ffn_out():
        ffnst_ref[...] = upbuf[SUBLANES + TB - (FFN_CONV_W - 1):SUBLANES + TB, :]

    def conv_cols(c0, c1):
        acc = fconv_b_ref[:, c0:c1] + fconv_w_ref[FFN_CONV_W - 1:FFN_CONV_W, c0:c1] * upbuf[SUBLANES:SUBLANES + TB, c0:c1]
        for jj in range(1, FFN_CONV_W):
            acc = acc + (fconv_w_ref[FFN_CONV_W - 1 - jj:FFN_CONV_W - jj, c0:c1]
                         * upbuf[SUBLANES - jj:SUBLANES - jj + TB, c0:c1])
        return acc

    half = D_FF // 2
    for ci in range(2):
        c0 = ci * half
        gate = conv_cols(c0, c0 + half)
        val = conv_cols(D_FF + c0, D_FF + c0 + half)
        act_s[:, c0:c0 + half] = (_gelu_tanh(gate) * val).astype(bf16)
    upbuf[0:SUBLANES, :] = upbuf[TB:TB + SUBLANES, :]

    f = jnp.dot(act_s[...], w_down_ref[...], preferred_element_type=f32)
    y_ref[...] = x1 + g2 * _rms_norm(f, g_post_ffn_ref[...])


def _const_spec(shape):
    nd = len(shape)
    return pl.BlockSpec(shape, lambda i: (0,) * nd)


def _prompt_call(x, mod_p, sinks, params):
    T = x.shape[0]
    TB = TOKEN_BLOCK
    (g_pre_mix, w_in, conv_w, conv_b, wa, ba, wi, bi, lam, g_attn, g_rnn, w_out,
     g_post_mix, g_pre_ffn, w_up, fconv_w, fconv_b, w_down, g_post_ffn) = params
    ins = [x, mod_p, sinks, g_pre_mix, w_in, conv_w, conv_b, wa, ba, wi, bi, lam, g_attn, g_rnn,
           w_out, g_post_mix, g_pre_ffn, w_up, fconv_w, fconv_b, w_down, g_post_ffn]
    in_specs = [pl.BlockSpec((TB, D_MODEL), lambda i: (i, 0)),
                _const_spec(mod_p.shape),
                pl.BlockSpec(memory_space=pltpu.SMEM)]
    in_specs += [_const_spec(a.shape) for a in ins[3:]]
    out_shape = (
        jax.ShapeDtypeStruct((T, D_MODEL), f32),
        jax.ShapeDtypeStruct((WINDOW, KV_WIDTH), f32),
        jax.ShapeDtypeStruct((WINDOW, KV_WIDTH), f32),
        jax.ShapeDtypeStruct((1, D_RNN), f32),
        jax.ShapeDtypeStruct((RNN_CONV_W - 1, D_RNN), f32),
        jax.ShapeDtypeStruct((FFN_CONV_W - 1, 2 * D_FF), f32),
    )
    out_specs = (
        pl.BlockSpec((TB, D_MODEL), lambda i: (i, 0)),
        _const_spec((WINDOW, KV_WIDTH)),
        _const_spec((WINDOW, KV_WIDTH)),
        _const_spec((1, D_RNN)),
        _const_spec((RNN_CONV_W - 1, D_RNN)),
        _const_spec((FFN_CONV_W - 1, 2 * D_FF)),
    )
    scratch = [
        pltpu.VMEM((WINDOW + TB, KV_WIDTH), bf16),
        pltpu.VMEM((WINDOW + TB, KV_WIDTH), bf16),
        pltpu.VMEM((SUBLANES + TB, D_RNN), f32),
        pltpu.VMEM((TB, D_RNN), f32),
        pltpu.VMEM((TB, D_RNN), f32),
        pltpu.VMEM((SUBLANES, D_RNN), f32),
        pltpu.VMEM((SUBLANES + TB, 2 * D_FF), f32),
        pltpu.VMEM((TB, ATTN_WIDTH), f32),
        pltpu.VMEM((TB, D_FF), bf16),
    ]
    return pl.pallas_call(
        _prompt_kernel,
        grid=(T // TB,),
        in_specs=in_specs,
        out_specs=out_specs,
        out_shape=out_shape,
        scratch_shapes=scratch,
        compiler_params=pltpu.CompilerParams(
            dimension_semantics=("arbitrary",), vmem_limit_bytes=VMEM_LIMIT_BYTES),
        name="prompt_layer",
    )(*ins)


def _sample_pre_kernel(x_ref, mod_ref, g_pre_mix_ref, w_in_ref, q_ref, kv_ref, xr_ref, yr_ref):
    sh1 = mod_ref[:, 0 * D_MODEL:1 * D_MODEL]
    sc1 = mod_ref[:, 1 * D_MODEL:2 * D_MODEL]
    hmix = (_rms_norm(x_ref[...], g_pre_mix_ref[...]) * (1.0 + sc1) + sh1).astype(bf16)
    q_ref[...] = jnp.dot(hmix, w_in_ref[:, 0:C_K], preferred_element_type=f32) * Q_SCALE
    kv_ref[...] = jnp.dot(hmix, w_in_ref[:, C_K:C_XR], preferred_element_type=f32)
    xr_ref[...] = jnp.dot(hmix, w_in_ref[:, C_XR:C_YR], preferred_element_type=f32)
    yr_ref[...] = jnp.dot(hmix, w_in_ref[:, C_YR:IN_COLS], preferred_element_type=f32)


def _sample_pre_call(x, mod_s, g_pre_mix, w_in):
    B = x.shape[0]
    return pl.pallas_call(
        _sample_pre_kernel,
        out_shape=(
            jax.ShapeDtypeStruct((B, ATTN_WIDTH), f32),
            jax.ShapeDtypeStruct((B, 2 * KV_WIDTH), f32),
            jax.ShapeDtypeStruct((B, D_RNN), f32),
            jax.ShapeDtypeStruct((B, D_RNN), f32),
        ),
        compiler_params=pltpu.CompilerParams(vmem_limit_bytes=VMEM_LIMIT_BYTES),
        name="sample_pre",
    )(x, mod_s, g_pre_mix, w_in)


def _sample_attn_kernel(q_ref, kv_ref, ck_ref, cv_ref, sinks_ref, o_ref, kwin_ref, vwin_ref):
    hrow = lax.broadcasted_iota(jnp.int32, (N_HEADS, WINDOW), 0)
    jcol = lax.broadcasted_iota(jnp.int32, (N_HEADS, WINDOW), 1)
    slope = jnp.exp2(-8.0 * (hrow + 1).astype(f32) / N_HEADS)
    bias = slope * (WINDOW - 1 - jcol).astype(f32)
    hrow2 = lax.broadcasted_iota(jnp.int32, (N_HEADS, KV_WIDTH), 0)
    lcol2 = lax.broadcasted_iota(jnp.int32, (N_HEADS, KV_WIDTH), 1)
    own_kv = (hrow2 // GQA_GROUP) == (lcol2 // HEAD_DIM)
    wrow = lax.broadcasted_iota(jnp.int32, (WINDOW, KV_WIDTH), 0)
    hsel = lax.broadcasted_iota(jnp.int32, (N_HEADS, HEAD_DIM), 0) < GQA_GROUP
    sink = sinks_ref[...]

    def body(b, carry):
        knew = kv_ref[pl.ds(b, 1), 0:KV_WIDTH]
        vnew = kv_ref[pl.ds(b, 1), KV_WIDTH:2 * KV_WIDTH]
        kw = jnp.where(wrow == WINDOW - 1, knew, pltpu.roll(ck_ref[b], WINDOW - 1, axis=0))
        vw = jnp.where(wrow == WINDOW - 1, vnew, pltpu.roll(cv_ref[b], WINDOW - 1, axis=0))
        kwin_ref[b] = kw
        vwin_ref[b] = vw
        q8 = q_ref[b]
        qz = jnp.where(own_kv, jnp.concatenate([q8, q8], axis=1), 0.0).astype(bf16)
        s = _dot_nt(qz, kw.astype(bf16)) - bias
        m = jnp.maximum(jnp.max(s, axis=-1, keepdims=True), sink)
        p = jnp.exp(s - m)
        denom = jnp.sum(p, axis=-1, keepdims=True) + jnp.exp(sink - m)
        o = jnp.dot(p.astype(bf16), vw.astype(bf16), preferred_element_type=f32) / denom
        o_ref[b] = jnp.where(hsel, o[:, 0:HEAD_DIM], o[:, HEAD_DIM:2 * HEAD_DIM])
        return carry

    lax.fori_loop(0, SAMPLE_CHUNK, body, 0)


def _sample_attn_call(q, kv, ck, cv, sinks_col):
    B = q.shape[0]
    BC = SAMPLE_CHUNK
    return pl.pallas_call(
        _sample_attn_kernel,
        grid=(B // BC,),
        in_specs=[
            pl.BlockSpec((BC, N_HEADS, HEAD_DIM), lambda i: (i, 0, 0)),
            pl.BlockSpec((BC, 2 * KV_WIDTH), lambda i: (i, 0)),
            pl.BlockSpec((BC, WINDOW, KV_WIDTH), lambda i: (i, 0, 0)),
            pl.BlockSpec((BC, WINDOW, KV_WIDTH), lambda i: (i, 0, 0)),
            pl.BlockSpec((N_HEADS, 1), lambda i: (0, 0)),
        ],
        out_specs=(
            pl.BlockSpec((BC, N_HEADS, HEAD_DIM), lambda i: (i, 0, 0)),
            pl.BlockSpec((BC, WINDOW, KV_WIDTH), lambda i: (i, 0, 0)),
            pl.BlockSpec((BC, WINDOW, KV_WIDTH), lambda i: (i, 0, 0)),
        ),
        out_shape=(
            jax.ShapeDtypeStruct((B, N_HEADS, HEAD_DIM), f32),
            jax.ShapeDtypeStruct((B, WINDOW, KV_WIDTH), f32),
            jax.ShapeDtypeStruct((B, WINDOW, KV_WIDTH), f32),
        ),
        compiler_params=pltpu.CompilerParams(dimension_semantics=("arbitrary",)),
        name="sample_attn",
    )(q, kv, ck, cv, sinks_col)


def _sample_post_kernel(
    x_ref, mod_ref, attn_ref, xr_ref, yr_ref, h0_ref, cbuf_ref, fbuf_ref,
    conv_w_ref, conv_b_ref, wa_ref, ba_ref, wi_ref, bi_ref, lam_ref,
    g_attn_ref, g_rnn_ref, w_out_ref, g_post_mix_ref, g_pre_ffn_ref, w_up_ref,
    fconv_w_ref, fconv_b_ref, w_down_ref, g_post_ffn_ref,
    y_ref, h_ref, cst_ref, fst_ref,
):
    g1 = mod_ref[:, 2 * D_MODEL:3 * D_MODEL]
    sh2 = mod_ref[:, 3 * D_MODEL:4 * D_MODEL]
    sc2 = mod_ref[:, 4 * D_MODEL:5 * D_MODEL]
    g2 = mod_ref[:, 5 * D_MODEL:6 * D_MODEL]
    x = x_ref[...]
    xr = xr_ref[...]

    xc = conv_b_ref[...] + conv_w_ref[RNN_CONV_W - 1:RNN_CONV_W, :] * xr
    for jj in range(RNN_CONV_W - 1):
        xc = xc + conv_w_ref[jj:jj + 1, :] * cbuf_ref[:, jj, :]
    for jj in range(RNN_CONV_W - 2):
        cst_ref[:, jj, :] = cbuf_ref[:, jj + 1, :]
    cst_ref[:, RNN_CONV_W - 2, :] = xr

    sp = _softplus_neg(lam_ref[...])
    a_parts, u_parts = _rglru_gates(xc, wa_ref, ba_ref[...], wi_ref, bi_ref[...], sp)
    a = jnp.concatenate(a_parts, axis=1)
    u = jnp.concatenate(u_parts, axis=1)
    h = a * h0_ref[...] + u
    h_ref[...] = h
    rnn = h * _gelu_tanh(yr_ref[...])

    attn_n = _rms_norm(attn_ref[...], g_attn_ref[...])
    rnn_n = _rms_norm(rnn, g_rnn_ref[...])
    mo = (_dot(attn_n, w_out_ref[0:ATTN_WIDTH, :])
          + _dot(rnn_n, w_out_ref[ATTN_WIDTH:ATTN_WIDTH + D_RNN, :]))
    x1 = x + g1 * _rms_norm(mo, g_post_mix_ref[...])

    hff = (_rms_norm(x1, g_pre_ffn_ref[...]) * (1.0 + sc2) + sh2).astype(bf16)
    up_pre = jnp.dot(hff, w_up_ref[...], preferred_element_type=f32)
    up = fconv_b_ref[...] + fconv_w_ref[FFN_CONV_W - 1:FFN_CONV_W, :] * up_pre
    for jj in range(FFN_CONV_W - 1):
        up = up + fconv_w_ref[jj:jj + 1, :] * fbuf_ref[:, jj, :]
    for jj in range(FFN_CONV_W - 2):
        fst_ref[:, jj, :] = fbuf_ref[:, jj + 1, :]
    fst_ref[:, FFN_CONV_W - 2, :] = up_pre

    act = (_gelu_tanh(up[:, 0:D_FF]) * up[:, D_FF:2 * D_FF]).astype(bf16)
    f = jnp.dot(act, w_down_ref[...], preferred_element_type=f32)
    y_ref[...] = x1 + g2 * _rms_norm(f, g_post_ffn_ref[...])


def _sample_post_call(x, mod_s, attn, xr, yr, h0, cbuf, fbuf, params):
    B = x.shape[0]
    (_, _, conv_w, conv_b, wa, ba, wi, bi, lam, g_attn, g_rnn, w_out,
     g_post_mix, g_pre_ffn, w_up, fconv_w, fconv_b, w_down, g_post_ffn) = params
    return pl.pallas_call(
        _sample_post_kernel,
        out_shape=(
            jax.ShapeDtypeStruct((B, D_MODEL), f32),
            jax.ShapeDtypeStruct((B, D_RNN), f32),
            jax.ShapeDtypeStruct((B, RNN_CONV_W - 1, D_RNN), f32),
            jax.ShapeDtypeStruct((B, FFN_CONV_W - 1, 2 * D_FF), f32),
        ),
        compiler_params=pltpu.CompilerParams(vmem_limit_bytes=VMEM_LIMIT_BYTES),
        name="sample_post",
    )(x, mod_s, attn, xr, yr, h0, cbuf, fbuf, conv_w, conv_b, wa, ba, wi, bi, lam,
      g_attn, g_rnn, w_out, g_post_mix, g_pre_ffn, w_up, fconv_w, fconv_b, w_down, g_post_ffn)


def kernel(x_prompt, x_sample, cache_k, cache_v, state_h, state_conv, state_ffn_conv, c_prompt, c_sample, w_ada, b_ada, g_pre_mix, w_in, conv_w, conv_b, w_a, b_a, w_i, b_i, lam, sinks, g_attn_out, g_rnn_out, w_out, g_post_mix, g_pre_ffn, w_up, ffn_conv_w, ffn_conv_b, w_down, g_post_ffn):
    depth = w_in.shape[0]
    assert depth == 1 and x_prompt.shape[0] == 1 and x_sample.shape[1] == 1
    T = x_prompt.shape[1]
    B = x_sample.shape[0]
    W = cache_k.shape[2]
    assert W == WINDOW and T % TOKEN_BLOCK == 0 and B % SAMPLE_CHUNK == 0

    row = lambda a: a[0].reshape(1, -1)
    params = (
        row(g_pre_mix), w_in[0].astype(bf16), conv_w[0], row(conv_b),
        w_a[0].astype(bf16), row(b_a), w_i[0].astype(bf16), row(b_i), row(lam),
        row(g_attn_out), row(g_rnn_out), w_out[0].astype(bf16), row(g_post_mix), row(g_pre_ffn),
        w_up[0].astype(bf16), ffn_conv_w[0], row(ffn_conv_b), w_down[0].astype(bf16),
        row(g_post_ffn),
    )

    c_all = jnp.concatenate(
        [jnp.broadcast_to(c_prompt, (MOD_PAD_ROWS, D_MODEL)), c_sample], axis=0)
    mod = _mod_call(c_all, w_ada[0], b_ada[0].reshape(1, -1))
    mod_p = mod[0:MOD_PAD_ROWS]
    mod_s = mod[MOD_PAD_ROWS:]

    yp, kp, vp, hp, convp, ffnp = _prompt_call(x_prompt[0], mod_p, sinks[0], params)

    xs = x_sample[:, 0, :]
    q, kv, xr, yr = _sample_pre_call(xs, mod_s, params[0], params[1])
    ck = cache_k[0].reshape(B, W, KV_WIDTH)
    cv = cache_v[0].reshape(B, W, KV_WIDTH)
    attn3, kwin, vwin = _sample_attn_call(
        q.reshape(B, N_HEADS, HEAD_DIM), kv, ck, cv, sinks[0].reshape(N_HEADS, 1))
    attn = attn3.reshape(B, ATTN_WIDTH)
    ys, hs, convs, ffns = _sample_post_call(
        xs, mod_s, attn, xr, yr, state_h[0], state_conv[0], state_ffn_conv[0], params)

    kv_shape = (1, 1, W, N_KV_HEADS, HEAD_DIM)
    kvs_shape = (1, B, W, N_KV_HEADS, HEAD_DIM)
    return (
        yp[None], ys[:, None, :],
        kp.reshape(kv_shape), vp.reshape(kv_shape), hp[None], convp[None, None], ffnp[None, None],
        kwin.reshape(kvs_shape), vwin.reshape(kvs_shape), hs[None], convs[None], ffns[None],
    )
```

```python
import functools
import math

import jax
import jax.numpy as jnp
from jax import lax
from jax.experimental import pallas as pl
from jax.experimental.pallas import tpu as pltpu

D_MODEL = 1024
N_HEADS = 8
N_KV_HEADS = 2
HEAD_DIM = 128
GQA_GROUP = N_HEADS // N_KV_HEADS
ATTN_WIDTH = N_HEADS * HEAD_DIM
KV_WIDTH = N_KV_HEADS * HEAD_DIM
WINDOW = 128
D_RNN = D_MODEL
RNN_BLOCKS = 8
RNN_BLOCK_W = D_RNN // RNN_BLOCKS
RG_C = 8.0
RNN_CONV_W = 4
D_FF = 2816
FFN_CONV_W = 3
RMS_EPS = 1e-6

C_K = ATTN_WIDTH
C_V = C_K + KV_WIDTH
C_XR = C_V + KV_WIDTH
C_YR = C_XR + D_RNN
IN_COLS = C_YR + D_RNN

SUBLANES = 8
TOKEN_BLOCK = 256
SAMPLE_CHUNK = 16
MOD_PAD_ROWS = 8
VMEM_LIMIT_BYTES = 56 * 1024 * 1024

ALIBI_SLOPES = tuple(2.0 ** (-8.0 * (h + 1) / N_HEADS) for h in range(N_HEADS))
Q_SCALE = HEAD_DIM ** -0.5
SQRT_2_OVER_PI = math.sqrt(2.0 / math.pi)

bf16 = jnp.bfloat16
f32 = jnp.float32


def _rms_norm(x, g):
    ms = jnp.mean(x * x, axis=-1, keepdims=True)
    return x * lax.rsqrt(ms + RMS_EPS) * g


def _gelu_tanh(x):
    return x * (0.5 * (1.0 + jnp.tanh(SQRT_2_OVER_PI * (x + 0.044715 * (x * x * x)))))


def _sigmoid(x):
    return 1.0 / (1.0 + jnp.exp(-x))


def _softplus_neg(lam):
    return jnp.maximum(-lam, 0.0) + jnp.log1p(jnp.exp(-jnp.abs(lam)))


def _dot(a, b):
    return jnp.dot(a.astype(bf16), b, preferred_element_type=f32)


def _dot_nt(a, b):
    return lax.dot_general(a, b, (((1,), (1,)), ((), ())), preferred_element_type=f32)


def _rglru_gates(xc, wa_ref, ba, wi_ref, bi, sp):
    a_parts, u_parts = [], []
    for n in range(RNN_BLOCKS):
        sl = slice(n * RNN_BLOCK_W, (n + 1) * RNN_BLOCK_W)
        xn = xc[:, sl]
        xb = xn.astype(bf16)
        r = _sigmoid(jnp.dot(xb, wa_ref[n], preferred_element_type=f32) + ba[:, sl])
        i = _sigmoid(jnp.dot(xb, wi_ref[n], preferred_element_type=f32) + bi[:, sl])
        log_a = (-RG_C) * r * sp[:, sl]
        t = jnp.tanh(log_a)
        one_minus_a2 = (-2.0 * t) / (1.0 - t)
        a_parts.append(jnp.exp(log_a))
        u_parts.append(jnp.sqrt(one_minus_a2) * (i * xn))
    return a_parts, u_parts


def _mod_kernel(c_ref, w_ref, b_ref, o_ref):
    c = c_ref[...]
    s = c * _sigmoid(c)
    o_ref[...] = _dot(s, w_ref[...].astype(bf16)) + b_ref[...]


def _mod_call(c_all, w_ada, b_ada):
    rows = c_all.shape[0]
    ncol = w_ada.shape[1]
    bn = D_MODEL
    return pl.pallas_call(
        _mod_kernel,
        grid=(ncol // bn,),
        in_specs=[
            pl.BlockSpec((rows, D_MODEL), lambda j: (0, 0)),
            pl.BlockSpec((D_MODEL, bn), lambda j: (0, j)),
            pl.BlockSpec((1, bn), lambda j: (0, j)),
        ],
        out_specs=pl.BlockSpec((rows, bn), lambda j: (0, j)),
        out_shape=jax.ShapeDtypeStruct((rows, ncol), f32),
        compiler_params=pltpu.CompilerParams(dimension_semantics=("arbitrary",)),
        name="adaln_mod",
    )(c_all, w_ada, b_ada)


def _prompt_kernel(
    x_ref, mod_ref, sinks_ref,
    g_pre_mix_ref, w_in_ref, conv_w_ref, conv_b_ref, wa_ref, ba_ref, wi_ref, bi_ref, lam_ref,
    g_attn_ref, g_rnn_ref, w_out_ref, g_post_mix_ref, g_pre_ffn_ref, w_up_ref,
    fconv_w_ref, fconv_b_ref, w_down_ref, g_post_ffn_ref,
    y_ref, kwin_ref, vwin_ref, hlast_ref, convst_ref, ffnst_ref,
    kbuf, vbuf, xrbuf, a_s, u_s, hcar, upbuf, attn_s, act_s,
):
    TB = TOKEN_BLOCK
    step = pl.program_id(0)
    last = pl.num_programs(0) - 1

    @pl.when(step == 0)
    def _init():
        kbuf[0:WINDOW, :] = jnp.zeros((WINDOW, KV_WIDTH), bf16)
        vbuf[0:WINDOW, :] = jnp.zeros((WINDOW, KV_WIDTH), bf16)
        xrbuf[0:SUBLANES, :] = jnp.zeros((SUBLANES, D_RNN), f32)
        upbuf[0:SUBLANES, :] = jnp.zeros((SUBLANES, 2 * D_FF), f32)
        hcar[...] = jnp.zeros((SUBLANES, D_RNN), f32)

    sh1 = mod_ref[0:1, 0 * D_MODEL:1 * D_MODEL]
    sc1 = mod_ref[0:1, 1 * D_MODEL:2 * D_MODEL]
    g1 = mod_ref[0:1, 2 * D_MODEL:3 * D_MODEL]
    sh2 = mod_ref[0:1, 3 * D_MODEL:4 * D_MODEL]
    sc2 = mod_ref[0:1, 4 * D_MODEL:5 * D_MODEL]
    g2 = mod_ref[0:1, 5 * D_MODEL:6 * D_MODEL]

    x = x_ref[...]
    hmix = (_rms_norm(x, g_pre_mix_ref[...]) * (1.0 + sc1) + sh1).astype(bf16)

    q = jnp.dot(hmix, w_in_ref[:, 0:C_K], preferred_element_type=f32) * Q_SCALE
    kv = jnp.dot(hmix, w_in_ref[:, C_K:C_XR], preferred_element_type=f32)
    k = kv[:, 0:KV_WIDTH]
    v = kv[:, KV_WIDTH:2 * KV_WIDTH]
    kbuf[WINDOW:WINDOW + TB, :] = k.astype(bf16)
    vbuf[WINDOW:WINDOW + TB, :] = v.astype(bf16)
    xr = jnp.dot(hmix, w_in_ref[:, C_XR:C_YR], preferred_element_type=f32)
    xrbuf[SUBLANES:SUBLANES + TB, :] = xr

    @pl.when(step == last)
    def _win_out():
        kwin_ref[...] = k[TB - WINDOW:TB, :]
        vwin_ref[...] = v[TB - WINDOW:TB, :]
        convst_ref[...] = xr[TB - (RNN_CONV_W - 1):TB, :]

    qi = lax.broadcasted_iota(jnp.int32, (WINDOW, 2 * WINDOW), 0)
    si = lax.broadcasted_iota(jnp.int32, (WINDOW, 2 * WINDOW), 1)
    dist = qi + WINDOW - si
    valid = (dist >= 0) & (dist < WINDOW)
    base = jnp.where(valid, dist.astype(f32), jnp.inf)
    first_pen = jnp.where(step == 0, jnp.inf, 0.0)
    base_first = jnp.where(si < WINDOW, base + first_pen, base)
    qb = q.astype(bf16)
    for j in range(TB // WINDOW):
        r0 = j * WINDOW
        bj = base_first if j == 0 else base
        for h in range(N_HEADS):
            c = h // GQA_GROUP
            qh = qb[r0:r0 + WINDOW, h * HEAD_DIM:(h + 1) * HEAD_DIM]
            kw = kbuf[r0:r0 + 2 * WINDOW, c * HEAD_DIM:(c + 1) * HEAD_DIM]
            vw = vbuf[r0:r0 + 2 * WINDOW, c * HEAD_DIM:(c + 1) * HEAD_DIM]
            s = _dot_nt(qh, kw) - ALIBI_SLOPES[h] * bj
            sink = sinks_ref[h]
            m = jnp.maximum(jnp.max(s, axis=-1, keepdims=True), sink)
            p = jnp.exp(s - m)
            denom = jnp.sum(p, axis=-1, keepdims=True) + jnp.exp(sink - m)
            o = jnp.dot(p.astype(bf16), vw, preferred_element_type=f32) / denom
            attn_s[r0:r0 + WINDOW, h * HEAD_DIM:(h + 1) * HEAD_DIM] = o

    kbuf[0:WINDOW, :] = kbuf[TB:TB + WINDOW, :]
    vbuf[0:WINDOW, :] = vbuf[TB:TB + WINDOW, :]

    xc = conv_b_ref[...] + conv_w_ref[RNN_CONV_W - 1:RNN_CONV_W, :] * xr
    for jj in range(1, RNN_CONV_W):
        xc = xc + (conv_w_ref[RNN_CONV_W - 1 - jj:RNN_CONV_W - jj, :]
                   * xrbuf[SUBLANES - jj:SUBLANES - jj + TB, :])
    xrbuf[0:SUBLANES, :] = xrbuf[TB:TB + SUBLANES, :]

    sp = _softplus_neg(lam_ref[...])
    a_parts, u_parts = _rglru_gates(xc, wa_ref, ba_ref[...], wi_ref, bi_ref[...], sp)
    for n in range(RNN_BLOCKS):
        sl = slice(n * RNN_BLOCK_W, (n + 1) * RNN_BLOCK_W)
        a_s[:, sl] = a_parts[n]
        u_s[:, sl] = u_parts[n]

    row = lax.broadcasted_iota(jnp.int32, (SUBLANES, D_RNN), 0)

    def scan_body(g, hprev):
        r = pl.multiple_of(g * SUBLANES, SUBLANES)
        a = a_s[pl.ds(r, SUBLANES), :]
        u = u_s[pl.ds(r, SUBLANES), :]
        for s_ in (1, 2, 4):
            ok = row >= s_
            a_sh = pltpu.roll(a, s_, axis=0)
            u_sh = pltpu.roll(u, s_, axis=0)
            u = jnp.where(ok, a * u_sh + u, u)
            a = jnp.where(ok, a * a_sh, a)
        hh = a * hprev + u
        u_s[pl.ds(r, SUBLANES), :] = hh
        return jnp.broadcast_to(hh[SUBLANES - 1:SUBLANES, :], (SUBLANES, D_RNN))

    hfin = lax.fori_loop(0, TB // SUBLANES, scan_body, hcar[...])
    hcar[...] = hfin

    @pl.when(step == last)
    def _h_out():
        hlast_ref[...] = hfin[0:1, :]

    yr = jnp.dot(hmix, w_in_ref[:, C_YR:IN_COLS], preferred_element_type=f32)
    rnn = u_s[...] * _gelu_tanh(yr)

    attn_n = _rms_norm(attn_s[...], g_attn_ref[...])
    rnn_n = _rms_norm(rnn, g_rnn_ref[...])
    mo = (_dot(attn_n, w_out_ref[0:ATTN_WIDTH, :])
          + _dot(rnn_n, w_out_ref[ATTN_WIDTH:ATTN_WIDTH + D_RNN, :]))
    x1 = x + g1 * _rms_norm(mo, g_post_mix_ref[...])

    hff = (_rms_norm(x1, g_pre_ffn_ref[...]) * (1.0 + sc2) + sh2).astype(bf16)
    upbuf[SUBLANES:SUBLANES + TB, :] = jnp.dot(hff, w_up_ref[...], preferred_element_type=f32)

    @pl.when(step == last)
    def _ffn_out():
        ffnst_ref[...] = upbuf[SUBLANES + TB - (FFN_CONV_W - 1):SUBLANES + TB, :]

    def conv_cols(c0, c1):
        acc = fconv_b_ref[:, c0:c1] + fconv_w_ref[FFN_CONV_W - 1:FFN_CONV_W, c0:c1] * upbuf[SUBLANES:SUBLANES + TB, c0:c1]
        for jj in range(1, FFN_CONV_W):
            acc = acc + (fconv_w_ref[FFN_CONV_W - 1 - jj:FFN_CONV_W - jj, c0:c1]
                         * upbuf[SUBLANES - jj:SUBLANES - jj + TB, c0:c1])
        return acc

    half = D_FF // 2
    for ci in range(2):
        c0 = ci * half
        gate = conv_cols(c0, c0 + half)
        val = conv_cols(D_FF + c0, D_FF + c0 + half)
        act_s[:, c0:c0 + half] = (_gelu_tanh(gate) * val).astype(bf16)
    upbuf[0:SUBLANES, :] = upbuf[TB:TB + SUBLANES, :]

    f = jnp.dot(act_s[...], w_down_ref[...], preferred_element_type=f32)
    y_ref[...] = x1 + g2 * _rms_norm(f, g_post_ffn_ref[...])


def _const_spec(shape):
    nd = len(shape)
    return pl.BlockSpec(shape, lambda i: (0,) * nd)


def _prompt_call(x, mod_p, sinks, params):
    T = x.shape[0]
    TB = TOKEN_BLOCK
    (g_pre_mix, w_in, conv_w, conv_b, wa, ba, wi, bi, lam, g_attn, g_rnn, w_out,
     g_post_mix, g_pre_ffn, w_up, fconv_w, fconv_b, w_down, g_post_ffn) = params
    ins = [x, mod_p, sinks, g_pre_mix, w_in, conv_w, conv_b, wa, ba, wi, bi, lam, g_attn, g_rnn,
           w_out, g_post_mix, g_pre_ffn, w_up, fconv_w, fconv_b, w_down, g_post_ffn]
    in_specs = [pl.BlockSpec((TB, D_MODEL), lambda i: (i, 0)),
                _const_spec(mod_p.shape),
                pl.BlockSpec(memory_space=pltpu.SMEM)]
    in_specs += [_const_spec(a.shape) for a in ins[3:]]
    out_shape = (
        jax.ShapeDtypeStruct((T, D_MODEL), f32),
        jax.ShapeDtypeStruct((WINDOW, KV_WIDTH), f32),
        jax.ShapeDtypeStruct((WINDOW, KV_WIDTH), f32),
        jax.ShapeDtypeStruct((1, D_RNN), f32),
        jax.ShapeDtypeStruct((RNN_CONV_W - 1, D_RNN), f32),
        jax.ShapeDtypeStruct((FFN_CONV_W - 1, 2 * D_FF), f32),
    )
    out_specs = (
        pl.BlockSpec((TB, D_MODEL), lambda i: (i, 0)),
        _const_spec((WINDOW, KV_WIDTH)),
        _const_spec((WINDOW, KV_WIDTH)),
        _const_spec((1, D_RNN)),
        _const_spec((RNN_CONV_W - 1, D_RNN)),
        _const_spec((FFN_CONV_W - 1, 2 * D_FF)),
    )
    scratch = [
        pltpu.VMEM((WINDOW + TB, KV_WIDTH), bf16),
        pltpu.VMEM((WINDOW + TB, KV_WIDTH), bf16),
        pltpu.VMEM((SUBLANES + TB, D_RNN), f32),
        pltpu.VMEM((TB, D_RNN), f32),
        pltpu.VMEM((TB, D_RNN), f32),
        pltpu.VMEM((SUBLANES, D_RNN), f32),
        pltpu.VMEM((SUBLANES + TB, 2 * D_FF), f32),
        pltpu.VMEM((TB, ATTN_WIDTH), f32),
        pltpu.VMEM((TB, D_FF), bf16),
    ]
    return pl.pallas_call(
        _prompt_kernel,
        grid=(T // TB,),
        in_specs=in_specs,
        out_specs=out_specs,
        out_shape=out_shape,
        scratch_shapes=scratch,
        compiler_params=pltpu.CompilerParams(
            dimension_semantics=("arbitrary",), vmem_limit_bytes=VMEM_LIMIT_BYTES),
        name="prompt_layer",
    )(*ins)


def _sample_pre_kernel(x_ref, mod_ref, g_pre_mix_ref, w_in_ref, q_ref, kv_ref, xr_ref, yr_ref):
    sh1 = mod_ref[:, 0 * D_MODEL:1 * D_MODEL]
    sc1 = mod_ref[:, 1 * D_MODEL:2 * D_MODEL]
    hmix = (_rms_norm(x_ref[...], g_pre_mix_ref[...]) * (1.0 + sc1) + sh1).astype(bf16)
    q_ref[...] = jnp.dot(hmix, w_in_ref[:, 0:C_K], preferred_element_type=f32) * Q_SCALE
    kv_ref[...] = jnp.dot(hmix, w_in_ref[:, C_K:C_XR], preferred_element_type=f32)
    xr_ref[...] = jnp.dot(hmix, w_in_ref[:, C_XR:C_YR], preferred_element_type=f32)
    yr_ref[...] = jnp.dot(hmix, w_in_ref[:, C_YR:IN_COLS], preferred_element_type=f32)


def _sample_pre_call(x, mod_s, g_pre_mix, w_in):
    B = x.shape[0]
    return pl.pallas_call(
        _sample_pre_kernel,
        out_shape=(
            jax.ShapeDtypeStruct((B, ATTN_WIDTH), f32),
            jax.ShapeDtypeStruct((B, 2 * KV_WIDTH), f32),
            jax.ShapeDtypeStruct((B, D_RNN), f32),
            jax.ShapeDtypeStruct((B, D_RNN), f32),
        ),
        compiler_params=pltpu.CompilerParams(vmem_limit_bytes=VMEM_LIMIT_BYTES),
        name="sample_pre",
    )(x, mod_s, g_pre_mix, w_in)


def _sample_attn_kernel(q_ref, kv_ref, ck_ref, cv_ref, sinks_ref, o_ref, kwin_ref, vwin_ref):
    R = N_KV_HEADS * WINDOW
    hrow = lax.broadcasted_iota(jnp.int32, (N_HEADS, R), 0)
    rcol = lax.broadcasted_iota(jnp.int32, (N_HEADS, R), 1)
    slope = jnp.exp2(-8.0 * (hrow + 1).astype(f32) / N_HEADS)
    own = (rcol % N_KV_HEADS) == (hrow // GQA_GROUP)
    bias = jnp.where(own, slope * (WINDOW - 1 - rcol // N_KV_HEADS).astype(f32), jnp.inf)
    wrow = lax.broadcasted_iota(jnp.int32, (R, HEAD_DIM), 0)
    sink = sinks_ref[...]

    def shifted(cache, new_rows):
        out = pltpu.roll(cache, R - N_KV_HEADS, axis=0)
        for c in range(N_KV_HEADS):
            out = jnp.where(wrow == R - N_KV_HEADS + c, new_rows[c], out)
        return out

    for b in range(SAMPLE_CHUNK):
        knew = [kv_ref[b:b + 1, c * HEAD_DIM:(c + 1) * HEAD_DIM] for c in range(N_KV_HEADS)]
        vnew = [kv_ref[b:b + 1, KV_WIDTH + c * HEAD_DIM:KV_WIDTH + (c + 1) * HEAD_DIM]
                for c in range(N_KV_HEADS)]
        kw = shifted(ck_ref[b], knew)
        vw = shifted(cv_ref[b], vnew)
        kwin_ref[b] = kw
        vwin_ref[b] = vw
        s = _dot_nt(q_ref[b].astype(bf16), kw.astype(bf16)) - bias
        m = jnp.maximum(jnp.max(s, axis=-1, keepdims=True), sink)
        p = jnp.exp(s - m)
        denom = jnp.sum(p, axis=-1, keepdims=True) + jnp.exp(sink - m)
        o_ref[b] = jnp.dot(p.astype(bf16), vw.astype(bf16), preferred_element_type=f32) / denom


def _sample_attn_call(q, kv, ck, cv, sinks_col):
    B = q.shape[0]
    BC = SAMPLE_CHUNK
    return pl.pallas_call(
        _sample_attn_kernel,
        grid=(B // BC,),
        in_specs=[
            pl.BlockSpec((BC, N_HEADS, HEAD_DIM), lambda i: (i, 0, 0)),
            pl.BlockSpec((BC, 2 * KV_WIDTH), lambda i: (i, 0)),
            pl.BlockSpec((BC, N_KV_HEADS * WINDOW, HEAD_DIM), lambda i: (i, 0, 0)),
            pl.BlockSpec((BC, N_KV_HEADS * WINDOW, HEAD_DIM), lambda i: (i, 0, 0)),
            pl.BlockSpec((N_HEADS, 1), lambda i: (0, 0)),
        ],
        out_specs=(
            pl.BlockSpec((BC, N_HEADS, HEAD_DIM), lambda i: (i, 0, 0)),
            pl.BlockSpec((BC, N_KV_HEADS * WINDOW, HEAD_DIM), lambda i: (i, 0, 0)),
            pl.BlockSpec((BC, N_KV_HEADS * WINDOW, HEAD_DIM), lambda i: (i, 0, 0)),
        ),
        out_shape=(
            jax.ShapeDtypeStruct((B, N_HEADS, HEAD_DIM), f32),
            jax.ShapeDtypeStruct((B, N_KV_HEADS * WINDOW, HEAD_DIM), f32),
            jax.ShapeDtypeStruct((B, N_KV_HEADS * WINDOW, HEAD_DIM), f32),
        ),
        compiler_params=pltpu.CompilerParams(dimension_semantics=("arbitrary",)),
        name="sample_attn",
    )(q, kv, ck, cv, sinks_col)


def _sample_post_kernel(
    x_ref, mod_ref, attn_ref, xr_ref, yr_ref, h0_ref, cbuf_ref, fbuf_ref,
    conv_w_ref, conv_b_ref, wa_ref, ba_ref, wi_ref, bi_ref, lam_ref,
    g_attn_ref, g_rnn_ref, w_out_ref, g_post_mix_ref, g_pre_ffn_ref, w_up_ref,
    fconv_w_ref, fconv_b_ref, w_down_ref, g_post_ffn_ref,
    y_ref, h_ref, cst_ref, fst_ref,
):
    g1 = mod_ref[:, 2 * D_MODEL:3 * D_MODEL]
    sh2 = mod_ref[:, 3 * D_MODEL:4 * D_MODEL]
    sc2 = mod_ref[:, 4 * D_MODEL:5 * D_MODEL]
    g2 = mod_ref[:, 5 * D_MODEL:6 * D_MODEL]
    x = x_ref[...]
    xr = xr_ref[...]

    xc = conv_b_ref[...] + conv_w_ref[RNN_CONV_W - 1:RNN_CONV_W, :] * xr
    for jj in range(RNN_CONV_W - 1):
        xc = xc + conv_w_ref[jj:jj + 1, :] * cbuf_ref[:, jj, :]
    for jj in range(RNN_CONV_W - 2):
        cst_ref[:, jj, :] = cbuf_ref[:, jj + 1, :]
    cst_ref[:, RNN_CONV_W - 2, :] = xr

    sp = _softplus_neg(lam_ref[...])
    a_parts, u_parts = _rglru_gates(xc, wa_ref, ba_ref[...], wi_ref, bi_ref[...], sp)
    a = jnp.concatenate(a_parts, axis=1)
    u = jnp.concatenate(u_parts, axis=1)
    h = a * h0_ref[...] + u
    h_ref[...] = h
    rnn = h * _gelu_tanh(yr_ref[...])

    attn_n = _rms_norm(attn_ref[...], g_attn_ref[...])
    rnn_n = _rms_norm(rnn, g_rnn_ref[...])
    mo = (_dot(attn_n, w_out_ref[0:ATTN_WIDTH, :])
          + _dot(rnn_n, w_out_ref[ATTN_WIDTH:ATTN_WIDTH + D_RNN, :]))
    x1 = x + g1 * _rms_norm(mo, g_post_mix_ref[...])

    hff = (_rms_norm(x1, g_pre_ffn_ref[...]) * (1.0 + sc2) + sh2).astype(bf16)
    up_pre = jnp.dot(hff, w_up_ref[...], preferred_element_type=f32)
    up = fconv_b_ref[...] + fconv_w_ref[FFN_CONV_W - 1:FFN_CONV_W, :] * up_pre
    for jj in range(FFN_CONV_W - 1):
        up = up + fconv_w_ref[jj:jj + 1, :] * fbuf_ref[:, jj, :]
    for jj in range(FFN_CONV_W - 2):
        fst_ref[:, jj, :] = fbuf_ref[:, jj + 1, :]
    fst_ref[:, FFN_CONV_W - 2, :] = up_pre

    act = (_gelu_tanh(up[:, 0:D_FF]) * up[:, D_FF:2 * D_FF]).astype(bf16)
    f = jnp.dot(act, w_down_ref[...], preferred_element_type=f32)
    y_ref[...] = x1 + g2 * _rms_norm(f, g_post_ffn_ref[...])


def _sample_post_call(x, mod_s, attn, xr, yr, h0, cbuf, fbuf, params):
    B = x.shape[0]
    (_, _, conv_w, conv_b, wa, ba, wi, bi, lam, g_attn, g_rnn, w_out,
     g_post_mix, g_pre_ffn, w_up, fconv_w, fconv_b, w_down, g_post_ffn) = params
    return pl.pallas_call(
        _sample_post_kernel,
        out_shape=(
            jax.ShapeDtypeStruct((B, D_MODEL), f32),
            jax.ShapeDtypeStruct((B, D_RNN), f32),
            jax.ShapeDtypeStruct((B, RNN_CONV_W - 1, D_RNN), f32),
            jax.ShapeDtypeStruct((B, FFN_CONV_W - 1, 2 * D_FF), f32),
        ),
        compiler_params=pltpu.CompilerParams(vmem_limit_bytes=VMEM_LIMIT_BYTES),
        name="sample_post",
    )(x, mod_s, attn, xr, yr, h0, cbuf, fbuf, conv_w, conv_b, wa, ba, wi, bi, lam,
      g_attn, g_rnn, w_out, g_post_mix, g_pre_ffn, w_up, fconv_w, fconv_b, w_down, g_post_ffn)


def kernel(x_prompt, x_sample, cache_k, cache_v, state_h, state_conv, state_ffn_conv, c_prompt, c_sample, w_ada, b_ada, g_pre_mix, w_in, conv_w, conv_b, w_a, b_a, w_i, b_i, lam, sinks, g_attn_out, g_rnn_out, w_out, g_post_mix, g_pre_ffn, w_up, ffn_conv_w, ffn_conv_b, w_down, g_post_ffn):
    depth = w_in.shape[0]
    assert depth == 1 and x_prompt.shape[0] == 1 and x_sample.shape[1] == 1
    T = x_prompt.shape[1]
    B = x_sample.shape[0]
    W = cache_k.shape[2]
    assert W == WINDOW and T % TOKEN_BLOCK == 0 and B % SAMPLE_CHUNK == 0

    row = lambda a: a[0].reshape(1, -1)
    params = (
        row(g_pre_mix), w_in[0].astype(bf16), conv_w[0], row(conv_b),
        w_a[0].astype(bf16), row(b_a), w_i[0].astype(bf16), row(b_i), row(lam),
        row(g_attn_out), row(g_rnn_out), w_out[0].astype(bf16), row(g_post_mix), row(g_pre_ffn),
        w_up[0].astype(bf16), ffn_conv_w[0], row(ffn_conv_b), w_down[0].astype(bf16),
        row(g_post_ffn),
    )

    c_all = jnp.concatenate(
        [jnp.broadcast_to(c_prompt, (MOD_PAD_ROWS, D_MODEL)), c_sample], axis=0)
    mod = _mod_call(c_all, w_ada[0], b_ada[0].reshape(1, -1))
    mod_p = mod[0:MOD_PAD_ROWS]
    mod_s = mod[MOD_PAD_ROWS:]

    yp, kp, vp, hp, convp, ffnp = _prompt_call(x_prompt[0], mod_p, sinks[0], params)

    xs = x_sample[:, 0, :]
    q, kv, xr, yr = _sample_pre_call(xs, mod_s, params[0], params[1])
    ck = cache_k.reshape(B, W * N_KV_HEADS, HEAD_DIM)
    cv = cache_v.reshape(B, W * N_KV_HEADS, HEAD_DIM)
    attn3, kwin, vwin = _sample_attn_call(
        q.reshape(B, N_HEADS, HEAD_DIM), kv, ck, cv, sinks[0].reshape(N_HEADS, 1))
    attn = attn3.reshape(B, ATTN_WIDTH)
    ys, hs, convs, ffns = _sample_post_call(
        xs, mod_s, attn, xr, yr, state_h[0], state_conv[0], state_ffn_conv[0], params)

    kv_shape = (1, 1, W, N_KV_HEADS, HEAD_DIM)
    kvs_shape = (1, B, W, N_KV_HEADS, HEAD_DIM)
    return (
        yp[None], ys[:, None, :],
        kp.reshape(kv_shape), vp.reshape(kv_shape), hp[None], convp[None, None], ffnp[None, None],
        kwin.reshape(kvs_shape), vwin.reshape(kvs_shape), hs[None], convs[None], ffns[None],
    )
```

```python
import math

import jax
import jax.numpy as jnp
from jax import lax
from jax.experimental import pallas as pl
from jax.experimental.pallas import tpu as pltpu

D_MODEL = 1024
N_HEADS = 8
N_KV_HEADS = 2
HEAD_DIM = 128
GQA_GROUP = N_HEADS // N_KV_HEADS
ATTN_WIDTH = N_HEADS * HEAD_DIM
KV_WIDTH = N_KV_HEADS * HEAD_DIM
WINDOW = 128
D_RNN = D_MODEL
RNN_BLOCKS = 8
RNN_BLOCK_W = D_RNN // RNN_BLOCKS
RG_C = 8.0
RNN_CONV_W = 4
D_FF = 2816
FFN_CONV_W = 3
RMS_EPS = 1e-6

C_K = ATTN_WIDTH
C_V = C_K + KV_WIDTH
C_XR = C_V + KV_WIDTH
C_YR = C_XR + D_RNN
IN_COLS = C_YR + D_RNN

SUBLANES = 8
SUB_ROWS = WINDOW
GROUPS = SUB_ROWS // SUBLANES
TOKEN_BLOCK = 256
NSUB = TOKEN_BLOCK // SUB_ROWS
FFN_CHUNK = 256
SAMPLE_CHUNK = 16
MOD_PAD_ROWS = 8
VMEM_LIMIT_BYTES = 56 * 1024 * 1024

ALIBI_SLOPES = tuple(2.0 ** (-8.0 * (h + 1) / N_HEADS) for h in range(N_HEADS))
Q_SCALE = HEAD_DIM ** -0.5
SQRT_2_OVER_PI = math.sqrt(2.0 / math.pi)

bf16 = jnp.bfloat16
f32 = jnp.float32


def _rms_scale(x):
    return lax.rsqrt(jnp.mean(x * x, axis=-1, keepdims=True) + RMS_EPS)


def _rms_norm(x, g):
    return x * _rms_scale(x) * g


def _gelu_tanh(x):
    return x * (0.5 * (1.0 + jnp.tanh(SQRT_2_OVER_PI * (x + 0.044715 * (x * x * x)))))


def _sigmoid(x):
    return 1.0 / (1.0 + jnp.exp(-x))


def _softplus_neg(lam):
    return jnp.maximum(-lam, 0.0) + jnp.log1p(jnp.exp(-jnp.abs(lam)))


def _dot(a, b):
    return jnp.dot(a.astype(bf16), b, preferred_element_type=f32)


def _dot_nt(a, b):
    return lax.dot_general(a, b, (((1,), (1,)), ((), ())), preferred_element_type=f32)


def _rglru_gates(xc, wa_ref, ba, wi_ref, bi, sp):
    a_parts, u_parts = [], []
    for n in range(RNN_BLOCKS):
        sl = slice(n * RNN_BLOCK_W, (n + 1) * RNN_BLOCK_W)
        xn = xc[:, sl]
        xb = xn.astype(bf16)
        r = _sigmoid(jnp.dot(xb, wa_ref[n], preferred_element_type=f32) + ba[:, sl])
        i = _sigmoid(jnp.dot(xb, wi_ref[n], preferred_element_type=f32) + bi[:, sl])
        log_a = (-RG_C) * r * sp[:, sl]
        t = jnp.tanh(log_a)
        one_minus_a2 = (-2.0 * t) / (1.0 - t)
        a_parts.append(jnp.exp(log_a))
        u_parts.append(jnp.sqrt(one_minus_a2) * (i * xn))
    return a_parts, u_parts


def _mod_kernel(c_ref, w_ref, b_ref, o_ref):
    c = c_ref[...]
    s = c * _sigmoid(c)
    o_ref[...] = _dot(s, w_ref[...].astype(bf16)) + b_ref[...]


def _mod_call(c_all, w_ada, b_ada):
    rows = c_all.shape[0]
    ncol = w_ada.shape[1]
    bn = D_MODEL
    return pl.pallas_call(
        _mod_kernel,
        grid=(ncol // bn,),
        in_specs=[
            pl.BlockSpec((rows, D_MODEL), lambda j: (0, 0)),
            pl.BlockSpec((D_MODEL, bn), lambda j: (0, j)),
            pl.BlockSpec((1, bn), lambda j: (0, j)),
        ],
        out_specs=pl.BlockSpec((rows, bn), lambda j: (0, j)),
        out_shape=jax.ShapeDtypeStruct((rows, ncol), f32),
        compiler_params=pltpu.CompilerParams(dimension_semantics=("arbitrary",)),
        name="adaln_mod",
    )(c_all, w_ada, b_ada)


def _groups(v, c0=None, c1=None):
    if c0 is None:
        return [v[j * SUBLANES:(j + 1) * SUBLANES, :] for j in range(GROUPS)]
    return [v[j * SUBLANES:(j + 1) * SUBLANES, c0:c1] for j in range(GROUPS)]


def _shifted_groups(X, tail_row, sub0, max_shift):
    wrapped = {}
    for j in range(GROUPS - max_shift, GROUPS):
        wrapped[j] = jnp.where(sub0, tail_row(j), pltpu.roll(X[j], 1, axis=0))
    sh = {}
    for d in range(1, max_shift + 1):
        sh[d] = [X[j - d] if j >= d else wrapped[j - d + GROUPS] for j in range(GROUPS)]
    return sh


def _scan_sub(a, u, h_in, sub_iota):
    A = _groups(a)
    L = _groups(u)
    for j in range(1, GROUPS):
        L[j] = A[j] * L[j - 1] + L[j]
        A[j] = A[j] * A[j - 1]
    ae, le = A[GROUPS - 1], L[GROUPS - 1]
    for s_ in (1, 2, 4):
        ok = sub_iota >= s_
        a_sh = pltpu.roll(ae, s_, axis=0)
        l_sh = pltpu.roll(le, s_, axis=0)
        le = jnp.where(ok, ae * l_sh + le, le)
        ae = jnp.where(ok, ae * a_sh, ae)
    hend = le + ae * h_in
    hprev = jnp.where(sub_iota == 0, h_in, pltpu.roll(hend, 1, axis=0))
    h = jnp.concatenate([L[j] + A[j] * hprev for j in range(GROUPS)], axis=0)
    h_out = jnp.broadcast_to(hend[SUBLANES - 1:SUBLANES, :], hend.shape)
    return h, h_out


def _prompt_kernel(
    x_hbm, mod_ref, sinks_ref,
    g_pre_mix_ref, w_in_ref, conv_w_ref, conv_b_ref, wa_ref, ba_ref, wi_ref, bi_ref, lam_ref,
    g_attn_ref, g_rnn_ref, w_out_ref, g_post_mix_ref, g_pre_ffn_ref, w_up_ref,
    fconv_w_ref, fconv_b_ref, w_down_ref, g_post_ffn_ref,
    y_hbm, kwin_hbm, vwin_hbm, hlast_ref, convst_ref, ffnst_ref,
    xbuf, ybuf, sem_in, sem_out, sem_st, kprev, vprev, kst, vst, xr_tail, up_tail, hcar, bias_s,
):
    step = pl.program_id(0)
    nsteps = pl.num_programs(0)
    slot = lax.rem(step, 2)

    def x_copies(st, sl):
        return [pltpu.make_async_copy(
            x_hbm.at[st * NSUB + s, :, j, :],
            xbuf.at[sl, pl.ds(s * SUB_ROWS + j * SUBLANES, SUBLANES), :],
            sem_in.at[sl]) for s in range(NSUB) for j in range(GROUPS)]

    def y_copies(st, sl):
        return [pltpu.make_async_copy(
            ybuf.at[sl, pl.ds(s * SUB_ROWS + j * SUBLANES, SUBLANES), :],
            y_hbm.at[st * NSUB + s, :, j, :],
            sem_out.at[sl]) for s in range(NSUB) for j in range(GROUPS)]

    def state_copies():
        cps = []
        for src, dst in ((kst, kwin_hbm), (vst, vwin_hbm)):
            cps += [pltpu.make_async_copy(src.at[pl.ds(j * SUBLANES, SUBLANES), :],
                                          dst.at[:, j, :], sem_st) for j in range(GROUPS)]
        return cps

    @pl.when(step == 0)
    def _init():
        for cp in x_copies(0, 0):
            cp.start()
        kprev[...] = jnp.zeros(kprev.shape, bf16)
        vprev[...] = jnp.zeros(vprev.shape, bf16)
        xr_tail[...] = jnp.zeros(xr_tail.shape, f32)
        up_tail[...] = jnp.zeros(up_tail.shape, f32)
        hcar[...] = jnp.zeros(hcar.shape, f32)
        rq = lax.broadcasted_iota(jnp.int32, (SUB_ROWS, 2 * SUB_ROWS), 0)
        ck = lax.broadcasted_iota(jnp.int32, (SUB_ROWS, 2 * SUB_ROWS), 1)
        rk = ck & (SUB_ROWS - 1)
        tq = (rq % SUBLANES) * GROUPS + rq // SUBLANES
        tk = (rk % SUBLANES) * GROUPS + rk // SUBLANES
        dist = tq - tk + jnp.where(ck < SUB_ROWS, WINDOW, 0)
        base = jnp.where((dist >= 0) & (dist < WINDOW), dist.astype(f32), jnp.inf)
        for h in range(N_HEADS):
            bias_s[h] = ALIBI_SLOPES[h] * base

    @pl.when(step + 1 < nsteps)
    def _prefetch():
        for cp in x_copies(step + 1, 1 - slot):
            cp.start()

    @pl.when(step >= 2)
    def _free_ybuf():
        for cp in y_copies(step, slot):
            cp.wait()

    for cp in x_copies(step, slot):
        cp.wait()

    sh1 = mod_ref[0:1, 0 * D_MODEL:1 * D_MODEL]
    sc1 = mod_ref[0:1, 1 * D_MODEL:2 * D_MODEL]
    g1 = mod_ref[0:1, 2 * D_MODEL:3 * D_MODEL]
    sh2 = mod_ref[0:1, 3 * D_MODEL:4 * D_MODEL]
    sc2 = mod_ref[0:1, 4 * D_MODEL:5 * D_MODEL]
    g2 = mod_ref[0:1, 5 * D_MODEL:6 * D_MODEL]
    gs1 = g_pre_mix_ref[...] * (1.0 + sc1)
    gs2 = g_pre_ffn_ref[...] * (1.0 + sc2)
    sp = _softplus_neg(lam_ref[...])
    first_pen = jnp.where(step == 0, jnp.inf, 0.0)
    sub_iota = lax.broadcasted_iota(jnp.int32, (SUBLANES, D_RNN), 0)
    sub0_rnn = sub_iota == 0
    sub0_ffn = lax.broadcasted_iota(jnp.int32, (SUBLANES, FFN_CHUNK), 0) == 0
    conv_w = [jnp.broadcast_to(conv_w_ref[jj:jj + 1, :], (SUBLANES, D_RNN))
              for jj in range(RNN_CONV_W)]
    conv_b = jnp.broadcast_to(conv_b_ref[...], (SUBLANES, D_RNN))

    sts = [dict() for _ in range(NSUB)]

    def rows(s):
        return pl.ds(s * SUB_ROWS, SUB_ROWS)

    def stage_pre(s):
        x = xbuf[slot, rows(s), :]
        sts[s]["hmix"] = (x * _rms_scale(x) * gs1 + sh1).astype(bf16)

    def stage_inproj(s):
        st = sts[s]
        hm = st["hmix"]
        st["q"] = (jnp.dot(hm, w_in_ref[:, 0:C_K], preferred_element_type=f32) * Q_SCALE).astype(bf16)
        kv = jnp.dot(hm, w_in_ref[:, C_K:C_XR], preferred_element_type=f32)
        st["kb"] = kv[:, 0:KV_WIDTH].astype(bf16)
        st["vb"] = kv[:, KV_WIDTH:2 * KV_WIDTH].astype(bf16)
        if s == NSUB - 1:
            kst[...] = kv[:, 0:KV_WIDTH]
            vst[...] = kv[:, KV_WIDTH:2 * KV_WIDTH]
        st["xr"] = jnp.dot(hm, w_in_ref[:, C_XR:C_YR], preferred_element_type=f32)

    def stage_attn(s):
        st = sts[s]
        if s == 0:
            kp, vp = kprev[...], vprev[...]
        else:
            kp, vp = sts[s - 1]["kb"], sts[s - 1]["vb"]
        kw = jnp.concatenate([kp, st["kb"]], axis=0)
        vw = jnp.concatenate([vp, st["vb"]], axis=0)
        outs = []
        for h in range(N_HEADS):
            c = h // GQA_GROUP
            hs = slice(h * HEAD_DIM, (h + 1) * HEAD_DIM)
            cs = slice(c * HEAD_DIM, (c + 1) * HEAD_DIM)
            sc = _dot_nt(st["q"][:, hs], kw[:, cs])
            if s == 0:
                sc = jnp.concatenate([sc[:, 0:SUB_ROWS] - first_pen, sc[:, SUB_ROWS:]], axis=1)
            sc = sc - bias_s[h]
            sink = sinks_ref[h]
            m = jnp.maximum(jnp.max(sc, axis=-1, keepdims=True), sink)
            p = jnp.exp(sc - m)
            denom = jnp.sum(p, axis=-1, keepdims=True) + jnp.exp(sink - m)
            outs.append(jnp.dot(p.astype(bf16), vw[:, cs], preferred_element_type=f32) / denom)
        attn = jnp.concatenate(outs, axis=1)
        st["attn_n"] = _rms_norm(attn, g_attn_ref[...]).astype(bf16)

    def stage_rnn(s):
        st = sts[s]
        xr = st["xr"]
        X = _groups(xr)
        if s == 0:
            tail = lambda j: xr_tail[(j - (GROUPS - RNN_CONV_W + 1)) * SUBLANES + SUBLANES - 1:
                                     (j - (GROUPS - RNN_CONV_W + 1)) * SUBLANES + SUBLANES, :]
        else:
            pxr = sts[s - 1]["xr"]
            tail = lambda j: pxr[j * SUBLANES + SUBLANES - 1:(j + 1) * SUBLANES, :]
        sh = _shifted_groups(X, tail, sub0_rnn, RNN_CONV_W - 1)
        xcs = []
        for j in range(GROUPS):
            acc = conv_b + conv_w[RNN_CONV_W - 1] * X[j]
            for d in range(1, RNN_CONV_W):
                acc = acc + conv_w[RNN_CONV_W - 1 - d] * sh[d][j]
            xcs.append(acc)
        xc = jnp.concatenate(xcs, axis=0)
        a_parts, u_parts = _rglru_gates(xc, wa_ref, ba_ref[...], wi_ref, bi_ref[...], sp)
        a = jnp.concatenate(a_parts, axis=1)
        u = jnp.concatenate(u_parts, axis=1)
        h_in = hcar[...] if s == 0 else sts[s - 1]["h_out"]
        h, st["h_out"] = _scan_sub(a, u, h_in, sub_iota)
        yr = jnp.dot(st["hmix"], w_in_ref[:, C_YR:IN_COLS], preferred_element_type=f32)
        rnn = h * _gelu_tanh(yr)
        st["rnn_n"] = _rms_norm(rnn, g_rnn_ref[...]).astype(bf16)
        if s == NSUB - 1:
            for t in range(RNN_CONV_W - 1):
                r = (GROUPS - (RNN_CONV_W - 1) + t) * SUBLANES + SUBLANES - 1
                convst_ref[t:t + 1, :] = xr[r:r + 1, :]
            hlast_ref[...] = st["h_out"][0:1, :]

    def stage_out(s):
        st = sts[s]
        mo = (jnp.dot(st["attn_n"], w_out_ref[0:ATTN_WIDTH, :], preferred_element_type=f32)
              + jnp.dot(st["rnn_n"], w_out_ref[ATTN_WIDTH:ATTN_WIDTH + D_RNN, :],
                        preferred_element_type=f32))
        x1 = xbuf[slot, rows(s), :] + g1 * _rms_norm(mo, g_post_mix_ref[...])
        ybuf[slot, rows(s), :] = x1
        st["hff"] = (x1 * _rms_scale(x1) * gs2 + sh2).astype(bf16)

    def stage_up(s):
        sts[s]["up"] = jnp.dot(sts[s]["hff"], w_up_ref[...], preferred_element_type=f32)

    def stage_ffn(s):
        st = sts[s]
        up = st["up"]
        if s == 0:
            tail_src, tail_g0 = up_tail, GROUPS - (FFN_CONV_W - 1)
        else:
            tail_src, tail_g0 = sts[s - 1]["up"], 0

        def conv_chunk(c0):
            c1 = c0 + FFN_CHUNK
            w = [jnp.broadcast_to(fconv_w_ref[jj:jj + 1, c0:c1], (SUBLANES, FFN_CHUNK))
                 for jj in range(FFN_CONV_W)]
            b = jnp.broadcast_to(fconv_b_ref[:, c0:c1], (SUBLANES, FFN_CHUNK))
            X = _groups(up, c0, c1)
            tail = lambda j: tail_src[(j - tail_g0) * SUBLANES + SUBLANES - 1:
                                      (j - tail_g0 + 1) * SUBLANES, c0:c1]
            sh = _shifted_groups(X, tail, sub0_ffn, FFN_CONV_W - 1)
            outs = []
            for j in range(GROUPS):
                acc = b + w[FFN_CONV_W - 1] * X[j]
                for d in range(1, FFN_CONV_W):
                    acc = acc + w[FFN_CONV_W - 1 - d] * sh[d][j]
                outs.append(acc)
            return jnp.concatenate(outs, axis=0)

        acts = []
        for cc in range(D_FF // FFN_CHUNK):
            gate = conv_chunk(cc * FFN_CHUNK)
            val = conv_chunk(D_FF + cc * FFN_CHUNK)
            acts.append((_gelu_tanh(gate) * val).astype(bf16))
        st["act"] = jnp.concatenate(acts, axis=1)
        if s == NSUB - 1:
            for t in range(FFN_CONV_W - 1):
                r = (GROUPS - (FFN_CONV_W - 1) + t) * SUBLANES + SUBLANES - 1
                ffnst_ref[t:t + 1, :] = up[r:r + 1, :]

    def stage_down(s):
        f = jnp.dot(sts[s]["act"], w_down_ref[...], preferred_element_type=f32)
        ybuf[slot, rows(s), :] = ybuf[slot, rows(s), :] + g2 * _rms_norm(f, g_post_ffn_ref[...])

    stages = (stage_pre, stage_inproj, stage_attn, stage_rnn, stage_out, stage_up, stage_ffn,
              stage_down)
    for t in range(len(stages) + NSUB - 1):
        for s in range(NSUB):
            if 0 <= t - s < len(stages):
                stages[t - s](s)

    lst = sts[NSUB - 1]
    kprev[...] = lst["kb"]
    vprev[...] = lst["vb"]
    xr_tail[...] = lst["xr"][(GROUPS - (RNN_CONV_W - 1)) * SUBLANES:, :]
    up_tail[...] = lst["up"][(GROUPS - (FFN_CONV_W - 1)) * SUBLANES:, :]
    hcar[...] = lst["h_out"]

    for cp in y_copies(step, slot):
        cp.start()

    @pl.when(step == nsteps - 1)
    def _finish():
        for cp in state_copies():
            cp.start()
        for cp in state_copies():
            cp.wait()
        for cp in y_copies(step, 1 - slot) + y_copies(step, slot):
            cp.wait()


def _const_spec(shape):
    nd = len(shape)
    return pl.BlockSpec(shape, lambda i: (0,) * nd)


def _prompt_call(x, mod_p, sinks, params):
    T = x.shape[0]
    TB = TOKEN_BLOCK
    assert T // TB >= 2
    (g_pre_mix, w_in, conv_w, conv_b, wa, ba, wi, bi, lam, g_attn, g_rnn, w_out,
     g_post_mix, g_pre_ffn, w_up, fconv_w, fconv_b, w_down, g_post_ffn) = params
    x4 = x.reshape(T // SUB_ROWS, SUBLANES, GROUPS, D_MODEL)
    ins = [x4, mod_p, sinks, g_pre_mix, w_in, conv_w, conv_b, wa, ba, wi, bi, lam, g_attn, g_rnn,
           w_out, g_post_mix, g_pre_ffn, w_up, fconv_w, fconv_b, w_down, g_post_ffn]
    in_specs = [pl.BlockSpec(memory_space=pl.ANY),
                _const_spec(mod_p.shape),
                pl.BlockSpec(memory_space=pltpu.SMEM)]
    in_specs += [_const_spec(a.shape) for a in ins[3:]]
    out_shape = (
        jax.ShapeDtypeStruct(x4.shape, f32),
        jax.ShapeDtypeStruct((SUBLANES, GROUPS, KV_WIDTH), f32),
        jax.ShapeDtypeStruct((SUBLANES, GROUPS, KV_WIDTH), f32),
        jax.ShapeDtypeStruct((1, D_RNN), f32),
        jax.ShapeDtypeStruct((RNN_CONV_W - 1, D_RNN), f32),
        jax.ShapeDtypeStruct((FFN_CONV_W - 1, 2 * D_FF), f32),
    )
    out_specs = (
        pl.BlockSpec(memory_space=pl.ANY),
        pl.BlockSpec(memory_space=pl.ANY),
        pl.BlockSpec(memory_space=pl.ANY),
        _const_spec((1, D_RNN)),
        _const_spec((RNN_CONV_W - 1, D_RNN)),
        _const_spec((FFN_CONV_W - 1, 2 * D_FF)),
    )
    scratch = [
        pltpu.VMEM((2, TB, D_MODEL), f32),
        pltpu.VMEM((2, TB, D_MODEL), f32),
        pltpu.SemaphoreType.DMA((2,)),
        pltpu.SemaphoreType.DMA((2,)),
        pltpu.SemaphoreType.DMA(()),
        pltpu.VMEM((SUB_ROWS, KV_WIDTH), bf16),
        pltpu.VMEM((SUB_ROWS, KV_WIDTH), bf16),
        pltpu.VMEM((SUB_ROWS, KV_WIDTH), f32),
        pltpu.VMEM((SUB_ROWS, KV_WIDTH), f32),
        pltpu.VMEM(((RNN_CONV_W - 1) * SUBLANES, D_RNN), f32),
        pltpu.VMEM(((FFN_CONV_W - 1) * SUBLANES, 2 * D_FF), f32),
        pltpu.VMEM((SUBLANES, D_RNN), f32),
        pltpu.VMEM((N_HEADS, SUB_ROWS, 2 * SUB_ROWS), f32),
    ]
    y4, kp, vp, hp, convp, ffnp = pl.pallas_call(
        _prompt_kernel,
        grid=(T // TB,),
        in_specs=in_specs,
        out_specs=out_specs,
        out_shape=out_shape,
        scratch_shapes=scratch,
        compiler_params=pltpu.CompilerParams(
            dimension_semantics=("arbitrary",), vmem_limit_bytes=VMEM_LIMIT_BYTES),
        name="prompt_layer",
    )(*ins)
    return (y4.reshape(T, D_MODEL), kp.reshape(WINDOW, KV_WIDTH), vp.reshape(WINDOW, KV_WIDTH),
            hp, convp, ffnp)


def _sample_pre_kernel(x_ref, mod_ref, g_pre_mix_ref, w_in_ref, q_ref, kv_ref, xr_ref, yr_ref):
    sh1 = mod_ref[:, 0 * D_MODEL:1 * D_MODEL]
    sc1 = mod_ref[:, 1 * D_MODEL:2 * D_MODEL]
    hmix = (_rms_norm(x_ref[...], g_pre_mix_ref[...]) * (1.0 + sc1) + sh1).astype(bf16)
    q_ref[...] = jnp.dot(hmix, w_in_ref[:, 0:C_K], preferred_element_type=f32) * Q_SCALE
    kv_ref[...] = jnp.dot(hmix, w_in_ref[:, C_K:C_XR], preferred_element_type=f32)
    xr_ref[...] = jnp.dot(hmix, w_in_ref[:, C_XR:C_YR], preferred_element_type=f32)
    yr_ref[...] = jnp.dot(hmix, w_in_ref[:, C_YR:IN_COLS], preferred_element_type=f32)


def _sample_pre_call(x, mod_s, g_pre_mix, w_in):
    B = x.shape[0]
    return pl.pallas_call(
        _sample_pre_kernel,
        out_shape=(
            jax.ShapeDtypeStruct((B, ATTN_WIDTH), f32),
            jax.ShapeDtypeStruct((B, 2 * KV_WIDTH), f32),
            jax.ShapeDtypeStruct((B, D_RNN), f32),
            jax.ShapeDtypeStruct((B, D_RNN), f32),
        ),
        compiler_params=pltpu.CompilerParams(vmem_limit_bytes=VMEM_LIMIT_BYTES),
        name="sample_pre",
    )(x, mod_s, g_pre_mix, w_in)


def _sample_attn_kernel(q_ref, kv_ref, ck_ref, cv_ref, sinks_ref, o_ref, kwin_ref, vwin_ref):
    R = N_KV_HEADS * WINDOW
    hrow = lax.broadcasted_iota(jnp.int32, (N_HEADS, R), 0)
    rcol = lax.broadcasted_iota(jnp.int32, (N_HEADS, R), 1)
    slope = jnp.exp2(-8.0 * (hrow + 1).astype(f32) / N_HEADS)
    own = (rcol % N_KV_HEADS) == (hrow // GQA_GROUP)
    bias = jnp.where(own, slope * (WINDOW - 1 - rcol // N_KV_HEADS).astype(f32), jnp.inf)
    wrow = lax.broadcasted_iota(jnp.int32, (R, HEAD_DIM), 0)
    sink = sinks_ref[...]

    def shifted(cache, new_rows):
        out = pltpu.roll(cache, R - N_KV_HEADS, axis=0)
        for c in range(N_KV_HEADS):
            out = jnp.where(wrow == R - N_KV_HEADS + c, new_rows[c], out)
        return out

    for b in range(SAMPLE_CHUNK):
        knew = [kv_ref[b:b + 1, c * HEAD_DIM:(c + 1) * HEAD_DIM] for c in range(N_KV_HEADS)]
        vnew = [kv_ref[b:b + 1, KV_WIDTH + c * HEAD_DIM:KV_WIDTH + (c + 1) * HEAD_DIM]
                for c in range(N_KV_HEADS)]
        kw = shifted(ck_ref[b], knew)
        vw = shifted(cv_ref[b], vnew)
        kwin_ref[b] = kw
        vwin_ref[b] = vw
        s = _dot_nt(q_ref[b].astype(bf16), kw.astype(bf16)) - bias
        m = jnp.maximum(jnp.max(s, axis=-1, keepdims=True), sink)
        p = jnp.exp(s - m)
        denom = jnp.sum(p, axis=-1, keepdims=True) + jnp.exp(sink - m)
        o_ref[b] = jnp.dot(p.astype(bf16), vw.astype(bf16), preferred_element_type=f32) / denom


def _sample_attn_call(q, kv, ck, cv, sinks_col):
    B = q.shape[0]
    BC = SAMPLE_CHUNK
    return pl.pallas_call(
        _sample_attn_kernel,
        grid=(B // BC,),
        in_specs=[
            pl.BlockSpec((BC, N_HEADS, HEAD_DIM), lambda i: (i, 0, 0)),
            pl.BlockSpec((BC, 2 * KV_WIDTH), lambda i: (i, 0)),
            pl.BlockSpec((BC, N_KV_HEADS * WINDOW, HEAD_DIM), lambda i: (i, 0, 0)),
            pl.BlockSpec((BC, N_KV_HEADS * WINDOW, HEAD_DIM), lambda i: (i, 0, 0)),
            pl.BlockSpec((N_HEADS, 1), lambda i: (0, 0)),
        ],
        out_specs=(
            pl.BlockSpec((BC, N_HEADS, HEAD_DIM), lambda i: (i, 0, 0)),
            pl.BlockSpec((BC, N_KV_HEADS * WINDOW, HEAD_DIM), lambda i: (i, 0, 0)),
            pl.BlockSpec((BC, N_KV_HEADS * WINDOW, HEAD_DIM), lambda i: (i, 0, 0)),
        ),
        out_shape=(
            jax.ShapeDtypeStruct((B, N_HEADS, HEAD_DIM), f32),
            jax.ShapeDtypeStruct((B, N_KV_HEADS * WINDOW, HEAD_DIM), f32),
            jax.ShapeDtypeStruct((B, N_KV_HEADS * WINDOW, HEAD_DIM), f32),
        ),
        compiler_params=pltpu.CompilerParams(dimension_semantics=("arbitrary",)),
        name="sample_attn",
    )(q, kv, ck, cv, sinks_col)


def _sample_post_kernel(
    x_ref, mod_ref, attn_ref, xr_ref, yr_ref, h0_ref, cbuf_ref, fbuf_ref,
    conv_w_ref, conv_b_ref, wa_ref, ba_ref, wi_ref, bi_ref, lam_ref,
    g_attn_ref, g_rnn_ref, w_out_ref, g_post_mix_ref, g_pre_ffn_ref, w_up_ref,
    fconv_w_ref, fconv_b_ref, w_down_ref, g_post_ffn_ref,
    y_ref, h_ref, cst_ref, fst_ref,
):
    g1 = mod_ref[:, 2 * D_MODEL:3 * D_MODEL]
    sh2 = mod_ref[:, 3 * D_MODEL:4 * D_MODEL]
    sc2 = mod_ref[:, 4 * D_MODEL:5 * D_MODEL]
    g2 = mod_ref[:, 5 * D_MODEL:6 * D_MODEL]
    x = x_ref[...]
    xr = xr_ref[...]

    xc = conv_b_ref[...] + conv_w_ref[RNN_CONV_W - 1:RNN_CONV_W, :] * xr
    for jj in range(RNN_CONV_W - 1):
        xc = xc + conv_w_ref[jj:jj + 1, :] * cbuf_ref[:, jj, :]
    for jj in range(RNN_CONV_W - 2):
        cst_ref[:, jj, :] = cbuf_ref[:, jj + 1, :]
    cst_ref[:, RNN_CONV_W - 2, :] = xr

    sp = _softplus_neg(lam_ref[...])
    a_parts, u_parts = _rglru_gates(xc, wa_ref, ba_ref[...], wi_ref, bi_ref[...], sp)
    a = jnp.concatenate(a_parts, axis=1)
    u = jnp.concatenate(u_parts, axis=1)
    h = a * h0_ref[...] + u
    h_ref[...] = h
    rnn = h * _gelu_tanh(yr_ref[...])

    attn_n = _rms_norm(attn_ref[...], g_attn_ref[...])
    rnn_n = _rms_norm(rnn, g_rnn_ref[...])
    mo = (_dot(attn_n, w_out_ref[0:ATTN_WIDTH, :])
          + _dot(rnn_n, w_out_ref[ATTN_WIDTH:ATTN_WIDTH + D_RNN, :]))
    x1 = x + g1 * _rms_norm(mo, g_post_mix_ref[...])

    hff = (_rms_norm(x1, g_pre_ffn_ref[...]) * (1.0 + sc2) + sh2).astype(bf16)
    up_pre = jnp.dot(hff, w_up_ref[...], preferred_element_type=f32)
    up = fconv_b_ref[...] + fconv_w_ref[FFN_CONV_W - 1:FFN_CONV_W, :] * up_pre
    for jj in range(FFN_CONV_W - 1):
        up = up + fconv_w_ref[jj:jj + 1, :] * fbuf_ref[:, jj, :]
    for jj in range(FFN_CONV_W - 2):
        fst_ref[:, jj, :] = fbuf_ref[:, jj + 1, :]
    fst_ref[:, FFN_CONV_W - 2, :] = up_pre

    act = (_gelu_tanh(up[:, 0:D_FF]) * up[:, D_FF:2 * D_FF]).astype(bf16)
    f = jnp.dot(act, w_down_ref[...], preferred_element_type=f32)
    y_ref[...] = x1 + g2 * _rms_norm(f, g_post_ffn_ref[...])


def _sample_post_call(x, mod_s, attn, xr, yr, h0, cbuf, fbuf, params):
    B = x.shape[0]
    (_, _, conv_w, conv_b, wa, ba, wi, bi, lam, g_attn, g_rnn, w_out,
     g_post_mix, g_pre_ffn, w_up, fconv_w, fconv_b, w_down, g_post_ffn) = params
    return pl.pallas_call(
        _sample_post_kernel,
        out_shape=(
            jax.ShapeDtypeStruct((B, D_MODEL), f32),
            jax.ShapeDtypeStruct((B, D_RNN), f32),
            jax.ShapeDtypeStruct((B, RNN_CONV_W - 1, D_RNN), f32),
            jax.ShapeDtypeStruct((B, FFN_CONV_W - 1, 2 * D_FF), f32),
        ),
        compiler_params=pltpu.CompilerParams(vmem_limit_bytes=VMEM_LIMIT_BYTES),
        name="sample_post",
    )(x, mod_s, attn, xr, yr, h0, cbuf, fbuf, conv_w, conv_b, wa, ba, wi, bi, lam,
      g_attn, g_rnn, w_out, g_post_mix, g_pre_ffn, w_up, fconv_w, fconv_b, w_down, g_post_ffn)


def kernel(x_prompt, x_sample, cache_k, cache_v, state_h, state_conv, state_ffn_conv, c_prompt, c_sample, w_ada, b_ada, g_pre_mix, w_in, conv_w, conv_b, w_a, b_a, w_i, b_i, lam, sinks, g_attn_out, g_rnn_out, w_out, g_post_mix, g_pre_ffn, w_up, ffn_conv_w, ffn_conv_b, w_down, g_post_ffn):
    depth = w_in.shape[0]
    assert depth == 1 and x_prompt.shape[0] == 1 and x_sample.shape[1] == 1
    T = x_prompt.shape[1]
    B = x_sample.shape[0]
    W = cache_k.shape[2]
    assert W == WINDOW and T % TOKEN_BLOCK == 0 and B % SAMPLE_CHUNK == 0

    row = lambda a: a[0].reshape(1, -1)
    params = (
        row(g_pre_mix), w_in[0].astype(bf16), conv_w[0], row(conv_b),
        w_a[0].astype(bf16), row(b_a), w_i[0].astype(bf16), row(b_i), row(lam),
        row(g_attn_out), row(g_rnn_out), w_out[0].astype(bf16), row(g_post_mix), row(g_pre_ffn),
        w_up[0].astype(bf16), ffn_conv_w[0], row(ffn_conv_b), w_down[0].astype(bf16),
        row(g_post_ffn),
    )

    c_all = jnp.concatenate(
        [jnp.broadcast_to(c_prompt, (MOD_PAD_ROWS, D_MODEL)), c_sample], axis=0)
    mod = _mod_call(c_all, w_ada[0], b_ada[0].reshape(1, -1))
    mod_p = mod[0:MOD_PAD_ROWS]
    mod_s = mod[MOD_PAD_ROWS:]

    yp, kp, vp, hp, convp, ffnp = _prompt_call(x_prompt[0], mod_p, sinks[0], params)

    xs = x_sample[:, 0, :]
    q, kv, xr, yr = _sample_pre_call(xs, mod_s, params[0], params[1])
    ck = cache_k.reshape(B, W * N_KV_HEADS, HEAD_DIM)
    cv = cache_v.reshape(B, W * N_KV_HEADS, HEAD_DIM)
    attn3, kwin, vwin = _sample_attn_call(
        q.reshape(B, N_HEADS, HEAD_DIM), kv, ck, cv, sinks[0].reshape(N_HEADS, 1))
    attn = attn3.reshape(B, ATTN_WIDTH)
    ys, hs, convs, ffns = _sample_post_call(
        xs, mod_s, attn, xr, yr, state_h[0], state_conv[0], state_ffn_conv[0], params)

    kv_shape = (1, 1, W, N_KV_HEADS, HEAD_DIM)
    kvs_shape = (1, B, W, N_KV_HEADS, HEAD_DIM)
    return (
        yp[None], ys[:, None, :],
        kp.reshape(kv_shape), vp.reshape(kv_shape), hp[None], convp[None, None], ffnp[None, None],
        kwin.reshape(kvs_shape), vwin.reshape(kvs_shape), hs[None], convs[None], ffns[None],
    )
```

```python
import math

import jax
import jax.numpy as jnp
from jax import lax
from jax.experimental import pallas as pl
from jax.experimental.pallas import tpu as pltpu

D_MODEL = 1024
N_HEADS = 8
N_KV_HEADS = 2
HEAD_DIM = 128
GQA_GROUP = N_HEADS // N_KV_HEADS
ATTN_WIDTH = N_HEADS * HEAD_DIM
KV_WIDTH = N_KV_HEADS * HEAD_DIM
WINDOW = 128
D_RNN = D_MODEL
RNN_BLOCKS = 8
RNN_BLOCK_W = D_RNN // RNN_BLOCKS
RG_C = 8.0
RNN_CONV_W = 4
D_FF = 2816
FFN_CONV_W = 3
RMS_EPS = 1e-6

C_K = ATTN_WIDTH
C_V = C_K + KV_WIDTH
C_XR = C_V + KV_WIDTH
C_YR = C_XR + D_RNN
IN_COLS = C_YR + D_RNN

SUBLANES = 8
SUB_ROWS = WINDOW
GROUPS = SUB_ROWS // SUBLANES
TOKEN_BLOCK = 256
NSUB = TOKEN_BLOCK // SUB_ROWS
FFN_CHUNK = 256
STAGE_SKEW = 1
SAMPLE_CHUNK = 16
MOD_PAD_ROWS = 8
VMEM_LIMIT_BYTES = 56 * 1024 * 1024

ALIBI_SLOPES = tuple(2.0 ** (-8.0 * (h + 1) / N_HEADS) for h in range(N_HEADS))
Q_SCALE = HEAD_DIM ** -0.5
SQRT_2_OVER_PI = math.sqrt(2.0 / math.pi)
LOG2_E = 1.0 / math.log(2.0)
PADDED_OUT_COLS = D_MODEL + 128

bf16 = jnp.bfloat16
f32 = jnp.float32


def _rms_scale(x):
    return lax.rsqrt(jnp.mean(x * x, axis=-1, keepdims=True) + RMS_EPS)


def _rms_norm(x, g):
    return x * _rms_scale(x) * g


def _gelu_tanh(x):
    k1 = -2.0 * LOG2_E * SQRT_2_OVER_PI
    k3 = k1 * 0.044715
    return x / (1.0 + jnp.exp2(x * (k1 + k3 * (x * x))))


def _sigmoid(x):
    return 1.0 / (1.0 + jnp.exp2(x * (-LOG2_E)))


def _softplus_neg(lam):
    return jnp.maximum(-lam, 0.0) + jnp.log1p(jnp.exp(-jnp.abs(lam)))


def _dot(a, b):
    return jnp.dot(a.astype(bf16), b, preferred_element_type=f32)


def _dot_nt(a, b):
    return lax.dot_general(a, b, (((1,), (1,)), ((), ())), preferred_element_type=f32)


def _rglru_gates(xc, wa_ref, ba, wi_ref, bi, sp):
    a_parts, u_parts = [], []
    for n in range(RNN_BLOCKS):
        sl = slice(n * RNN_BLOCK_W, (n + 1) * RNN_BLOCK_W)
        xn = xc[:, sl]
        xb = xn.astype(bf16)
        r = _sigmoid(jnp.dot(xb, wa_ref[n], preferred_element_type=f32) + ba[:, sl])
        i = _sigmoid(jnp.dot(xb, wi_ref[n], preferred_element_type=f32) + bi[:, sl])
        log_a = (-RG_C) * r * sp[:, sl]
        t = jnp.tanh(log_a)
        one_minus_a2 = (-2.0 * t) / (1.0 - t)
        a_parts.append(jnp.exp(log_a))
        u_parts.append(jnp.sqrt(one_minus_a2) * (i * xn))
    return a_parts, u_parts


def _mod_kernel(c_ref, w_ref, b_ref, o_ref):
    c = c_ref[...]
    s = c * _sigmoid(c)
    o_ref[...] = _dot(s, w_ref[...].astype(bf16)) + b_ref[...]


def _mod_call(c_all, w_ada, b_ada):
    rows = c_all.shape[0]
    ncol = w_ada.shape[1]
    bn = D_MODEL
    return pl.pallas_call(
        _mod_kernel,
        grid=(ncol // bn,),
        in_specs=[
            pl.BlockSpec((rows, D_MODEL), lambda j: (0, 0)),
            pl.BlockSpec((D_MODEL, bn), lambda j: (0, j)),
            pl.BlockSpec((1, bn), lambda j: (0, j)),
        ],
        out_specs=pl.BlockSpec((rows, bn), lambda j: (0, j)),
        out_shape=jax.ShapeDtypeStruct((rows, ncol), f32),
        compiler_params=pltpu.CompilerParams(dimension_semantics=("arbitrary",)),
        name="adaln_mod",
    )(c_all, w_ada, b_ada)


def _groups(v, c0=None, c1=None):
    if c0 is None:
        return [v[j * SUBLANES:(j + 1) * SUBLANES, :] for j in range(GROUPS)]
    return [v[j * SUBLANES:(j + 1) * SUBLANES, c0:c1] for j in range(GROUPS)]


def _shifted_groups(X, tail_row, sub0, max_shift):
    wrapped = {}
    for j in range(GROUPS - max_shift, GROUPS):
        wrapped[j] = jnp.where(sub0, tail_row(j), pltpu.roll(X[j], 1, axis=0))
    sh = {}
    for d in range(1, max_shift + 1):
        sh[d] = [X[j - d] if j >= d else wrapped[j - d + GROUPS] for j in range(GROUPS)]
    return sh


def _scan_sub(a, u, h_in, sub_iota):
    A = _groups(a)
    L = _groups(u)
    for j in range(1, GROUPS):
        L[j] = A[j] * L[j - 1] + L[j]
        A[j] = A[j] * A[j - 1]
    ae, le = A[GROUPS - 1], L[GROUPS - 1]
    for s_ in (1, 2, 4):
        ok = sub_iota >= s_
        a_sh = pltpu.roll(ae, s_, axis=0)
        l_sh = pltpu.roll(le, s_, axis=0)
        le = jnp.where(ok, ae * l_sh + le, le)
        ae = jnp.where(ok, ae * a_sh, ae)
    hend = le + ae * h_in
    hprev = jnp.where(sub_iota == 0, h_in, pltpu.roll(hend, 1, axis=0))
    h = jnp.concatenate([L[j] + A[j] * hprev for j in range(GROUPS)], axis=0)
    h_out = jnp.broadcast_to(hend[SUBLANES - 1:SUBLANES, :], hend.shape)
    return h, h_out


def _prompt_kernel(
    x_hbm, mod_ref, sinks_ref,
    g_pre_mix_ref, w_in_ref, conv_w_ref, conv_b_ref, wa_ref, ba_ref, wi_ref, bi_ref, lam_ref,
    g_attn_ref, g_rnn_ref, w_out_ref, g_post_mix_ref, g_pre_ffn_ref, w_up_ref,
    fconv_w_ref, fconv_b_ref, w_down_ref, g_post_ffn_ref,
    y_hbm, kwin_hbm, vwin_hbm, hlast_ref, convst_ref, ffnst_ref,
    xbuf, ybuf, sem_in, sem_out, sem_st, kprev, vprev, kst, vst, xr_tail, up_tail, hcar, bias_s,
    hff_s,
):
    step = pl.program_id(0)
    nblocks = pl.num_programs(0) - 1
    slot = lax.rem(step, 2)
    yslot = lax.rem(step, 3)
    fslot = lax.rem(step + 2, 3)

    def x_copies(st, sl):
        return [pltpu.make_async_copy(
            x_hbm.at[st * NSUB + s, :, j, :],
            xbuf.at[sl, pl.ds(s * SUB_ROWS + j * SUBLANES, SUBLANES), :],
            sem_in.at[sl]) for s in range(NSUB) for j in range(GROUPS)]

    def y_copies(st, sl):
        return [pltpu.make_async_copy(
            ybuf.at[sl, pl.ds(s * SUB_ROWS + j * SUBLANES, SUBLANES), :],
            y_hbm.at[st * NSUB + s, :, j, :],
            sem_out.at[sl]) for s in range(NSUB) for j in range(GROUPS)]

    def state_copies():
        cps = []
        for src, dst in ((kst, kwin_hbm), (vst, vwin_hbm)):
            cps += [pltpu.make_async_copy(src.at[pl.ds(j * SUBLANES, SUBLANES), :],
                                          dst.at[:, j, :], sem_st) for j in range(GROUPS)]
        return cps

    @pl.when(step == 0)
    def _init():
        for cp in x_copies(0, 0):
            cp.start()
        kprev[...] = jnp.zeros(kprev.shape, bf16)
        vprev[...] = jnp.zeros(vprev.shape, bf16)
        xr_tail[...] = jnp.zeros(xr_tail.shape, f32)
        up_tail[...] = jnp.zeros(up_tail.shape, f32)
        hcar[...] = jnp.zeros(hcar.shape, f32)
        hff_s[...] = jnp.zeros(hff_s.shape, bf16)
        ybuf[2] = jnp.zeros(ybuf.shape[1:], f32)
        rq = lax.broadcasted_iota(jnp.int32, (SUB_ROWS, 2 * SUB_ROWS), 0)
        ck = lax.broadcasted_iota(jnp.int32, (SUB_ROWS, 2 * SUB_ROWS), 1)
        rk = ck & (SUB_ROWS - 1)
        tq = (rq % SUBLANES) * GROUPS + rq // SUBLANES
        tk = (rk % SUBLANES) * GROUPS + rk // SUBLANES
        dist = tq - tk + jnp.where(ck < SUB_ROWS, WINDOW, 0)
        base = jnp.where((dist >= 0) & (dist < WINDOW), dist.astype(f32), jnp.inf)
        for h in range(N_HEADS):
            bias_s[h] = ALIBI_SLOPES[h] * base

    @pl.when(step + 1 < nblocks)
    def _prefetch():
        for cp in x_copies(step + 1, 1 - slot):
            cp.start()

    @pl.when(step >= 3)
    def _free_ybuf():
        for cp in y_copies(0, yslot):
            cp.wait()

    @pl.when(step < nblocks)
    def _wait_x():
        for cp in x_copies(step, slot):
            cp.wait()

    sh1 = mod_ref[0:1, 0 * D_MODEL:1 * D_MODEL]
    sc1 = mod_ref[0:1, 1 * D_MODEL:2 * D_MODEL]
    g1 = mod_ref[0:1, 2 * D_MODEL:3 * D_MODEL]
    sh2 = mod_ref[0:1, 3 * D_MODEL:4 * D_MODEL]
    sc2 = mod_ref[0:1, 4 * D_MODEL:5 * D_MODEL]
    g2 = mod_ref[0:1, 5 * D_MODEL:6 * D_MODEL]
    gs1 = g_pre_mix_ref[...] * (1.0 + sc1)
    gs2 = g_pre_ffn_ref[...] * (1.0 + sc2)
    sp = _softplus_neg(lam_ref[...])
    first_pen = jnp.where(step == 0, jnp.inf, 0.0)
    sub_iota = lax.broadcasted_iota(jnp.int32, (SUBLANES, D_RNN), 0)
    sub0_rnn = sub_iota == 0
    sub0_ffn = lax.broadcasted_iota(jnp.int32, (SUBLANES, FFN_CHUNK), 0) == 0
    conv_w = [conv_w_ref.at[jj:jj + 1, :] for jj in range(RNN_CONV_W)]

    sts = [dict() for _ in range(NSUB)]

    def rows(s):
        return pl.ds(s * SUB_ROWS, SUB_ROWS)

    def stage_pre(s):
        x = xbuf[slot, rows(s), :]
        sts[s]["hmix"] = (x * _rms_scale(x) * gs1 + sh1).astype(bf16)

    def stage_inproj(s):
        st = sts[s]
        hm = st["hmix"]
        st["q"] = (jnp.dot(hm, w_in_ref[:, 0:C_K], preferred_element_type=f32) * Q_SCALE).astype(bf16)
        kv = jnp.dot(hm, w_in_ref[:, C_K:C_XR], preferred_element_type=f32)
        st["kb"] = kv[:, 0:KV_WIDTH].astype(bf16)
        st["vb"] = kv[:, KV_WIDTH:2 * KV_WIDTH].astype(bf16)
        st["kv"] = kv
        st["xr"] = jnp.dot(hm, w_in_ref[:, C_XR:C_YR], preferred_element_type=f32)

    def stage_attn(s):
        st = sts[s]
        if s == 0:
            kp, vp = kprev[...], vprev[...]
        else:
            kp, vp = sts[s - 1]["kb"], sts[s - 1]["vb"]
        kw = jnp.concatenate([kp, st["kb"]], axis=0)
        vw = jnp.concatenate([vp, st["vb"]], axis=0)
        outs = []
        for h in range(N_HEADS):
            c = h // GQA_GROUP
            hs = slice(h * HEAD_DIM, (h + 1) * HEAD_DIM)
            cs = slice(c * HEAD_DIM, (c + 1) * HEAD_DIM)
            sc = _dot_nt(st["q"][:, hs], kw[:, cs])
            if s == 0:
                sc = jnp.concatenate([sc[:, 0:SUB_ROWS] - first_pen, sc[:, SUB_ROWS:]], axis=1)
            sc = sc - bias_s[h]
            sink = sinks_ref[h]
            m = jnp.maximum(jnp.max(sc, axis=-1, keepdims=True), sink)
            p = jnp.exp(sc - m)
            denom = jnp.sum(p, axis=-1, keepdims=True) + jnp.exp(sink - m)
            outs.append(jnp.dot(p.astype(bf16), vw[:, cs], preferred_element_type=f32) / denom)
        attn = jnp.concatenate(outs, axis=1)
        st["attn_n"] = _rms_norm(attn, g_attn_ref[...]).astype(bf16)

    def stage_rnn(s):
        st = sts[s]
        xr = st["xr"]
        X = _groups(xr)
        if s == 0:
            tail = lambda j: xr_tail[(j - (GROUPS - RNN_CONV_W + 1)) * SUBLANES + SUBLANES - 1:
                                     (j - (GROUPS - RNN_CONV_W + 1)) * SUBLANES + SUBLANES, :]
        else:
            pxr = sts[s - 1]["xr"]
            tail = lambda j: pxr[j * SUBLANES + SUBLANES - 1:(j + 1) * SUBLANES, :]
        sh = _shifted_groups(X, tail, sub0_rnn, RNN_CONV_W - 1)
        xcs = []
        for j in range(GROUPS):
            acc = conv_b_ref[...] + conv_w[RNN_CONV_W - 1][...] * X[j]
            for d in range(1, RNN_CONV_W):
                acc = acc + conv_w[RNN_CONV_W - 1 - d][...] * sh[d][j]
            xcs.append(acc)
        xc = jnp.concatenate(xcs, axis=0)
        a_parts, u_parts = _rglru_gates(xc, wa_ref, ba_ref[...], wi_ref, bi_ref[...], sp)
        a = jnp.concatenate(a_parts, axis=1)
        u = jnp.concatenate(u_parts, axis=1)
        h_in = hcar[...] if s == 0 else sts[s - 1]["h_out"]
        h, st["h_out"] = _scan_sub(a, u, h_in, sub_iota)
        yr = jnp.dot(st["hmix"], w_in_ref[:, C_YR:IN_COLS], preferred_element_type=f32)
        rnn = h * _gelu_tanh(yr)
        st["rnn_n"] = _rms_norm(rnn, g_rnn_ref[...]).astype(bf16)

    def stage_out(s):
        st = sts[s]
        mo = (jnp.dot(st["attn_n"], w_out_ref[0:ATTN_WIDTH, 0:D_MODEL], preferred_element_type=f32)
              + jnp.dot(st["rnn_n"], w_out_ref[ATTN_WIDTH:ATTN_WIDTH + D_RNN, 0:D_MODEL],
                        preferred_element_type=f32))
        x1 = xbuf[slot, rows(s), :] + g1 * _rms_norm(mo, g_post_mix_ref[...])
        ybuf[yslot, rows(s), :] = x1
        hff_s[slot, rows(s), :] = (x1 * _rms_scale(x1) * gs2 + sh2).astype(bf16)

    tail_g0 = GROUPS - (FFN_CONV_W - 1)
    hff = hff_s[1 - slot]
    acts, tails = [], {}

    def conv_chunk(c0):
        c1 = c0 + FFN_CHUNK
        up = jnp.dot(hff, w_up_ref[:, c0:c1], preferred_element_type=f32)
        w = [fconv_w_ref.at[jj:jj + 1, c0:c1] for jj in range(FFN_CONV_W)]
        outs = []
        for s in range(NSUB):
            r0 = s * SUB_ROWS
            X = [up[r0 + j * SUBLANES:r0 + (j + 1) * SUBLANES, :] for j in range(GROUPS)]
            if s == 0:
                tail = lambda j: up_tail[(j - tail_g0) * SUBLANES + SUBLANES - 1:
                                         (j - tail_g0 + 1) * SUBLANES, c0:c1]
            else:
                tail = lambda j, p0=r0 - SUB_ROWS: up[p0 + j * SUBLANES + SUBLANES - 1:
                                                      p0 + (j + 1) * SUBLANES, :]
            sh = _shifted_groups(X, tail, sub0_ffn, FFN_CONV_W - 1)
            for j in range(GROUPS):
                acc = fconv_b_ref[:, c0:c1] + w[FFN_CONV_W - 1][...] * X[j]
                for d in range(1, FFN_CONV_W):
                    acc = acc + w[FFN_CONV_W - 1 - d][...] * sh[d][j]
                outs.append(acc)
        last0 = (NSUB - 1) * SUB_ROWS
        for t in range(FFN_CONV_W - 1):
            r = last0 + (tail_g0 + t) * SUBLANES + SUBLANES - 1
            ffnst_ref[t:t + 1, c0:c1] = up[r:r + 1, :]
        tails[c0] = up[last0 + tail_g0 * SUBLANES:, :]
        return jnp.concatenate(outs, axis=0)

    def ffn_chunk(cc):
        gate = conv_chunk(cc * FFN_CHUNK)
        val = conv_chunk(D_FF + cc * FFN_CHUNK)
        acts.append((_gelu_tanh(gate) * val).astype(bf16))

    def ffn_finish():
        for c0, tl in tails.items():
            up_tail[:, c0:c0 + FFN_CHUNK] = tl
        act = jnp.concatenate(acts, axis=1)
        f = jnp.dot(act, w_down_ref[:, 0:D_MODEL], preferred_element_type=f32)
        ybuf[fslot] = ybuf[fslot] + g2 * _rms_norm(f, g_post_ffn_ref[...])

    stages = (stage_pre, stage_inproj, stage_attn, stage_rnn, stage_out)
    n_times = len(stages) + (NSUB - 1) * STAGE_SKEW
    n_chunks = D_FF // FFN_CHUNK
    ffn_chunk(0)
    nxt = 1
    for t in range(n_times):
        for s in range(NSUB):
            k = t - s * STAGE_SKEW
            if 0 <= k < len(stages):
                stages[k](s)
                if nxt < n_chunks:
                    ffn_chunk(nxt)
                    nxt += 1
    while nxt < n_chunks:
        ffn_chunk(nxt)
        nxt += 1
    ffn_finish()

    lst = sts[NSUB - 1]
    kprev[...] = lst["kb"]
    vprev[...] = lst["vb"]
    xr_tail[...] = lst["xr"][(GROUPS - (RNN_CONV_W - 1)) * SUBLANES:, :]
    hcar[...] = lst["h_out"]

    @pl.when(step < nblocks)
    def _state():
        kst[...] = lst["kv"][:, 0:KV_WIDTH]
        vst[...] = lst["kv"][:, KV_WIDTH:2 * KV_WIDTH]
        for t in range(RNN_CONV_W - 1):
            r = (GROUPS - (RNN_CONV_W - 1) + t) * SUBLANES + SUBLANES - 1
            convst_ref[t:t + 1, :] = lst["xr"][r:r + 1, :]
        hlast_ref[...] = lst["h_out"][0:1, :]

    @pl.when(step >= 1)
    def _store_y():
        for cp in y_copies(step - 1, fslot):
            cp.start()

    @pl.when(step == nblocks)
    def _finish():
        for cp in state_copies():
            cp.start()
        for cp in state_copies():
            cp.wait()
        for cp in y_copies(0, lax.rem(step + 1, 3)) + y_copies(0, fslot):
            cp.wait()


def _const_spec(shape):
    nd = len(shape)
    return pl.BlockSpec(shape, lambda i: (0,) * nd)


def _prompt_call(x, mod_p, sinks, params):
    T = x.shape[0]
    TB = TOKEN_BLOCK
    assert T // TB >= 3
    (g_pre_mix, w_in, conv_w, conv_b, wa, ba, wi, bi, lam, g_attn, g_rnn, w_out,
     g_post_mix, g_pre_ffn, w_up, fconv_w, fconv_b, w_down, g_post_ffn) = params
    x4 = x.reshape(T // SUB_ROWS, SUBLANES, GROUPS, D_MODEL)
    ins = [x4, mod_p, sinks, g_pre_mix, w_in, conv_w, conv_b, wa, ba, wi, bi, lam, g_attn, g_rnn,
           w_out, g_post_mix, g_pre_ffn, w_up, fconv_w, fconv_b, w_down, g_post_ffn]
    in_specs = [pl.BlockSpec(memory_space=pl.ANY),
                _const_spec(mod_p.shape),
                pl.BlockSpec(memory_space=pltpu.SMEM)]
    in_specs += [_const_spec(a.shape) for a in ins[3:]]
    out_shape = (
        jax.ShapeDtypeStruct(x4.shape, f32),
        jax.ShapeDtypeStruct((SUBLANES, GROUPS, KV_WIDTH), f32),
        jax.ShapeDtypeStruct((SUBLANES, GROUPS, KV_WIDTH), f32),
        jax.ShapeDtypeStruct((1, D_RNN), f32),
        jax.ShapeDtypeStruct((RNN_CONV_W - 1, D_RNN), f32),
        jax.ShapeDtypeStruct((FFN_CONV_W - 1, 2 * D_FF), f32),
    )
    out_specs = (
        pl.BlockSpec(memory_space=pl.ANY),
        pl.BlockSpec(memory_space=pl.ANY),
        pl.BlockSpec(memory_space=pl.ANY),
        _const_spec((1, D_RNN)),
        _const_spec((RNN_CONV_W - 1, D_RNN)),
        _const_spec((FFN_CONV_W - 1, 2 * D_FF)),
    )
    scratch = [
        pltpu.VMEM((2, TB, D_MODEL), f32),
        pltpu.VMEM((3, TB, D_MODEL), f32),
        pltpu.SemaphoreType.DMA((2,)),
        pltpu.SemaphoreType.DMA((3,)),
        pltpu.SemaphoreType.DMA(()),
        pltpu.VMEM((SUB_ROWS, KV_WIDTH), bf16),
        pltpu.VMEM((SUB_ROWS, KV_WIDTH), bf16),
        pltpu.VMEM((SUB_ROWS, KV_WIDTH), f32),
        pltpu.VMEM((SUB_ROWS, KV_WIDTH), f32),
        pltpu.VMEM(((RNN_CONV_W - 1) * SUBLANES, D_RNN), f32),
        pltpu.VMEM(((FFN_CONV_W - 1) * SUBLANES, 2 * D_FF), f32),
        pltpu.VMEM((SUBLANES, D_RNN), f32),
        pltpu.VMEM((N_HEADS, SUB_ROWS, 2 * SUB_ROWS), f32),
        pltpu.VMEM((2, TOKEN_BLOCK, D_MODEL), bf16),
    ]
    y4, kp, vp, hp, convp, ffnp = pl.pallas_call(
        _prompt_kernel,
        grid=(T // TB + 1,),
        in_specs=in_specs,
        out_specs=out_specs,
        out_shape=out_shape,
        scratch_shapes=scratch,
        compiler_params=pltpu.CompilerParams(
            dimension_semantics=("arbitrary",), vmem_limit_bytes=VMEM_LIMIT_BYTES),
        name="prompt_layer",
    )(*ins)
    return (y4.reshape(T, D_MODEL), kp.reshape(WINDOW, KV_WIDTH), vp.reshape(WINDOW, KV_WIDTH),
            hp, convp, ffnp)


def _sample_pre_kernel(x_ref, mod_ref, g_pre_mix_ref, w_in_ref, q_ref, kv_ref, xr_ref, yr_ref):
    sh1 = mod_ref[:, 0 * D_MODEL:1 * D_MODEL]
    sc1 = mod_ref[:, 1 * D_MODEL:2 * D_MODEL]
    hmix = (_rms_norm(x_ref[...], g_pre_mix_ref[...]) * (1.0 + sc1) + sh1).astype(bf16)
    q_ref[...] = jnp.dot(hmix, w_in_ref[:, 0:C_K], preferred_element_type=f32) * Q_SCALE
    kv_ref[...] = jnp.dot(hmix, w_in_ref[:, C_K:C_XR], preferred_element_type=f32)
    xr_ref[...] = jnp.dot(hmix, w_in_ref[:, C_XR:C_YR], preferred_element_type=f32)
    yr_ref[...] = jnp.dot(hmix, w_in_ref[:, C_YR:IN_COLS], preferred_element_type=f32)


def _sample_pre_call(x, mod_s, g_pre_mix, w_in):
    B = x.shape[0]
    return pl.pallas_call(
        _sample_pre_kernel,
        out_shape=(
            jax.ShapeDtypeStruct((B, ATTN_WIDTH), f32),
            jax.ShapeDtypeStruct((B, 2 * KV_WIDTH), f32),
            jax.ShapeDtypeStruct((B, D_RNN), f32),
            jax.ShapeDtypeStruct((B, D_RNN), f32),
        ),
        compiler_params=pltpu.CompilerParams(vmem_limit_bytes=VMEM_LIMIT_BYTES),
        name="sample_pre",
    )(x, mod_s, g_pre_mix, w_in)


def _sample_attn_kernel(q_ref, kv_ref, ck_ref, cv_ref, sinks_ref, o_ref, kwin_ref, vwin_ref):
    R = N_KV_HEADS * WINDOW
    hrow = lax.broadcasted_iota(jnp.int32, (N_HEADS, R), 0)
    rcol = lax.broadcasted_iota(jnp.int32, (N_HEADS, R), 1)
    slope = jnp.exp2(-8.0 * (hrow + 1).astype(f32) / N_HEADS)
    own = (rcol % N_KV_HEADS) == (hrow // GQA_GROUP)
    bias = jnp.where(own, slope * (WINDOW - 1 - rcol // N_KV_HEADS).astype(f32), jnp.inf)
    wrow = lax.broadcasted_iota(jnp.int32, (R, HEAD_DIM), 0)
    sink = sinks_ref[...]

    def shifted(cache, new_rows):
        out = pltpu.roll(cache, R - N_KV_HEADS, axis=0)
        for c in range(N_KV_HEADS):
            out = jnp.where(wrow == R - N_KV_HEADS + c, new_rows[c], out)
        return out

    for b in range(SAMPLE_CHUNK):
        knew = [kv_ref[b:b + 1, c * HEAD_DIM:(c + 1) * HEAD_DIM] for c in range(N_KV_HEADS)]
        vnew = [kv_ref[b:b + 1, KV_WIDTH + c * HEAD_DIM:KV_WIDTH + (c + 1) * HEAD_DIM]
                for c in range(N_KV_HEADS)]
        kw = shifted(ck_ref[b], knew)
        vw = shifted(cv_ref[b], vnew)
        kwin_ref[b] = kw
        vwin_ref[b] = vw
        s = _dot_nt(q_ref[b].astype(bf16), kw.astype(bf16)) - bias
        m = jnp.maximum(jnp.max(s, axis=-1, keepdims=True), sink)
        p = jnp.exp(s - m)
        denom = jnp.sum(p, axis=-1, keepdims=True) + jnp.exp(sink - m)
        o_ref[b] = jnp.dot(p.astype(bf16), vw.astype(bf16), preferred_element_type=f32) / denom


def _sample_attn_call(q, kv, ck, cv, sinks_col):
    B = q.shape[0]
    BC = SAMPLE_CHUNK
    return pl.pallas_call(
        _sample_attn_kernel,
        grid=(B // BC,),
        in_specs=[
            pl.BlockSpec((BC, N_HEADS, HEAD_DIM), lambda i: (i, 0, 0)),
            pl.BlockSpec((BC, 2 * KV_WIDTH), lambda i: (i, 0)),
            pl.BlockSpec((BC, N_KV_HEADS * WINDOW, HEAD_DIM), lambda i: (i, 0, 0)),
            pl.BlockSpec((BC, N_KV_HEADS * WINDOW, HEAD_DIM), lambda i: (i, 0, 0)),
            pl.BlockSpec((N_HEADS, 1), lambda i: (0, 0)),
        ],
        out_specs=(
            pl.BlockSpec((BC, N_HEADS, HEAD_DIM), lambda i: (i, 0, 0)),
            pl.BlockSpec((BC, N_KV_HEADS * WINDOW, HEAD_DIM), lambda i: (i, 0, 0)),
            pl.BlockSpec((BC, N_KV_HEADS * WINDOW, HEAD_DIM), lambda i: (i, 0, 0)),
        ),
        out_shape=(
            jax.ShapeDtypeStruct((B, N_HEADS, HEAD_DIM), f32),
            jax.ShapeDtypeStruct((B, N_KV_HEADS * WINDOW, HEAD_DIM), f32),
            jax.ShapeDtypeStruct((B, N_KV_HEADS * WINDOW, HEAD_DIM), f32),
        ),
        compiler_params=pltpu.CompilerParams(dimension_semantics=("arbitrary",)),
        name="sample_attn",
    )(q, kv, ck, cv, sinks_col)


def _sample_post_kernel(
    x_ref, mod_ref, attn_ref, xr_ref, yr_ref, h0_ref, cbuf_ref, fbuf_ref,
    conv_w_ref, conv_b_ref, wa_ref, ba_ref, wi_ref, bi_ref, lam_ref,
    g_attn_ref, g_rnn_ref, w_out_ref, g_post_mix_ref, g_pre_ffn_ref, w_up_ref,
    fconv_w_ref, fconv_b_ref, w_down_ref, g_post_ffn_ref,
    y_ref, h_ref, cst_ref, fst_ref,
):
    g1 = mod_ref[:, 2 * D_MODEL:3 * D_MODEL]
    sh2 = mod_ref[:, 3 * D_MODEL:4 * D_MODEL]
    sc2 = mod_ref[:, 4 * D_MODEL:5 * D_MODEL]
    g2 = mod_ref[:, 5 * D_MODEL:6 * D_MODEL]
    x = x_ref[...]
    xr = xr_ref[...]

    xc = conv_b_ref[...] + conv_w_ref[RNN_CONV_W - 1:RNN_CONV_W, :] * xr
    for jj in range(RNN_CONV_W - 1):
        xc = xc + conv_w_ref[jj:jj + 1, :] * cbuf_ref[:, jj, :]
    for jj in range(RNN_CONV_W - 2):
        cst_ref[:, jj, :] = cbuf_ref[:, jj + 1, :]
    cst_ref[:, RNN_CONV_W - 2, :] = xr

    sp = _softplus_neg(lam_ref[...])
    a_parts, u_parts = _rglru_gates(xc, wa_ref, ba_ref[...], wi_ref, bi_ref[...], sp)
    a = jnp.concatenate(a_parts, axis=1)
    u = jnp.concatenate(u_parts, axis=1)
    h = a * h0_ref[...] + u
    h_ref[...] = h
    rnn = h * _gelu_tanh(yr_ref[...])

    attn_n = _rms_norm(attn_ref[...], g_attn_ref[...])
    rnn_n = _rms_norm(rnn, g_rnn_ref[...])
    mo = (_dot(attn_n, w_out_ref[0:ATTN_WIDTH, 0:D_MODEL])
          + _dot(rnn_n, w_out_ref[ATTN_WIDTH:ATTN_WIDTH + D_RNN, 0:D_MODEL]))
    x1 = x + g1 * _rms_norm(mo, g_post_mix_ref[...])

    hff = (_rms_norm(x1, g_pre_ffn_ref[...]) * (1.0 + sc2) + sh2).astype(bf16)
    up_pre = jnp.dot(hff, w_up_ref[...], preferred_element_type=f32)
    up = fconv_b_ref[...] + fconv_w_ref[FFN_CONV_W - 1:FFN_CONV_W, :] * up_pre
    for jj in range(FFN_CONV_W - 1):
        up = up + fconv_w_ref[jj:jj + 1, :] * fbuf_ref[:, jj, :]
    for jj in range(FFN_CONV_W - 2):
        fst_ref[:, jj, :] = fbuf_ref[:, jj + 1, :]
    fst_ref[:, FFN_CONV_W - 2, :] = up_pre

    act = (_gelu_tanh(up[:, 0:D_FF]) * up[:, D_FF:2 * D_FF]).astype(bf16)
    f = jnp.dot(act, w_down_ref[:, 0:D_MODEL], preferred_element_type=f32)
    y_ref[...] = x1 + g2 * _rms_norm(f, g_post_ffn_ref[...])


def _sample_post_call(x, mod_s, attn, xr, yr, h0, cbuf, fbuf, params):
    B = x.shape[0]
    (_, _, conv_w, conv_b, wa, ba, wi, bi, lam, g_attn, g_rnn, w_out,
     g_post_mix, g_pre_ffn, w_up, fconv_w, fconv_b, w_down, g_post_ffn) = params
    return pl.pallas_call(
        _sample_post_kernel,
        out_shape=(
            jax.ShapeDtypeStruct((B, D_MODEL), f32),
            jax.ShapeDtypeStruct((B, D_RNN), f32),
            jax.ShapeDtypeStruct((B, RNN_CONV_W - 1, D_RNN), f32),
            jax.ShapeDtypeStruct((B, FFN_CONV_W - 1, 2 * D_FF), f32),
        ),
        compiler_params=pltpu.CompilerParams(vmem_limit_bytes=VMEM_LIMIT_BYTES),
        name="sample_post",
    )(x, mod_s, attn, xr, yr, h0, cbuf, fbuf, conv_w, conv_b, wa, ba, wi, bi, lam,
      g_attn, g_rnn, w_out, g_post_mix, g_pre_ffn, w_up, fconv_w, fconv_b, w_down, g_post_ffn)


def kernel(x_prompt, x_sample, cache_k, cache_v, state_h, state_conv, state_ffn_conv, c_prompt, c_sample, w_ada, b_ada, g_pre_mix, w_in, conv_w, conv_b, w_a, b_a, w_i, b_i, lam, sinks, g_attn_out, g_rnn_out, w_out, g_post_mix, g_pre_ffn, w_up, ffn_conv_w, ffn_conv_b, w_down, g_post_ffn):
    depth = w_in.shape[0]
    assert depth == 1 and x_prompt.shape[0] == 1 and x_sample.shape[1] == 1
    T = x_prompt.shape[1]
    B = x_sample.shape[0]
    W = cache_k.shape[2]
    assert W == WINDOW and T % TOKEN_BLOCK == 0 and B % SAMPLE_CHUNK == 0

    row = lambda a: a[0].reshape(1, -1)
    pad_cols = lambda w: jnp.pad(w, ((0, 0), (0, PADDED_OUT_COLS - w.shape[1])))
    params = (
        row(g_pre_mix), w_in[0].astype(bf16), conv_w[0], row(conv_b),
        w_a[0].astype(bf16), row(b_a), w_i[0].astype(bf16), row(b_i), row(lam),
        row(g_attn_out), row(g_rnn_out), pad_cols(w_out[0].astype(bf16)), row(g_post_mix),
        row(g_pre_ffn),
        w_up[0].astype(bf16), ffn_conv_w[0], row(ffn_conv_b), pad_cols(w_down[0].astype(bf16)),
        row(g_post_ffn),
    )

    c_all = jnp.concatenate(
        [jnp.broadcast_to(c_prompt, (MOD_PAD_ROWS, D_MODEL)), c_sample], axis=0)
    mod = _mod_call(c_all, w_ada[0], b_ada[0].reshape(1, -1))
    mod_p = mod[0:MOD_PAD_ROWS]
    mod_s = mod[MOD_PAD_ROWS:]

    yp, kp, vp, hp, convp, ffnp = _prompt_call(x_prompt[0], mod_p, sinks[0], params)

    xs = x_sample[:, 0, :]
    q, kv, xr, yr = _sample_pre_call(xs, mod_s, params[0], params[1])
    ck = cache_k.reshape(B, W * N_KV_HEADS, HEAD_DIM)
    cv = cache_v.reshape(B, W * N_KV_HEADS, HEAD_DIM)
    attn3, kwin, vwin = _sample_attn_call(
        q.reshape(B, N_HEADS, HEAD_DIM), kv, ck, cv, sinks[0].reshape(N_HEADS, 1))
    attn = attn3.reshape(B, ATTN_WIDTH)
    ys, hs, convs, ffns = _sample_post_call(
        xs, mod_s, attn, xr, yr, state_h[0], state_conv[0], state_ffn_conv[0], params)

    kv_shape = (1, 1, W, N_KV_HEADS, HEAD_DIM)
    kvs_shape = (1, B, W, N_KV_HEADS, HEAD_DIM)
    return (
        yp[None], ys[:, None, :],
        kp.reshape(kv_shape), vp.reshape(kv_shape), hp[None], convp[None, None], ffnp[None, None],
        kwin.reshape(kvs_shape), vwin.reshape(kvs_shape), hs[None], convs[None], ffns[None],
    )
```

```python
import math

import jax
import jax.numpy as jnp
from jax import lax
from jax.experimental import pallas as pl
from jax.experimental.pallas import tpu as pltpu

D_MODEL = 1024
N_HEADS = 8
N_KV_HEADS = 2
HEAD_DIM = 128
GQA_GROUP = N_HEADS // N_KV_HEADS
ATTN_WIDTH = N_HEADS * HEAD_DIM
KV_WIDTH = N_KV_HEADS * HEAD_DIM
WINDOW = 128
D_RNN = D_MODEL
RNN_BLOCKS = 8
RNN_BLOCK_W = D_RNN // RNN_BLOCKS
RG_C = 8.0
RNN_CONV_W = 4
D_FF = 2816
FFN_CONV_W = 3
RMS_EPS = 1e-6

C_K = ATTN_WIDTH
C_V = C_K + KV_WIDTH
C_XR = C_V + KV_WIDTH
C_YR = C_XR + D_RNN
IN_COLS = C_YR + D_RNN

SUBLANES = 8
SUB_ROWS = WINDOW
GROUPS = SUB_ROWS // SUBLANES
TOKEN_BLOCK = 256
NSUB = TOKEN_BLOCK // SUB_ROWS
FFN_CHUNK = 256
STAGE_SKEW = 1
SAMPLE_CHUNK = 16
SAMPLE_ATTN_VMEM_BYTES = (2 * 2 * 2 * SAMPLE_CHUNK * N_KV_HEADS * WINDOW * HEAD_DIM * 4
                          + 8 * 1024 * 1024)
MOD_PAD_ROWS = 8
VMEM_LIMIT_BYTES = 56 * 1024 * 1024

ALIBI_SLOPES = tuple(2.0 ** (-8.0 * (h + 1) / N_HEADS) for h in range(N_HEADS))
Q_SCALE = HEAD_DIM ** -0.5
SQRT_2_OVER_PI = math.sqrt(2.0 / math.pi)
LOG2_E = 1.0 / math.log(2.0)
PADDED_OUT_COLS = D_MODEL + 128

bf16 = jnp.bfloat16
f32 = jnp.float32


def _rms_scale(x):
    return lax.rsqrt(jnp.mean(x * x, axis=-1, keepdims=True) + RMS_EPS)


def _rms_norm(x, g):
    return x * _rms_scale(x) * g


def _gelu_tanh(x):
    k1 = -2.0 * LOG2_E * SQRT_2_OVER_PI
    k3 = k1 * 0.044715
    return x / (1.0 + jnp.exp2(x * (k1 + k3 * (x * x))))


def _sigmoid(x):
    return 1.0 / (1.0 + jnp.exp2(x * (-LOG2_E)))


def _softplus_neg(lam):
    return jnp.maximum(-lam, 0.0) + jnp.log1p(jnp.exp(-jnp.abs(lam)))


def _dot(a, b):
    return jnp.dot(a.astype(bf16), b, preferred_element_type=f32)


def _dot_nt(a, b):
    return lax.dot_general(a, b, (((1,), (1,)), ((), ())), preferred_element_type=f32)


def _rglru_gates(xc, wa_ref, ba, wi_ref, bi, sp):
    a_parts, u_parts = [], []
    for n in range(RNN_BLOCKS):
        sl = slice(n * RNN_BLOCK_W, (n + 1) * RNN_BLOCK_W)
        xn = xc[:, sl]
        xb = xn.astype(bf16)
        r = _sigmoid(jnp.dot(xb, wa_ref[n], preferred_element_type=f32) + ba[:, sl])
        i = _sigmoid(jnp.dot(xb, wi_ref[n], preferred_element_type=f32) + bi[:, sl])
        log_a = (-RG_C) * r * sp[:, sl]
        t = jnp.tanh(log_a)
        one_minus_a2 = (-2.0 * t) / (1.0 - t)
        a_parts.append(jnp.exp(log_a))
        u_parts.append(jnp.sqrt(one_minus_a2) * (i * xn))
    return a_parts, u_parts


def _mod_kernel(c_ref, w_ref, b_ref, o_ref):
    c = c_ref[...]
    s = c * _sigmoid(c)
    o_ref[...] = _dot(s, w_ref[...].astype(bf16)) + b_ref[...]


def _mod_call(c_all, w_ada, b_ada):
    rows = c_all.shape[0]
    ncol = w_ada.shape[1]
    bn = D_MODEL
    return pl.pallas_call(
        _mod_kernel,
        grid=(ncol // bn,),
        in_specs=[
            pl.BlockSpec((rows, D_MODEL), lambda j: (0, 0)),
            pl.BlockSpec((D_MODEL, bn), lambda j: (0, j)),
            pl.BlockSpec((1, bn), lambda j: (0, j)),
        ],
        out_specs=pl.BlockSpec((rows, bn), lambda j: (0, j)),
        out_shape=jax.ShapeDtypeStruct((rows, ncol), f32),
        compiler_params=pltpu.CompilerParams(dimension_semantics=("arbitrary",)),
        name="adaln_mod",
    )(c_all, w_ada, b_ada)


def _groups(v, c0=None, c1=None):
    if c0 is None:
        return [v[j * SUBLANES:(j + 1) * SUBLANES, :] for j in range(GROUPS)]
    return [v[j * SUBLANES:(j + 1) * SUBLANES, c0:c1] for j in range(GROUPS)]


def _shifted_groups(X, tail_row, sub0, max_shift):
    wrapped = {}
    for j in range(GROUPS - max_shift, GROUPS):
        wrapped[j] = jnp.where(sub0, tail_row(j), pltpu.roll(X[j], 1, axis=0))
    sh = {}
    for d in range(1, max_shift + 1):
        sh[d] = [X[j - d] if j >= d else wrapped[j - d + GROUPS] for j in range(GROUPS)]
    return sh


def _scan_sub(a, u, h_in, sub_iota):
    A = _groups(a)
    L = _groups(u)
    for j in range(1, GROUPS):
        L[j] = A[j] * L[j - 1] + L[j]
        A[j] = A[j] * A[j - 1]
    ae, le = A[GROUPS - 1], L[GROUPS - 1]
    for s_ in (1, 2, 4):
        ok = sub_iota >= s_
        a_sh = pltpu.roll(ae, s_, axis=0)
        l_sh = pltpu.roll(le, s_, axis=0)
        le = jnp.where(ok, ae * l_sh + le, le)
        ae = jnp.where(ok, ae * a_sh, ae)
    hend = le + ae * h_in
    hprev = jnp.where(sub_iota == 0, h_in, pltpu.roll(hend, 1, axis=0))
    h = jnp.concatenate([L[j] + A[j] * hprev for j in range(GROUPS)], axis=0)
    h_out = jnp.broadcast_to(hend[SUBLANES - 1:SUBLANES, :], hend.shape)
    return h, h_out


def _prompt_kernel(
    x_hbm, mod_ref, sinks_ref,
    g_pre_mix_ref, w_in_ref, conv_w_ref, conv_b_ref, wa_ref, ba_ref, wi_ref, bi_ref, lam_ref,
    g_attn_ref, g_rnn_ref, w_out_ref, g_post_mix_ref, g_pre_ffn_ref, w_up_ref,
    fconv_w_ref, fconv_b_ref, w_down_ref, g_post_ffn_ref,
    y_hbm, kwin_hbm, vwin_hbm, hlast_ref, convst_ref, ffnst_ref,
    xbuf, ybuf, sem_in, sem_out, sem_st, kprev, vprev, kst, vst, xr_tail, up_tail, hcar, bias_s,
    hff_s,
):
    step = pl.program_id(0)
    nblocks = pl.num_programs(0) - 1
    slot = lax.rem(step, 2)
    yslot = lax.rem(step, 3)
    fslot = lax.rem(step + 2, 3)

    def x_copies(st, sl):
        return [pltpu.make_async_copy(
            x_hbm.at[st * NSUB + s, :, j, :],
            xbuf.at[sl, pl.ds(s * SUB_ROWS + j * SUBLANES, SUBLANES), :],
            sem_in.at[sl]) for s in range(NSUB) for j in range(GROUPS)]

    def y_copies(st, sl):
        return [pltpu.make_async_copy(
            ybuf.at[sl, pl.ds(s * SUB_ROWS + j * SUBLANES, SUBLANES), :],
            y_hbm.at[st * NSUB + s, :, j, :],
            sem_out.at[sl]) for s in range(NSUB) for j in range(GROUPS)]

    def state_copies():
        cps = []
        for src, dst in ((kst, kwin_hbm), (vst, vwin_hbm)):
            cps += [pltpu.make_async_copy(src.at[pl.ds(j * SUBLANES, SUBLANES), :],
                                          dst.at[:, j, :], sem_st) for j in range(GROUPS)]
        return cps

    @pl.when(step == 0)
    def _init():
        for cp in x_copies(0, 0):
            cp.start()
        kprev[...] = jnp.zeros(kprev.shape, bf16)
        vprev[...] = jnp.zeros(vprev.shape, bf16)
        xr_tail[...] = jnp.zeros(xr_tail.shape, f32)
        up_tail[...] = jnp.zeros(up_tail.shape, f32)
        hcar[...] = jnp.zeros(hcar.shape, f32)
        hff_s[...] = jnp.zeros(hff_s.shape, bf16)
        ybuf[2] = jnp.zeros(ybuf.shape[1:], f32)
        rq = lax.broadcasted_iota(jnp.int32, (SUB_ROWS, 2 * SUB_ROWS), 0)
        ck = lax.broadcasted_iota(jnp.int32, (SUB_ROWS, 2 * SUB_ROWS), 1)
        rk = ck & (SUB_ROWS - 1)
        tq = (rq % SUBLANES) * GROUPS + rq // SUBLANES
        tk = (rk % SUBLANES) * GROUPS + rk // SUBLANES
        dist = tq - tk + jnp.where(ck < SUB_ROWS, WINDOW, 0)
        base = jnp.where((dist >= 0) & (dist < WINDOW), dist.astype(f32), jnp.inf)
        for h in range(N_HEADS):
            bias_s[h] = ALIBI_SLOPES[h] * base

    @pl.when(step + 1 < nblocks)
    def _prefetch():
        for cp in x_copies(step + 1, 1 - slot):
            cp.start()

    @pl.when(step >= 3)
    def _free_ybuf():
        for cp in y_copies(0, yslot):
            cp.wait()

    @pl.when(step < nblocks)
    def _wait_x():
        for cp in x_copies(step, slot):
            cp.wait()

    sh1 = mod_ref[0:1, 0 * D_MODEL:1 * D_MODEL]
    sc1 = mod_ref[0:1, 1 * D_MODEL:2 * D_MODEL]
    g1 = mod_ref[0:1, 2 * D_MODEL:3 * D_MODEL]
    sh2 = mod_ref[0:1, 3 * D_MODEL:4 * D_MODEL]
    sc2 = mod_ref[0:1, 4 * D_MODEL:5 * D_MODEL]
    g2 = mod_ref[0:1, 5 * D_MODEL:6 * D_MODEL]
    gs1 = g_pre_mix_ref[...] * (1.0 + sc1)
    gs2 = g_pre_ffn_ref[...] * (1.0 + sc2)
    sp = _softplus_neg(lam_ref[...])
    first_pen = jnp.where(step == 0, jnp.inf, 0.0)
    sub_iota = lax.broadcasted_iota(jnp.int32, (SUBLANES, D_RNN), 0)
    sub0_rnn = sub_iota == 0
    sub0_ffn = lax.broadcasted_iota(jnp.int32, (SUBLANES, FFN_CHUNK), 0) == 0
    conv_w = [conv_w_ref.at[jj:jj + 1, :] for jj in range(RNN_CONV_W)]

    sts = [dict() for _ in range(NSUB)]

    def rows(s):
        return pl.ds(s * SUB_ROWS, SUB_ROWS)

    def stage_pre(s):
        x = xbuf[slot, rows(s), :]
        sts[s]["hmix"] = (x * _rms_scale(x) * gs1 + sh1).astype(bf16)

    def stage_inproj(s):
        st = sts[s]
        hm = st["hmix"]
        st["q"] = (jnp.dot(hm, w_in_ref[:, 0:C_K], preferred_element_type=f32) * Q_SCALE).astype(bf16)
        kv = jnp.dot(hm, w_in_ref[:, C_K:C_XR], preferred_element_type=f32)
        st["kb"] = kv[:, 0:KV_WIDTH].astype(bf16)
        st["vb"] = kv[:, KV_WIDTH:2 * KV_WIDTH].astype(bf16)
        st["kv"] = kv
        st["xr"] = jnp.dot(hm, w_in_ref[:, C_XR:C_YR], preferred_element_type=f32)

    def stage_attn(s):
        st = sts[s]
        if s == 0:
            kp, vp = kprev[...], vprev[...]
        else:
            kp, vp = sts[s - 1]["kb"], sts[s - 1]["vb"]
        kw = jnp.concatenate([kp, st["kb"]], axis=0)
        vw = jnp.concatenate([vp, st["vb"]], axis=0)
        outs = []
        for h in range(N_HEADS):
            c = h // GQA_GROUP
            hs = slice(h * HEAD_DIM, (h + 1) * HEAD_DIM)
            cs = slice(c * HEAD_DIM, (c + 1) * HEAD_DIM)
            sc = _dot_nt(st["q"][:, hs], kw[:, cs])
            if s == 0:
                sc = jnp.concatenate([sc[:, 0:SUB_ROWS] - first_pen, sc[:, SUB_ROWS:]], axis=1)
            sc = sc - bias_s[h]
            sink = sinks_ref[h]
            m = jnp.maximum(jnp.max(sc, axis=-1, keepdims=True), sink)
            p = jnp.exp(sc - m)
            denom = jnp.sum(p, axis=-1, keepdims=True) + jnp.exp(sink - m)
            outs.append(jnp.dot(p.astype(bf16), vw[:, cs], preferred_element_type=f32) / denom)
        attn = jnp.concatenate(outs, axis=1)
        st["attn_n"] = _rms_norm(attn, g_attn_ref[...]).astype(bf16)

    def stage_rnn(s):
        st = sts[s]
        xr = st["xr"]
        X = _groups(xr)
        if s == 0:
            tail = lambda j: xr_tail[(j - (GROUPS - RNN_CONV_W + 1)) * SUBLANES + SUBLANES - 1:
                                     (j - (GROUPS - RNN_CONV_W + 1)) * SUBLANES + SUBLANES, :]
        else:
            pxr = sts[s - 1]["xr"]
            tail = lambda j: pxr[j * SUBLANES + SUBLANES - 1:(j + 1) * SUBLANES, :]
        sh = _shifted_groups(X, tail, sub0_rnn, RNN_CONV_W - 1)
        xcs = []
        for j in range(GROUPS):
            acc = conv_b_ref[...] + conv_w[RNN_CONV_W - 1][...] * X[j]
            for d in range(1, RNN_CONV_W):
                acc = acc + conv_w[RNN_CONV_W - 1 - d][...] * sh[d][j]
            xcs.append(acc)
        xc = jnp.concatenate(xcs, axis=0)
        a_parts, u_parts = _rglru_gates(xc, wa_ref, ba_ref[...], wi_ref, bi_ref[...], sp)
        a = jnp.concatenate(a_parts, axis=1)
        u = jnp.concatenate(u_parts, axis=1)
        h_in = hcar[...] if s == 0 else sts[s - 1]["h_out"]
        h, st["h_out"] = _scan_sub(a, u, h_in, sub_iota)
        yr = jnp.dot(st["hmix"], w_in_ref[:, C_YR:IN_COLS], preferred_element_type=f32)
        rnn = h * _gelu_tanh(yr)
        st["rnn_n"] = _rms_norm(rnn, g_rnn_ref[...]).astype(bf16)

    def stage_out(s):
        st = sts[s]
        mo = (jnp.dot(st["attn_n"], w_out_ref[0:ATTN_WIDTH, 0:D_MODEL], preferred_element_type=f32)
              + jnp.dot(st["rnn_n"], w_out_ref[ATTN_WIDTH:ATTN_WIDTH + D_RNN, 0:D_MODEL],
                        preferred_element_type=f32))
        x1 = xbuf[slot, rows(s), :] + g1 * _rms_norm(mo, g_post_mix_ref[...])
        ybuf[yslot, rows(s), :] = x1
        hff_s[slot, rows(s), :] = (x1 * _rms_scale(x1) * gs2 + sh2).astype(bf16)

    tail_g0 = GROUPS - (FFN_CONV_W - 1)
    hff = hff_s[1 - slot]
    acts, tails = [], {}

    def conv_chunk(c0):
        c1 = c0 + FFN_CHUNK
        up = jnp.dot(hff, w_up_ref[:, c0:c1], preferred_element_type=f32)
        w = [fconv_w_ref.at[jj:jj + 1, c0:c1] for jj in range(FFN_CONV_W)]
        outs = []
        for s in range(NSUB):
            r0 = s * SUB_ROWS
            X = [up[r0 + j * SUBLANES:r0 + (j + 1) * SUBLANES, :] for j in range(GROUPS)]
            if s == 0:
                tail = lambda j: up_tail[(j - tail_g0) * SUBLANES + SUBLANES - 1:
                                         (j - tail_g0 + 1) * SUBLANES, c0:c1]
            else:
                tail = lambda j, p0=r0 - SUB_ROWS: up[p0 + j * SUBLANES + SUBLANES - 1:
                                                      p0 + (j + 1) * SUBLANES, :]
            sh = _shifted_groups(X, tail, sub0_ffn, FFN_CONV_W - 1)
            for j in range(GROUPS):
                acc = fconv_b_ref[:, c0:c1] + w[FFN_CONV_W - 1][...] * X[j]
                for d in range(1, FFN_CONV_W):
                    acc = acc + w[FFN_CONV_W - 1 - d][...] * sh[d][j]
                outs.append(acc)
        last0 = (NSUB - 1) * SUB_ROWS
        for t in range(FFN_CONV_W - 1):
            r = last0 + (tail_g0 + t) * SUBLANES + SUBLANES - 1
            ffnst_ref[t:t + 1, c0:c1] = up[r:r + 1, :]
        tails[c0] = up[last0 + tail_g0 * SUBLANES:, :]
        return jnp.concatenate(outs, axis=0)

    def ffn_chunk(cc):
        gate = conv_chunk(cc * FFN_CHUNK)
        val = conv_chunk(D_FF + cc * FFN_CHUNK)
        acts.append((_gelu_tanh(gate) * val).astype(bf16))

    def ffn_finish():
        for c0, tl in tails.items():
            up_tail[:, c0:c0 + FFN_CHUNK] = tl
        act = jnp.concatenate(acts, axis=1)
        f = jnp.dot(act, w_down_ref[:, 0:D_MODEL], preferred_element_type=f32)
        ybuf[fslot] = ybuf[fslot] + g2 * _rms_norm(f, g_post_ffn_ref[...])

    stages = (stage_pre, stage_inproj, stage_attn, stage_rnn, stage_out)
    n_times = len(stages) + (NSUB - 1) * STAGE_SKEW
    n_chunks = D_FF // FFN_CHUNK
    ffn_chunk(0)
    nxt = 1
    for t in range(n_times):
        for s in range(NSUB):
            k = t - s * STAGE_SKEW
            if 0 <= k < len(stages):
                stages[k](s)
                if nxt < n_chunks:
                    ffn_chunk(nxt)
                    nxt += 1
    while nxt < n_chunks:
        ffn_chunk(nxt)
        nxt += 1
    ffn_finish()

    lst = sts[NSUB - 1]
    kprev[...] = lst["kb"]
    vprev[...] = lst["vb"]
    xr_tail[...] = lst["xr"][(GROUPS - (RNN_CONV_W - 1)) * SUBLANES:, :]
    hcar[...] = lst["h_out"]

    @pl.when(step < nblocks)
    def _state():
        kst[...] = lst["kv"][:, 0:KV_WIDTH]
        vst[...] = lst["kv"][:, KV_WIDTH:2 * KV_WIDTH]
        for t in range(RNN_CONV_W - 1):
            r = (GROUPS - (RNN_CONV_W - 1) + t) * SUBLANES + SUBLANES - 1
            convst_ref[t:t + 1, :] = lst["xr"][r:r + 1, :]
        hlast_ref[...] = lst["h_out"][0:1, :]

    @pl.when(step >= 1)
    def _store_y():
        for cp in y_copies(step - 1, fslot):
            cp.start()

    @pl.when(step == nblocks)
    def _finish():
        for cp in state_copies():
            cp.start()
        for cp in state_copies():
            cp.wait()
        for cp in y_copies(0, lax.rem(step + 1, 3)) + y_copies(0, fslot):
            cp.wait()


def _const_spec(shape):
    nd = len(shape)
    return pl.BlockSpec(shape, lambda i: (0,) * nd)


def _prompt_call(x, mod, mod_row0, sinks, params):
    T = x.shape[0]
    TB = TOKEN_BLOCK
    assert T // TB >= 3
    (g_pre_mix, w_in, conv_w, conv_b, wa, ba, wi, bi, lam, g_attn, g_rnn, w_out,
     g_post_mix, g_pre_ffn, w_up, fconv_w, fconv_b, w_down, g_post_ffn) = params
    x4 = x.reshape(T // SUB_ROWS, SUBLANES, GROUPS, D_MODEL)
    ins = [x4, mod, sinks, g_pre_mix, w_in, conv_w, conv_b, wa, ba, wi, bi, lam, g_attn, g_rnn,
           w_out, g_post_mix, g_pre_ffn, w_up, fconv_w, fconv_b, w_down, g_post_ffn]
    assert mod_row0 % MOD_PAD_ROWS == 0
    in_specs = [pl.BlockSpec(memory_space=pl.ANY),
                pl.BlockSpec((MOD_PAD_ROWS, mod.shape[1]), lambda i: (mod_row0 // MOD_PAD_ROWS, 0)),
                pl.BlockSpec(memory_space=pltpu.SMEM)]
    in_specs += [_const_spec(a.shape) for a in ins[3:]]
    out_shape = (
        jax.ShapeDtypeStruct(x4.shape, f32),
        jax.ShapeDtypeStruct((SUBLANES, GROUPS, KV_WIDTH), f32),
        jax.ShapeDtypeStruct((SUBLANES, GROUPS, KV_WIDTH), f32),
        jax.ShapeDtypeStruct((1, D_RNN), f32),
        jax.ShapeDtypeStruct((RNN_CONV_W - 1, D_RNN), f32),
        jax.ShapeDtypeStruct((FFN_CONV_W - 1, 2 * D_FF), f32),
    )
    out_specs = (
        pl.BlockSpec(memory_space=pl.ANY),
        pl.BlockSpec(memory_space=pl.ANY),
        pl.BlockSpec(memory_space=pl.ANY),
        _const_spec((1, D_RNN)),
        _const_spec((RNN_CONV_W - 1, D_RNN)),
        _const_spec((FFN_CONV_W - 1, 2 * D_FF)),
    )
    scratch = [
        pltpu.VMEM((2, TB, D_MODEL), f32),
        pltpu.VMEM((3, TB, D_MODEL), f32),
        pltpu.SemaphoreType.DMA((2,)),
        pltpu.SemaphoreType.DMA((3,)),
        pltpu.SemaphoreType.DMA(()),
        pltpu.VMEM((SUB_ROWS, KV_WIDTH), bf16),
        pltpu.VMEM((SUB_ROWS, KV_WIDTH), bf16),
        pltpu.VMEM((SUB_ROWS, KV_WIDTH), f32),
        pltpu.VMEM((SUB_ROWS, KV_WIDTH), f32),
        pltpu.VMEM(((RNN_CONV_W - 1) * SUBLANES, D_RNN), f32),
        pltpu.VMEM(((FFN_CONV_W - 1) * SUBLANES, 2 * D_FF), f32),
        pltpu.VMEM((SUBLANES, D_RNN), f32),
        pltpu.VMEM((N_HEADS, SUB_ROWS, 2 * SUB_ROWS), f32),
        pltpu.VMEM((2, TOKEN_BLOCK, D_MODEL), bf16),
    ]
    y4, kp, vp, hp, convp, ffnp = pl.pallas_call(
        _prompt_kernel,
        grid=(T // TB + 1,),
        in_specs=in_specs,
        out_specs=out_specs,
        out_shape=out_shape,
        scratch_shapes=scratch,
        compiler_params=pltpu.CompilerParams(
            dimension_semantics=("arbitrary",), vmem_limit_bytes=VMEM_LIMIT_BYTES),
        name="prompt_layer",
    )(*ins)
    return (y4.reshape(T, D_MODEL), kp.reshape(WINDOW, KV_WIDTH), vp.reshape(WINDOW, KV_WIDTH),
            hp, convp, ffnp)


def _sample_pre_kernel(x_ref, mod_ref, g_pre_mix_ref, w_in_ref, q_ref, kv_ref, xr_ref, yr_ref):
    B = x_ref.shape[0]
    sh1 = mod_ref[0:B, 0 * D_MODEL:1 * D_MODEL]
    sc1 = mod_ref[0:B, 1 * D_MODEL:2 * D_MODEL]
    hmix = (_rms_norm(x_ref[...], g_pre_mix_ref[...]) * (1.0 + sc1) + sh1).astype(bf16)
    q_ref[...] = jnp.dot(hmix, w_in_ref[:, 0:C_K], preferred_element_type=f32) * Q_SCALE
    kv_ref[...] = jnp.dot(hmix, w_in_ref[:, C_K:C_XR], preferred_element_type=f32)
    xr_ref[...] = jnp.dot(hmix, w_in_ref[:, C_XR:C_YR], preferred_element_type=f32)
    yr_ref[...] = jnp.dot(hmix, w_in_ref[:, C_YR:IN_COLS], preferred_element_type=f32)


def _sample_pre_call(x, mod_s, g_pre_mix, w_in):
    B = x.shape[0]
    return pl.pallas_call(
        _sample_pre_kernel,
        out_shape=(
            jax.ShapeDtypeStruct((B, ATTN_WIDTH), f32),
            jax.ShapeDtypeStruct((B, 2 * KV_WIDTH), f32),
            jax.ShapeDtypeStruct((B, D_RNN), f32),
            jax.ShapeDtypeStruct((B, D_RNN), f32),
        ),
        compiler_params=pltpu.CompilerParams(vmem_limit_bytes=VMEM_LIMIT_BYTES),
        name="sample_pre",
    )(x, mod_s, g_pre_mix, w_in)


def _sample_attn_kernel(q_ref, kv_ref, ck_ref, cv_ref, sinks_ref, o_ref, kwin_ref, vwin_ref):
    R = N_KV_HEADS * WINDOW
    hrow = lax.broadcasted_iota(jnp.int32, (N_HEADS, R), 0)
    rcol = lax.broadcasted_iota(jnp.int32, (N_HEADS, R), 1)
    slope = jnp.exp2(-8.0 * (hrow + 1).astype(f32) / N_HEADS)
    own = (rcol % N_KV_HEADS) == (hrow // GQA_GROUP)
    bias = jnp.where(own, slope * (WINDOW - 1 - rcol // N_KV_HEADS).astype(f32), jnp.inf)
    wrow = lax.broadcasted_iota(jnp.int32, (R, HEAD_DIM), 0)
    sink = sinks_ref[...]

    def shifted(cache, new_rows):
        out = pltpu.roll(cache, R - N_KV_HEADS, axis=0)
        for c in range(N_KV_HEADS):
            out = jnp.where(wrow == R - N_KV_HEADS + c, new_rows[c], out)
        return out

    scores = []
    for b in range(SAMPLE_CHUNK):
        knew = [kv_ref[b:b + 1, c * HEAD_DIM:(c + 1) * HEAD_DIM] for c in range(N_KV_HEADS)]
        kw = shifted(ck_ref[b], knew)
        kwin_ref[b] = kw
        scores.append(_dot_nt(q_ref[b].astype(bf16), kw.astype(bf16)) - bias)
    probs, denoms = [], []
    for b in range(SAMPLE_CHUNK):
        s = scores[b]
        m = jnp.maximum(jnp.max(s, axis=-1, keepdims=True), sink)
        p = jnp.exp(s - m)
        denoms.append(jnp.sum(p, axis=-1, keepdims=True) + jnp.exp(sink - m))
        probs.append(p.astype(bf16))
    for b in range(SAMPLE_CHUNK):
        vnew = [kv_ref[b:b + 1, KV_WIDTH + c * HEAD_DIM:KV_WIDTH + (c + 1) * HEAD_DIM]
                for c in range(N_KV_HEADS)]
        vw = shifted(cv_ref[b], vnew)
        vwin_ref[b] = vw
        o_ref[b] = (jnp.dot(probs[b], vw.astype(bf16), preferred_element_type=f32)
                    / denoms[b])


def _sample_attn_call(q, kv, ck, cv, sinks_col):
    B = q.shape[0]
    BC = SAMPLE_CHUNK
    return pl.pallas_call(
        _sample_attn_kernel,
        grid=(B // BC,),
        in_specs=[
            pl.BlockSpec((BC, N_HEADS, HEAD_DIM), lambda i: (i, 0, 0)),
            pl.BlockSpec((BC, 2 * KV_WIDTH), lambda i: (i, 0)),
            pl.BlockSpec((BC, N_KV_HEADS * WINDOW, HEAD_DIM), lambda i: (i, 0, 0)),
            pl.BlockSpec((BC, N_KV_HEADS * WINDOW, HEAD_DIM), lambda i: (i, 0, 0)),
            pl.BlockSpec((N_HEADS, 1), lambda i: (0, 0)),
        ],
        out_specs=(
            pl.BlockSpec((BC, N_HEADS, HEAD_DIM), lambda i: (i, 0, 0)),
            pl.BlockSpec((BC, N_KV_HEADS * WINDOW, HEAD_DIM), lambda i: (i, 0, 0)),
            pl.BlockSpec((BC, N_KV_HEADS * WINDOW, HEAD_DIM), lambda i: (i, 0, 0)),
        ),
        out_shape=(
            jax.ShapeDtypeStruct((B, N_HEADS, HEAD_DIM), f32),
            jax.ShapeDtypeStruct((B, N_KV_HEADS * WINDOW, HEAD_DIM), f32),
            jax.ShapeDtypeStruct((B, N_KV_HEADS * WINDOW, HEAD_DIM), f32),
        ),
        compiler_params=pltpu.CompilerParams(
            dimension_semantics=("arbitrary",), vmem_limit_bytes=SAMPLE_ATTN_VMEM_BYTES),
        name="sample_attn",
    )(q, kv, ck, cv, sinks_col)


def _sample_post_kernel(
    x_ref, mod_ref, attn_ref, xr_ref, yr_ref, h0_ref, cbuf_ref, fbuf_ref,
    conv_w_ref, conv_b_ref, wa_ref, ba_ref, wi_ref, bi_ref, lam_ref,
    g_attn_ref, g_rnn_ref, w_out_ref, g_post_mix_ref, g_pre_ffn_ref, w_up_ref,
    fconv_w_ref, fconv_b_ref, w_down_ref, g_post_ffn_ref,
    y_ref, h_ref, cst_ref, fst_ref,
):
    B = x_ref.shape[0]
    g1 = mod_ref[0:B, 2 * D_MODEL:3 * D_MODEL]
    sh2 = mod_ref[0:B, 3 * D_MODEL:4 * D_MODEL]
    sc2 = mod_ref[0:B, 4 * D_MODEL:5 * D_MODEL]
    g2 = mod_ref[0:B, 5 * D_MODEL:6 * D_MODEL]
    x = x_ref[...]
    xr = xr_ref[...]

    xc = conv_b_ref[...] + conv_w_ref[RNN_CONV_W - 1:RNN_CONV_W, :] * xr
    for jj in range(RNN_CONV_W - 1):
        xc = xc + conv_w_ref[jj:jj + 1, :] * cbuf_ref[:, jj, :]
    for jj in range(RNN_CONV_W - 2):
        cst_ref[:, jj, :] = cbuf_ref[:, jj + 1, :]
    cst_ref[:, RNN_CONV_W - 2, :] = xr

    sp = _softplus_neg(lam_ref[...])
    a_parts, u_parts = _rglru_gates(xc, wa_ref, ba_ref[...], wi_ref, bi_ref[...], sp)
    a = jnp.concatenate(a_parts, axis=1)
    u = jnp.concatenate(u_parts, axis=1)
    h = a * h0_ref[...] + u
    h_ref[...] = h
    rnn = h * _gelu_tanh(yr_ref[...])

    attn_n = _rms_norm(attn_ref[...], g_attn_ref[...])
    rnn_n = _rms_norm(rnn, g_rnn_ref[...])
    mo = (_dot(attn_n, w_out_ref[0:ATTN_WIDTH, 0:D_MODEL])
          + _dot(rnn_n, w_out_ref[ATTN_WIDTH:ATTN_WIDTH + D_RNN, 0:D_MODEL]))
    x1 = x + g1 * _rms_norm(mo, g_post_mix_ref[...])

    hff = (_rms_norm(x1, g_pre_ffn_ref[...]) * (1.0 + sc2) + sh2).astype(bf16)
    up_pre = jnp.dot(hff, w_up_ref[...], preferred_element_type=f32)
    up = fconv_b_ref[...] + fconv_w_ref[FFN_CONV_W - 1:FFN_CONV_W, :] * up_pre
    for jj in range(FFN_CONV_W - 1):
        up = up + fconv_w_ref[jj:jj + 1, :] * fbuf_ref[:, jj, :]
    for jj in range(FFN_CONV_W - 2):
        fst_ref[:, jj, :] = fbuf_ref[:, jj + 1, :]
    fst_ref[:, FFN_CONV_W - 2, :] = up_pre

    act = (_gelu_tanh(up[:, 0:D_FF]) * up[:, D_FF:2 * D_FF]).astype(bf16)
    f = jnp.dot(act, w_down_ref[:, 0:D_MODEL], preferred_element_type=f32)
    y_ref[...] = x1 + g2 * _rms_norm(f, g_post_ffn_ref[...])


def _sample_post_call(x, mod_s, attn, xr, yr, h0, cbuf, fbuf, params):
    B = x.shape[0]
    (_, _, conv_w, conv_b, wa, ba, wi, bi, lam, g_attn, g_rnn, w_out,
     g_post_mix, g_pre_ffn, w_up, fconv_w, fconv_b, w_down, g_post_ffn) = params
    return pl.pallas_call(
        _sample_post_kernel,
        out_shape=(
            jax.ShapeDtypeStruct((B, D_MODEL), f32),
            jax.ShapeDtypeStruct((B, D_RNN), f32),
            jax.ShapeDtypeStruct((B, RNN_CONV_W - 1, D_RNN), f32),
            jax.ShapeDtypeStruct((B, FFN_CONV_W - 1, 2 * D_FF), f32),
        ),
        compiler_params=pltpu.CompilerParams(vmem_limit_bytes=VMEM_LIMIT_BYTES),
        name="sample_post",
    )(x, mod_s, attn, xr, yr, h0, cbuf, fbuf, conv_w, conv_b, wa, ba, wi, bi, lam,
      g_attn, g_rnn, w_out, g_post_mix, g_pre_ffn, w_up, fconv_w, fconv_b, w_down, g_post_ffn)


def kernel(x_prompt, x_sample, cache_k, cache_v, state_h, state_conv, state_ffn_conv, c_prompt, c_sample, w_ada, b_ada, g_pre_mix, w_in, conv_w, conv_b, w_a, b_a, w_i, b_i, lam, sinks, g_attn_out, g_rnn_out, w_out, g_post_mix, g_pre_ffn, w_up, ffn_conv_w, ffn_conv_b, w_down, g_post_ffn):
    depth = w_in.shape[0]
    assert depth == 1 and x_prompt.shape[0] == 1 and x_sample.shape[1] == 1
    T = x_prompt.shape[1]
    B = x_sample.shape[0]
    W = cache_k.shape[2]
    assert W == WINDOW and T % TOKEN_BLOCK == 0 and B % SAMPLE_CHUNK == 0

    row = lambda a: a[0].reshape(1, -1)
    pad_cols = lambda w: jnp.pad(w, ((0, 0), (0, PADDED_OUT_COLS - w.shape[1]))).astype(bf16)
    params = (
        row(g_pre_mix), w_in[0].astype(bf16), conv_w[0], row(conv_b),
        w_a[0].astype(bf16), row(b_a), w_i[0].astype(bf16), row(b_i), row(lam),
        row(g_attn_out), row(g_rnn_out), pad_cols(w_out[0]), row(g_post_mix),
        row(g_pre_ffn),
        w_up[0].astype(bf16), ffn_conv_w[0], row(ffn_conv_b), pad_cols(w_down[0]),
        row(g_post_ffn),
    )

    c_all = jnp.concatenate(
        [c_sample, jnp.broadcast_to(c_prompt, (MOD_PAD_ROWS, D_MODEL))], axis=0)
    mod = _mod_call(c_all, w_ada[0], b_ada[0].reshape(1, -1))
    mod_s = mod

    yp, kp, vp, hp, convp, ffnp = _prompt_call(x_prompt[0], mod, B, sinks[0], params)

    xs = x_sample[:, 0, :]
    q, kv, xr, yr = _sample_pre_call(xs, mod_s, params[0], params[1])
    ck = cache_k.reshape(B, W * N_KV_HEADS, HEAD_DIM)
    cv = cache_v.reshape(B, W * N_KV_HEADS, HEAD_DIM)
    attn3, kwin, vwin = _sample_attn_call(
        q.reshape(B, N_HEADS, HEAD_DIM), kv, ck, cv, sinks[0].reshape(N_HEADS, 1))
    attn = attn3.reshape(B, ATTN_WIDTH)
    ys, hs, convs, ffns = _sample_post_call(
        xs, mod_s, attn, xr, yr, state_h[0], state_conv[0], state_ffn_conv[0], params)

    kv_shape = (1, 1, W, N_KV_HEADS, HEAD_DIM)
    kvs_shape = (1, B, W, N_KV_HEADS, HEAD_DIM)
    return (
        yp[None], ys[:, None, :],
        kp.reshape(kv_shape), vp.reshape(kv_shape), hp[None], convp[None, None], ffnp[None, None],
        kwin.reshape(kvs_shape), vwin.reshape(kvs_shape), hs[None], convs[None], ffns[None],
    )
```

```python
import math

import jax
import jax.numpy as jnp
from jax import lax
from jax.experimental import pallas as pl
from jax.experimental.pallas import tpu as pltpu

D_MODEL = 1024
N_HEADS = 8
N_KV_HEADS = 2
HEAD_DIM = 128
GQA_GROUP = N_HEADS // N_KV_HEADS
ATTN_WIDTH = N_HEADS * HEAD_DIM
KV_WIDTH = N_KV_HEADS * HEAD_DIM
WINDOW = 128
D_RNN = D_MODEL
RNN_BLOCKS = 8
RNN_BLOCK_W = D_RNN // RNN_BLOCKS
RG_C = 8.0
RNN_CONV_W = 4
D_FF = 2816
FFN_CONV_W = 3
RMS_EPS = 1e-6

C_K = ATTN_WIDTH
C_V = C_K + KV_WIDTH
C_XR = C_V + KV_WIDTH
C_YR = C_XR + D_RNN
IN_COLS = C_YR + D_RNN

SUBLANES = 8
SUB_ROWS = WINDOW
GROUPS = SUB_ROWS // SUBLANES
TOKEN_BLOCK = 256
NSUB = TOKEN_BLOCK // SUB_ROWS
FFN_CHUNK = 256
STAGE_SKEW = 3
SAMPLE_CHUNK = 16
SAMPLE_ATTN_VMEM_BYTES = (2 * 2 * 2 * SAMPLE_CHUNK * N_KV_HEADS * WINDOW * HEAD_DIM * 4
                          + 8 * 1024 * 1024)
MOD_PAD_ROWS = 8
VMEM_LIMIT_BYTES = 56 * 1024 * 1024

ALIBI_SLOPES = tuple(2.0 ** (-8.0 * (h + 1) / N_HEADS) for h in range(N_HEADS))
Q_SCALE = HEAD_DIM ** -0.5
SQRT_2_OVER_PI = math.sqrt(2.0 / math.pi)
LOG2_E = 1.0 / math.log(2.0)
PADDED_OUT_COLS = D_MODEL + 128

bf16 = jnp.bfloat16
f32 = jnp.float32


def _rms_scale(x):
    return lax.rsqrt(jnp.mean(x * x, axis=-1, keepdims=True) + RMS_EPS)


def _rms_norm(x, g):
    return x * _rms_scale(x) * g


def _gelu_tanh(x):
    k1 = -2.0 * LOG2_E * SQRT_2_OVER_PI
    k3 = k1 * 0.044715
    return x / (1.0 + jnp.exp2(x * (k1 + k3 * (x * x))))


def _sigmoid(x):
    return 1.0 / (1.0 + jnp.exp2(x * (-LOG2_E)))


def _softplus_neg(lam):
    return jnp.maximum(-lam, 0.0) + jnp.log1p(jnp.exp(-jnp.abs(lam)))


def _dot(a, b):
    return jnp.dot(a.astype(bf16), b, preferred_element_type=f32)


def _dot_nt(a, b):
    return lax.dot_general(a, b, (((1,), (1,)), ((), ())), preferred_element_type=f32)


def _rglru_gates(xc, wa_ref, ba, wi_ref, bi, sp):
    a_parts, u_parts = [], []
    for n in range(RNN_BLOCKS):
        sl = slice(n * RNN_BLOCK_W, (n + 1) * RNN_BLOCK_W)
        xn = xc[:, sl]
        xb = xn.astype(bf16)
        r = _sigmoid(jnp.dot(xb, wa_ref[n], preferred_element_type=f32) + ba[:, sl])
        i = _sigmoid(jnp.dot(xb, wi_ref[n], preferred_element_type=f32) + bi[:, sl])
        log_a = (-RG_C) * r * sp[:, sl]
        t = jnp.tanh(log_a)
        one_minus_a2 = (-2.0 * t) / (1.0 - t)
        a_parts.append(jnp.exp(log_a))
        u_parts.append(jnp.sqrt(one_minus_a2) * (i * xn))
    return a_parts, u_parts


def _mod_kernel(c_ref, w_ref, b_ref, o_ref):
    c = c_ref[...]
    s = c * _sigmoid(c)
    o_ref[...] = _dot(s, w_ref[...].astype(bf16)) + b_ref[...]


def _mod_call(c_all, w_ada, b_ada):
    rows = c_all.shape[0]
    ncol = w_ada.shape[1]
    bn = D_MODEL
    return pl.pallas_call(
        _mod_kernel,
        grid=(ncol // bn,),
        in_specs=[
            pl.BlockSpec((rows, D_MODEL), lambda j: (0, 0)),
            pl.BlockSpec((D_MODEL, bn), lambda j: (0, j)),
            pl.BlockSpec((1, bn), lambda j: (0, j)),
        ],
        out_specs=pl.BlockSpec((rows, bn), lambda j: (0, j)),
        out_shape=jax.ShapeDtypeStruct((rows, ncol), f32),
        compiler_params=pltpu.CompilerParams(dimension_semantics=("arbitrary",)),
        name="adaln_mod",
    )(c_all, w_ada, b_ada)


def _groups(v, c0=None, c1=None):
    if c0 is None:
        return [v[j * SUBLANES:(j + 1) * SUBLANES, :] for j in range(GROUPS)]
    return [v[j * SUBLANES:(j + 1) * SUBLANES, c0:c1] for j in range(GROUPS)]


def _shifted_groups(X, tail_row, sub0, max_shift):
    wrapped = {}
    for j in range(GROUPS - max_shift, GROUPS):
        wrapped[j] = jnp.where(sub0, tail_row(j), pltpu.roll(X[j], 1, axis=0))
    sh = {}
    for d in range(1, max_shift + 1):
        sh[d] = [X[j - d] if j >= d else wrapped[j - d + GROUPS] for j in range(GROUPS)]
    return sh


def _scan_sub(a, u, h_in, sub_iota):
    A = _groups(a)
    L = _groups(u)
    for j in range(1, GROUPS):
        L[j] = A[j] * L[j - 1] + L[j]
        A[j] = A[j] * A[j - 1]
    ae, le = A[GROUPS - 1], L[GROUPS - 1]
    for s_ in (1, 2, 4):
        ok = sub_iota >= s_
        a_sh = pltpu.roll(ae, s_, axis=0)
        l_sh = pltpu.roll(le, s_, axis=0)
        le = jnp.where(ok, ae * l_sh + le, le)
        ae = jnp.where(ok, ae * a_sh, ae)
    hend = le + ae * h_in
    hprev = jnp.where(sub_iota == 0, h_in, pltpu.roll(hend, 1, axis=0))
    h = jnp.concatenate([L[j] + A[j] * hprev for j in range(GROUPS)], axis=0)
    h_out = jnp.broadcast_to(hend[SUBLANES - 1:SUBLANES, :], hend.shape)
    return h, h_out


def _prompt_kernel(
    x_hbm, mod_ref, sinks_ref,
    g_pre_mix_ref, w_in_ref, conv_w_ref, conv_b_ref, wa_ref, ba_ref, wi_ref, bi_ref, lam_ref,
    g_attn_ref, g_rnn_ref, w_out_ref, g_post_mix_ref, g_pre_ffn_ref, w_up_ref,
    fconv_w_ref, fconv_b_ref, w_down_ref, g_post_ffn_ref,
    y_hbm, kwin_hbm, vwin_hbm, hlast_ref, convst_ref, ffnst_ref,
    xbuf, ybuf, sem_in, sem_out, sem_st, kprev, vprev, kst, vst, xr_tail, up_tail, hcar, bias_s,
    hff_s,
):
    step = pl.program_id(0)
    nblocks = pl.num_programs(0) - 1
    slot = lax.rem(step, 2)
    yslot = lax.rem(step, 3)
    fslot = lax.rem(step + 2, 3)

    def x_copies(st, sl):
        return [pltpu.make_async_copy(
            x_hbm.at[st * NSUB + s, :, j, :],
            xbuf.at[sl, pl.ds(s * SUB_ROWS + j * SUBLANES, SUBLANES), :],
            sem_in.at[sl]) for s in range(NSUB) for j in range(GROUPS)]

    def y_copies(st, sl):
        return [pltpu.make_async_copy(
            ybuf.at[sl, pl.ds(s * SUB_ROWS + j * SUBLANES, SUBLANES), :],
            y_hbm.at[st * NSUB + s, :, j, :],
            sem_out.at[sl]) for s in range(NSUB) for j in range(GROUPS)]

    def state_copies():
        cps = []
        for src, dst in ((kst, kwin_hbm), (vst, vwin_hbm)):
            cps += [pltpu.make_async_copy(src.at[pl.ds(j * SUBLANES, SUBLANES), :],
                                          dst.at[:, j, :], sem_st) for j in range(GROUPS)]
        return cps

    @pl.when(step == 0)
    def _init():
        for cp in x_copies(0, 0):
            cp.start()
        kprev[...] = jnp.zeros(kprev.shape, bf16)
        vprev[...] = jnp.zeros(vprev.shape, bf16)
        xr_tail[...] = jnp.zeros(xr_tail.shape, f32)
        up_tail[...] = jnp.zeros(up_tail.shape, f32)
        hcar[...] = jnp.zeros(hcar.shape, f32)
        hff_s[...] = jnp.zeros(hff_s.shape, bf16)
        ybuf[2] = jnp.zeros(ybuf.shape[1:], f32)
        rq = lax.broadcasted_iota(jnp.int32, (SUB_ROWS, 2 * SUB_ROWS), 0)
        ck = lax.broadcasted_iota(jnp.int32, (SUB_ROWS, 2 * SUB_ROWS), 1)
        rk = ck & (SUB_ROWS - 1)
        tq = (rq % SUBLANES) * GROUPS + rq // SUBLANES
        tk = (rk % SUBLANES) * GROUPS + rk // SUBLANES
        dist = tq - tk + jnp.where(ck < SUB_ROWS, WINDOW, 0)
        base = jnp.where((dist >= 0) & (dist < WINDOW), dist.astype(f32), jnp.inf)
        for h in range(N_HEADS):
            bias_s[h] = ALIBI_SLOPES[h] * base

    @pl.when(step + 1 < nblocks)
    def _prefetch():
        for cp in x_copies(step + 1, 1 - slot):
            cp.start()

    @pl.when(step >= 3)
    def _free_ybuf():
        for cp in y_copies(0, yslot):
            cp.wait()

    @pl.when(step < nblocks)
    def _wait_x():
        for cp in x_copies(step, slot):
            cp.wait()

    sh1 = mod_ref[0:1, 0 * D_MODEL:1 * D_MODEL]
    sc1 = mod_ref[0:1, 1 * D_MODEL:2 * D_MODEL]
    g1 = mod_ref[0:1, 2 * D_MODEL:3 * D_MODEL]
    sh2 = mod_ref[0:1, 3 * D_MODEL:4 * D_MODEL]
    sc2 = mod_ref[0:1, 4 * D_MODEL:5 * D_MODEL]
    g2 = mod_ref[0:1, 5 * D_MODEL:6 * D_MODEL]
    gs1 = g_pre_mix_ref[...] * (1.0 + sc1)
    gs2 = g_pre_ffn_ref[...] * (1.0 + sc2)
    sp = _softplus_neg(lam_ref[...])
    first_pen = jnp.where(step == 0, jnp.inf, 0.0)
    sub_iota = lax.broadcasted_iota(jnp.int32, (SUBLANES, D_RNN), 0)
    sub0_rnn = sub_iota == 0
    sub0_ffn = lax.broadcasted_iota(jnp.int32, (SUBLANES, FFN_CHUNK), 0) == 0
    conv_w = [conv_w_ref.at[jj:jj + 1, :] for jj in range(RNN_CONV_W)]

    sts = [dict() for _ in range(NSUB)]

    def rows(s):
        return pl.ds(s * SUB_ROWS, SUB_ROWS)

    def stage_pre(s):
        x = xbuf[slot, rows(s), :]
        sts[s]["hmix"] = (x * _rms_scale(x) * gs1 + sh1).astype(bf16)

    def stage_inproj(s):
        st = sts[s]
        hm = st["hmix"]
        st["q"] = (jnp.dot(hm, w_in_ref[:, 0:C_K], preferred_element_type=f32) * Q_SCALE).astype(bf16)
        kv = jnp.dot(hm, w_in_ref[:, C_K:C_XR], preferred_element_type=f32)
        st["kb"] = kv[:, 0:KV_WIDTH].astype(bf16)
        st["vb"] = kv[:, KV_WIDTH:2 * KV_WIDTH].astype(bf16)
        st["kv"] = kv
        st["xr"] = jnp.dot(hm, w_in_ref[:, C_XR:C_YR], preferred_element_type=f32)

    def stage_attn(s):
        st = sts[s]
        if s == 0:
            kp, vp = kprev[...], vprev[...]
        else:
            kp, vp = sts[s - 1]["kb"], sts[s - 1]["vb"]
        kw = jnp.concatenate([kp, st["kb"]], axis=0)
        vw = jnp.concatenate([vp, st["vb"]], axis=0)
        outs = []
        for h in range(N_HEADS):
            c = h // GQA_GROUP
            hs = slice(h * HEAD_DIM, (h + 1) * HEAD_DIM)
            cs = slice(c * HEAD_DIM, (c + 1) * HEAD_DIM)
            sc = _dot_nt(st["q"][:, hs], kw[:, cs])
            if s == 0:
                sc = jnp.concatenate([sc[:, 0:SUB_ROWS] - first_pen, sc[:, SUB_ROWS:]], axis=1)
            sc = sc - bias_s[h]
            sink = sinks_ref[h]
            m = jnp.maximum(jnp.max(sc, axis=-1, keepdims=True), sink)
            p = jnp.exp(sc - m)
            denom = jnp.sum(p, axis=-1, keepdims=True) + jnp.exp(sink - m)
            outs.append(jnp.dot(p.astype(bf16), vw[:, cs], preferred_element_type=f32) / denom)
        attn = jnp.concatenate(outs, axis=1)
        st["attn_n"] = _rms_norm(attn, g_attn_ref[...]).astype(bf16)

    def stage_rnn(s):
        st = sts[s]
        xr = st["xr"]
        X = _groups(xr)
        if s == 0:
            tail = lambda j: xr_tail[(j - (GROUPS - RNN_CONV_W + 1)) * SUBLANES + SUBLANES - 1:
                                     (j - (GROUPS - RNN_CONV_W + 1)) * SUBLANES + SUBLANES, :]
        else:
            pxr = sts[s - 1]["xr"]
            tail = lambda j: pxr[j * SUBLANES + SUBLANES - 1:(j + 1) * SUBLANES, :]
        sh = _shifted_groups(X, tail, sub0_rnn, RNN_CONV_W - 1)
        xcs = []
        for j in range(GROUPS):
            acc = conv_b_ref[...] + conv_w[RNN_CONV_W - 1][...] * X[j]
            for d in range(1, RNN_CONV_W):
                acc = acc + conv_w[RNN_CONV_W - 1 - d][...] * sh[d][j]
            xcs.append(acc)
        xc = jnp.concatenate(xcs, axis=0)
        a_parts, u_parts = _rglru_gates(xc, wa_ref, ba_ref[...], wi_ref, bi_ref[...], sp)
        a = jnp.concatenate(a_parts, axis=1)
        u = jnp.concatenate(u_parts, axis=1)
        h_in = hcar[...] if s == 0 else sts[s - 1]["h_out"]
        h, st["h_out"] = _scan_sub(a, u, h_in, sub_iota)
        yr = jnp.dot(st["hmix"], w_in_ref[:, C_YR:IN_COLS], preferred_element_type=f32)
        rnn = h * _gelu_tanh(yr)
        st["rnn_n"] = _rms_norm(rnn, g_rnn_ref[...]).astype(bf16)

    def stage_out(s):
        st = sts[s]
        mo = (jnp.dot(st["attn_n"], w_out_ref[0:ATTN_WIDTH, 0:D_MODEL], preferred_element_type=f32)
              + jnp.dot(st["rnn_n"], w_out_ref[ATTN_WIDTH:ATTN_WIDTH + D_RNN, 0:D_MODEL],
                        preferred_element_type=f32))
        x1 = xbuf[slot, rows(s), :] + g1 * _rms_norm(mo, g_post_mix_ref[...])
        ybuf[yslot, rows(s), :] = x1
        hff_s[slot, rows(s), :] = (x1 * _rms_scale(x1) * gs2 + sh2).astype(bf16)

    tail_g0 = GROUPS - (FFN_CONV_W - 1)
    hff = hff_s[1 - slot]
    acts, tails = [], {}

    def conv_chunk(c0):
        c1 = c0 + FFN_CHUNK
        up = jnp.dot(hff, w_up_ref[:, c0:c1], preferred_element_type=f32)
        w = [fconv_w_ref.at[jj:jj + 1, c0:c1] for jj in range(FFN_CONV_W)]
        outs = []
        for s in range(NSUB):
            r0 = s * SUB_ROWS
            X = [up[r0 + j * SUBLANES:r0 + (j + 1) * SUBLANES, :] for j in range(GROUPS)]
            if s == 0:
                tail = lambda j: up_tail[(j - tail_g0) * SUBLANES + SUBLANES - 1:
                                         (j - tail_g0 + 1) * SUBLANES, c0:c1]
            else:
                tail = lambda j, p0=r0 - SUB_ROWS: up[p0 + j * SUBLANES + SUBLANES - 1:
                                                      p0 + (j + 1) * SUBLANES, :]
            sh = _shifted_groups(X, tail, sub0_ffn, FFN_CONV_W - 1)
            for j in range(GROUPS):
                acc = fconv_b_ref[:, c0:c1] + w[FFN_CONV_W - 1][...] * X[j]
                for d in range(1, FFN_CONV_W):
                    acc = acc + w[FFN_CONV_W - 1 - d][...] * sh[d][j]
                outs.append(acc)
        last0 = (NSUB - 1) * SUB_ROWS
        for t in range(FFN_CONV_W - 1):
            r = last0 + (tail_g0 + t) * SUBLANES + SUBLANES - 1
            ffnst_ref[t:t + 1, c0:c1] = up[r:r + 1, :]
        tails[c0] = up[last0 + tail_g0 * SUBLANES:, :]
        return jnp.concatenate(outs, axis=0)

    def ffn_chunk(cc):
        gate = conv_chunk(cc * FFN_CHUNK)
        val = conv_chunk(D_FF + cc * FFN_CHUNK)
        acts.append((_gelu_tanh(gate) * val).astype(bf16))

    def ffn_finish():
        for c0, tl in tails.items():
            up_tail[:, c0:c0 + FFN_CHUNK] = tl
        act = jnp.concatenate(acts, axis=1)
        f = jnp.dot(act, w_down_ref[:, 0:D_MODEL], preferred_element_type=f32)
        ybuf[fslot] = ybuf[fslot] + g2 * _rms_norm(f, g_post_ffn_ref[...])

    stages = (stage_pre, stage_inproj, stage_attn, stage_rnn, stage_out)
    n_times = len(stages) + (NSUB - 1) * STAGE_SKEW
    n_chunks = D_FF // FFN_CHUNK
    ffn_chunk(0)
    nxt = 1
    for t in range(n_times):
        for s in range(NSUB):
            k = t - s * STAGE_SKEW
            if 0 <= k < len(stages):
                stages[k](s)
                if nxt < n_chunks:
                    ffn_chunk(nxt)
                    nxt += 1
    while nxt < n_chunks:
        ffn_chunk(nxt)
        nxt += 1
    ffn_finish()

    lst = sts[NSUB - 1]
    kprev[...] = lst["kb"]
    vprev[...] = lst["vb"]
    xr_tail[...] = lst["xr"][(GROUPS - (RNN_CONV_W - 1)) * SUBLANES:, :]
    hcar[...] = lst["h_out"]

    @pl.when(step < nblocks)
    def _state():
        kst[...] = lst["kv"][:, 0:KV_WIDTH]
        vst[...] = lst["kv"][:, KV_WIDTH:2 * KV_WIDTH]
        for t in range(RNN_CONV_W - 1):
            r = (GROUPS - (RNN_CONV_W - 1) + t) * SUBLANES + SUBLANES - 1
            convst_ref[t:t + 1, :] = lst["xr"][r:r + 1, :]
        hlast_ref[...] = lst["h_out"][0:1, :]

    @pl.when(step >= 1)
    def _store_y():
        for cp in y_copies(step - 1, fslot):
            cp.start()

    @pl.when(step == nblocks)
    def _finish():
        for cp in state_copies():
            cp.start()
        for cp in state_copies():
            cp.wait()
        for cp in y_copies(0, lax.rem(step + 1, 3)) + y_copies(0, fslot):
            cp.wait()


def _const_spec(shape):
    nd = len(shape)
    return pl.BlockSpec(shape, lambda i: (0,) * nd)


def _prompt_call(x, mod, mod_row0, sinks, params):
    T = x.shape[0]
    TB = TOKEN_BLOCK
    assert T // TB >= 3
    (g_pre_mix, w_in, conv_w, conv_b, wa, ba, wi, bi, lam, g_attn, g_rnn, w_out,
     g_post_mix, g_pre_ffn, w_up, fconv_w, fconv_b, w_down, g_post_ffn) = params
    x4 = x.reshape(T // SUB_ROWS, SUBLANES, GROUPS, D_MODEL)
    ins = [x4, mod, sinks, g_pre_mix, w_in, conv_w, conv_b, wa, ba, wi, bi, lam, g_attn, g_rnn,
           w_out, g_post_mix, g_pre_ffn, w_up, fconv_w, fconv_b, w_down, g_post_ffn]
    assert mod_row0 % MOD_PAD_ROWS == 0
    in_specs = [pl.BlockSpec(memory_space=pl.ANY),
                pl.BlockSpec((MOD_PAD_ROWS, mod.shape[1]), lambda i: (mod_row0 // MOD_PAD_ROWS, 0)),
                pl.BlockSpec(memory_space=pltpu.SMEM)]
    in_specs += [_const_spec(a.shape) for a in ins[3:]]
    out_shape = (
        jax.ShapeDtypeStruct(x4.shape, f32),
        jax.ShapeDtypeStruct((SUBLANES, GROUPS, KV_WIDTH), f32),
        jax.ShapeDtypeStruct((SUBLANES, GROUPS, KV_WIDTH), f32),
        jax.ShapeDtypeStruct((1, D_RNN), f32),
        jax.ShapeDtypeStruct((RNN_CONV_W - 1, D_RNN), f32),
        jax.ShapeDtypeStruct((FFN_CONV_W - 1, 2 * D_FF), f32),
    )
    out_specs = (
        pl.BlockSpec(memory_space=pl.ANY),
        pl.BlockSpec(memory_space=pl.ANY),
        pl.BlockSpec(memory_space=pl.ANY),
        _const_spec((1, D_RNN)),
        _const_spec((RNN_CONV_W - 1, D_RNN)),
        _const_spec((FFN_CONV_W - 1, 2 * D_FF)),
    )
    scratch = [
        pltpu.VMEM((2, TB, D_MODEL), f32),
        pltpu.VMEM((3, TB, D_MODEL), f32),
        pltpu.SemaphoreType.DMA((2,)),
        pltpu.SemaphoreType.DMA((3,)),
        pltpu.SemaphoreType.DMA(()),
        pltpu.VMEM((SUB_ROWS, KV_WIDTH), bf16),
        pltpu.VMEM((SUB_ROWS, KV_WIDTH), bf16),
        pltpu.VMEM((SUB_ROWS, KV_WIDTH), f32),
        pltpu.VMEM((SUB_ROWS, KV_WIDTH), f32),
        pltpu.VMEM(((RNN_CONV_W - 1) * SUBLANES, D_RNN), f32),
        pltpu.VMEM(((FFN_CONV_W - 1) * SUBLANES, 2 * D_FF), f32),
        pltpu.VMEM((SUBLANES, D_RNN), f32),
        pltpu.VMEM((N_HEADS, SUB_ROWS, 2 * SUB_ROWS), f32),
        pltpu.VMEM((2, TOKEN_BLOCK, D_MODEL), bf16),
    ]
    y4, kp, vp, hp, convp, ffnp = pl.pallas_call(
        _prompt_kernel,
        grid=(T // TB + 1,),
        in_specs=in_specs,
        out_specs=out_specs,
        out_shape=out_shape,
        scratch_shapes=scratch,
        compiler_params=pltpu.CompilerParams(
            dimension_semantics=("arbitrary",), vmem_limit_bytes=VMEM_LIMIT_BYTES),
        name="prompt_layer",
    )(*ins)
    return (y4.reshape(T, D_MODEL), kp.reshape(WINDOW, KV_WIDTH), vp.reshape(WINDOW, KV_WIDTH),
            hp, convp, ffnp)


def _sample_pre_kernel(x_ref, mod_ref, g_pre_mix_ref, w_in_ref, q_ref, kv_ref, xr_ref, yr_ref):
    B = x_ref.shape[0]
    sh1 = mod_ref[0:B, 0 * D_MODEL:1 * D_MODEL]
    sc1 = mod_ref[0:B, 1 * D_MODEL:2 * D_MODEL]
    hmix = (_rms_norm(x_ref[...], g_pre_mix_ref[...]) * (1.0 + sc1) + sh1).astype(bf16)
    q_ref[...] = jnp.dot(hmix, w_in_ref[:, 0:C_K], preferred_element_type=f32) * Q_SCALE
    kv_ref[...] = jnp.dot(hmix, w_in_ref[:, C_K:C_XR], preferred_element_type=f32)
    xr_ref[...] = jnp.dot(hmix, w_in_ref[:, C_XR:C_YR], preferred_element_type=f32)
    yr_ref[...] = jnp.dot(hmix, w_in_ref[:, C_YR:IN_COLS], preferred_element_type=f32)


def _sample_pre_call(x, mod_s, g_pre_mix, w_in):
    B = x.shape[0]
    return pl.pallas_call(
        _sample_pre_kernel,
        out_shape=(
            jax.ShapeDtypeStruct((B, ATTN_WIDTH), f32),
            jax.ShapeDtypeStruct((B, 2 * KV_WIDTH), f32),
            jax.ShapeDtypeStruct((B, D_RNN), f32),
            jax.ShapeDtypeStruct((B, D_RNN), f32),
        ),
        compiler_params=pltpu.CompilerParams(vmem_limit_bytes=VMEM_LIMIT_BYTES),
        name="sample_pre",
    )(x, mod_s, g_pre_mix, w_in)


def _sample_attn_kernel(q_ref, kv_ref, ck_ref, cv_ref, sinks_ref, o_ref, kwin_ref, vwin_ref):
    R = N_KV_HEADS * WINDOW
    hrow = lax.broadcasted_iota(jnp.int32, (N_HEADS, R), 0)
    rcol = lax.broadcasted_iota(jnp.int32, (N_HEADS, R), 1)
    slope = jnp.exp2(-8.0 * (hrow + 1).astype(f32) / N_HEADS)
    own = (rcol % N_KV_HEADS) == (hrow // GQA_GROUP)
    bias = jnp.where(own, slope * (WINDOW - 1 - rcol // N_KV_HEADS).astype(f32), jnp.inf)
    wrow = lax.broadcasted_iota(jnp.int32, (R, HEAD_DIM), 0)
    sink = sinks_ref[...]

    def shifted(cache, new_rows):
        out = pltpu.roll(cache, R - N_KV_HEADS, axis=0)
        for c in range(N_KV_HEADS):
            out = jnp.where(wrow == R - N_KV_HEADS + c, new_rows[c], out)
        return out

    scores = []
    for b in range(SAMPLE_CHUNK):
        knew = [kv_ref[b:b + 1, c * HEAD_DIM:(c + 1) * HEAD_DIM] for c in range(N_KV_HEADS)]
        kw = shifted(ck_ref[b], knew)
        kwin_ref[b] = kw
        scores.append(_dot_nt(q_ref[b].astype(bf16), kw.astype(bf16)) - bias)
    probs, denoms = [], []
    for b in range(SAMPLE_CHUNK):
        s = scores[b]
        m = jnp.maximum(jnp.max(s, axis=-1, keepdims=True), sink)
        p = jnp.exp(s - m)
        denoms.append(jnp.sum(p, axis=-1, keepdims=True) + jnp.exp(sink - m))
        probs.append(p.astype(bf16))
    for b in range(SAMPLE_CHUNK):
        vnew = [kv_ref[b:b + 1, KV_WIDTH + c * HEAD_DIM:KV_WIDTH + (c + 1) * HEAD_DIM]
                for c in range(N_KV_HEADS)]
        vw = shifted(cv_ref[b], vnew)
        vwin_ref[b] = vw
        o_ref[b] = (jnp.dot(probs[b], vw.astype(bf16), preferred_element_type=f32)
                    / denoms[b])


def _sample_attn_call(q, kv, ck, cv, sinks_col):
    B = q.shape[0]
    BC = SAMPLE_CHUNK
    return pl.pallas_call(
        _sample_attn_kernel,
        grid=(B // BC,),
        in_specs=[
            pl.BlockSpec((BC, N_HEADS, HEAD_DIM), lambda i: (i, 0, 0)),
            pl.BlockSpec((BC, 2 * KV_WIDTH), lambda i: (i, 0)),
            pl.BlockSpec((BC, N_KV_HEADS * WINDOW, HEAD_DIM), lambda i: (i, 0, 0)),
            pl.BlockSpec((BC, N_KV_HEADS * WINDOW, HEAD_DIM), lambda i: (i, 0, 0)),
            pl.BlockSpec((N_HEADS, 1), lambda i: (0, 0)),
        ],
        out_specs=(
            pl.BlockSpec((BC, N_HEADS, HEAD_DIM), lambda i: (i, 0, 0)),
            pl.BlockSpec((BC, N_KV_HEADS * WINDOW, HEAD_DIM), lambda i: (i, 0, 0)),
            pl.BlockSpec((BC, N_KV_HEADS * WINDOW, HEAD_DIM), lambda i: (i, 0, 0)),
        ),
        out_shape=(
            jax.ShapeDtypeStruct((B, N_HEADS, HEAD_DIM), f32),
            jax.ShapeDtypeStruct((B, N_KV_HEADS * WINDOW, HEAD_DIM), f32),
            jax.ShapeDtypeStruct((B, N_KV_HEADS * WINDOW, HEAD_DIM), f32),
        ),
        compiler_params=pltpu.CompilerParams(
            dimension_semantics=("arbitrary",), vmem_limit_bytes=SAMPLE_ATTN_VMEM_BYTES),
        name="sample_attn",
    )(q, kv, ck, cv, sinks_col)


def _sample_post_kernel(
    x_ref, mod_ref, attn_ref, xr_ref, yr_ref, h0_ref, cbuf_ref, fbuf_ref,
    conv_w_ref, conv_b_ref, wa_ref, ba_ref, wi_ref, bi_ref, lam_ref,
    g_attn_ref, g_rnn_ref, w_out_ref, g_post_mix_ref, g_pre_ffn_ref, w_up_ref,
    fconv_w_ref, fconv_b_ref, w_down_ref, g_post_ffn_ref,
    y_ref, h_ref, cst_ref, fst_ref,
):
    B = x_ref.shape[0]
    g1 = mod_ref[0:B, 2 * D_MODEL:3 * D_MODEL]
    sh2 = mod_ref[0:B, 3 * D_MODEL:4 * D_MODEL]
    sc2 = mod_ref[0:B, 4 * D_MODEL:5 * D_MODEL]
    g2 = mod_ref[0:B, 5 * D_MODEL:6 * D_MODEL]
    x = x_ref[...]
    xr = xr_ref[...]

    xc = conv_b_ref[...] + conv_w_ref[RNN_CONV_W - 1:RNN_CONV_W, :] * xr
    for jj in range(RNN_CONV_W - 1):
        xc = xc + conv_w_ref[jj:jj + 1, :] * cbuf_ref[:, jj, :]
    for jj in range(RNN_CONV_W - 2):
        cst_ref[:, jj, :] = cbuf_ref[:, jj + 1, :]
    cst_ref[:, RNN_CONV_W - 2, :] = xr

    sp = _softplus_neg(lam_ref[...])
    a_parts, u_parts = _rglru_gates(xc, wa_ref, ba_ref[...], wi_ref, bi_ref[...], sp)
    a = jnp.concatenate(a_parts, axis=1)
    u = jnp.concatenate(u_parts, axis=1)
    h = a * h0_ref[...] + u
    h_ref[...] = h
    rnn = h * _gelu_tanh(yr_ref[...])

    attn_n = _rms_norm(attn_ref[...], g_attn_ref[...])
    rnn_n = _rms_norm(rnn, g_rnn_ref[...])
    mo = (_dot(attn_n, w_out_ref[0:ATTN_WIDTH, 0:D_MODEL])
          + _dot(rnn_n, w_out_ref[ATTN_WIDTH:ATTN_WIDTH + D_RNN, 0:D_MODEL]))
    x1 = x + g1 * _rms_norm(mo, g_post_mix_ref[...])

    hff = (_rms_norm(x1, g_pre_ffn_ref[...]) * (1.0 + sc2) + sh2).astype(bf16)
    up_pre = jnp.dot(hff, w_up_ref[...], preferred_element_type=f32)
    up = fconv_b_ref[...] + fconv_w_ref[FFN_CONV_W - 1:FFN_CONV_W, :] * up_pre
    for jj in range(FFN_CONV_W - 1):
        up = up + fconv_w_ref[jj:jj + 1, :] * fbuf_ref[:, jj, :]
    for jj in range(FFN_CONV_W - 2):
        fst_ref[:, jj, :] = fbuf_ref[:, jj + 1, :]
    fst_ref[:, FFN_CONV_W - 2, :] = up_pre

    act = (_gelu_tanh(up[:, 0:D_FF]) * up[:, D_FF:2 * D_FF]).astype(bf16)
    f = jnp.dot(act, w_down_ref[:, 0:D_MODEL], preferred_element_type=f32)
    y_ref[...] = x1 + g2 * _rms_norm(f, g_post_ffn_ref[...])


def _sample_post_call(x, mod_s, attn, xr, yr, h0, cbuf, fbuf, params):
    B = x.shape[0]
    (_, _, conv_w, conv_b, wa, ba, wi, bi, lam, g_attn, g_rnn, w_out,
     g_post_mix, g_pre_ffn, w_up, fconv_w, fconv_b, w_down, g_post_ffn) = params
    return pl.pallas_call(
        _sample_post_kernel,
        out_shape=(
            jax.ShapeDtypeStruct((B, D_MODEL), f32),
            jax.ShapeDtypeStruct((B, D_RNN), f32),
            jax.ShapeDtypeStruct((B, RNN_CONV_W - 1, D_RNN), f32),
            jax.ShapeDtypeStruct((B, FFN_CONV_W - 1, 2 * D_FF), f32),
        ),
        compiler_params=pltpu.CompilerParams(vmem_limit_bytes=VMEM_LIMIT_BYTES),
        name="sample_post",
    )(x, mod_s, attn, xr, yr, h0, cbuf, fbuf, conv_w, conv_b, wa, ba, wi, bi, lam,
      g_attn, g_rnn, w_out, g_post_mix, g_pre_ffn, w_up, fconv_w, fconv_b, w_down, g_post_ffn)


def kernel(x_prompt, x_sample, cache_k, cache_v, state_h, state_conv, state_ffn_conv, c_prompt, c_sample, w_ada, b_ada, g_pre_mix, w_in, conv_w, conv_b, w_a, b_a, w_i, b_i, lam, sinks, g_attn_out, g_rnn_out, w_out, g_post_mix, g_pre_ffn, w_up, ffn_conv_w, ffn_conv_b, w_down, g_post_ffn):
    depth = w_in.shape[0]
    assert depth == 1 and x_prompt.shape[0] == 1 and x_sample.shape[1] == 1
    T = x_prompt.shape[1]
    B = x_sample.shape[0]
    W = cache_k.shape[2]
    assert W == WINDOW and T % TOKEN_BLOCK == 0 and B % SAMPLE_CHUNK == 0

    row = lambda a: a[0].reshape(1, -1)
    pad_cols = lambda w: jnp.concatenate(
        [w.astype(bf16), jnp.zeros((w.shape[0], PADDED_OUT_COLS - w.shape[1]), bf16)], axis=1)
    params = (
        row(g_pre_mix), w_in[0].astype(bf16), conv_w[0], row(conv_b),
        w_a[0].astype(bf16), row(b_a), w_i[0].astype(bf16), row(b_i), row(lam),
        row(g_attn_out), row(g_rnn_out), pad_cols(w_out[0]), row(g_post_mix),
        row(g_pre_ffn),
        w_up[0].astype(bf16), ffn_conv_w[0], row(ffn_conv_b), pad_cols(w_down[0]),
        row(g_post_ffn),
    )

    c_all = jnp.concatenate(
        [c_sample, jnp.broadcast_to(c_prompt, (MOD_PAD_ROWS, D_MODEL))], axis=0)
    mod = _mod_call(c_all, w_ada[0], b_ada[0].reshape(1, -1))
    mod_s = mod

    yp, kp, vp, hp, convp, ffnp = _prompt_call(x_prompt[0], mod, B, sinks[0], params)

    xs = x_sample[:, 0, :]
    q, kv, xr, yr = _sample_pre_call(xs, mod_s, params[0], params[1])
    ck = cache_k.reshape(B, W * N_KV_HEADS, HEAD_DIM)
    cv = cache_v.reshape(B, W * N_KV_HEADS, HEAD_DIM)
    attn3, kwin, vwin = _sample_attn_call(
        q.reshape(B, N_HEADS, HEAD_DIM), kv, ck, cv, sinks[0].reshape(N_HEADS, 1))
    attn = attn3.reshape(B, ATTN_WIDTH)
    ys, hs, convs, ffns = _sample_post_call(
        xs, mod_s, attn, xr, yr, state_h[0], state_conv[0], state_ffn_conv[0], params)

    kv_shape = (1, 1, W, N_KV_HEADS, HEAD_DIM)
    kvs_shape = (1, B, W, N_KV_HEADS, HEAD_DIM)
    return (
        yp[None], ys[:, None, :],
        kp.reshape(kv_shape), vp.reshape(kv_shape), hp[None], convp[None, None], ffnp[None, None],
        kwin.reshape(kvs_shape), vwin.reshape(kvs_shape), hs[None], convs[None], ffns[None],
    )
```

```python
import math

import jax
import jax.numpy as jnp
from jax import lax
from jax.experimental import pallas as pl
from jax.experimental.pallas import tpu as pltpu

D_MODEL = 1024
N_HEADS = 8
N_KV_HEADS = 2
HEAD_DIM = 128
GQA_GROUP = N_HEADS // N_KV_HEADS
ATTN_WIDTH = N_HEADS * HEAD_DIM
KV_WIDTH = N_KV_HEADS * HEAD_DIM
WINDOW = 128
D_RNN = D_MODEL
RNN_BLOCKS = 8
RNN_BLOCK_W = D_RNN // RNN_BLOCKS
RG_C = 8.0
RNN_CONV_W = 4
D_FF = 2816
FFN_CONV_W = 3
RMS_EPS = 1e-6

C_K = ATTN_WIDTH
C_V = C_K + KV_WIDTH
C_XR = C_V + KV_WIDTH
C_YR = C_XR + D_RNN
IN_COLS = C_YR + D_RNN

SUBLANES = 8
SUB_ROWS = WINDOW
GROUPS = SUB_ROWS // SUBLANES
TOKEN_BLOCK = 256
NSUB = TOKEN_BLOCK // SUB_ROWS
FFN_CHUNK = 256
STAGE_SKEW = 3
SAMPLE_CHUNK = 16
SAMPLE_ATTN_VMEM_BYTES = (2 * 2 * 2 * SAMPLE_CHUNK * N_KV_HEADS * WINDOW * HEAD_DIM * 4
                          + 8 * 1024 * 1024)
MOD_PAD_ROWS = 8
VMEM_LIMIT_BYTES = 56 * 1024 * 1024

ALIBI_SLOPES = tuple(2.0 ** (-8.0 * (h + 1) / N_HEADS) for h in range(N_HEADS))
Q_SCALE = HEAD_DIM ** -0.5
SQRT_2_OVER_PI = math.sqrt(2.0 / math.pi)
LOG2_E = 1.0 / math.log(2.0)
PADDED_OUT_COLS = D_MODEL + 128

bf16 = jnp.bfloat16
f32 = jnp.float32


def _rms_scale(x):
    return lax.rsqrt(jnp.mean(x * x, axis=-1, keepdims=True) + RMS_EPS)


def _rms_norm(x, g):
    return x * _rms_scale(x) * g


def _gelu_tanh(x):
    k1 = -2.0 * LOG2_E * SQRT_2_OVER_PI
    k3 = k1 * 0.044715
    return x / (1.0 + jnp.exp2(x * (k1 + k3 * (x * x))))


def _sigmoid(x):
    return 1.0 / (1.0 + jnp.exp2(x * (-LOG2_E)))


def _softplus_neg(lam):
    return jnp.maximum(-lam, 0.0) + jnp.log1p(jnp.exp(-jnp.abs(lam)))


def _dot(a, b):
    return jnp.dot(a.astype(bf16), b, preferred_element_type=f32)


def _dot_nt(a, b):
    return lax.dot_general(a, b, (((1,), (1,)), ((), ())), preferred_element_type=f32)


def _rglru_gates(xc, wa_ref, ba, wi_ref, bi, sp):
    a_parts, u_parts = [], []
    for n in range(RNN_BLOCKS):
        sl = slice(n * RNN_BLOCK_W, (n + 1) * RNN_BLOCK_W)
        xn = xc[:, sl]
        xb = xn.astype(bf16)
        r = _sigmoid(jnp.dot(xb, wa_ref[n], preferred_element_type=f32) + ba[:, sl])
        i = _sigmoid(jnp.dot(xb, wi_ref[n], preferred_element_type=f32) + bi[:, sl])
        log_a = (-RG_C) * r * sp[:, sl]
        t = jnp.tanh(log_a)
        one_minus_a2 = (-2.0 * t) / (1.0 - t)
        a_parts.append(jnp.exp(log_a))
        u_parts.append(jnp.sqrt(one_minus_a2) * (i * xn))
    return a_parts, u_parts


def _mod_kernel(c_ref, w_ref, b_ref, o_ref):
    c = c_ref[...]
    s = c * _sigmoid(c)
    o_ref[...] = _dot(s, w_ref[...].astype(bf16)) + b_ref[...]


def _mod_call(c_all, w_ada, b_ada):
    rows = c_all.shape[0]
    ncol = w_ada.shape[1]
    bn = D_MODEL
    return pl.pallas_call(
        _mod_kernel,
        grid=(ncol // bn,),
        in_specs=[
            pl.BlockSpec((rows, D_MODEL), lambda j: (0, 0)),
            pl.BlockSpec((D_MODEL, bn), lambda j: (0, j)),
            pl.BlockSpec((1, bn), lambda j: (0, j)),
        ],
        out_specs=pl.BlockSpec((rows, bn), lambda j: (0, j)),
        out_shape=jax.ShapeDtypeStruct((rows, ncol), f32),
        compiler_params=pltpu.CompilerParams(dimension_semantics=("arbitrary",)),
        name="adaln_mod",
    )(c_all, w_ada, b_ada)


def _groups(v, c0=None, c1=None):
    if c0 is None:
        return [v[j * SUBLANES:(j + 1) * SUBLANES, :] for j in range(GROUPS)]
    return [v[j * SUBLANES:(j + 1) * SUBLANES, c0:c1] for j in range(GROUPS)]


def _shifted_groups(X, tail_row, sub0, max_shift):
    wrapped = {}
    for j in range(GROUPS - max_shift, GROUPS):
        wrapped[j] = jnp.where(sub0, tail_row(j), pltpu.roll(X[j], 1, axis=0))
    sh = {}
    for d in range(1, max_shift + 1):
        sh[d] = [X[j - d] if j >= d else wrapped[j - d + GROUPS] for j in range(GROUPS)]
    return sh


def _scan_sub(a, u, h_in, sub_iota):
    A = _groups(a)
    L = _groups(u)
    for j in range(1, GROUPS):
        L[j] = A[j] * L[j - 1] + L[j]
        A[j] = A[j] * A[j - 1]
    ae, le = A[GROUPS - 1], L[GROUPS - 1]
    for s_ in (1, 2, 4):
        ok = sub_iota >= s_
        a_sh = pltpu.roll(ae, s_, axis=0)
        l_sh = pltpu.roll(le, s_, axis=0)
        le = jnp.where(ok, ae * l_sh + le, le)
        ae = jnp.where(ok, ae * a_sh, ae)
    hend = le + ae * h_in
    hprev = jnp.where(sub_iota == 0, h_in, pltpu.roll(hend, 1, axis=0))
    h = jnp.concatenate([L[j] + A[j] * hprev for j in range(GROUPS)], axis=0)
    h_out = jnp.broadcast_to(hend[SUBLANES - 1:SUBLANES, :], hend.shape)
    return h, h_out


def _prompt_kernel(
    x_hbm, mod_ref, sinks_ref,
    g_pre_mix_ref, w_in_ref, conv_w_ref, conv_b_ref, wa_ref, ba_ref, wi_ref, bi_ref, lam_ref,
    g_attn_ref, g_rnn_ref, w_out_ref, g_post_mix_ref, g_pre_ffn_ref, w_up_ref,
    fconv_w_ref, fconv_b_ref, w_down_ref, g_post_ffn_ref,
    y_hbm, kwin_hbm, vwin_hbm, hlast_ref, convst_ref, ffnst_ref,
    xbuf, ybuf, sem_in, sem_out, sem_st, kprev, vprev, kst, vst, xr_tail, up_tail, hcar, bias_s,
    hff_s,
):
    step = pl.program_id(0)
    nblocks = pl.num_programs(0) - 1
    slot = lax.rem(step, 2)
    yslot = lax.rem(step, 3)
    fslot = lax.rem(step + 2, 3)

    def x_copies(st, sl):
        return [pltpu.make_async_copy(
            x_hbm.at[st * NSUB + s, :, j, :],
            xbuf.at[sl, pl.ds(s * SUB_ROWS + j * SUBLANES, SUBLANES), :],
            sem_in.at[sl]) for s in range(NSUB) for j in range(GROUPS)]

    def y_copies(st, sl):
        return [pltpu.make_async_copy(
            ybuf.at[sl, pl.ds(s * SUB_ROWS + j * SUBLANES, SUBLANES), :],
            y_hbm.at[st * NSUB + s, :, j, :],
            sem_out.at[sl]) for s in range(NSUB) for j in range(GROUPS)]

    def state_copies():
        cps = []
        for src, dst in ((kst, kwin_hbm), (vst, vwin_hbm)):
            cps += [pltpu.make_async_copy(src.at[pl.ds(j * SUBLANES, SUBLANES), :],
                                          dst.at[:, j, :], sem_st) for j in range(GROUPS)]
        return cps

    @pl.when(step == 0)
    def _init():
        for cp in x_copies(0, 0):
            cp.start()
        kprev[...] = jnp.zeros(kprev.shape, bf16)
        vprev[...] = jnp.zeros(vprev.shape, bf16)
        xr_tail[...] = jnp.zeros(xr_tail.shape, f32)
        up_tail[...] = jnp.zeros(up_tail.shape, f32)
        hcar[...] = jnp.zeros(hcar.shape, f32)
        hff_s[...] = jnp.zeros(hff_s.shape, bf16)
        ybuf[2] = jnp.zeros(ybuf.shape[1:], f32)
        rq = lax.broadcasted_iota(jnp.int32, (SUB_ROWS, 2 * SUB_ROWS), 0)
        ck = lax.broadcasted_iota(jnp.int32, (SUB_ROWS, 2 * SUB_ROWS), 1)
        rk = ck & (SUB_ROWS - 1)
        tq = (rq % SUBLANES) * GROUPS + rq // SUBLANES
        tk = (rk % SUBLANES) * GROUPS + rk // SUBLANES
        dist = tq - tk + jnp.where(ck < SUB_ROWS, WINDOW, 0)
        base = jnp.where((dist >= 0) & (dist < WINDOW), dist.astype(f32), jnp.inf)
        for h in range(N_HEADS):
            bias_s[h] = ALIBI_SLOPES[h] * base

    @pl.when(step + 1 < nblocks)
    def _prefetch():
        for cp in x_copies(step + 1, 1 - slot):
            cp.start()

    @pl.when(step >= 3)
    def _free_ybuf():
        for cp in y_copies(0, yslot):
            cp.wait()

    @pl.when(step < nblocks)
    def _wait_x():
        for cp in x_copies(step, slot):
            cp.wait()

    sh1 = mod_ref[0:1, 0 * D_MODEL:1 * D_MODEL]
    sc1 = mod_ref[0:1, 1 * D_MODEL:2 * D_MODEL]
    g1 = mod_ref[0:1, 2 * D_MODEL:3 * D_MODEL]
    sh2 = mod_ref[0:1, 3 * D_MODEL:4 * D_MODEL]
    sc2 = mod_ref[0:1, 4 * D_MODEL:5 * D_MODEL]
    g2 = mod_ref[0:1, 5 * D_MODEL:6 * D_MODEL]
    gs1 = g_pre_mix_ref[...] * (1.0 + sc1)
    gs2 = g_pre_ffn_ref[...] * (1.0 + sc2)
    sp = _softplus_neg(lam_ref[...])
    first_pen = jnp.where(step == 0, jnp.inf, 0.0)
    sub_iota = lax.broadcasted_iota(jnp.int32, (SUBLANES, D_RNN), 0)
    sub0_rnn = sub_iota == 0
    sub0_ffn = lax.broadcasted_iota(jnp.int32, (SUBLANES, FFN_CHUNK), 0) == 0
    conv_w = [conv_w_ref.at[jj:jj + 1, :] for jj in range(RNN_CONV_W)]

    sts = [dict() for _ in range(NSUB)]

    def rows(s):
        return pl.ds(s * SUB_ROWS, SUB_ROWS)

    def stage_pre(s):
        x = xbuf[slot, rows(s), :]
        sts[s]["hmix"] = (x * _rms_scale(x) * gs1 + sh1).astype(bf16)

    def stage_inproj(s):
        st = sts[s]
        hm = st["hmix"]
        st["q"] = (jnp.dot(hm, w_in_ref[:, 0:C_K], preferred_element_type=f32) * Q_SCALE).astype(bf16)
        kv = jnp.dot(hm, w_in_ref[:, C_K:C_XR], preferred_element_type=f32)
        st["kb"] = kv[:, 0:KV_WIDTH].astype(bf16)
        st["vb"] = kv[:, KV_WIDTH:2 * KV_WIDTH].astype(bf16)
        st["kv"] = kv
        st["xr"] = jnp.dot(hm, w_in_ref[:, C_XR:C_YR], preferred_element_type=f32)

    def stage_attn(s):
        st = sts[s]
        if s == 0:
            kp, vp = kprev[...], vprev[...]
        else:
            kp, vp = sts[s - 1]["kb"], sts[s - 1]["vb"]
        kw = jnp.concatenate([kp, st["kb"]], axis=0)
        vw = jnp.concatenate([vp, st["vb"]], axis=0)
        outs = []
        for h in range(N_HEADS):
            c = h // GQA_GROUP
            hs = slice(h * HEAD_DIM, (h + 1) * HEAD_DIM)
            cs = slice(c * HEAD_DIM, (c + 1) * HEAD_DIM)
            sc = _dot_nt(st["q"][:, hs], kw[:, cs])
            if s == 0:
                sc = jnp.concatenate([sc[:, 0:SUB_ROWS] - first_pen, sc[:, SUB_ROWS:]], axis=1)
            sc = sc - bias_s[h]
            sink = sinks_ref[h]
            m = jnp.maximum(jnp.max(sc, axis=-1, keepdims=True), sink)
            p = jnp.exp(sc - m)
            denom = jnp.sum(p, axis=-1, keepdims=True) + jnp.exp(sink - m)
            outs.append(jnp.dot(p.astype(bf16), vw[:, cs], preferred_element_type=f32) / denom)
        attn = jnp.concatenate(outs, axis=1)
        st["attn_n"] = _rms_norm(attn, g_attn_ref[...]).astype(bf16)

    def stage_rnn(s):
        st = sts[s]
        xr = st["xr"]
        X = _groups(xr)
        if s == 0:
            tail = lambda j: xr_tail[(j - (GROUPS - RNN_CONV_W + 1)) * SUBLANES + SUBLANES - 1:
                                     (j - (GROUPS - RNN_CONV_W + 1)) * SUBLANES + SUBLANES, :]
        else:
            pxr = sts[s - 1]["xr"]
            tail = lambda j: pxr[j * SUBLANES + SUBLANES - 1:(j + 1) * SUBLANES, :]
        sh = _shifted_groups(X, tail, sub0_rnn, RNN_CONV_W - 1)
        xcs = []
        for j in range(GROUPS):
            acc = conv_b_ref[...] + conv_w[RNN_CONV_W - 1][...] * X[j]
            for d in range(1, RNN_CONV_W):
                acc = acc + conv_w[RNN_CONV_W - 1 - d][...] * sh[d][j]
            xcs.append(acc)
        xc = jnp.concatenate(xcs, axis=0)
        a_parts, u_parts = _rglru_gates(xc, wa_ref, ba_ref[...], wi_ref, bi_ref[...], sp)
        a = jnp.concatenate(a_parts, axis=1)
        u = jnp.concatenate(u_parts, axis=1)
        h_in = hcar[...] if s == 0 else sts[s - 1]["h_out"]
        h, st["h_out"] = _scan_sub(a, u, h_in, sub_iota)
        yr = jnp.dot(st["hmix"], w_in_ref[:, C_YR:IN_COLS], preferred_element_type=f32)
        rnn = h * _gelu_tanh(yr)
        st["rnn_n"] = _rms_norm(rnn, g_rnn_ref[...]).astype(bf16)

    def stage_out(s):
        st = sts[s]
        mo = (jnp.dot(st["attn_n"], w_out_ref[0:ATTN_WIDTH, 0:D_MODEL], preferred_element_type=f32)
              + jnp.dot(st["rnn_n"], w_out_ref[ATTN_WIDTH:ATTN_WIDTH + D_RNN, 0:D_MODEL],
                        preferred_element_type=f32))
        x1 = xbuf[slot, rows(s), :] + g1 * _rms_norm(mo, g_post_mix_ref[...])
        ybuf[yslot, rows(s), :] = x1
        hff_s[slot, rows(s), :] = (x1 * _rms_scale(x1) * gs2 + sh2).astype(bf16)

    tail_g0 = GROUPS - (FFN_CONV_W - 1)
    hff = hff_s[1 - slot]
    acts, tails = [], {}

    def conv_chunk(c0):
        c1 = c0 + FFN_CHUNK
        up = jnp.dot(hff, w_up_ref[:, c0:c1], preferred_element_type=f32)
        w = [fconv_w_ref.at[jj:jj + 1, c0:c1] for jj in range(FFN_CONV_W)]
        outs = []
        for s in range(NSUB):
            r0 = s * SUB_ROWS
            X = [up[r0 + j * SUBLANES:r0 + (j + 1) * SUBLANES, :] for j in range(GROUPS)]
            if s == 0:
                tail = lambda j: up_tail[(j - tail_g0) * SUBLANES + SUBLANES - 1:
                                         (j - tail_g0 + 1) * SUBLANES, c0:c1]
            else:
                tail = lambda j, p0=r0 - SUB_ROWS: up[p0 + j * SUBLANES + SUBLANES - 1:
                                                      p0 + (j + 1) * SUBLANES, :]
            sh = _shifted_groups(X, tail, sub0_ffn, FFN_CONV_W - 1)
            for j in range(GROUPS):
                acc = fconv_b_ref[:, c0:c1] + w[FFN_CONV_W - 1][...] * X[j]
                for d in range(1, FFN_CONV_W):
                    acc = acc + w[FFN_CONV_W - 1 - d][...] * sh[d][j]
                outs.append(acc)
        last0 = (NSUB - 1) * SUB_ROWS
        for t in range(FFN_CONV_W - 1):
            r = last0 + (tail_g0 + t) * SUBLANES + SUBLANES - 1
            ffnst_ref[t:t + 1, c0:c1] = up[r:r + 1, :]
        tails[c0] = up[last0 + tail_g0 * SUBLANES:, :]
        return jnp.concatenate(outs, axis=0)

    def ffn_chunk(cc):
        gate = conv_chunk(cc * FFN_CHUNK)
        val = conv_chunk(D_FF + cc * FFN_CHUNK)
        acts.append((_gelu_tanh(gate) * val).astype(bf16))

    def ffn_finish():
        for c0, tl in tails.items():
            up_tail[:, c0:c0 + FFN_CHUNK] = tl
        act = jnp.concatenate(acts, axis=1)
        f = jnp.dot(act, w_down_ref[:, 0:D_MODEL], preferred_element_type=f32)
        ybuf[fslot] = ybuf[fslot] + g2 * _rms_norm(f, g_post_ffn_ref[...])

    stages = (stage_pre, stage_inproj, stage_attn, stage_rnn, stage_out)
    n_times = len(stages) + (NSUB - 1) * STAGE_SKEW
    n_chunks = D_FF // FFN_CHUNK
    ffn_chunk(0)
    nxt = 1
    for t in range(n_times):
        for s in range(NSUB):
            k = t - s * STAGE_SKEW
            if 0 <= k < len(stages):
                stages[k](s)
                if nxt < n_chunks:
                    ffn_chunk(nxt)
                    nxt += 1
    while nxt < n_chunks:
        ffn_chunk(nxt)
        nxt += 1
    ffn_finish()

    lst = sts[NSUB - 1]
    kprev[...] = lst["kb"]
    vprev[...] = lst["vb"]
    xr_tail[...] = lst["xr"][(GROUPS - (RNN_CONV_W - 1)) * SUBLANES:, :]
    hcar[...] = lst["h_out"]

    @pl.when(step < nblocks)
    def _state():
        kst[...] = lst["kv"][:, 0:KV_WIDTH]
        vst[...] = lst["kv"][:, KV_WIDTH:2 * KV_WIDTH]
        for t in range(RNN_CONV_W - 1):
            r = (GROUPS - (RNN_CONV_W - 1) + t) * SUBLANES + SUBLANES - 1
            convst_ref[t:t + 1, :] = lst["xr"][r:r + 1, :]
        hlast_ref[...] = lst["h_out"][0:1, :]

    @pl.when(step >= 1)
    def _store_y():
        for cp in y_copies(step - 1, fslot):
            cp.start()

    @pl.when(step == nblocks)
    def _finish():
        for cp in state_copies():
            cp.start()
        for cp in state_copies():
            cp.wait()
        for cp in y_copies(0, lax.rem(step + 1, 3)) + y_copies(0, fslot):
            cp.wait()


def _const_spec(shape):
    nd = len(shape)
    return pl.BlockSpec(shape, lambda i: (0,) * nd)


def _prompt_call(x, mod, mod_row0, sinks, params):
    T = x.shape[0]
    TB = TOKEN_BLOCK
    assert T // TB >= 3
    (g_pre_mix, w_in, conv_w, conv_b, wa, ba, wi, bi, lam, g_attn, g_rnn, w_out,
     g_post_mix, g_pre_ffn, w_up, fconv_w, fconv_b, w_down, g_post_ffn) = params
    x4 = x.reshape(T // SUB_ROWS, SUBLANES, GROUPS, D_MODEL)
    ins = [x4, mod, sinks, g_pre_mix, w_in, conv_w, conv_b, wa, ba, wi, bi, lam, g_attn, g_rnn,
           w_out, g_post_mix, g_pre_ffn, w_up, fconv_w, fconv_b, w_down, g_post_ffn]
    assert mod_row0 % MOD_PAD_ROWS == 0
    in_specs = [pl.BlockSpec(memory_space=pl.ANY),
                pl.BlockSpec((MOD_PAD_ROWS, mod.shape[1]), lambda i: (mod_row0 // MOD_PAD_ROWS, 0)),
                pl.BlockSpec(memory_space=pltpu.SMEM)]
    in_specs += [_const_spec(a.shape) for a in ins[3:]]
    out_shape = (
        jax.ShapeDtypeStruct(x4.shape, f32),
        jax.ShapeDtypeStruct((SUBLANES, GROUPS, KV_WIDTH), f32),
        jax.ShapeDtypeStruct((SUBLANES, GROUPS, KV_WIDTH), f32),
        jax.ShapeDtypeStruct((1, D_RNN), f32),
        jax.ShapeDtypeStruct((RNN_CONV_W - 1, D_RNN), f32),
        jax.ShapeDtypeStruct((FFN_CONV_W - 1, 2 * D_FF), f32),
    )
    out_specs = (
        pl.BlockSpec(memory_space=pl.ANY),
        pl.BlockSpec(memory_space=pl.ANY),
        pl.BlockSpec(memory_space=pl.ANY),
        _const_spec((1, D_RNN)),
        _const_spec((RNN_CONV_W - 1, D_RNN)),
        _const_spec((FFN_CONV_W - 1, 2 * D_FF)),
    )
    scratch = [
        pltpu.VMEM((2, TB, D_MODEL), f32),
        pltpu.VMEM((3, TB, D_MODEL), f32),
        pltpu.SemaphoreType.DMA((2,)),
        pltpu.SemaphoreType.DMA((3,)),
        pltpu.SemaphoreType.DMA(()),
        pltpu.VMEM((SUB_ROWS, KV_WIDTH), bf16),
        pltpu.VMEM((SUB_ROWS, KV_WIDTH), bf16),
        pltpu.VMEM((SUB_ROWS, KV_WIDTH), f32),
        pltpu.VMEM((SUB_ROWS, KV_WIDTH), f32),
        pltpu.VMEM(((RNN_CONV_W - 1) * SUBLANES, D_RNN), f32),
        pltpu.VMEM(((FFN_CONV_W - 1) * SUBLANES, 2 * D_FF), f32),
        pltpu.VMEM((SUBLANES, D_RNN), f32),
        pltpu.VMEM((N_HEADS, SUB_ROWS, 2 * SUB_ROWS), f32),
        pltpu.VMEM((2, TOKEN_BLOCK, D_MODEL), bf16),
    ]
    y4, kp, vp, hp, convp, ffnp = pl.pallas_call(
        _prompt_kernel,
        grid=(T // TB + 1,),
        in_specs=in_specs,
        out_specs=out_specs,
        out_shape=out_shape,
        scratch_shapes=scratch,
        compiler_params=pltpu.CompilerParams(
            dimension_semantics=("arbitrary",), vmem_limit_bytes=VMEM_LIMIT_BYTES),
        name="prompt_layer",
    )(*ins)
    return (y4.reshape(T, D_MODEL), kp.reshape(WINDOW, KV_WIDTH), vp.reshape(WINDOW, KV_WIDTH),
            hp, convp, ffnp)


def _sample_pre_kernel(x_ref, mod_ref, g_pre_mix_ref, w_in_ref, q_ref, kv_ref, xr_ref, yr_ref):
    B = x_ref.shape[0]
    sh1 = mod_ref[0:B, 0 * D_MODEL:1 * D_MODEL]
    sc1 = mod_ref[0:B, 1 * D_MODEL:2 * D_MODEL]
    hmix = (_rms_norm(x_ref[:, 0, :], g_pre_mix_ref[...]) * (1.0 + sc1) + sh1).astype(bf16)
    q = jnp.dot(hmix, w_in_ref[:, 0:C_K], preferred_element_type=f32) * Q_SCALE
    for h in range(N_HEADS):
        q_ref[:, h, :] = q[:, h * HEAD_DIM:(h + 1) * HEAD_DIM]
    kv_ref[...] = jnp.dot(hmix, w_in_ref[:, C_K:C_XR], preferred_element_type=f32)
    xr_ref[...] = jnp.dot(hmix, w_in_ref[:, C_XR:C_YR], preferred_element_type=f32)
    yr_ref[...] = jnp.dot(hmix, w_in_ref[:, C_YR:IN_COLS], preferred_element_type=f32)


def _sample_pre_call(x, mod_s, g_pre_mix, w_in):
    B = x.shape[0]
    return pl.pallas_call(
        _sample_pre_kernel,
        out_shape=(
            jax.ShapeDtypeStruct((B, N_HEADS, HEAD_DIM), f32),
            jax.ShapeDtypeStruct((B, 2 * KV_WIDTH), f32),
            jax.ShapeDtypeStruct((B, D_RNN), f32),
            jax.ShapeDtypeStruct((B, D_RNN), f32),
        ),
        compiler_params=pltpu.CompilerParams(vmem_limit_bytes=VMEM_LIMIT_BYTES),
        name="sample_pre",
    )(x, mod_s, g_pre_mix, w_in)


def _sample_attn_kernel(q_ref, kv_ref, ck_ref, cv_ref, sinks_ref, o_ref, kwin_ref, vwin_ref):
    R = N_KV_HEADS * WINDOW
    hrow = lax.broadcasted_iota(jnp.int32, (N_HEADS, R), 0)
    rcol = lax.broadcasted_iota(jnp.int32, (N_HEADS, R), 1)
    slope = jnp.exp2(-8.0 * (hrow + 1).astype(f32) / N_HEADS)
    own = (rcol % N_KV_HEADS) == (hrow // GQA_GROUP)
    bias = jnp.where(own, slope * (WINDOW - 1 - rcol // N_KV_HEADS).astype(f32), jnp.inf)
    wrow = lax.broadcasted_iota(jnp.int32, (R, HEAD_DIM), 0)
    sink = sinks_ref[...]

    def shifted(cache, new_rows):
        out = pltpu.roll(cache, R - N_KV_HEADS, axis=0)
        for c in range(N_KV_HEADS):
            out = jnp.where(wrow == R - N_KV_HEADS + c, new_rows[c], out)
        return out

    scores = []
    for b in range(SAMPLE_CHUNK):
        knew = [kv_ref[b:b + 1, c * HEAD_DIM:(c + 1) * HEAD_DIM] for c in range(N_KV_HEADS)]
        kw = shifted(ck_ref[b], knew)
        kwin_ref[b] = kw
        scores.append(_dot_nt(q_ref[b].astype(bf16), kw.astype(bf16)) - bias)
    probs, denoms = [], []
    for b in range(SAMPLE_CHUNK):
        s = scores[b]
        m = jnp.maximum(jnp.max(s, axis=-1, keepdims=True), sink)
        p = jnp.exp(s - m)
        denoms.append(jnp.sum(p, axis=-1, keepdims=True) + jnp.exp(sink - m))
        probs.append(p.astype(bf16))
    for b in range(SAMPLE_CHUNK):
        vnew = [kv_ref[b:b + 1, KV_WIDTH + c * HEAD_DIM:KV_WIDTH + (c + 1) * HEAD_DIM]
                for c in range(N_KV_HEADS)]
        vw = shifted(cv_ref[b], vnew)
        vwin_ref[b] = vw
        o_ref[b] = (jnp.dot(probs[b], vw.astype(bf16), preferred_element_type=f32)
                    / denoms[b])


def _sample_attn_call(q, kv, ck, cv, sinks_col):
    B = q.shape[0]
    BC = SAMPLE_CHUNK
    return pl.pallas_call(
        _sample_attn_kernel,
        grid=(B // BC,),
        in_specs=[
            pl.BlockSpec((BC, N_HEADS, HEAD_DIM), lambda i: (i, 0, 0)),
            pl.BlockSpec((BC, 2 * KV_WIDTH), lambda i: (i, 0)),
            pl.BlockSpec((BC, N_KV_HEADS * WINDOW, HEAD_DIM), lambda i: (i, 0, 0)),
            pl.BlockSpec((BC, N_KV_HEADS * WINDOW, HEAD_DIM), lambda i: (i, 0, 0)),
            pl.BlockSpec((N_HEADS, 1), lambda i: (0, 0)),
        ],
        out_specs=(
            pl.BlockSpec((BC, N_HEADS, HEAD_DIM), lambda i: (i, 0, 0)),
            pl.BlockSpec((BC, N_KV_HEADS * WINDOW, HEAD_DIM), lambda i: (i, 0, 0)),
            pl.BlockSpec((BC, N_KV_HEADS * WINDOW, HEAD_DIM), lambda i: (i, 0, 0)),
        ),
        out_shape=(
            jax.ShapeDtypeStruct((B, N_HEADS, HEAD_DIM), f32),
            jax.ShapeDtypeStruct((B, N_KV_HEADS * WINDOW, HEAD_DIM), f32),
            jax.ShapeDtypeStruct((B, N_KV_HEADS * WINDOW, HEAD_DIM), f32),
        ),
        compiler_params=pltpu.CompilerParams(
            dimension_semantics=("arbitrary",), vmem_limit_bytes=SAMPLE_ATTN_VMEM_BYTES),
        name="sample_attn",
    )(q, kv, ck, cv, sinks_col)


def _sample_post_kernel(
    x_ref, mod_ref, attn_ref, xr_ref, yr_ref, h0_ref, cbuf_ref, fbuf_ref,
    conv_w_ref, conv_b_ref, wa_ref, ba_ref, wi_ref, bi_ref, lam_ref,
    g_attn_ref, g_rnn_ref, w_out_ref, g_post_mix_ref, g_pre_ffn_ref, w_up_ref,
    fconv_w_ref, fconv_b_ref, w_down_ref, g_post_ffn_ref,
    y_ref, h_ref, cst_ref, fst_ref,
):
    B = x_ref.shape[0]
    g1 = mod_ref[0:B, 2 * D_MODEL:3 * D_MODEL]
    sh2 = mod_ref[0:B, 3 * D_MODEL:4 * D_MODEL]
    sc2 = mod_ref[0:B, 4 * D_MODEL:5 * D_MODEL]
    g2 = mod_ref[0:B, 5 * D_MODEL:6 * D_MODEL]
    x = x_ref[:, 0, :]
    xr = xr_ref[...]

    xc = conv_b_ref[...] + conv_w_ref[RNN_CONV_W - 1:RNN_CONV_W, :] * xr
    for jj in range(RNN_CONV_W - 1):
        xc = xc + conv_w_ref[jj:jj + 1, :] * cbuf_ref[jj]
    for jj in range(RNN_CONV_W - 2):
        cst_ref[jj] = cbuf_ref[jj + 1]
    cst_ref[RNN_CONV_W - 2] = xr

    sp = _softplus_neg(lam_ref[...])
    a_parts, u_parts = _rglru_gates(xc, wa_ref, ba_ref[...], wi_ref, bi_ref[...], sp)
    a = jnp.concatenate(a_parts, axis=1)
    u = jnp.concatenate(u_parts, axis=1)
    h = a * h0_ref[...] + u
    h_ref[...] = h
    rnn = h * _gelu_tanh(yr_ref[...])

    attn = jnp.concatenate([attn_ref[:, h, :] for h in range(N_HEADS)], axis=1)
    attn_n = _rms_norm(attn, g_attn_ref[...])
    rnn_n = _rms_norm(rnn, g_rnn_ref[...])
    mo = (_dot(attn_n, w_out_ref[0:ATTN_WIDTH, 0:D_MODEL])
          + _dot(rnn_n, w_out_ref[ATTN_WIDTH:ATTN_WIDTH + D_RNN, 0:D_MODEL]))
    x1 = x + g1 * _rms_norm(mo, g_post_mix_ref[...])

    hff = (_rms_norm(x1, g_pre_ffn_ref[...]) * (1.0 + sc2) + sh2).astype(bf16)
    up_pre = jnp.dot(hff, w_up_ref[...], preferred_element_type=f32)
    up = fconv_b_ref[...] + fconv_w_ref[FFN_CONV_W - 1:FFN_CONV_W, :] * up_pre
    for jj in range(FFN_CONV_W - 1):
        up = up + fconv_w_ref[jj:jj + 1, :] * fbuf_ref[jj]
    for jj in range(FFN_CONV_W - 2):
        fst_ref[jj] = fbuf_ref[jj + 1]
    fst_ref[FFN_CONV_W - 2] = up_pre

    act = (_gelu_tanh(up[:, 0:D_FF]) * up[:, D_FF:2 * D_FF]).astype(bf16)
    f = jnp.dot(act, w_down_ref[:, 0:D_MODEL], preferred_element_type=f32)
    y_ref[:, 0, :] = x1 + g2 * _rms_norm(f, g_post_ffn_ref[...])


def _sample_post_call(x, mod_s, attn, xr, yr, h0, cbuf, fbuf, params):
    B = x.shape[0]
    (_, _, conv_w, conv_b, wa, ba, wi, bi, lam, g_attn, g_rnn, w_out,
     g_post_mix, g_pre_ffn, w_up, fconv_w, fconv_b, w_down, g_post_ffn) = params
    return pl.pallas_call(
        _sample_post_kernel,
        out_shape=(
            jax.ShapeDtypeStruct((B, 1, D_MODEL), f32),
            jax.ShapeDtypeStruct((B, D_RNN), f32),
            jax.ShapeDtypeStruct((RNN_CONV_W - 1, B, D_RNN), f32),
            jax.ShapeDtypeStruct((FFN_CONV_W - 1, B, 2 * D_FF), f32),
        ),
        compiler_params=pltpu.CompilerParams(vmem_limit_bytes=VMEM_LIMIT_BYTES),
        name="sample_post",
    )(x, mod_s, attn, xr, yr, h0, cbuf, fbuf, conv_w, conv_b, wa, ba, wi, bi, lam,
      g_attn, g_rnn, w_out, g_post_mix, g_pre_ffn, w_up, fconv_w, fconv_b, w_down, g_post_ffn)


def kernel(x_prompt, x_sample, cache_k, cache_v, state_h, state_conv, state_ffn_conv, c_prompt, c_sample, w_ada, b_ada, g_pre_mix, w_in, conv_w, conv_b, w_a, b_a, w_i, b_i, lam, sinks, g_attn_out, g_rnn_out, w_out, g_post_mix, g_pre_ffn, w_up, ffn_conv_w, ffn_conv_b, w_down, g_post_ffn):
    depth = w_in.shape[0]
    assert depth == 1 and x_prompt.shape[0] == 1 and x_sample.shape[1] == 1
    T = x_prompt.shape[1]
    B = x_sample.shape[0]
    W = cache_k.shape[2]
    assert W == WINDOW and T % TOKEN_BLOCK == 0 and B % SAMPLE_CHUNK == 0

    row = lambda a: a[0].reshape(1, -1)
    pad_cols = lambda w: jnp.concatenate(
        [w.astype(bf16), jnp.zeros((w.shape[0], PADDED_OUT_COLS - w.shape[1]), bf16)], axis=1)
    params = (
        row(g_pre_mix), w_in[0].astype(bf16), conv_w[0], row(conv_b),
        w_a[0].astype(bf16), row(b_a), w_i[0].astype(bf16), row(b_i), row(lam),
        row(g_attn_out), row(g_rnn_out), pad_cols(w_out[0]), row(g_post_mix),
        row(g_pre_ffn),
        w_up[0].astype(bf16), ffn_conv_w[0], row(ffn_conv_b), pad_cols(w_down[0]),
        row(g_post_ffn),
    )

    c_all = jnp.concatenate(
        [c_sample, jnp.broadcast_to(c_prompt, (MOD_PAD_ROWS, D_MODEL))], axis=0)
    mod = _mod_call(c_all, w_ada[0], b_ada[0].reshape(1, -1))
    mod_s = mod

    yp, kp, vp, hp, convp, ffnp = _prompt_call(x_prompt[0], mod, B, sinks[0], params)

    xs = x_sample
    q, kv, xr, yr = _sample_pre_call(xs, mod_s, params[0], params[1])
    ck = cache_k.reshape(B, W * N_KV_HEADS, HEAD_DIM)
    cv = cache_v.reshape(B, W * N_KV_HEADS, HEAD_DIM)
    attn3, kwin, vwin = _sample_attn_call(
        q, kv, ck, cv, sinks[0].reshape(N_HEADS, 1))
    attn = attn3
    ys, hs, convs, ffns = _sample_post_call(
        xs, mod_s, attn, xr, yr, state_h[0],
        jnp.transpose(state_conv[0], (1, 0, 2)), jnp.transpose(state_ffn_conv[0], (1, 0, 2)),
        params)
    convs = jnp.transpose(convs, (1, 0, 2))
    ffns = jnp.transpose(ffns, (1, 0, 2))

    kv_shape = (1, 1, W, N_KV_HEADS, HEAD_DIM)
    kvs_shape = (1, B, W, N_KV_HEADS, HEAD_DIM)
    return (
        yp[None], ys,
        kp.reshape(kv_shape), vp.reshape(kv_shape), hp[None], convp[None, None], ffnp[None, None],
        kwin.reshape(kvs_shape), vwin.reshape(kvs_shape), hs[None], convs[None], ffns[None],
    )
```

```python
import math

import jax
import jax.numpy as jnp
from jax import lax
from jax.experimental import pallas as pl
from jax.experimental.pallas import tpu as pltpu

D_MODEL = 1024
N_HEADS = 8
N_KV_HEADS = 2
HEAD_DIM = 128
GQA_GROUP = N_HEADS // N_KV_HEADS
ATTN_WIDTH = N_HEADS * HEAD_DIM
KV_WIDTH = N_KV_HEADS * HEAD_DIM
WINDOW = 128
D_RNN = D_MODEL
RNN_BLOCKS = 8
RNN_BLOCK_W = D_RNN // RNN_BLOCKS
RG_C = 8.0
RNN_CONV_W = 4
D_FF = 2816
FFN_CONV_W = 3
RMS_EPS = 1e-6

C_K = ATTN_WIDTH
C_V = C_K + KV_WIDTH
C_XR = C_V + KV_WIDTH
C_YR = C_XR + D_RNN
IN_COLS = C_YR + D_RNN

SUBLANES = 8
SUB_ROWS = WINDOW
GROUPS = SUB_ROWS // SUBLANES
TOKEN_BLOCK = 256
NSUB = TOKEN_BLOCK // SUB_ROWS
FFN_CHUNK = 256
STAGE_SKEW = 3
SAMPLE_CHUNK = 16
SAMPLE_ATTN_VMEM_BYTES = (2 * 2 * 2 * SAMPLE_CHUNK * N_KV_HEADS * WINDOW * HEAD_DIM * 4
                          + 8 * 1024 * 1024)
MOD_PAD_ROWS = 8
VMEM_LIMIT_BYTES = 56 * 1024 * 1024

ALIBI_SLOPES = tuple(2.0 ** (-8.0 * (h + 1) / N_HEADS) for h in range(N_HEADS))
Q_SCALE = HEAD_DIM ** -0.5
SQRT_2_OVER_PI = math.sqrt(2.0 / math.pi)
LOG2_E = 1.0 / math.log(2.0)
PADDED_OUT_COLS = D_MODEL + 128

bf16 = jnp.bfloat16
f32 = jnp.float32


def _rms_scale(x):
    return lax.rsqrt(jnp.mean(x * x, axis=-1, keepdims=True) + RMS_EPS)


def _rms_norm(x, g):
    return x * _rms_scale(x) * g


def _gelu_tanh(x):
    k1 = -2.0 * LOG2_E * SQRT_2_OVER_PI
    k3 = k1 * 0.044715
    return x / (1.0 + jnp.exp2(x * (k1 + k3 * (x * x))))


def _sigmoid(x):
    return 1.0 / (1.0 + jnp.exp2(x * (-LOG2_E)))


def _softplus_neg(lam):
    return jnp.maximum(-lam, 0.0) + jnp.log1p(jnp.exp(-jnp.abs(lam)))


def _dot(a, b):
    return jnp.dot(a.astype(bf16), b, preferred_element_type=f32)


def _dot_nt(a, b):
    return lax.dot_general(a, b, (((1,), (1,)), ((), ())), preferred_element_type=f32)


def _rglru_gates(xc, wa_ref, ba, wi_ref, bi, sp):
    a_parts, u_parts = [], []
    for n in range(RNN_BLOCKS):
        sl = slice(n * RNN_BLOCK_W, (n + 1) * RNN_BLOCK_W)
        xn = xc[:, sl]
        xb = xn.astype(bf16)
        r = _sigmoid(jnp.dot(xb, wa_ref[n], preferred_element_type=f32) + ba[:, sl])
        i = _sigmoid(jnp.dot(xb, wi_ref[n], preferred_element_type=f32) + bi[:, sl])
        log_a = (-RG_C) * r * sp[:, sl]
        t = jnp.tanh(log_a)
        one_minus_a2 = (-2.0 * t) / (1.0 - t)
        a_parts.append(jnp.exp(log_a))
        u_parts.append(jnp.sqrt(one_minus_a2) * (i * xn))
    return a_parts, u_parts


def _mod_kernel(c_ref, w_ref, b_ref, o_ref):
    c = c_ref[...]
    s = c * _sigmoid(c)
    o_ref[...] = _dot(s, w_ref[...].astype(bf16)) + b_ref[...]


def _mod_call(c_all, w_ada, b_ada):
    rows = c_all.shape[0]
    ncol = w_ada.shape[1]
    bn = D_MODEL
    return pl.pallas_call(
        _mod_kernel,
        grid=(ncol // bn,),
        in_specs=[
            pl.BlockSpec((rows, D_MODEL), lambda j: (0, 0)),
            pl.BlockSpec((D_MODEL, bn), lambda j: (0, j)),
            pl.BlockSpec((1, bn), lambda j: (0, j)),
        ],
        out_specs=pl.BlockSpec((rows, bn), lambda j: (0, j)),
        out_shape=jax.ShapeDtypeStruct((rows, ncol), f32),
        compiler_params=pltpu.CompilerParams(dimension_semantics=("arbitrary",)),
        name="adaln_mod",
    )(c_all, w_ada, b_ada)


def _groups(v, c0=None, c1=None):
    if c0 is None:
        return [v[j * SUBLANES:(j + 1) * SUBLANES, :] for j in range(GROUPS)]
    return [v[j * SUBLANES:(j + 1) * SUBLANES, c0:c1] for j in range(GROUPS)]


def _shifted_groups(X, tail_row, sub0, max_shift):
    wrapped = {}
    for j in range(GROUPS - max_shift, GROUPS):
        wrapped[j] = jnp.where(sub0, tail_row(j), pltpu.roll(X[j], 1, axis=0))
    sh = {}
    for d in range(1, max_shift + 1):
        sh[d] = [X[j - d] if j >= d else wrapped[j - d + GROUPS] for j in range(GROUPS)]
    return sh


def _scan_sub(a, u, h_in, sub_iota):
    A = _groups(a)
    L = _groups(u)
    for j in range(1, GROUPS):
        L[j] = A[j] * L[j - 1] + L[j]
        A[j] = A[j] * A[j - 1]
    ae, le = A[GROUPS - 1], L[GROUPS - 1]
    for s_ in (1, 2, 4):
        ok = sub_iota >= s_
        a_sh = pltpu.roll(ae, s_, axis=0)
        l_sh = pltpu.roll(le, s_, axis=0)
        le = jnp.where(ok, ae * l_sh + le, le)
        ae = jnp.where(ok, ae * a_sh, ae)
    hend = le + ae * h_in
    hprev = jnp.where(sub_iota == 0, h_in, pltpu.roll(hend, 1, axis=0))
    h = jnp.concatenate([L[j] + A[j] * hprev for j in range(GROUPS)], axis=0)
    h_out = jnp.broadcast_to(hend[SUBLANES - 1:SUBLANES, :], hend.shape)
    return h, h_out


def _prompt_kernel(
    x_hbm, mod_ref, sinks_ref,
    g_pre_mix_ref, w_in_ref, conv_w_ref, conv_b_ref, wa_ref, ba_ref, wi_ref, bi_ref, lam_ref,
    g_attn_ref, g_rnn_ref, w_out_ref, g_post_mix_ref, g_pre_ffn_ref, w_up_ref,
    fconv_w_ref, fconv_b_ref, w_down_ref, g_post_ffn_ref,
    y_hbm, kwin_hbm, vwin_hbm, hlast_ref, convst_ref, ffnst_ref,
    xbuf, ybuf, sem_in, sem_out, sem_st, kprev, vprev, kst, vst, xr_tail, up_tail, hcar, bias_s,
    hff_s,
):
    step = pl.program_id(0)
    nblocks = pl.num_programs(0) - 1
    slot = lax.rem(step, 2)
    yslot = lax.rem(step, 3)
    fslot = lax.rem(step + 2, 3)

    def x_copies(st, sl):
        return [pltpu.make_async_copy(
            x_hbm.at[st * NSUB + s, :, j, :],
            xbuf.at[sl, pl.ds(s * SUB_ROWS + j * SUBLANES, SUBLANES), :],
            sem_in.at[sl]) for s in range(NSUB) for j in range(GROUPS)]

    def y_copies(st, sl):
        return [pltpu.make_async_copy(
            ybuf.at[sl, pl.ds(s * SUB_ROWS + j * SUBLANES, SUBLANES), :],
            y_hbm.at[st * NSUB + s, :, j, :],
            sem_out.at[sl]) for s in range(NSUB) for j in range(GROUPS)]

    def state_copies():
        cps = []
        for src, dst in ((kst, kwin_hbm), (vst, vwin_hbm)):
            cps += [pltpu.make_async_copy(src.at[pl.ds(j * SUBLANES, SUBLANES), :],
                                          dst.at[:, j, :], sem_st) for j in range(GROUPS)]
        return cps

    @pl.when(step == 0)
    def _init():
        for cp in x_copies(0, 0):
            cp.start()
        kprev[...] = jnp.zeros(kprev.shape, bf16)
        vprev[...] = jnp.zeros(vprev.shape, bf16)
        xr_tail[...] = jnp.zeros(xr_tail.shape, f32)
        up_tail[...] = jnp.zeros(up_tail.shape, f32)
        hcar[...] = jnp.zeros(hcar.shape, f32)
        hff_s[...] = jnp.zeros(hff_s.shape, bf16)
        ybuf[2] = jnp.zeros(ybuf.shape[1:], f32)
        rq = lax.broadcasted_iota(jnp.int32, (SUB_ROWS, 2 * SUB_ROWS), 0)
        ck = lax.broadcasted_iota(jnp.int32, (SUB_ROWS, 2 * SUB_ROWS), 1)
        rk = ck & (SUB_ROWS - 1)
        tq = (rq % SUBLANES) * GROUPS + rq // SUBLANES
        tk = (rk % SUBLANES) * GROUPS + rk // SUBLANES
        dist = tq - tk + jnp.where(ck < SUB_ROWS, WINDOW, 0)
        base = jnp.where((dist >= 0) & (dist < WINDOW), dist.astype(f32), jnp.inf)
        for h in range(N_HEADS):
            bias_s[h] = ALIBI_SLOPES[h] * base

    @pl.when(step + 1 < nblocks)
    def _prefetch():
        for cp in x_copies(step + 1, 1 - slot):
            cp.start()

    @pl.when(step >= 3)
    def _free_ybuf():
        for cp in y_copies(0, yslot):
            cp.wait()

    @pl.when(step < nblocks)
    def _wait_x():
        for cp in x_copies(step, slot):
            cp.wait()

    sh1 = mod_ref[0:1, 0 * D_MODEL:1 * D_MODEL]
    sc1 = mod_ref[0:1, 1 * D_MODEL:2 * D_MODEL]
    g1 = mod_ref[0:1, 2 * D_MODEL:3 * D_MODEL]
    sh2 = mod_ref[0:1, 3 * D_MODEL:4 * D_MODEL]
    sc2 = mod_ref[0:1, 4 * D_MODEL:5 * D_MODEL]
    g2 = mod_ref[0:1, 5 * D_MODEL:6 * D_MODEL]
    gs1 = g_pre_mix_ref[...] * (1.0 + sc1)
    gs2 = g_pre_ffn_ref[...] * (1.0 + sc2)
    sp = _softplus_neg(lam_ref[...])
    first_pen = jnp.where(step == 0, jnp.inf, 0.0)
    sub_iota = lax.broadcasted_iota(jnp.int32, (SUBLANES, D_RNN), 0)
    sub0_rnn = sub_iota == 0
    sub0_ffn = lax.broadcasted_iota(jnp.int32, (SUBLANES, FFN_CHUNK), 0) == 0
    conv_w = [conv_w_ref.at[jj:jj + 1, :] for jj in range(RNN_CONV_W)]

    sts = [dict() for _ in range(NSUB)]

    def rows(s):
        return pl.ds(s * SUB_ROWS, SUB_ROWS)

    def stage_pre(s):
        x = xbuf[slot, rows(s), :]
        sts[s]["hmix"] = (x * _rms_scale(x) * gs1 + sh1).astype(bf16)

    def stage_inproj(s):
        st = sts[s]
        hm = st["hmix"]
        st["q"] = (jnp.dot(hm, w_in_ref[:, 0:C_K], preferred_element_type=f32) * Q_SCALE).astype(bf16)
        kv = jnp.dot(hm, w_in_ref[:, C_K:C_XR], preferred_element_type=f32)
        st["kb"] = kv[:, 0:KV_WIDTH].astype(bf16)
        st["vb"] = kv[:, KV_WIDTH:2 * KV_WIDTH].astype(bf16)
        st["kv"] = kv
        st["xr"] = jnp.dot(hm, w_in_ref[:, C_XR:C_YR], preferred_element_type=f32)

    def stage_attn(s):
        st = sts[s]
        if s == 0:
            kp, vp = kprev[...], vprev[...]
        else:
            kp, vp = sts[s - 1]["kb"], sts[s - 1]["vb"]
        kw = jnp.concatenate([kp, st["kb"]], axis=0)
        vw = jnp.concatenate([vp, st["vb"]], axis=0)
        outs = []
        for h in range(N_HEADS):
            c = h // GQA_GROUP
            hs = slice(h * HEAD_DIM, (h + 1) * HEAD_DIM)
            cs = slice(c * HEAD_DIM, (c + 1) * HEAD_DIM)
            sc = _dot_nt(st["q"][:, hs], kw[:, cs])
            if s == 0:
                sc = jnp.concatenate([sc[:, 0:SUB_ROWS] - first_pen, sc[:, SUB_ROWS:]], axis=1)
            sc = sc - bias_s[h]
            sink = sinks_ref[h]
            m = jnp.maximum(jnp.max(sc, axis=-1, keepdims=True), sink)
            p = jnp.exp(sc - m)
            denom = jnp.sum(p, axis=-1, keepdims=True) + jnp.exp(sink - m)
            outs.append(jnp.dot(p.astype(bf16), vw[:, cs], preferred_element_type=f32) / denom)
        attn = jnp.concatenate(outs, axis=1)
        st["attn_n"] = _rms_norm(attn, g_attn_ref[...]).astype(bf16)

    def stage_rnn(s):
        st = sts[s]
        xr = st["xr"]
        X = _groups(xr)
        if s == 0:
            tail = lambda j: xr_tail[(j - (GROUPS - RNN_CONV_W + 1)) * SUBLANES + SUBLANES - 1:
                                     (j - (GROUPS - RNN_CONV_W + 1)) * SUBLANES + SUBLANES, :]
        else:
            pxr = sts[s - 1]["xr"]
            tail = lambda j: pxr[j * SUBLANES + SUBLANES - 1:(j + 1) * SUBLANES, :]
        sh = _shifted_groups(X, tail, sub0_rnn, RNN_CONV_W - 1)
        xcs = []
        for j in range(GROUPS):
            acc = conv_b_ref[...] + conv_w[RNN_CONV_W - 1][...] * X[j]
            for d in range(1, RNN_CONV_W):
                acc = acc + conv_w[RNN_CONV_W - 1 - d][...] * sh[d][j]
            xcs.append(acc)
        xc = jnp.concatenate(xcs, axis=0)
        a_parts, u_parts = _rglru_gates(xc, wa_ref, ba_ref[...], wi_ref, bi_ref[...], sp)
        a = jnp.concatenate(a_parts, axis=1)
        u = jnp.concatenate(u_parts, axis=1)
        h_in = hcar[...] if s == 0 else sts[s - 1]["h_out"]
        h, st["h_out"] = _scan_sub(a, u, h_in, sub_iota)
        yr = jnp.dot(st["hmix"], w_in_ref[:, C_YR:IN_COLS], preferred_element_type=f32)
        rnn = h * _gelu_tanh(yr)
        st["rnn_n"] = _rms_norm(rnn, g_rnn_ref[...]).astype(bf16)

    def stage_out(s):
        st = sts[s]
        mo = (jnp.dot(st["attn_n"], w_out_ref[0:ATTN_WIDTH, 0:D_MODEL], preferred_element_type=f32)
              + jnp.dot(st["rnn_n"], w_out_ref[ATTN_WIDTH:ATTN_WIDTH + D_RNN, 0:D_MODEL],
                        preferred_element_type=f32))
        x1 = xbuf[slot, rows(s), :] + g1 * _rms_norm(mo, g_post_mix_ref[...])
        ybuf[yslot, rows(s), :] = x1
        hff_s[slot, rows(s), :] = (x1 * _rms_scale(x1) * gs2 + sh2).astype(bf16)

    tail_g0 = GROUPS - (FFN_CONV_W - 1)
    hff = hff_s[1 - slot]
    acts, tails = [], {}

    def conv_chunk(c0):
        c1 = c0 + FFN_CHUNK
        up = jnp.dot(hff, w_up_ref[:, c0:c1], preferred_element_type=f32)
        w = [fconv_w_ref.at[jj:jj + 1, c0:c1] for jj in range(FFN_CONV_W)]
        outs = []
        for s in range(NSUB):
            r0 = s * SUB_ROWS
            X = [up[r0 + j * SUBLANES:r0 + (j + 1) * SUBLANES, :] for j in range(GROUPS)]
            if s == 0:
                tail = lambda j: up_tail[(j - tail_g0) * SUBLANES + SUBLANES - 1:
                                         (j - tail_g0 + 1) * SUBLANES, c0:c1]
            else:
                tail = lambda j, p0=r0 - SUB_ROWS: up[p0 + j * SUBLANES + SUBLANES - 1:
                                                      p0 + (j + 1) * SUBLANES, :]
            sh = _shifted_groups(X, tail, sub0_ffn, FFN_CONV_W - 1)
            for j in range(GROUPS):
                acc = fconv_b_ref[:, c0:c1] + w[FFN_CONV_W - 1][...] * X[j]
                for d in range(1, FFN_CONV_W):
                    acc = acc + w[FFN_CONV_W - 1 - d][...] * sh[d][j]
                outs.append(acc)
        last0 = (NSUB - 1) * SUB_ROWS
        for t in range(FFN_CONV_W - 1):
            r = last0 + (tail_g0 + t) * SUBLANES + SUBLANES - 1
            ffnst_ref[t:t + 1, c0:c1] = up[r:r + 1, :]
        tails[c0] = up[last0 + tail_g0 * SUBLANES:, :]
        return jnp.concatenate(outs, axis=0)

    def ffn_chunk(cc):
        gate = conv_chunk(cc * FFN_CHUNK)
        val = conv_chunk(D_FF + cc * FFN_CHUNK)
        acts.append((_gelu_tanh(gate) * val).astype(bf16))

    def ffn_finish():
        for c0, tl in tails.items():
            up_tail[:, c0:c0 + FFN_CHUNK] = tl
        act = jnp.concatenate(acts, axis=1)
        f = jnp.dot(act, w_down_ref[:, 0:D_MODEL], preferred_element_type=f32)
        ybuf[fslot] = ybuf[fslot] + g2 * _rms_norm(f, g_post_ffn_ref[...])

    stages = (stage_pre, stage_inproj, stage_attn, stage_rnn, stage_out)
    n_times = len(stages) + (NSUB - 1) * STAGE_SKEW
    n_chunks = D_FF // FFN_CHUNK
    ffn_chunk(0)
    nxt = 1
    for t in range(n_times):
        for s in range(NSUB):
            k = t - s * STAGE_SKEW
            if 0 <= k < len(stages):
                stages[k](s)
                if nxt < n_chunks:
                    ffn_chunk(nxt)
                    nxt += 1
    while nxt < n_chunks:
        ffn_chunk(nxt)
        nxt += 1
    ffn_finish()

    lst = sts[NSUB - 1]
    kprev[...] = lst["kb"]
    vprev[...] = lst["vb"]
    xr_tail[...] = lst["xr"][(GROUPS - (RNN_CONV_W - 1)) * SUBLANES:, :]
    hcar[...] = lst["h_out"]

    @pl.when(step < nblocks)
    def _state():
        kst[...] = lst["kv"][:, 0:KV_WIDTH]
        vst[...] = lst["kv"][:, KV_WIDTH:2 * KV_WIDTH]
        for t in range(RNN_CONV_W - 1):
            r = (GROUPS - (RNN_CONV_W - 1) + t) * SUBLANES + SUBLANES - 1
            convst_ref[t:t + 1, :] = lst["xr"][r:r + 1, :]
        hlast_ref[...] = lst["h_out"][0:1, :]

    @pl.when(step >= 1)
    def _store_y():
        for cp in y_copies(step - 1, fslot):
            cp.start()

    @pl.when(step == nblocks)
    def _finish():
        for cp in state_copies():
            cp.start()
        for cp in state_copies():
            cp.wait()
        for cp in y_copies(0, lax.rem(step + 1, 3)) + y_copies(0, fslot):
            cp.wait()


def _const_spec(shape):
    nd = len(shape)
    return pl.BlockSpec(shape, lambda i: (0,) * nd)


def _prompt_call(x, mod, mod_row0, sinks, params):
    T = x.shape[0]
    TB = TOKEN_BLOCK
    assert T // TB >= 3
    (g_pre_mix, w_in, conv_w, conv_b, wa, ba, wi, bi, lam, g_attn, g_rnn, w_out,
     g_post_mix, g_pre_ffn, w_up, fconv_w, fconv_b, w_down, g_post_ffn) = params
    x4 = x.reshape(T // SUB_ROWS, SUBLANES, GROUPS, D_MODEL)
    ins = [x4, mod, sinks, g_pre_mix, w_in, conv_w, conv_b, wa, ba, wi, bi, lam, g_attn, g_rnn,
           w_out, g_post_mix, g_pre_ffn, w_up, fconv_w, fconv_b, w_down, g_post_ffn]
    assert mod_row0 % MOD_PAD_ROWS == 0
    in_specs = [pl.BlockSpec(memory_space=pl.ANY),
                pl.BlockSpec((MOD_PAD_ROWS, mod.shape[1]), lambda i: (mod_row0 // MOD_PAD_ROWS, 0)),
                pl.BlockSpec(memory_space=pltpu.SMEM)]
    in_specs += [_const_spec(a.shape) for a in ins[3:]]
    out_shape = (
        jax.ShapeDtypeStruct(x4.shape, f32),
        jax.ShapeDtypeStruct((SUBLANES, GROUPS, KV_WIDTH), f32),
        jax.ShapeDtypeStruct((SUBLANES, GROUPS, KV_WIDTH), f32),
        jax.ShapeDtypeStruct((1, D_RNN), f32),
        jax.ShapeDtypeStruct((RNN_CONV_W - 1, D_RNN), f32),
        jax.ShapeDtypeStruct((FFN_CONV_W - 1, 2 * D_FF), f32),
    )
    out_specs = (
        pl.BlockSpec(memory_space=pl.ANY),
        pl.BlockSpec(memory_space=pl.ANY),
        pl.BlockSpec(memory_space=pl.ANY),
        _const_spec((1, D_RNN)),
        _const_spec((RNN_CONV_W - 1, D_RNN)),
        _const_spec((FFN_CONV_W - 1, 2 * D_FF)),
    )
    scratch = [
        pltpu.VMEM((2, TB, D_MODEL), f32),
        pltpu.VMEM((3, TB, D_MODEL), f32),
        pltpu.SemaphoreType.DMA((2,)),
        pltpu.SemaphoreType.DMA((3,)),
        pltpu.SemaphoreType.DMA(()),
        pltpu.VMEM((SUB_ROWS, KV_WIDTH), bf16),
        pltpu.VMEM((SUB_ROWS, KV_WIDTH), bf16),
        pltpu.VMEM((SUB_ROWS, KV_WIDTH), f32),
        pltpu.VMEM((SUB_ROWS, KV_WIDTH), f32),
        pltpu.VMEM(((RNN_CONV_W - 1) * SUBLANES, D_RNN), f32),
        pltpu.VMEM(((FFN_CONV_W - 1) * SUBLANES, 2 * D_FF), f32),
        pltpu.VMEM((SUBLANES, D_RNN), f32),
        pltpu.VMEM((N_HEADS, SUB_ROWS, 2 * SUB_ROWS), f32),
        pltpu.VMEM((2, TOKEN_BLOCK, D_MODEL), bf16),
    ]
    y4, kp, vp, hp, convp, ffnp = pl.pallas_call(
        _prompt_kernel,
        grid=(T // TB + 1,),
        in_specs=in_specs,
        out_specs=out_specs,
        out_shape=out_shape,
        scratch_shapes=scratch,
        compiler_params=pltpu.CompilerParams(
            dimension_semantics=("arbitrary",), vmem_limit_bytes=VMEM_LIMIT_BYTES),
        name="prompt_layer",
    )(*ins)
    return (y4.reshape(T, D_MODEL), kp.reshape(WINDOW, KV_WIDTH), vp.reshape(WINDOW, KV_WIDTH),
            hp, convp, ffnp)


def _sample_pre_kernel(x_ref, mod_ref, g_pre_mix_ref, w_in_ref, q_ref, kv_ref, xr_ref, yr_ref):
    B = x_ref.shape[0]
    sh1 = mod_ref[0:B, 0 * D_MODEL:1 * D_MODEL]
    sc1 = mod_ref[0:B, 1 * D_MODEL:2 * D_MODEL]
    hmix = (_rms_norm(x_ref[:, 0, :], g_pre_mix_ref[...]) * (1.0 + sc1) + sh1).astype(bf16)
    q = jnp.dot(hmix, w_in_ref[:, 0:C_K], preferred_element_type=f32) * Q_SCALE
    for h in range(N_HEADS):
        q_ref[:, h, :] = q[:, h * HEAD_DIM:(h + 1) * HEAD_DIM]
    kv_ref[...] = jnp.dot(hmix, w_in_ref[:, C_K:C_XR], preferred_element_type=f32)
    xr_ref[...] = jnp.dot(hmix, w_in_ref[:, C_XR:C_YR], preferred_element_type=f32)
    yr_ref[...] = jnp.dot(hmix, w_in_ref[:, C_YR:IN_COLS], preferred_element_type=f32)


def _sample_pre_call(x, mod_s, g_pre_mix, w_in):
    B = x.shape[0]
    return pl.pallas_call(
        _sample_pre_kernel,
        out_shape=(
            jax.ShapeDtypeStruct((B, N_HEADS, HEAD_DIM), f32),
            jax.ShapeDtypeStruct((B, 2 * KV_WIDTH), f32),
            jax.ShapeDtypeStruct((B, D_RNN), f32),
            jax.ShapeDtypeStruct((B, D_RNN), f32),
        ),
        compiler_params=pltpu.CompilerParams(vmem_limit_bytes=VMEM_LIMIT_BYTES),
        name="sample_pre",
    )(x, mod_s, g_pre_mix, w_in)


def _sample_attn_kernel(q_ref, kv_ref, ck_ref, cv_ref, sinks_ref, o_ref, kwin_ref, vwin_ref):
    R = N_KV_HEADS * WINDOW
    hrow = lax.broadcasted_iota(jnp.int32, (N_HEADS, R), 0)
    rcol = lax.broadcasted_iota(jnp.int32, (N_HEADS, R), 1)
    slope = jnp.exp2(-8.0 * (hrow + 1).astype(f32) / N_HEADS)
    own = (rcol % N_KV_HEADS) == (hrow // GQA_GROUP)
    bias = jnp.where(own, slope * (WINDOW - 1 - rcol // N_KV_HEADS).astype(f32), jnp.inf)
    wrow = lax.broadcasted_iota(jnp.int32, (R, HEAD_DIM), 0)
    sink = sinks_ref[...]

    def shifted(cache, new_rows):
        out = pltpu.roll(cache, R - N_KV_HEADS, axis=0)
        for c in range(N_KV_HEADS):
            out = jnp.where(wrow == R - N_KV_HEADS + c, new_rows[c], out)
        return out

    scores = []
    for b in range(SAMPLE_CHUNK):
        knew = [kv_ref[b:b + 1, c * HEAD_DIM:(c + 1) * HEAD_DIM] for c in range(N_KV_HEADS)]
        kw = shifted(ck_ref[b], knew)
        kwin_ref[b] = kw
        scores.append(_dot_nt(q_ref[b].astype(bf16), kw.astype(bf16)) - bias)
    probs, denoms = [], []
    for b in range(SAMPLE_CHUNK):
        s = scores[b]
        m = jnp.maximum(jnp.max(s, axis=-1, keepdims=True), sink)
        p = jnp.exp(s - m)
        denoms.append(jnp.sum(p, axis=-1, keepdims=True) + jnp.exp(sink - m))
        probs.append(p.astype(bf16))
    for b in range(SAMPLE_CHUNK):
        vnew = [kv_ref[b:b + 1, KV_WIDTH + c * HEAD_DIM:KV_WIDTH + (c + 1) * HEAD_DIM]
                for c in range(N_KV_HEADS)]
        vw = shifted(cv_ref[b], vnew)
        vwin_ref[b] = vw
        o_ref[b] = (jnp.dot(probs[b], vw.astype(bf16), preferred_element_type=f32)
                    / denoms[b])


def _sample_attn_call(q, kv, ck, cv, sinks_col):
    B = q.shape[0]
    BC = SAMPLE_CHUNK
    return pl.pallas_call(
        _sample_attn_kernel,
        grid=(B // BC,),
        in_specs=[
            pl.BlockSpec((BC, N_HEADS, HEAD_DIM), lambda i: (i, 0, 0)),
            pl.BlockSpec((BC, 2 * KV_WIDTH), lambda i: (i, 0)),
            pl.BlockSpec((BC, N_KV_HEADS * WINDOW, HEAD_DIM), lambda i: (i, 0, 0)),
            pl.BlockSpec((BC, N_KV_HEADS * WINDOW, HEAD_DIM), lambda i: (i, 0, 0)),
            pl.BlockSpec((N_HEADS, 1), lambda i: (0, 0)),
        ],
        out_specs=(
            pl.BlockSpec((BC, N_HEADS, HEAD_DIM), lambda i: (i, 0, 0)),
            pl.BlockSpec((BC, N_KV_HEADS * WINDOW, HEAD_DIM), lambda i: (i, 0, 0)),
            pl.BlockSpec((BC, N_KV_HEADS * WINDOW, HEAD_DIM), lambda i: (i, 0, 0)),
        ),
        out_shape=(
            jax.ShapeDtypeStruct((B, N_HEADS, HEAD_DIM), f32),
            jax.ShapeDtypeStruct((B, N_KV_HEADS * WINDOW, HEAD_DIM), f32),
            jax.ShapeDtypeStruct((B, N_KV_HEADS * WINDOW, HEAD_DIM), f32),
        ),
        compiler_params=pltpu.CompilerParams(
            dimension_semantics=("arbitrary",), vmem_limit_bytes=SAMPLE_ATTN_VMEM_BYTES),
        name="sample_attn",
    )(q, kv, ck, cv, sinks_col)


def _sample_post_kernel(
    x_ref, mod_ref, attn_ref, xr_ref, yr_ref, h0_ref, cbuf_ref, fbuf_ref,
    conv_w_ref, conv_b_ref, wa_ref, ba_ref, wi_ref, bi_ref, lam_ref,
    g_attn_ref, g_rnn_ref, w_out_ref, g_post_mix_ref, g_pre_ffn_ref, w_up_ref,
    fconv_w_ref, fconv_b_ref, w_down_ref, g_post_ffn_ref,
    y_ref, h_ref, cst_ref, fst_ref,
):
    B = x_ref.shape[0]
    g1 = mod_ref[0:B, 2 * D_MODEL:3 * D_MODEL]
    sh2 = mod_ref[0:B, 3 * D_MODEL:4 * D_MODEL]
    sc2 = mod_ref[0:B, 4 * D_MODEL:5 * D_MODEL]
    g2 = mod_ref[0:B, 5 * D_MODEL:6 * D_MODEL]
    x = x_ref[:, 0, :]
    xr = xr_ref[...]

    xc = conv_b_ref[...] + conv_w_ref[RNN_CONV_W - 1:RNN_CONV_W, :] * xr
    for jj in range(RNN_CONV_W - 1):
        xc = xc + conv_w_ref[jj:jj + 1, :] * cbuf_ref[jj]
    for jj in range(RNN_CONV_W - 2):
        cst_ref[jj] = cbuf_ref[jj + 1]
    cst_ref[RNN_CONV_W - 2] = xr

    sp = _softplus_neg(lam_ref[...])
    a_parts, u_parts = _rglru_gates(xc, wa_ref, ba_ref[...], wi_ref, bi_ref[...], sp)
    a = jnp.concatenate(a_parts, axis=1)
    u = jnp.concatenate(u_parts, axis=1)
    h = a * h0_ref[...] + u
    h_ref[...] = h
    rnn = h * _gelu_tanh(yr_ref[...])

    attn = jnp.concatenate([attn_ref[:, h, :] for h in range(N_HEADS)], axis=1)
    attn_n = _rms_norm(attn, g_attn_ref[...])
    rnn_n = _rms_norm(rnn, g_rnn_ref[...])
    mo = (_dot(attn_n, w_out_ref[0:ATTN_WIDTH, 0:D_MODEL])
          + _dot(rnn_n, w_out_ref[ATTN_WIDTH:ATTN_WIDTH + D_RNN, 0:D_MODEL]))
    x1 = x + g1 * _rms_norm(mo, g_post_mix_ref[...])

    hff = (_rms_norm(x1, g_pre_ffn_ref[...]) * (1.0 + sc2) + sh2).astype(bf16)
    up_pre = jnp.dot(hff, w_up_ref[...], preferred_element_type=f32)
    up = fconv_b_ref[...] + fconv_w_ref[FFN_CONV_W - 1:FFN_CONV_W, :] * up_pre
    for jj in range(FFN_CONV_W - 1):
        up = up + fconv_w_ref[jj:jj + 1, :] * fbuf_ref[:, jj, :]
    for jj in range(FFN_CONV_W - 2):
        fst_ref[:, jj, :] = fbuf_ref[:, jj + 1, :]
    fst_ref[:, FFN_CONV_W - 2, :] = up_pre

    act = (_gelu_tanh(up[:, 0:D_FF]) * up[:, D_FF:2 * D_FF]).astype(bf16)
    f = jnp.dot(act, w_down_ref[:, 0:D_MODEL], preferred_element_type=f32)
    y_ref[:, 0, :] = x1 + g2 * _rms_norm(f, g_post_ffn_ref[...])


def _sample_post_call(x, mod_s, attn, xr, yr, h0, cbuf, fbuf, params):
    B = x.shape[0]
    (_, _, conv_w, conv_b, wa, ba, wi, bi, lam, g_attn, g_rnn, w_out,
     g_post_mix, g_pre_ffn, w_up, fconv_w, fconv_b, w_down, g_post_ffn) = params
    return pl.pallas_call(
        _sample_post_kernel,
        out_shape=(
            jax.ShapeDtypeStruct((B, 1, D_MODEL), f32),
            jax.ShapeDtypeStruct((B, D_RNN), f32),
            jax.ShapeDtypeStruct((RNN_CONV_W - 1, B, D_RNN), f32),
            jax.ShapeDtypeStruct((B, FFN_CONV_W - 1, 2 * D_FF), f32),
        ),
        compiler_params=pltpu.CompilerParams(vmem_limit_bytes=VMEM_LIMIT_BYTES),
        name="sample_post",
    )(x, mod_s, attn, xr, yr, h0, cbuf, fbuf, conv_w, conv_b, wa, ba, wi, bi, lam,
      g_attn, g_rnn, w_out, g_post_mix, g_pre_ffn, w_up, fconv_w, fconv_b, w_down, g_post_ffn)


def kernel(x_prompt, x_sample, cache_k, cache_v, state_h, state_conv, state_ffn_conv, c_prompt, c_sample, w_ada, b_ada, g_pre_mix, w_in, conv_w, conv_b, w_a, b_a, w_i, b_i, lam, sinks, g_attn_out, g_rnn_out, w_out, g_post_mix, g_pre_ffn, w_up, ffn_conv_w, ffn_conv_b, w_down, g_post_ffn):
    depth = w_in.shape[0]
    assert depth == 1 and x_prompt.shape[0] == 1 and x_sample.shape[1] == 1
    T = x_prompt.shape[1]
    B = x_sample.shape[0]
    W = cache_k.shape[2]
    assert W == WINDOW and T % TOKEN_BLOCK == 0 and B % SAMPLE_CHUNK == 0

    row = lambda a: a[0].reshape(1, -1)
    pad_cols = lambda w: jnp.concatenate(
        [w.astype(bf16), jnp.zeros((w.shape[0], PADDED_OUT_COLS - w.shape[1]), bf16)], axis=1)
    params = (
        row(g_pre_mix), w_in[0].astype(bf16), conv_w[0], row(conv_b),
        w_a[0].astype(bf16), row(b_a), w_i[0].astype(bf16), row(b_i), row(lam),
        row(g_attn_out), row(g_rnn_out), pad_cols(w_out[0]), row(g_post_mix),
        row(g_pre_ffn),
        w_up[0].astype(bf16), ffn_conv_w[0], row(ffn_conv_b), pad_cols(w_down[0]),
        row(g_post_ffn),
    )

    c_all = jnp.concatenate(
        [c_sample, jnp.broadcast_to(c_prompt, (MOD_PAD_ROWS, D_MODEL))], axis=0)
    mod = _mod_call(c_all, w_ada[0], b_ada[0].reshape(1, -1))
    mod_s = mod

    yp, kp, vp, hp, convp, ffnp = _prompt_call(x_prompt[0], mod, B, sinks[0], params)

    xs = x_sample
    q, kv, xr, yr = _sample_pre_call(xs, mod_s, params[0], params[1])
    ck = cache_k.reshape(B, W * N_KV_HEADS, HEAD_DIM)
    cv = cache_v.reshape(B, W * N_KV_HEADS, HEAD_DIM)
    attn3, kwin, vwin = _sample_attn_call(
        q, kv, ck, cv, sinks[0].reshape(N_HEADS, 1))
    attn = attn3
    ys, hs, convs, ffns = _sample_post_call(
        xs, mod_s, attn, xr, yr, state_h[0],
        jnp.transpose(state_conv[0], (1, 0, 2)), state_ffn_conv[0],
        params)
    convs = jnp.transpose(convs, (1, 0, 2))

    kv_shape = (1, 1, W, N_KV_HEADS, HEAD_DIM)
    kvs_shape = (1, B, W, N_KV_HEADS, HEAD_DIM)
    return (
        yp[None], ys,
        kp.reshape(kv_shape), vp.reshape(kv_shape), hp[None], convp[None, None], ffnp[None, None],
        kwin.reshape(kvs_shape), vwin.reshape(kvs_shape), hs[None], convs[None], ffns[None],
    )
```

```python
import math

import jax
import jax.numpy as jnp
from jax import lax
from jax.experimental import pallas as pl
from jax.experimental.pallas import tpu as pltpu

D_MODEL = 1024
N_HEADS = 8
N_KV_HEADS = 2
HEAD_DIM = 128
GQA_GROUP = N_HEADS // N_KV_HEADS
ATTN_WIDTH = N_HEADS * HEAD_DIM
KV_WIDTH = N_KV_HEADS * HEAD_DIM
WINDOW = 128
D_RNN = D_MODEL
RNN_BLOCKS = 8
RNN_BLOCK_W = D_RNN // RNN_BLOCKS
RG_C = 8.0
RNN_CONV_W = 4
D_FF = 2816
FFN_CONV_W = 3
RMS_EPS = 1e-6

C_K = ATTN_WIDTH
C_V = C_K + KV_WIDTH
C_XR = C_V + KV_WIDTH
C_YR = C_XR + D_RNN
IN_COLS = C_YR + D_RNN

SUBLANES = 8
SUB_ROWS = WINDOW
GROUPS = SUB_ROWS // SUBLANES
TOKEN_BLOCK = 256
NSUB = TOKEN_BLOCK // SUB_ROWS
FFN_CHUNK = 256
STAGE_SKEW = 3
WSTAGE_ROWS = 64
SAMPLE_CHUNK = 16
SAMPLE_ATTN_VMEM_BYTES = (2 * 2 * 2 * SAMPLE_CHUNK * N_KV_HEADS * WINDOW * HEAD_DIM * 4
                          + 8 * 1024 * 1024)
MOD_PAD_ROWS = 8
VMEM_LIMIT_BYTES = 56 * 1024 * 1024

ALIBI_SLOPES = tuple(2.0 ** (-8.0 * (h + 1) / N_HEADS) for h in range(N_HEADS))
Q_SCALE = HEAD_DIM ** -0.5
SQRT_2_OVER_PI = math.sqrt(2.0 / math.pi)
LOG2_E = 1.0 / math.log(2.0)
PADDED_OUT_COLS = D_MODEL + 128

bf16 = jnp.bfloat16
f32 = jnp.float32


def _rms_scale(x):
    return lax.rsqrt(jnp.mean(x * x, axis=-1, keepdims=True) + RMS_EPS)


def _rms_norm(x, g):
    return x * _rms_scale(x) * g


def _gelu_tanh(x):
    k1 = -2.0 * LOG2_E * SQRT_2_OVER_PI
    k3 = k1 * 0.044715
    return x / (1.0 + jnp.exp2(x * (k1 + k3 * (x * x))))


def _sigmoid(x):
    return 1.0 / (1.0 + jnp.exp2(x * (-LOG2_E)))


def _softplus_neg(lam):
    return jnp.maximum(-lam, 0.0) + jnp.log1p(jnp.exp(-jnp.abs(lam)))


def _dot(a, b):
    return jnp.dot(a.astype(bf16), b, preferred_element_type=f32)


def _dot_nt(a, b):
    return lax.dot_general(a, b, (((1,), (1,)), ((), ())), preferred_element_type=f32)


def _rglru_gates(xc, wa_ref, ba, wi_ref, bi, sp):
    a_parts, u_parts = [], []
    for n in range(RNN_BLOCKS):
        sl = slice(n * RNN_BLOCK_W, (n + 1) * RNN_BLOCK_W)
        xn = xc[:, sl]
        xb = xn.astype(bf16)
        r = _sigmoid(jnp.dot(xb, wa_ref[n], preferred_element_type=f32) + ba[:, sl])
        i = _sigmoid(jnp.dot(xb, wi_ref[n], preferred_element_type=f32) + bi[:, sl])
        log_a = (-RG_C) * r * sp[:, sl]
        t = jnp.tanh(log_a)
        one_minus_a2 = (-2.0 * t) / (1.0 - t)
        a_parts.append(jnp.exp(log_a))
        u_parts.append(jnp.sqrt(one_minus_a2) * (i * xn))
    return a_parts, u_parts


def _mod_kernel(c_ref, w_ref, b_ref, o_ref):
    c = c_ref[...]
    s = c * _sigmoid(c)
    o_ref[...] = _dot(s, w_ref[...].astype(bf16)) + b_ref[...]


def _mod_call(c_all, w_ada, b_ada):
    rows = c_all.shape[0]
    ncol = w_ada.shape[1]
    bn = D_MODEL
    return pl.pallas_call(
        _mod_kernel,
        grid=(ncol // bn,),
        in_specs=[
            pl.BlockSpec((rows, D_MODEL), lambda j: (0, 0)),
            pl.BlockSpec((D_MODEL, bn), lambda j: (0, j)),
            pl.BlockSpec((1, bn), lambda j: (0, j)),
        ],
        out_specs=pl.BlockSpec((rows, bn), lambda j: (0, j)),
        out_shape=jax.ShapeDtypeStruct((rows, ncol), f32),
        compiler_params=pltpu.CompilerParams(dimension_semantics=("arbitrary",)),
        name="adaln_mod",
    )(c_all, w_ada, b_ada)


def _groups(v, c0=None, c1=None):
    if c0 is None:
        return [v[j * SUBLANES:(j + 1) * SUBLANES, :] for j in range(GROUPS)]
    return [v[j * SUBLANES:(j + 1) * SUBLANES, c0:c1] for j in range(GROUPS)]


def _shifted_groups(X, tail_row, sub0, max_shift):
    wrapped = {}
    for j in range(GROUPS - max_shift, GROUPS):
        wrapped[j] = jnp.where(sub0, tail_row(j), pltpu.roll(X[j], 1, axis=0))
    sh = {}
    for d in range(1, max_shift + 1):
        sh[d] = [X[j - d] if j >= d else wrapped[j - d + GROUPS] for j in range(GROUPS)]
    return sh


def _scan_sub(a, u, h_in, sub_iota):
    A = _groups(a)
    L = _groups(u)
    for j in range(1, GROUPS):
        L[j] = A[j] * L[j - 1] + L[j]
        A[j] = A[j] * A[j - 1]
    ae, le = A[GROUPS - 1], L[GROUPS - 1]
    for s_ in (1, 2, 4):
        ok = sub_iota >= s_
        a_sh = pltpu.roll(ae, s_, axis=0)
        l_sh = pltpu.roll(le, s_, axis=0)
        le = jnp.where(ok, ae * l_sh + le, le)
        ae = jnp.where(ok, ae * a_sh, ae)
    hend = le + ae * h_in
    hprev = jnp.where(sub_iota == 0, h_in, pltpu.roll(hend, 1, axis=0))
    h = jnp.concatenate([L[j] + A[j] * hprev for j in range(GROUPS)], axis=0)
    h_out = jnp.broadcast_to(hend[SUBLANES - 1:SUBLANES, :], hend.shape)
    return h, h_out


def _prompt_kernel(
    x_hbm, mod_ref, sinks_ref,
    g_pre_mix_ref, w_in_hbm, conv_w_ref, conv_b_ref, wa_ref, ba_ref, wi_ref, bi_ref, lam_ref,
    g_attn_ref, g_rnn_ref, w_out_hbm, g_post_mix_ref, g_pre_ffn_ref, w_up_hbm,
    fconv_w_ref, fconv_b_ref, w_down_hbm, g_post_ffn_ref,
    y_hbm, kwin_hbm, vwin_hbm, hlast_ref, convst_ref, ffnst_ref,
    w_in_bf_hbm, w_out_bf_hbm, w_up_bf_hbm, w_down_bf_hbm,
    xbuf, ybuf, sem_in, sem_out, sem_st, kprev, vprev, kst, vst, xr_tail, up_tail, hcar, bias_s,
    hff_s, w_in_ref, w_out_ref, w_up_ref, w_down_ref, wstage, sem_w, sem_wout,
):
    step = pl.program_id(0)
    nblocks = pl.num_programs(0) - 1
    slot = lax.rem(step, 2)
    yslot = lax.rem(step, 3)
    fslot = lax.rem(step + 2, 3)

    def x_copies(st, sl):
        return [pltpu.make_async_copy(
            x_hbm.at[st * NSUB + s, :, j, :],
            xbuf.at[sl, pl.ds(s * SUB_ROWS + j * SUBLANES, SUBLANES), :],
            sem_in.at[sl]) for s in range(NSUB) for j in range(GROUPS)]

    def y_copies(st, sl):
        return [pltpu.make_async_copy(
            ybuf.at[sl, pl.ds(s * SUB_ROWS + j * SUBLANES, SUBLANES), :],
            y_hbm.at[st * NSUB + s, :, j, :],
            sem_out.at[sl]) for s in range(NSUB) for j in range(GROUPS)]

    def state_copies():
        cps = []
        for src, dst in ((kst, kwin_hbm), (vst, vwin_hbm)):
            cps += [pltpu.make_async_copy(src.at[pl.ds(j * SUBLANES, SUBLANES), :],
                                          dst.at[:, j, :], sem_st) for j in range(GROUPS)]
        return cps

    def weight_out_copies():
        return [pltpu.make_async_copy(w_in_ref, w_in_bf_hbm, sem_wout),
                pltpu.make_async_copy(w_up_ref, w_up_bf_hbm, sem_wout),
                pltpu.make_async_copy(w_out_ref.at[:, pl.ds(0, D_MODEL)], w_out_bf_hbm, sem_wout),
                pltpu.make_async_copy(w_down_ref.at[:, pl.ds(0, D_MODEL)], w_down_bf_hbm, sem_wout)]

    def load_cast(w_hbm, w_s, stage, chunk_rows):
        n_rows, n_cols = w_hbm.shape
        n = n_rows // chunk_rows

        def chunk(i, sl):
            return pltpu.make_async_copy(
                w_hbm.at[pl.ds(i * chunk_rows, chunk_rows), :],
                stage.at[sl, pl.ds(0, chunk_rows), pl.ds(0, n_cols)], sem_w.at[sl])

        chunk(0, 0).start()

        def body(i, carry):
            sl = lax.rem(i, 2)

            @pl.when(i + 1 < n)
            def _next():
                chunk(i + 1, 1 - sl).start()

            chunk(i, sl).wait()
            r0 = pl.multiple_of(i * chunk_rows, chunk_rows)
            w_s[pl.ds(r0, chunk_rows), 0:n_cols] = (
                stage[sl, 0:chunk_rows, 0:n_cols].astype(bf16))
            return carry

        lax.fori_loop(0, n, body, 0)

    @pl.when(step == 0)
    def _init():
        for cp in x_copies(0, 0):
            cp.start()
        load_cast(w_in_hbm, w_in_ref, wstage, WSTAGE_ROWS)
        load_cast(w_out_hbm, w_out_ref, ybuf, TOKEN_BLOCK)
        load_cast(w_up_hbm, w_up_ref, wstage, WSTAGE_ROWS)
        load_cast(w_down_hbm, w_down_ref, ybuf, TOKEN_BLOCK)
        for cp in weight_out_copies():
            cp.start()
        kprev[...] = jnp.zeros(kprev.shape, bf16)
        vprev[...] = jnp.zeros(vprev.shape, bf16)
        xr_tail[...] = jnp.zeros(xr_tail.shape, f32)
        up_tail[...] = jnp.zeros(up_tail.shape, f32)
        hcar[...] = jnp.zeros(hcar.shape, f32)
        hff_s[...] = jnp.zeros(hff_s.shape, bf16)
        ybuf[2] = jnp.zeros(ybuf.shape[1:], f32)
        rq = lax.broadcasted_iota(jnp.int32, (SUB_ROWS, 2 * SUB_ROWS), 0)
        ck = lax.broadcasted_iota(jnp.int32, (SUB_ROWS, 2 * SUB_ROWS), 1)
        rk = ck & (SUB_ROWS - 1)
        tq = (rq % SUBLANES) * GROUPS + rq // SUBLANES
        tk = (rk % SUBLANES) * GROUPS + rk // SUBLANES
        dist = tq - tk + jnp.where(ck < SUB_ROWS, WINDOW, 0)
        base = jnp.where((dist >= 0) & (dist < WINDOW), dist.astype(f32), jnp.inf)
        for h in range(N_HEADS):
            bias_s[h] = ALIBI_SLOPES[h] * base

    @pl.when(step + 1 < nblocks)
    def _prefetch():
        for cp in x_copies(step + 1, 1 - slot):
            cp.start()

    @pl.when(step >= 3)
    def _free_ybuf():
        for cp in y_copies(0, yslot):
            cp.wait()

    @pl.when(step < nblocks)
    def _wait_x():
        for cp in x_copies(step, slot):
            cp.wait()

    sh1 = mod_ref[0:1, 0 * D_MODEL:1 * D_MODEL]
    sc1 = mod_ref[0:1, 1 * D_MODEL:2 * D_MODEL]
    g1 = mod_ref[0:1, 2 * D_MODEL:3 * D_MODEL]
    sh2 = mod_ref[0:1, 3 * D_MODEL:4 * D_MODEL]
    sc2 = mod_ref[0:1, 4 * D_MODEL:5 * D_MODEL]
    g2 = mod_ref[0:1, 5 * D_MODEL:6 * D_MODEL]
    gs1 = g_pre_mix_ref[...] * (1.0 + sc1)
    gs2 = g_pre_ffn_ref[...] * (1.0 + sc2)
    sp = _softplus_neg(lam_ref[...])
    first_pen = jnp.where(step == 0, jnp.inf, 0.0)
    sub_iota = lax.broadcasted_iota(jnp.int32, (SUBLANES, D_RNN), 0)
    sub0_rnn = sub_iota == 0
    sub0_ffn = lax.broadcasted_iota(jnp.int32, (SUBLANES, FFN_CHUNK), 0) == 0
    conv_w = [conv_w_ref.at[jj:jj + 1, :] for jj in range(RNN_CONV_W)]

    sts = [dict() for _ in range(NSUB)]

    def rows(s):
        return pl.ds(s * SUB_ROWS, SUB_ROWS)

    def stage_pre(s):
        x = xbuf[slot, rows(s), :]
        sts[s]["hmix"] = (x * _rms_scale(x) * gs1 + sh1).astype(bf16)

    def stage_inproj(s):
        st = sts[s]
        hm = st["hmix"]
        st["q"] = (jnp.dot(hm, w_in_ref[:, 0:C_K], preferred_element_type=f32) * Q_SCALE).astype(bf16)
        kv = jnp.dot(hm, w_in_ref[:, C_K:C_XR], preferred_element_type=f32)
        st["kb"] = kv[:, 0:KV_WIDTH].astype(bf16)
        st["vb"] = kv[:, KV_WIDTH:2 * KV_WIDTH].astype(bf16)
        st["kv"] = kv
        st["xr"] = jnp.dot(hm, w_in_ref[:, C_XR:C_YR], preferred_element_type=f32)

    def stage_attn(s):
        st = sts[s]
        if s == 0:
            kp, vp = kprev[...], vprev[...]
        else:
            kp, vp = sts[s - 1]["kb"], sts[s - 1]["vb"]
        kw = jnp.concatenate([kp, st["kb"]], axis=0)
        vw = jnp.concatenate([vp, st["vb"]], axis=0)
        outs = []
        for h in range(N_HEADS):
            c = h // GQA_GROUP
            hs = slice(h * HEAD_DIM, (h + 1) * HEAD_DIM)
            cs = slice(c * HEAD_DIM, (c + 1) * HEAD_DIM)
            sc = _dot_nt(st["q"][:, hs], kw[:, cs])
            if s == 0:
                sc = jnp.concatenate([sc[:, 0:SUB_ROWS] - first_pen, sc[:, SUB_ROWS:]], axis=1)
            sc = sc - bias_s[h]
            sink = sinks_ref[h]
            m = jnp.maximum(jnp.max(sc, axis=-1, keepdims=True), sink)
            p = jnp.exp(sc - m)
            denom = jnp.sum(p, axis=-1, keepdims=True) + jnp.exp(sink - m)
            outs.append(jnp.dot(p.astype(bf16), vw[:, cs], preferred_element_type=f32) / denom)
        attn = jnp.concatenate(outs, axis=1)
        st["attn_n"] = _rms_norm(attn, g_attn_ref[...]).astype(bf16)

    def stage_rnn(s):
        st = sts[s]
        xr = st["xr"]
        X = _groups(xr)
        if s == 0:
            tail = lambda j: xr_tail[(j - (GROUPS - RNN_CONV_W + 1)) * SUBLANES + SUBLANES - 1:
                                     (j - (GROUPS - RNN_CONV_W + 1)) * SUBLANES + SUBLANES, :]
        else:
            pxr = sts[s - 1]["xr"]
            tail = lambda j: pxr[j * SUBLANES + SUBLANES - 1:(j + 1) * SUBLANES, :]
        sh = _shifted_groups(X, tail, sub0_rnn, RNN_CONV_W - 1)
        xcs = []
        for j in range(GROUPS):
            acc = conv_b_ref[...] + conv_w[RNN_CONV_W - 1][...] * X[j]
            for d in range(1, RNN_CONV_W):
                acc = acc + conv_w[RNN_CONV_W - 1 - d][...] * sh[d][j]
            xcs.append(acc)
        xc = jnp.concatenate(xcs, axis=0)
        a_parts, u_parts = _rglru_gates(xc, wa_ref, ba_ref[...], wi_ref, bi_ref[...], sp)
        a = jnp.concatenate(a_parts, axis=1)
        u = jnp.concatenate(u_parts, axis=1)
        h_in = hcar[...] if s == 0 else sts[s - 1]["h_out"]
        h, st["h_out"] = _scan_sub(a, u, h_in, sub_iota)
        yr = jnp.dot(st["hmix"], w_in_ref[:, C_YR:IN_COLS], preferred_element_type=f32)
        rnn = h * _gelu_tanh(yr)
        st["rnn_n"] = _rms_norm(rnn, g_rnn_ref[...]).astype(bf16)

    def stage_out(s):
        st = sts[s]
        mo = (jnp.dot(st["attn_n"], w_out_ref[0:ATTN_WIDTH, 0:D_MODEL], preferred_element_type=f32)
              + jnp.dot(st["rnn_n"], w_out_ref[ATTN_WIDTH:ATTN_WIDTH + D_RNN, 0:D_MODEL],
                        preferred_element_type=f32))
        x1 = xbuf[slot, rows(s), :] + g1 * _rms_norm(mo, g_post_mix_ref[...])
        ybuf[yslot, rows(s), :] = x1
        hff_s[slot, rows(s), :] = (x1 * _rms_scale(x1) * gs2 + sh2).astype(bf16)

    tail_g0 = GROUPS - (FFN_CONV_W - 1)
    hff = hff_s[1 - slot]
    acts, tails = [], {}

    def conv_chunk(c0):
        c1 = c0 + FFN_CHUNK
        up = jnp.dot(hff, w_up_ref[:, c0:c1], preferred_element_type=f32)
        w = [fconv_w_ref.at[jj:jj + 1, c0:c1] for jj in range(FFN_CONV_W)]
        outs = []
        for s in range(NSUB):
            r0 = s * SUB_ROWS
            X = [up[r0 + j * SUBLANES:r0 + (j + 1) * SUBLANES, :] for j in range(GROUPS)]
            if s == 0:
                tail = lambda j: up_tail[(j - tail_g0) * SUBLANES + SUBLANES - 1:
                                         (j - tail_g0 + 1) * SUBLANES, c0:c1]
            else:
                tail = lambda j, p0=r0 - SUB_ROWS: up[p0 + j * SUBLANES + SUBLANES - 1:
                                                      p0 + (j + 1) * SUBLANES, :]
            sh = _shifted_groups(X, tail, sub0_ffn, FFN_CONV_W - 1)
            for j in range(GROUPS):
                acc = fconv_b_ref[:, c0:c1] + w[FFN_CONV_W - 1][...] * X[j]
                for d in range(1, FFN_CONV_W):
                    acc = acc + w[FFN_CONV_W - 1 - d][...] * sh[d][j]
                outs.append(acc)
        last0 = (NSUB - 1) * SUB_ROWS
        for t in range(FFN_CONV_W - 1):
            r = last0 + (tail_g0 + t) * SUBLANES + SUBLANES - 1
            ffnst_ref[t:t + 1, c0:c1] = up[r:r + 1, :]
        tails[c0] = up[last0 + tail_g0 * SUBLANES:, :]
        return jnp.concatenate(outs, axis=0)

    def ffn_chunk(cc):
        gate = conv_chunk(cc * FFN_CHUNK)
        val = conv_chunk(D_FF + cc * FFN_CHUNK)
        acts.append((_gelu_tanh(gate) * val).astype(bf16))

    def ffn_finish():
        for c0, tl in tails.items():
            up_tail[:, c0:c0 + FFN_CHUNK] = tl
        act = jnp.concatenate(acts, axis=1)
        f = jnp.dot(act, w_down_ref[:, 0:D_MODEL], preferred_element_type=f32)
        ybuf[fslot] = ybuf[fslot] + g2 * _rms_norm(f, g_post_ffn_ref[...])

    stages = (stage_pre, stage_inproj, stage_attn, stage_rnn, stage_out)
    n_times = len(stages) + (NSUB - 1) * STAGE_SKEW
    n_chunks = D_FF // FFN_CHUNK
    ffn_chunk(0)
    nxt = 1
    for t in range(n_times):
        for s in range(NSUB):
            k = t - s * STAGE_SKEW
            if 0 <= k < len(stages):
                stages[k](s)
                if nxt < n_chunks:
                    ffn_chunk(nxt)
                    nxt += 1
    while nxt < n_chunks:
        ffn_chunk(nxt)
        nxt += 1
    ffn_finish()

    lst = sts[NSUB - 1]
    kprev[...] = lst["kb"]
    vprev[...] = lst["vb"]
    xr_tail[...] = lst["xr"][(GROUPS - (RNN_CONV_W - 1)) * SUBLANES:, :]
    hcar[...] = lst["h_out"]

    @pl.when(step < nblocks)
    def _state():
        kst[...] = lst["kv"][:, 0:KV_WIDTH]
        vst[...] = lst["kv"][:, KV_WIDTH:2 * KV_WIDTH]
        for t in range(RNN_CONV_W - 1):
            r = (GROUPS - (RNN_CONV_W - 1) + t) * SUBLANES + SUBLANES - 1
            convst_ref[t:t + 1, :] = lst["xr"][r:r + 1, :]
        hlast_ref[...] = lst["h_out"][0:1, :]

    @pl.when(step >= 1)
    def _store_y():
        for cp in y_copies(step - 1, fslot):
            cp.start()

    @pl.when(step == nblocks)
    def _finish():
        for cp in state_copies():
            cp.start()
        for cp in state_copies() + weight_out_copies():
            cp.wait()
        for cp in y_copies(0, lax.rem(step + 1, 3)) + y_copies(0, fslot):
            cp.wait()


def _const_spec(shape):
    nd = len(shape)
    return pl.BlockSpec(shape, lambda i: (0,) * nd)


def _prompt_call(x, mod, mod_row0, sinks, params):
    T = x.shape[0]
    TB = TOKEN_BLOCK
    assert T // TB >= 3
    (g_pre_mix, w_in, conv_w, conv_b, wa, ba, wi, bi, lam, g_attn, g_rnn, w_out,
     g_post_mix, g_pre_ffn, w_up, fconv_w, fconv_b, w_down, g_post_ffn) = params
    x4 = x.reshape(T // SUB_ROWS, SUBLANES, GROUPS, D_MODEL)
    ins = [x4, mod, sinks, g_pre_mix, w_in, conv_w, conv_b, wa, ba, wi, bi, lam, g_attn, g_rnn,
           w_out, g_post_mix, g_pre_ffn, w_up, fconv_w, fconv_b, w_down, g_post_ffn]
    assert mod_row0 % MOD_PAD_ROWS == 0
    in_specs = [pl.BlockSpec(memory_space=pl.ANY),
                pl.BlockSpec((MOD_PAD_ROWS, mod.shape[1]), lambda i: (mod_row0 // MOD_PAD_ROWS, 0)),
                pl.BlockSpec(memory_space=pltpu.SMEM)]
    big = (w_in, w_out, w_up, w_down)
    assert all(w.dtype == f32 and w.shape[0] % TOKEN_BLOCK == 0 for w in big)
    in_specs += [pl.BlockSpec(memory_space=pl.ANY) if any(a is w for w in big)
                 else _const_spec(a.shape) for a in ins[3:]]
    out_shape = (
        jax.ShapeDtypeStruct(x4.shape, f32),
        jax.ShapeDtypeStruct((SUBLANES, GROUPS, KV_WIDTH), f32),
        jax.ShapeDtypeStruct((SUBLANES, GROUPS, KV_WIDTH), f32),
        jax.ShapeDtypeStruct((1, D_RNN), f32),
        jax.ShapeDtypeStruct((RNN_CONV_W - 1, D_RNN), f32),
        jax.ShapeDtypeStruct((FFN_CONV_W - 1, 2 * D_FF), f32),
    ) + tuple(jax.ShapeDtypeStruct(w.shape, bf16) for w in big)
    out_specs = (
        pl.BlockSpec(memory_space=pl.ANY),
        pl.BlockSpec(memory_space=pl.ANY),
        pl.BlockSpec(memory_space=pl.ANY),
        _const_spec((1, D_RNN)),
        _const_spec((RNN_CONV_W - 1, D_RNN)),
        _const_spec((FFN_CONV_W - 1, 2 * D_FF)),
    ) + (pl.BlockSpec(memory_space=pl.ANY),) * len(big)
    scratch = [
        pltpu.VMEM((2, TB, D_MODEL), f32),
        pltpu.VMEM((3, TB, D_MODEL), f32),
        pltpu.SemaphoreType.DMA((2,)),
        pltpu.SemaphoreType.DMA((3,)),
        pltpu.SemaphoreType.DMA(()),
        pltpu.VMEM((SUB_ROWS, KV_WIDTH), bf16),
        pltpu.VMEM((SUB_ROWS, KV_WIDTH), bf16),
        pltpu.VMEM((SUB_ROWS, KV_WIDTH), f32),
        pltpu.VMEM((SUB_ROWS, KV_WIDTH), f32),
        pltpu.VMEM(((RNN_CONV_W - 1) * SUBLANES, D_RNN), f32),
        pltpu.VMEM(((FFN_CONV_W - 1) * SUBLANES, 2 * D_FF), f32),
        pltpu.VMEM((SUBLANES, D_RNN), f32),
        pltpu.VMEM((N_HEADS, SUB_ROWS, 2 * SUB_ROWS), f32),
        pltpu.VMEM((2, TOKEN_BLOCK, D_MODEL), bf16),
        pltpu.VMEM(w_in.shape, bf16),
        pltpu.VMEM((w_out.shape[0], PADDED_OUT_COLS), bf16),
        pltpu.VMEM(w_up.shape, bf16),
        pltpu.VMEM((w_down.shape[0], PADDED_OUT_COLS), bf16),
        pltpu.VMEM((2, WSTAGE_ROWS, w_up.shape[1]), f32),
        pltpu.SemaphoreType.DMA((2,)),
        pltpu.SemaphoreType.DMA(()),
    ]
    assert w_in.shape[1] <= w_up.shape[1] and w_in.shape[0] % WSTAGE_ROWS == 0
    y4, kp, vp, hp, convp, ffnp, w_in_bf, w_out_bf, w_up_bf, w_down_bf = pl.pallas_call(
        _prompt_kernel,
        grid=(T // TB + 1,),
        in_specs=in_specs,
        out_specs=out_specs,
        out_shape=out_shape,
        scratch_shapes=scratch,
        compiler_params=pltpu.CompilerParams(
            dimension_semantics=("arbitrary",), vmem_limit_bytes=VMEM_LIMIT_BYTES),
        name="prompt_layer",
    )(*ins)
    return (y4.reshape(T, D_MODEL), kp.reshape(WINDOW, KV_WIDTH), vp.reshape(WINDOW, KV_WIDTH),
            hp, convp, ffnp, (w_in_bf, w_out_bf, w_up_bf, w_down_bf))


def _sample_pre_kernel(x_ref, mod_ref, g_pre_mix_ref, w_in_ref, q_ref, kv_ref, xr_ref, yr_ref):
    B = x_ref.shape[0]
    sh1 = mod_ref[0:B, 0 * D_MODEL:1 * D_MODEL]
    sc1 = mod_ref[0:B, 1 * D_MODEL:2 * D_MODEL]
    hmix = (_rms_norm(x_ref[:, 0, :], g_pre_mix_ref[...]) * (1.0 + sc1) + sh1).astype(bf16)
    q = jnp.dot(hmix, w_in_ref[:, 0:C_K], preferred_element_type=f32) * Q_SCALE
    for h in range(N_HEADS):
        q_ref[:, h, :] = q[:, h * HEAD_DIM:(h + 1) * HEAD_DIM]
    kv_ref[...] = jnp.dot(hmix, w_in_ref[:, C_K:C_XR], preferred_element_type=f32)
    xr_ref[...] = jnp.dot(hmix, w_in_ref[:, C_XR:C_YR], preferred_element_type=f32)
    yr_ref[...] = jnp.dot(hmix, w_in_ref[:, C_YR:IN_COLS], preferred_element_type=f32)


def _sample_pre_call(x, mod_s, g_pre_mix, w_in):
    B = x.shape[0]
    return pl.pallas_call(
        _sample_pre_kernel,
        out_shape=(
            jax.ShapeDtypeStruct((B, N_HEADS, HEAD_DIM), f32),
            jax.ShapeDtypeStruct((B, 2 * KV_WIDTH), f32),
            jax.ShapeDtypeStruct((B, D_RNN), f32),
            jax.ShapeDtypeStruct((B, D_RNN), f32),
        ),
        compiler_params=pltpu.CompilerParams(vmem_limit_bytes=VMEM_LIMIT_BYTES),
        name="sample_pre",
    )(x, mod_s, g_pre_mix, w_in)


def _sample_attn_kernel(q_ref, kv_ref, ck_ref, cv_ref, sinks_ref, o_ref, kwin_ref, vwin_ref):
    R = N_KV_HEADS * WINDOW
    hrow = lax.broadcasted_iota(jnp.int32, (N_HEADS, R), 0)
    rcol = lax.broadcasted_iota(jnp.int32, (N_HEADS, R), 1)
    slope = jnp.exp2(-8.0 * (hrow + 1).astype(f32) / N_HEADS)
    own = (rcol % N_KV_HEADS) == (hrow // GQA_GROUP)
    bias = jnp.where(own, slope * (WINDOW - 1 - rcol // N_KV_HEADS).astype(f32), jnp.inf)
    wrow = lax.broadcasted_iota(jnp.int32, (R, HEAD_DIM), 0)
    sink = sinks_ref[...]

    def shifted(cache, new_rows):
        out = pltpu.roll(cache, R - N_KV_HEADS, axis=0)
        for c in range(N_KV_HEADS):
            out = jnp.where(wrow == R - N_KV_HEADS + c, new_rows[c], out)
        return out

    scores = []
    for b in range(SAMPLE_CHUNK):
        knew = [kv_ref[b:b + 1, c * HEAD_DIM:(c + 1) * HEAD_DIM] for c in range(N_KV_HEADS)]
        kw = shifted(ck_ref[b], knew)
        kwin_ref[b] = kw
        scores.append(_dot_nt(q_ref[b].astype(bf16), kw.astype(bf16)) - bias)
    probs, denoms = [], []
    for b in range(SAMPLE_CHUNK):
        s = scores[b]
        m = jnp.maximum(jnp.max(s, axis=-1, keepdims=True), sink)
        p = jnp.exp(s - m)
        denoms.append(jnp.sum(p, axis=-1, keepdims=True) + jnp.exp(sink - m))
        probs.append(p.astype(bf16))
    for b in range(SAMPLE_CHUNK):
        vnew = [kv_ref[b:b + 1, KV_WIDTH + c * HEAD_DIM:KV_WIDTH + (c + 1) * HEAD_DIM]
                for c in range(N_KV_HEADS)]
        vw = shifted(cv_ref[b], vnew)
        vwin_ref[b] = vw
        o_ref[b] = (jnp.dot(probs[b], vw.astype(bf16), preferred_element_type=f32)
                    / denoms[b])


def _sample_attn_call(q, kv, ck, cv, sinks_col):
    B = q.shape[0]
    BC = SAMPLE_CHUNK
    return pl.pallas_call(
        _sample_attn_kernel,
        grid=(B // BC,),
        in_specs=[
            pl.BlockSpec((BC, N_HEADS, HEAD_DIM), lambda i: (i, 0, 0)),
            pl.BlockSpec((BC, 2 * KV_WIDTH), lambda i: (i, 0)),
            pl.BlockSpec((BC, N_KV_HEADS * WINDOW, HEAD_DIM), lambda i: (i, 0, 0)),
            pl.BlockSpec((BC, N_KV_HEADS * WINDOW, HEAD_DIM), lambda i: (i, 0, 0)),
            pl.BlockSpec((N_HEADS, 1), lambda i: (0, 0)),
        ],
        out_specs=(
            pl.BlockSpec((BC, N_HEADS, HEAD_DIM), lambda i: (i, 0, 0)),
            pl.BlockSpec((BC, N_KV_HEADS * WINDOW, HEAD_DIM), lambda i: (i, 0, 0)),
            pl.BlockSpec((BC, N_KV_HEADS * WINDOW, HEAD_DIM), lambda i: (i, 0, 0)),
        ),
        out_shape=(
            jax.ShapeDtypeStruct((B, N_HEADS, HEAD_DIM), f32),
            jax.ShapeDtypeStruct((B, N_KV_HEADS * WINDOW, HEAD_DIM), f32),
            jax.ShapeDtypeStruct((B, N_KV_HEADS * WINDOW, HEAD_DIM), f32),
        ),
        compiler_params=pltpu.CompilerParams(
            dimension_semantics=("arbitrary",), vmem_limit_bytes=SAMPLE_ATTN_VMEM_BYTES),
        name="sample_attn",
    )(q, kv, ck, cv, sinks_col)


def _sample_post_kernel(
    x_ref, mod_ref, attn_ref, xr_ref, yr_ref, h0_ref, cbuf_ref, fbuf_ref,
    conv_w_ref, conv_b_ref, wa_ref, ba_ref, wi_ref, bi_ref, lam_ref,
    g_attn_ref, g_rnn_ref, w_out_ref, g_post_mix_ref, g_pre_ffn_ref, w_up_ref,
    fconv_w_ref, fconv_b_ref, w_down_ref, g_post_ffn_ref,
    y_ref, h_ref, cst_ref, fst_ref,
):
    B = x_ref.shape[0]
    g1 = mod_ref[0:B, 2 * D_MODEL:3 * D_MODEL]
    sh2 = mod_ref[0:B, 3 * D_MODEL:4 * D_MODEL]
    sc2 = mod_ref[0:B, 4 * D_MODEL:5 * D_MODEL]
    g2 = mod_ref[0:B, 5 * D_MODEL:6 * D_MODEL]
    x = x_ref[:, 0, :]
    xr = xr_ref[...]

    xc = conv_b_ref[...] + conv_w_ref[RNN_CONV_W - 1:RNN_CONV_W, :] * xr
    for jj in range(RNN_CONV_W - 1):
        xc = xc + conv_w_ref[jj:jj + 1, :] * cbuf_ref[jj]
    for jj in range(RNN_CONV_W - 2):
        cst_ref[jj] = cbuf_ref[jj + 1]
    cst_ref[RNN_CONV_W - 2] = xr

    sp = _softplus_neg(lam_ref[...])
    a_parts, u_parts = _rglru_gates(xc, wa_ref, ba_ref[...], wi_ref, bi_ref[...], sp)
    a = jnp.concatenate(a_parts, axis=1)
    u = jnp.concatenate(u_parts, axis=1)
    h = a * h0_ref[...] + u
    h_ref[...] = h
    rnn = h * _gelu_tanh(yr_ref[...])

    attn = jnp.concatenate([attn_ref[:, h, :] for h in range(N_HEADS)], axis=1)
    attn_n = _rms_norm(attn, g_attn_ref[...])
    rnn_n = _rms_norm(rnn, g_rnn_ref[...])
    mo = (_dot(attn_n, w_out_ref[0:ATTN_WIDTH, 0:D_MODEL])
          + _dot(rnn_n, w_out_ref[ATTN_WIDTH:ATTN_WIDTH + D_RNN, 0:D_MODEL]))
    x1 = x + g1 * _rms_norm(mo, g_post_mix_ref[...])

    hff = (_rms_norm(x1, g_pre_ffn_ref[...]) * (1.0 + sc2) + sh2).astype(bf16)
    up_pre = jnp.dot(hff, w_up_ref[...], preferred_element_type=f32)
    up = fconv_b_ref[...] + fconv_w_ref[FFN_CONV_W - 1:FFN_CONV_W, :] * up_pre
    for jj in range(FFN_CONV_W - 1):
        up = up + fconv_w_ref[jj:jj + 1, :] * fbuf_ref[:, jj, :]
    for jj in range(FFN_CONV_W - 2):
        fst_ref[:, jj, :] = fbuf_ref[:, jj + 1, :]
    fst_ref[:, FFN_CONV_W - 2, :] = up_pre

    act = (_gelu_tanh(up[:, 0:D_FF]) * up[:, D_FF:2 * D_FF]).astype(bf16)
    f = jnp.dot(act, w_down_ref[:, 0:D_MODEL], preferred_element_type=f32)
    y_ref[:, 0, :] = x1 + g2 * _rms_norm(f, g_post_ffn_ref[...])


def _sample_post_call(x, mod_s, attn, xr, yr, h0, cbuf, fbuf, params):
    B = x.shape[0]
    (_, _, conv_w, conv_b, wa, ba, wi, bi, lam, g_attn, g_rnn, w_out,
     g_post_mix, g_pre_ffn, w_up, fconv_w, fconv_b, w_down, g_post_ffn) = params
    return pl.pallas_call(
        _sample_post_kernel,
        out_shape=(
            jax.ShapeDtypeStruct((B, 1, D_MODEL), f32),
            jax.ShapeDtypeStruct((B, D_RNN), f32),
            jax.ShapeDtypeStruct((RNN_CONV_W - 1, B, D_RNN), f32),
            jax.ShapeDtypeStruct((B, FFN_CONV_W - 1, 2 * D_FF), f32),
        ),
        compiler_params=pltpu.CompilerParams(vmem_limit_bytes=VMEM_LIMIT_BYTES),
        name="sample_post",
    )(x, mod_s, attn, xr, yr, h0, cbuf, fbuf, conv_w, conv_b, wa, ba, wi, bi, lam,
      g_attn, g_rnn, w_out, g_post_mix, g_pre_ffn, w_up, fconv_w, fconv_b, w_down, g_post_ffn)


def kernel(x_prompt, x_sample, cache_k, cache_v, state_h, state_conv, state_ffn_conv, c_prompt, c_sample, w_ada, b_ada, g_pre_mix, w_in, conv_w, conv_b, w_a, b_a, w_i, b_i, lam, sinks, g_attn_out, g_rnn_out, w_out, g_post_mix, g_pre_ffn, w_up, ffn_conv_w, ffn_conv_b, w_down, g_post_ffn):
    depth = w_in.shape[0]
    assert depth == 1 and x_prompt.shape[0] == 1 and x_sample.shape[1] == 1
    T = x_prompt.shape[1]
    B = x_sample.shape[0]
    W = cache_k.shape[2]
    assert W == WINDOW and T % TOKEN_BLOCK == 0 and B % SAMPLE_CHUNK == 0

    row = lambda a: a[0].reshape(1, -1)
    params = (
        row(g_pre_mix), w_in[0], conv_w[0], row(conv_b),
        w_a[0].astype(bf16), row(b_a), w_i[0].astype(bf16), row(b_i), row(lam),
        row(g_attn_out), row(g_rnn_out), w_out[0], row(g_post_mix),
        row(g_pre_ffn),
        w_up[0], ffn_conv_w[0], row(ffn_conv_b), w_down[0],
        row(g_post_ffn),
    )

    c_all = jnp.concatenate(
        [c_sample, jnp.broadcast_to(c_prompt, (MOD_PAD_ROWS, D_MODEL))], axis=0)
    mod = _mod_call(c_all, w_ada[0], b_ada[0].reshape(1, -1))
    mod_s = mod

    yp, kp, vp, hp, convp, ffnp, (w_in_bf, w_out_bf, w_up_bf, w_down_bf) = _prompt_call(
        x_prompt[0], mod, B, sinks[0], params)
    params = list(params)
    params[1], params[11], params[14], params[17] = w_in_bf, w_out_bf, w_up_bf, w_down_bf
    assert params[11].shape == w_out.shape[1:] and params[17].shape == w_down.shape[1:]

    xs = x_sample
    q, kv, xr, yr = _sample_pre_call(xs, mod_s, params[0], params[1])
    ck = cache_k.reshape(B, W * N_KV_HEADS, HEAD_DIM)
    cv = cache_v.reshape(B, W * N_KV_HEADS, HEAD_DIM)
    attn3, kwin, vwin = _sample_attn_call(
        q, kv, ck, cv, sinks[0].reshape(N_HEADS, 1))
    attn = attn3
    ys, hs, convs, ffns = _sample_post_call(
        xs, mod_s, attn, xr, yr, state_h[0],
        jnp.transpose(state_conv[0], (1, 0, 2)), state_ffn_conv[0],
        params)
    convs = jnp.transpose(convs, (1, 0, 2))

    kv_shape = (1, 1, W, N_KV_HEADS, HEAD_DIM)
    kvs_shape = (1, B, W, N_KV_HEADS, HEAD_DIM)
    return (
        yp[None], ys,
        kp.reshape(kv_shape), vp.reshape(kv_shape), hp[None], convp[None, None], ffnp[None, None],
        kwin.reshape(kvs_shape), vwin.reshape(kvs_shape), hs[None], convs[None], ffns[None],
    )
```

```python
import math

import jax
import jax.numpy as jnp
from jax import lax
from jax.experimental import pallas as pl
from jax.experimental.pallas import tpu as pltpu

D_MODEL = 1024
N_HEADS = 8
N_KV_HEADS = 2
HEAD_DIM = 128
GQA_GROUP = N_HEADS // N_KV_HEADS
ATTN_WIDTH = N_HEADS * HEAD_DIM
KV_WIDTH = N_KV_HEADS * HEAD_DIM
WINDOW = 128
D_RNN = D_MODEL
RNN_BLOCKS = 8
RNN_BLOCK_W = D_RNN // RNN_BLOCKS
RG_C = 8.0
RNN_CONV_W = 4
D_FF = 2816
FFN_CONV_W = 3
RMS_EPS = 1e-6

C_K = ATTN_WIDTH
C_V = C_K + KV_WIDTH
C_XR = C_V + KV_WIDTH
C_YR = C_XR + D_RNN
IN_COLS = C_YR + D_RNN

SUBLANES = 8
SUB_ROWS = WINDOW
GROUPS = SUB_ROWS // SUBLANES
TOKEN_BLOCK = 256
NSUB = TOKEN_BLOCK // SUB_ROWS
FFN_CHUNK = 256
STAGE_SKEW = 3
WSTAGE_ROWS = 64
WSTAGE_SLOTS = 4
SAMPLE_CHUNK = 16
SAMPLE_ATTN_VMEM_BYTES = (2 * 2 * 2 * SAMPLE_CHUNK * N_KV_HEADS * WINDOW * HEAD_DIM * 4
                          + 8 * 1024 * 1024)
MOD_PAD_ROWS = 8
VMEM_LIMIT_BYTES = 56 * 1024 * 1024

ALIBI_SLOPES = tuple(2.0 ** (-8.0 * (h + 1) / N_HEADS) for h in range(N_HEADS))
Q_SCALE = HEAD_DIM ** -0.5
SQRT_2_OVER_PI = math.sqrt(2.0 / math.pi)
LOG2_E = 1.0 / math.log(2.0)
PADDED_OUT_COLS = D_MODEL + 128

bf16 = jnp.bfloat16
f32 = jnp.float32


def _rms_scale(x):
    return lax.rsqrt(jnp.mean(x * x, axis=-1, keepdims=True) + RMS_EPS)


def _rms_norm(x, g):
    return x * _rms_scale(x) * g


def _gelu_tanh(x):
    k1 = -2.0 * LOG2_E * SQRT_2_OVER_PI
    k3 = k1 * 0.044715
    return x / (1.0 + jnp.exp2(x * (k1 + k3 * (x * x))))


def _sigmoid(x):
    return 1.0 / (1.0 + jnp.exp2(x * (-LOG2_E)))


def _softplus_neg(lam):
    return jnp.maximum(-lam, 0.0) + jnp.log1p(jnp.exp(-jnp.abs(lam)))


def _dot(a, b):
    return jnp.dot(a.astype(bf16), b, preferred_element_type=f32)


def _dot_nt(a, b):
    return lax.dot_general(a, b, (((1,), (1,)), ((), ())), preferred_element_type=f32)


def _rglru_gates(xc, wa_ref, ba, wi_ref, bi, sp):
    a_parts, u_parts = [], []
    for n in range(RNN_BLOCKS):
        sl = slice(n * RNN_BLOCK_W, (n + 1) * RNN_BLOCK_W)
        xn = xc[:, sl]
        xb = xn.astype(bf16)
        r = _sigmoid(jnp.dot(xb, wa_ref[n], preferred_element_type=f32) + ba[:, sl])
        i = _sigmoid(jnp.dot(xb, wi_ref[n], preferred_element_type=f32) + bi[:, sl])
        log_a = (-RG_C) * r * sp[:, sl]
        t = jnp.tanh(log_a)
        one_minus_a2 = (-2.0 * t) / (1.0 - t)
        a_parts.append(jnp.exp(log_a))
        u_parts.append(jnp.sqrt(one_minus_a2) * (i * xn))
    return a_parts, u_parts


def _mod_kernel(c_ref, w_ref, b_ref, o_ref):
    c = c_ref[...]
    s = c * _sigmoid(c)
    o_ref[...] = _dot(s, w_ref[...].astype(bf16)) + b_ref[...]


def _mod_call(c_all, w_ada, b_ada):
    rows = c_all.shape[0]
    ncol = w_ada.shape[1]
    bn = D_MODEL
    return pl.pallas_call(
        _mod_kernel,
        grid=(ncol // bn,),
        in_specs=[
            pl.BlockSpec((rows, D_MODEL), lambda j: (0, 0)),
            pl.BlockSpec((D_MODEL, bn), lambda j: (0, j)),
            pl.BlockSpec((1, bn), lambda j: (0, j)),
        ],
        out_specs=pl.BlockSpec((rows, bn), lambda j: (0, j)),
        out_shape=jax.ShapeDtypeStruct((rows, ncol), f32),
        compiler_params=pltpu.CompilerParams(dimension_semantics=("arbitrary",)),
        name="adaln_mod",
    )(c_all, w_ada, b_ada)


def _groups(v, c0=None, c1=None):
    if c0 is None:
        return [v[j * SUBLANES:(j + 1) * SUBLANES, :] for j in range(GROUPS)]
    return [v[j * SUBLANES:(j + 1) * SUBLANES, c0:c1] for j in range(GROUPS)]


def _shifted_groups(X, tail_row, sub0, max_shift):
    wrapped = {}
    for j in range(GROUPS - max_shift, GROUPS):
        wrapped[j] = jnp.where(sub0, tail_row(j), pltpu.roll(X[j], 1, axis=0))
    sh = {}
    for d in range(1, max_shift + 1):
        sh[d] = [X[j - d] if j >= d else wrapped[j - d + GROUPS] for j in range(GROUPS)]
    return sh


def _scan_sub(a, u, h_in, sub_iota):
    A = _groups(a)
    L = _groups(u)
    for j in range(1, GROUPS):
        L[j] = A[j] * L[j - 1] + L[j]
        A[j] = A[j] * A[j - 1]
    ae, le = A[GROUPS - 1], L[GROUPS - 1]
    for s_ in (1, 2, 4):
        ok = sub_iota >= s_
        a_sh = pltpu.roll(ae, s_, axis=0)
        l_sh = pltpu.roll(le, s_, axis=0)
        le = jnp.where(ok, ae * l_sh + le, le)
        ae = jnp.where(ok, ae * a_sh, ae)
    hend = le + ae * h_in
    hprev = jnp.where(sub_iota == 0, h_in, pltpu.roll(hend, 1, axis=0))
    h = jnp.concatenate([L[j] + A[j] * hprev for j in range(GROUPS)], axis=0)
    h_out = jnp.broadcast_to(hend[SUBLANES - 1:SUBLANES, :], hend.shape)
    return h, h_out


def _prompt_kernel(
    x_hbm, mod_ref, sinks_ref,
    g_pre_mix_ref, w_in_hbm, conv_w_ref, conv_b_ref, wa_ref, ba_ref, wi_ref, bi_ref, lam_ref,
    g_attn_ref, g_rnn_ref, w_out_hbm, g_post_mix_ref, g_pre_ffn_ref, w_up_hbm,
    fconv_w_ref, fconv_b_ref, w_down_hbm, g_post_ffn_ref,
    y_hbm, kwin_hbm, vwin_hbm, hlast_ref, convst_ref, ffnst_ref,
    w_in_bf_hbm, w_out_bf_hbm, w_up_bf_hbm, w_down_bf_hbm,
    xbuf, ybuf, sem_in, sem_out, sem_st, kprev, vprev, kst, vst, xr_tail, up_tail, hcar, bias_s,
    hff_s, w_in_ref, w_out_ref, w_up_ref, w_down_ref, wstage, sem_w, sem_wout,
):
    step = pl.program_id(0)
    nblocks = pl.num_programs(0) - 1
    slot = lax.rem(step, 2)
    yslot = lax.rem(step, 3)
    fslot = lax.rem(step + 2, 3)

    def x_copies(st, sl):
        return [pltpu.make_async_copy(
            x_hbm.at[st * NSUB + s, :, j, :],
            xbuf.at[sl, pl.ds(s * SUB_ROWS + j * SUBLANES, SUBLANES), :],
            sem_in.at[sl]) for s in range(NSUB) for j in range(GROUPS)]

    def y_copies(st, sl):
        return [pltpu.make_async_copy(
            ybuf.at[sl, pl.ds(s * SUB_ROWS + j * SUBLANES, SUBLANES), :],
            y_hbm.at[st * NSUB + s, :, j, :],
            sem_out.at[sl]) for s in range(NSUB) for j in range(GROUPS)]

    def state_copies():
        cps = []
        for src, dst in ((kst, kwin_hbm), (vst, vwin_hbm)):
            cps += [pltpu.make_async_copy(src.at[pl.ds(j * SUBLANES, SUBLANES), :],
                                          dst.at[:, j, :], sem_st) for j in range(GROUPS)]
        return cps

    def weight_out_copies():
        return [pltpu.make_async_copy(w_in_ref, w_in_bf_hbm, sem_wout),
                pltpu.make_async_copy(w_up_ref, w_up_bf_hbm, sem_wout),
                pltpu.make_async_copy(w_out_ref.at[:, pl.ds(0, D_MODEL)], w_out_bf_hbm, sem_wout),
                pltpu.make_async_copy(w_down_ref.at[:, pl.ds(0, D_MODEL)], w_down_bf_hbm, sem_wout)]

    def load_cast(w_hbm, w_s, stage, chunk_rows):
        n_rows, n_cols = w_hbm.shape
        n = n_rows // chunk_rows
        depth = stage.shape[0]
        assert n >= depth

        def chunk(i, sl):
            return pltpu.make_async_copy(
                w_hbm.at[pl.ds(i * chunk_rows, chunk_rows), :],
                stage.at[sl, pl.ds(0, chunk_rows), pl.ds(0, n_cols)], sem_w.at[sl])

        for i in range(depth):
            chunk(i, i).start()

        def body(i, carry):
            sl = lax.rem(i, depth)
            chunk(i, sl).wait()
            r0 = pl.multiple_of(i * chunk_rows, chunk_rows)
            w_s[pl.ds(r0, chunk_rows), 0:n_cols] = (
                stage[sl, 0:chunk_rows, 0:n_cols].astype(bf16))

            @pl.when(i + depth < n)
            def _next():
                chunk(i + depth, sl).start()

            return carry

        lax.fori_loop(0, n, body, 0)

    @pl.when(step == 0)
    def _init():
        for cp in x_copies(0, 0):
            cp.start()
        load_cast(w_in_hbm, w_in_ref, wstage, WSTAGE_ROWS)
        load_cast(w_out_hbm, w_out_ref, ybuf, TOKEN_BLOCK)
        load_cast(w_up_hbm, w_up_ref, wstage, WSTAGE_ROWS)
        load_cast(w_down_hbm, w_down_ref, ybuf, TOKEN_BLOCK)
        for cp in weight_out_copies():
            cp.start()
        kprev[...] = jnp.zeros(kprev.shape, bf16)
        vprev[...] = jnp.zeros(vprev.shape, bf16)
        xr_tail[...] = jnp.zeros(xr_tail.shape, f32)
        up_tail[...] = jnp.zeros(up_tail.shape, f32)
        hcar[...] = jnp.zeros(hcar.shape, f32)
        hff_s[...] = jnp.zeros(hff_s.shape, bf16)
        ybuf[2] = jnp.zeros(ybuf.shape[1:], f32)
        rq = lax.broadcasted_iota(jnp.int32, (SUB_ROWS, 2 * SUB_ROWS), 0)
        ck = lax.broadcasted_iota(jnp.int32, (SUB_ROWS, 2 * SUB_ROWS), 1)
        rk = ck & (SUB_ROWS - 1)
        tq = (rq % SUBLANES) * GROUPS + rq // SUBLANES
        tk = (rk % SUBLANES) * GROUPS + rk // SUBLANES
        dist = tq - tk + jnp.where(ck < SUB_ROWS, WINDOW, 0)
        base = jnp.where((dist >= 0) & (dist < WINDOW), dist.astype(f32), jnp.inf)
        for h in range(N_HEADS):
            bias_s[h] = ALIBI_SLOPES[h] * base

    @pl.when(step + 1 < nblocks)
    def _prefetch():
        for cp in x_copies(step + 1, 1 - slot):
            cp.start()

    @pl.when(step >= 3)
    def _free_ybuf():
        for cp in y_copies(0, yslot):
            cp.wait()

    @pl.when(step < nblocks)
    def _wait_x():
        for cp in x_copies(step, slot):
            cp.wait()

    sh1 = mod_ref[0:1, 0 * D_MODEL:1 * D_MODEL]
    sc1 = mod_ref[0:1, 1 * D_MODEL:2 * D_MODEL]
    g1 = mod_ref[0:1, 2 * D_MODEL:3 * D_MODEL]
    sh2 = mod_ref[0:1, 3 * D_MODEL:4 * D_MODEL]
    sc2 = mod_ref[0:1, 4 * D_MODEL:5 * D_MODEL]
    g2 = mod_ref[0:1, 5 * D_MODEL:6 * D_MODEL]
    gs1 = g_pre_mix_ref[...] * (1.0 + sc1)
    gs2 = g_pre_ffn_ref[...] * (1.0 + sc2)
    sp = _softplus_neg(lam_ref[...])
    first_pen = jnp.where(step == 0, jnp.inf, 0.0)
    sub_iota = lax.broadcasted_iota(jnp.int32, (SUBLANES, D_RNN), 0)
    sub0_rnn = sub_iota == 0
    sub0_ffn = lax.broadcasted_iota(jnp.int32, (SUBLANES, FFN_CHUNK), 0) == 0
    conv_w = [conv_w_ref.at[jj:jj + 1, :] for jj in range(RNN_CONV_W)]

    sts = [dict() for _ in range(NSUB)]

    def rows(s):
        return pl.ds(s * SUB_ROWS, SUB_ROWS)

    def stage_pre(s):
        x = xbuf[slot, rows(s), :]
        sts[s]["hmix"] = (x * _rms_scale(x) * gs1 + sh1).astype(bf16)

    def stage_inproj(s):
        st = sts[s]
        hm = st["hmix"]
        st["q"] = (jnp.dot(hm, w_in_ref[:, 0:C_K], preferred_element_type=f32) * Q_SCALE).astype(bf16)
        kv = jnp.dot(hm, w_in_ref[:, C_K:C_XR], preferred_element_type=f32)
        st["kb"] = kv[:, 0:KV_WIDTH].astype(bf16)
        st["vb"] = kv[:, KV_WIDTH:2 * KV_WIDTH].astype(bf16)
        st["kv"] = kv
        st["xr"] = jnp.dot(hm, w_in_ref[:, C_XR:C_YR], preferred_element_type=f32)

    def stage_attn(s):
        st = sts[s]
        if s == 0:
            kp, vp = kprev[...], vprev[...]
        else:
            kp, vp = sts[s - 1]["kb"], sts[s - 1]["vb"]
        kw = jnp.concatenate([kp, st["kb"]], axis=0)
        vw = jnp.concatenate([vp, st["vb"]], axis=0)
        outs = []
        for h in range(N_HEADS):
            c = h // GQA_GROUP
            hs = slice(h * HEAD_DIM, (h + 1) * HEAD_DIM)
            cs = slice(c * HEAD_DIM, (c + 1) * HEAD_DIM)
            sc = _dot_nt(st["q"][:, hs], kw[:, cs])
            if s == 0:
                sc = jnp.concatenate([sc[:, 0:SUB_ROWS] - first_pen, sc[:, SUB_ROWS:]], axis=1)
            sc = sc - bias_s[h]
            sink = sinks_ref[h]
            m = jnp.maximum(jnp.max(sc, axis=-1, keepdims=True), sink)
            p = jnp.exp(sc - m)
            denom = jnp.sum(p, axis=-1, keepdims=True) + jnp.exp(sink - m)
            outs.append(jnp.dot(p.astype(bf16), vw[:, cs], preferred_element_type=f32) / denom)
        attn = jnp.concatenate(outs, axis=1)
        st["attn_n"] = _rms_norm(attn, g_attn_ref[...]).astype(bf16)

    def stage_rnn(s):
        st = sts[s]
        xr = st["xr"]
        X = _groups(xr)
        if s == 0:
            tail = lambda j: xr_tail[(j - (GROUPS - RNN_CONV_W + 1)) * SUBLANES + SUBLANES - 1:
                                     (j - (GROUPS - RNN_CONV_W + 1)) * SUBLANES + SUBLANES, :]
        else:
            pxr = sts[s - 1]["xr"]
            tail = lambda j: pxr[j * SUBLANES + SUBLANES - 1:(j + 1) * SUBLANES, :]
        sh = _shifted_groups(X, tail, sub0_rnn, RNN_CONV_W - 1)
        xcs = []
        for j in range(GROUPS):
            acc = conv_b_ref[...] + conv_w[RNN_CONV_W - 1][...] * X[j]
            for d in range(1, RNN_CONV_W):
                acc = acc + conv_w[RNN_CONV_W - 1 - d][...] * sh[d][j]
            xcs.append(acc)
        xc = jnp.concatenate(xcs, axis=0)
        a_parts, u_parts = _rglru_gates(xc, wa_ref, ba_ref[...], wi_ref, bi_ref[...], sp)
        a = jnp.concatenate(a_parts, axis=1)
        u = jnp.concatenate(u_parts, axis=1)
        h_in = hcar[...] if s == 0 else sts[s - 1]["h_out"]
        h, st["h_out"] = _scan_sub(a, u, h_in, sub_iota)
        yr = jnp.dot(st["hmix"], w_in_ref[:, C_YR:IN_COLS], preferred_element_type=f32)
        rnn = h * _gelu_tanh(yr)
        st["rnn_n"] = _rms_norm(rnn, g_rnn_ref[...]).astype(bf16)

    def stage_out(s):
        st = sts[s]
        mo = (jnp.dot(st["attn_n"], w_out_ref[0:ATTN_WIDTH, 0:D_MODEL], preferred_element_type=f32)
              + jnp.dot(st["rnn_n"], w_out_ref[ATTN_WIDTH:ATTN_WIDTH + D_RNN, 0:D_MODEL],
                        preferred_element_type=f32))
        x1 = xbuf[slot, rows(s), :] + g1 * _rms_norm(mo, g_post_mix_ref[...])
        ybuf[yslot, rows(s), :] = x1
        hff_s[slot, rows(s), :] = (x1 * _rms_scale(x1) * gs2 + sh2).astype(bf16)

    tail_g0 = GROUPS - (FFN_CONV_W - 1)
    hff = hff_s[1 - slot]
    acts, tails = [], {}

    def conv_chunk(c0):
        c1 = c0 + FFN_CHUNK
        up = jnp.dot(hff, w_up_ref[:, c0:c1], preferred_element_type=f32)
        w = [fconv_w_ref.at[jj:jj + 1, c0:c1] for jj in range(FFN_CONV_W)]
        outs = []
        for s in range(NSUB):
            r0 = s * SUB_ROWS
            X = [up[r0 + j * SUBLANES:r0 + (j + 1) * SUBLANES, :] for j in range(GROUPS)]
            if s == 0:
                tail = lambda j: up_tail[(j - tail_g0) * SUBLANES + SUBLANES - 1:
                                         (j - tail_g0 + 1) * SUBLANES, c0:c1]
            else:
                tail = lambda j, p0=r0 - SUB_ROWS: up[p0 + j * SUBLANES + SUBLANES - 1:
                                                      p0 + (j + 1) * SUBLANES, :]
            sh = _shifted_groups(X, tail, sub0_ffn, FFN_CONV_W - 1)
            for j in range(GROUPS):
                acc = fconv_b_ref[:, c0:c1] + w[FFN_CONV_W - 1][...] * X[j]
                for d in range(1, FFN_CONV_W):
                    acc = acc + w[FFN_CONV_W - 1 - d][...] * sh[d][j]
                outs.append(acc)
        last0 = (NSUB - 1) * SUB_ROWS
        for t in range(FFN_CONV_W - 1):
            r = last0 + (tail_g0 + t) * SUBLANES + SUBLANES - 1
            ffnst_ref[t:t + 1, c0:c1] = up[r:r + 1, :]
        tails[c0] = up[last0 + tail_g0 * SUBLANES:, :]
        return jnp.concatenate(outs, axis=0)

    def ffn_chunk(cc):
        gate = conv_chunk(cc * FFN_CHUNK)
        val = conv_chunk(D_FF + cc * FFN_CHUNK)
        acts.append((_gelu_tanh(gate) * val).astype(bf16))

    def ffn_finish():
        for c0, tl in tails.items():
            up_tail[:, c0:c0 + FFN_CHUNK] = tl
        act = jnp.concatenate(acts, axis=1)
        f = jnp.dot(act, w_down_ref[:, 0:D_MODEL], preferred_element_type=f32)
        ybuf[fslot] = ybuf[fslot] + g2 * _rms_norm(f, g_post_ffn_ref[...])

    stages = (stage_pre, stage_inproj, stage_attn, stage_rnn, stage_out)
    n_times = len(stages) + (NSUB - 1) * STAGE_SKEW
    n_chunks = D_FF // FFN_CHUNK
    ffn_chunk(0)
    nxt = 1
    for t in range(n_times):
        for s in range(NSUB):
            k = t - s * STAGE_SKEW
            if 0 <= k < len(stages):
                stages[k](s)
                if nxt < n_chunks:
                    ffn_chunk(nxt)
                    nxt += 1
    while nxt < n_chunks:
        ffn_chunk(nxt)
        nxt += 1
    ffn_finish()

    lst = sts[NSUB - 1]
    kprev[...] = lst["kb"]
    vprev[...] = lst["vb"]
    xr_tail[...] = lst["xr"][(GROUPS - (RNN_CONV_W - 1)) * SUBLANES:, :]
    hcar[...] = lst["h_out"]

    @pl.when(step < nblocks)
    def _state():
        kst[...] = lst["kv"][:, 0:KV_WIDTH]
        vst[...] = lst["kv"][:, KV_WIDTH:2 * KV_WIDTH]
        for t in range(RNN_CONV_W - 1):
            r = (GROUPS - (RNN_CONV_W - 1) + t) * SUBLANES + SUBLANES - 1
            convst_ref[t:t + 1, :] = lst["xr"][r:r + 1, :]
        hlast_ref[...] = lst["h_out"][0:1, :]

    @pl.when(step >= 1)
    def _store_y():
        for cp in y_copies(step - 1, fslot):
            cp.start()

    @pl.when(step == nblocks)
    def _finish():
        for cp in state_copies():
            cp.start()
        for cp in state_copies() + weight_out_copies():
            cp.wait()
        for cp in y_copies(0, lax.rem(step + 1, 3)) + y_copies(0, fslot):
            cp.wait()


def _const_spec(shape):
    nd = len(shape)
    return pl.BlockSpec(shape, lambda i: (0,) * nd)


def _prompt_call(x, mod, mod_row0, sinks, params):
    T = x.shape[0]
    TB = TOKEN_BLOCK
    assert T // TB >= 3
    (g_pre_mix, w_in, conv_w, conv_b, wa, ba, wi, bi, lam, g_attn, g_rnn, w_out,
     g_post_mix, g_pre_ffn, w_up, fconv_w, fconv_b, w_down, g_post_ffn) = params
    x4 = x.reshape(T // SUB_ROWS, SUBLANES, GROUPS, D_MODEL)
    ins = [x4, mod, sinks, g_pre_mix, w_in, conv_w, conv_b, wa, ba, wi, bi, lam, g_attn, g_rnn,
           w_out, g_post_mix, g_pre_ffn, w_up, fconv_w, fconv_b, w_down, g_post_ffn]
    assert mod_row0 % MOD_PAD_ROWS == 0
    in_specs = [pl.BlockSpec(memory_space=pl.ANY),
                pl.BlockSpec((MOD_PAD_ROWS, mod.shape[1]), lambda i: (mod_row0 // MOD_PAD_ROWS, 0)),
                pl.BlockSpec(memory_space=pltpu.SMEM)]
    big = (w_in, w_out, w_up, w_down)
    assert all(w.dtype == f32 and w.shape[0] % TOKEN_BLOCK == 0 for w in big)
    in_specs += [pl.BlockSpec(memory_space=pl.ANY) if any(a is w for w in big)
                 else _const_spec(a.shape) for a in ins[3:]]
    out_shape = (
        jax.ShapeDtypeStruct(x4.shape, f32),
        jax.ShapeDtypeStruct((SUBLANES, GROUPS, KV_WIDTH), f32),
        jax.ShapeDtypeStruct((SUBLANES, GROUPS, KV_WIDTH), f32),
        jax.ShapeDtypeStruct((1, D_RNN), f32),
        jax.ShapeDtypeStruct((RNN_CONV_W - 1, D_RNN), f32),
        jax.ShapeDtypeStruct((FFN_CONV_W - 1, 2 * D_FF), f32),
    ) + tuple(jax.ShapeDtypeStruct(w.shape, bf16) for w in big)
    out_specs = (
        pl.BlockSpec(memory_space=pl.ANY),
        pl.BlockSpec(memory_space=pl.ANY),
        pl.BlockSpec(memory_space=pl.ANY),
        _const_spec((1, D_RNN)),
        _const_spec((RNN_CONV_W - 1, D_RNN)),
        _const_spec((FFN_CONV_W - 1, 2 * D_FF)),
    ) + (pl.BlockSpec(memory_space=pl.ANY),) * len(big)
    scratch = [
        pltpu.VMEM((2, TB, D_MODEL), f32),
        pltpu.VMEM((3, TB, D_MODEL), f32),
        pltpu.SemaphoreType.DMA((2,)),
        pltpu.SemaphoreType.DMA((3,)),
        pltpu.SemaphoreType.DMA(()),
        pltpu.VMEM((SUB_ROWS, KV_WIDTH), bf16),
        pltpu.VMEM((SUB_ROWS, KV_WIDTH), bf16),
        pltpu.VMEM((SUB_ROWS, KV_WIDTH), f32),
        pltpu.VMEM((SUB_ROWS, KV_WIDTH), f32),
        pltpu.VMEM(((RNN_CONV_W - 1) * SUBLANES, D_RNN), f32),
        pltpu.VMEM(((FFN_CONV_W - 1) * SUBLANES, 2 * D_FF), f32),
        pltpu.VMEM((SUBLANES, D_RNN), f32),
        pltpu.VMEM((N_HEADS, SUB_ROWS, 2 * SUB_ROWS), f32),
        pltpu.VMEM((2, TOKEN_BLOCK, D_MODEL), bf16),
        pltpu.VMEM(w_in.shape, bf16),
        pltpu.VMEM((w_out.shape[0], PADDED_OUT_COLS), bf16),
        pltpu.VMEM(w_up.shape, bf16),
        pltpu.VMEM((w_down.shape[0], PADDED_OUT_COLS), bf16),
        pltpu.VMEM((WSTAGE_SLOTS, WSTAGE_ROWS, w_up.shape[1]), f32),
        pltpu.SemaphoreType.DMA((WSTAGE_SLOTS,)),
        pltpu.SemaphoreType.DMA(()),
    ]
    assert w_in.shape[1] <= w_up.shape[1] and w_in.shape[0] % WSTAGE_ROWS == 0
    y4, kp, vp, hp, convp, ffnp, w_in_bf, w_out_bf, w_up_bf, w_down_bf = pl.pallas_call(
        _prompt_kernel,
        grid=(T // TB + 1,),
        in_specs=in_specs,
        out_specs=out_specs,
        out_shape=out_shape,
        scratch_shapes=scratch,
        compiler_params=pltpu.CompilerParams(
            dimension_semantics=("arbitrary",), vmem_limit_bytes=VMEM_LIMIT_BYTES),
        name="prompt_layer",
    )(*ins)
    return (y4.reshape(T, D_MODEL), kp.reshape(WINDOW, KV_WIDTH), vp.reshape(WINDOW, KV_WIDTH),
            hp, convp, ffnp, (w_in_bf, w_out_bf, w_up_bf, w_down_bf))


def _sample_pre_kernel(x_ref, mod_ref, g_pre_mix_ref, w_in_ref, q_ref, kv_ref, xr_ref, yr_ref):
    B = x_ref.shape[0]
    sh1 = mod_ref[0:B, 0 * D_MODEL:1 * D_MODEL]
    sc1 = mod_ref[0:B, 1 * D_MODEL:2 * D_MODEL]
    hmix = (_rms_norm(x_ref[:, 0, :], g_pre_mix_ref[...]) * (1.0 + sc1) + sh1).astype(bf16)
    q = jnp.dot(hmix, w_in_ref[:, 0:C_K], preferred_element_type=f32) * Q_SCALE
    for h in range(N_HEADS):
        q_ref[:, h, :] = q[:, h * HEAD_DIM:(h + 1) * HEAD_DIM]
    kv_ref[...] = jnp.dot(hmix, w_in_ref[:, C_K:C_XR], preferred_element_type=f32)
    xr_ref[...] = jnp.dot(hmix, w_in_ref[:, C_XR:C_YR], preferred_element_type=f32)
    yr_ref[...] = jnp.dot(hmix, w_in_ref[:, C_YR:IN_COLS], preferred_element_type=f32)


def _sample_pre_call(x, mod_s, g_pre_mix, w_in):
    B = x.shape[0]
    return pl.pallas_call(
        _sample_pre_kernel,
        out_shape=(
            jax.ShapeDtypeStruct((B, N_HEADS, HEAD_DIM), f32),
            jax.ShapeDtypeStruct((B, 2 * KV_WIDTH), f32),
            jax.ShapeDtypeStruct((B, D_RNN), f32),
            jax.ShapeDtypeStruct((B, D_RNN), f32),
        ),
        compiler_params=pltpu.CompilerParams(vmem_limit_bytes=VMEM_LIMIT_BYTES),
        name="sample_pre",
    )(x, mod_s, g_pre_mix, w_in)


def _sample_attn_kernel(q_ref, kv_ref, ck_ref, cv_ref, sinks_ref, o_ref, kwin_ref, vwin_ref):
    R = N_KV_HEADS * WINDOW
    hrow = lax.broadcasted_iota(jnp.int32, (N_HEADS, R), 0)
    rcol = lax.broadcasted_iota(jnp.int32, (N_HEADS, R), 1)
    slope = jnp.exp2(-8.0 * (hrow + 1).astype(f32) / N_HEADS)
    own = (rcol % N_KV_HEADS) == (hrow // GQA_GROUP)
    bias = jnp.where(own, slope * (WINDOW - 1 - rcol // N_KV_HEADS).astype(f32), jnp.inf)
    wrow = lax.broadcasted_iota(jnp.int32, (R, HEAD_DIM), 0)
    sink = sinks_ref[...]

    def shifted(cache, new_rows):
        out = pltpu.roll(cache, R - N_KV_HEADS, axis=0)
        for c in range(N_KV_HEADS):
            out = jnp.where(wrow == R - N_KV_HEADS + c, new_rows[c], out)
        return out

    scores = []
    for b in range(SAMPLE_CHUNK):
        knew = [kv_ref[b:b + 1, c * HEAD_DIM:(c + 1) * HEAD_DIM] for c in range(N_KV_HEADS)]
        kw = shifted(ck_ref[b], knew)
        kwin_ref[b] = kw
        scores.append(_dot_nt(q_ref[b].astype(bf16), kw.astype(bf16)) - bias)
    probs, denoms = [], []
    for b in range(SAMPLE_CHUNK):
        s = scores[b]
        m = jnp.maximum(jnp.max(s, axis=-1, keepdims=True), sink)
        p = jnp.exp(s - m)
        denoms.append(jnp.sum(p, axis=-1, keepdims=True) + jnp.exp(sink - m))
        probs.append(p.astype(bf16))
    for b in range(SAMPLE_CHUNK):
        vnew = [kv_ref[b:b + 1, KV_WIDTH + c * HEAD_DIM:KV_WIDTH + (c + 1) * HEAD_DIM]
                for c in range(N_KV_HEADS)]
        vw = shifted(cv_ref[b], vnew)
        vwin_ref[b] = vw
        o_ref[b] = (jnp.dot(probs[b], vw.astype(bf16), preferred_element_type=f32)
                    / denoms[b])


def _sample_attn_call(q, kv, ck, cv, sinks_col):
    B = q.shape[0]
    BC = SAMPLE_CHUNK
    return pl.pallas_call(
        _sample_attn_kernel,
        grid=(B // BC,),
        in_specs=[
            pl.BlockSpec((BC, N_HEADS, HEAD_DIM), lambda i: (i, 0, 0)),
            pl.BlockSpec((BC, 2 * KV_WIDTH), lambda i: (i, 0)),
            pl.BlockSpec((BC, N_KV_HEADS * WINDOW, HEAD_DIM), lambda i: (i, 0, 0)),
            pl.BlockSpec((BC, N_KV_HEADS * WINDOW, HEAD_DIM), lambda i: (i, 0, 0)),
            pl.BlockSpec((N_HEADS, 1), lambda i: (0, 0)),
        ],
        out_specs=(
            pl.BlockSpec((BC, N_HEADS, HEAD_DIM), lambda i: (i, 0, 0)),
            pl.BlockSpec((BC, N_KV_HEADS * WINDOW, HEAD_DIM), lambda i: (i, 0, 0)),
            pl.BlockSpec((BC, N_KV_HEADS * WINDOW, HEAD_DIM), lambda i: (i, 0, 0)),
        ),
        out_shape=(
            jax.ShapeDtypeStruct((B, N_HEADS, HEAD_DIM), f32),
            jax.ShapeDtypeStruct((B, N_KV_HEADS * WINDOW, HEAD_DIM), f32),
            jax.ShapeDtypeStruct((B, N_KV_HEADS * WINDOW, HEAD_DIM), f32),
        ),
        compiler_params=pltpu.CompilerParams(
            dimension_semantics=("arbitrary",), vmem_limit_bytes=SAMPLE_ATTN_VMEM_BYTES),
        name="sample_attn",
    )(q, kv, ck, cv, sinks_col)


def _sample_post_kernel(
    x_ref, mod_ref, attn_ref, xr_ref, yr_ref, h0_ref, cbuf_ref, fbuf_ref,
    conv_w_ref, conv_b_ref, wa_ref, ba_ref, wi_ref, bi_ref, lam_ref,
    g_attn_ref, g_rnn_ref, w_out_ref, g_post_mix_ref, g_pre_ffn_ref, w_up_ref,
    fconv_w_ref, fconv_b_ref, w_down_ref, g_post_ffn_ref,
    y_ref, h_ref, cst_ref, fst_ref,
):
    B = x_ref.shape[0]
    g1 = mod_ref[0:B, 2 * D_MODEL:3 * D_MODEL]
    sh2 = mod_ref[0:B, 3 * D_MODEL:4 * D_MODEL]
    sc2 = mod_ref[0:B, 4 * D_MODEL:5 * D_MODEL]
    g2 = mod_ref[0:B, 5 * D_MODEL:6 * D_MODEL]
    x = x_ref[:, 0, :]
    xr = xr_ref[...]

    xc = conv_b_ref[...] + conv_w_ref[RNN_CONV_W - 1:RNN_CONV_W, :] * xr
    for jj in range(RNN_CONV_W - 1):
        xc = xc + conv_w_ref[jj:jj + 1, :] * cbuf_ref[jj]
    for jj in range(RNN_CONV_W - 2):
        cst_ref[jj] = cbuf_ref[jj + 1]
    cst_ref[RNN_CONV_W - 2] = xr

    sp = _softplus_neg(lam_ref[...])
    a_parts, u_parts = _rglru_gates(xc, wa_ref, ba_ref[...], wi_ref, bi_ref[...], sp)
    a = jnp.concatenate(a_parts, axis=1)
    u = jnp.concatenate(u_parts, axis=1)
    h = a * h0_ref[...] + u
    h_ref[...] = h
    rnn = h * _gelu_tanh(yr_ref[...])

    attn = jnp.concatenate([attn_ref[:, h, :] for h in range(N_HEADS)], axis=1)
    attn_n = _rms_norm(attn, g_attn_ref[...])
    rnn_n = _rms_norm(rnn, g_rnn_ref[...])
    mo = (_dot(attn_n, w_out_ref[0:ATTN_WIDTH, 0:D_MODEL])
          + _dot(rnn_n, w_out_ref[ATTN_WIDTH:ATTN_WIDTH + D_RNN, 0:D_MODEL]))
    x1 = x + g1 * _rms_norm(mo, g_post_mix_ref[...])

    hff = (_rms_norm(x1, g_pre_ffn_ref[...]) * (1.0 + sc2) + sh2).astype(bf16)
    up_pre = jnp.dot(hff, w_up_ref[...], preferred_element_type=f32)
    up = fconv_b_ref[...] + fconv_w_ref[FFN_CONV_W - 1:FFN_CONV_W, :] * up_pre
    for jj in range(FFN_CONV_W - 1):
        up = up + fconv_w_ref[jj:jj + 1, :] * fbuf_ref[:, jj, :]
    for jj in range(FFN_CONV_W - 2):
        fst_ref[:, jj, :] = fbuf_ref[:, jj + 1, :]
    fst_ref[:, FFN_CONV_W - 2, :] = up_pre

    act = (_gelu_tanh(up[:, 0:D_FF]) * up[:, D_FF:2 * D_FF]).astype(bf16)
    f = jnp.dot(act, w_down_ref[:, 0:D_MODEL], preferred_element_type=f32)
    y_ref[:, 0, :] = x1 + g2 * _rms_norm(f, g_post_ffn_ref[...])


def _sample_post_call(x, mod_s, attn, xr, yr, h0, cbuf, fbuf, params):
    B = x.shape[0]
    (_, _, conv_w, conv_b, wa, ba, wi, bi, lam, g_attn, g_rnn, w_out,
     g_post_mix, g_pre_ffn, w_up, fconv_w, fconv_b, w_down, g_post_ffn) = params
    return pl.pallas_call(
        _sample_post_kernel,
        out_shape=(
            jax.ShapeDtypeStruct((B, 1, D_MODEL), f32),
            jax.ShapeDtypeStruct((B, D_RNN), f32),
            jax.ShapeDtypeStruct((RNN_CONV_W - 1, B, D_RNN), f32),
            jax.ShapeDtypeStruct((B, FFN_CONV_W - 1, 2 * D_FF), f32),
        ),
        compiler_params=pltpu.CompilerParams(vmem_limit_bytes=VMEM_LIMIT_BYTES),
        name="sample_post",
    )(x, mod_s, attn, xr, yr, h0, cbuf, fbuf, conv_w, conv_b, wa, ba, wi, bi, lam,
      g_attn, g_rnn, w_out, g_post_mix, g_pre_ffn, w_up, fconv_w, fconv_b, w_down, g_post_ffn)


def kernel(x_prompt, x_sample, cache_k, cache_v, state_h, state_conv, state_ffn_conv, c_prompt, c_sample, w_ada, b_ada, g_pre_mix, w_in, conv_w, conv_b, w_a, b_a, w_i, b_i, lam, sinks, g_attn_out, g_rnn_out, w_out, g_post_mix, g_pre_ffn, w_up, ffn_conv_w, ffn_conv_b, w_down, g_post_ffn):
    depth = w_in.shape[0]
    assert depth == 1 and x_prompt.shape[0] == 1 and x_sample.shape[1] == 1
    T = x_prompt.shape[1]
    B = x_sample.shape[0]
    W = cache_k.shape[2]
    assert W == WINDOW and T % TOKEN_BLOCK == 0 and B % SAMPLE_CHUNK == 0

    row = lambda a: a[0].reshape(1, -1)
    params = (
        row(g_pre_mix), w_in[0], conv_w[0], row(conv_b),
        w_a[0].astype(bf16), row(b_a), w_i[0].astype(bf16), row(b_i), row(lam),
        row(g_attn_out), row(g_rnn_out), w_out[0], row(g_post_mix),
        row(g_pre_ffn),
        w_up[0], ffn_conv_w[0], row(ffn_conv_b), w_down[0],
        row(g_post_ffn),
    )

    c_all = jnp.concatenate(
        [c_sample, jnp.broadcast_to(c_prompt, (MOD_PAD_ROWS, D_MODEL))], axis=0)
    mod = _mod_call(c_all, w_ada[0], b_ada[0].reshape(1, -1))
    mod_s = mod

    yp, kp, vp, hp, convp, ffnp, (w_in_bf, w_out_bf, w_up_bf, w_down_bf) = _prompt_call(
        x_prompt[0], mod, B, sinks[0], params)
    params = list(params)
    params[1], params[11], params[14], params[17] = w_in_bf, w_out_bf, w_up_bf, w_down_bf
    assert params[11].shape == w_out.shape[1:] and params[17].shape == w_down.shape[1:]

    xs = x_sample
    q, kv, xr, yr = _sample_pre_call(xs, mod_s, params[0], params[1])
    ck = cache_k.reshape(B, W * N_KV_HEADS, HEAD_DIM)
    cv = cache_v.reshape(B, W * N_KV_HEADS, HEAD_DIM)
    attn3, kwin, vwin = _sample_attn_call(
        q, kv, ck, cv, sinks[0].reshape(N_HEADS, 1))
    attn = attn3
    ys, hs, convs, ffns = _sample_post_call(
        xs, mod_s, attn, xr, yr, state_h[0],
        jnp.transpose(state_conv[0], (1, 0, 2)), state_ffn_conv[0],
        params)
    convs = jnp.transpose(convs, (1, 0, 2))

    kv_shape = (1, 1, W, N_KV_HEADS, HEAD_DIM)
    kvs_shape = (1, B, W, N_KV_HEADS, HEAD_DIM)
    return (
        yp[None], ys,
        kp.reshape(kv_shape), vp.reshape(kv_shape), hp[None], convp[None, None], ffnp[None, None],
        kwin.reshape(kvs_shape), vwin.reshape(kvs_shape), hs[None], convs[None], ffns[None],
    )
```

```python
import math

import jax
import jax.numpy as jnp
from jax import lax
from jax.experimental import pallas as pl
from jax.experimental.pallas import tpu as pltpu

D_MODEL = 1024
N_HEADS = 8
N_KV_HEADS = 2
HEAD_DIM = 128
GQA_GROUP = N_HEADS // N_KV_HEADS
ATTN_WIDTH = N_HEADS * HEAD_DIM
KV_WIDTH = N_KV_HEADS * HEAD_DIM
WINDOW = 128
D_RNN = D_MODEL
RNN_BLOCKS = 8
RNN_BLOCK_W = D_RNN // RNN_BLOCKS
RG_C = 8.0
RNN_CONV_W = 4
D_FF = 2816
FFN_CONV_W = 3
RMS_EPS = 1e-6

C_K = ATTN_WIDTH
C_V = C_K + KV_WIDTH
C_XR = C_V + KV_WIDTH
C_YR = C_XR + D_RNN
IN_COLS = C_YR + D_RNN

SUBLANES = 8
SUB_ROWS = WINDOW
GROUPS = SUB_ROWS // SUBLANES
TOKEN_BLOCK = 256
NSUB = TOKEN_BLOCK // SUB_ROWS
FFN_CHUNK = 256
STAGE_SKEW = 3
WSTAGE_ROWS = 64
WSTAGE_SLOTS = 4
SAMPLE_CHUNK = 16
SAMPLE_ATTN_VMEM_BYTES = (2 * 2 * 2 * SAMPLE_CHUNK * N_KV_HEADS * WINDOW * HEAD_DIM * 4
                          + 8 * 1024 * 1024)
MOD_PAD_ROWS = 8
VMEM_LIMIT_BYTES = 56 * 1024 * 1024

ALIBI_SLOPES = tuple(2.0 ** (-8.0 * (h + 1) / N_HEADS) for h in range(N_HEADS))
Q_SCALE = HEAD_DIM ** -0.5
SQRT_2_OVER_PI = math.sqrt(2.0 / math.pi)
LOG2_E = 1.0 / math.log(2.0)
PADDED_OUT_COLS = D_MODEL + 128

bf16 = jnp.bfloat16
f32 = jnp.float32


def _rms_scale(x):
    return lax.rsqrt(jnp.mean(x * x, axis=-1, keepdims=True) + RMS_EPS)


def _rms_norm(x, g):
    return x * _rms_scale(x) * g


def _gelu_tanh(x):
    k1 = -2.0 * LOG2_E * SQRT_2_OVER_PI
    k3 = k1 * 0.044715
    return x / (1.0 + jnp.exp2(x * (k1 + k3 * (x * x))))


def _sigmoid(x):
    return 1.0 / (1.0 + jnp.exp2(x * (-LOG2_E)))


def _softplus_neg(lam):
    return jnp.maximum(-lam, 0.0) + jnp.log1p(jnp.exp(-jnp.abs(lam)))


def _dot(a, b):
    return jnp.dot(a.astype(bf16), b, preferred_element_type=f32)


def _dot_nt(a, b):
    return lax.dot_general(a, b, (((1,), (1,)), ((), ())), preferred_element_type=f32)


def _rglru_gates(xc, wa_ref, ba, wi_ref, bi, sp):
    a_parts, u_parts = [], []
    for n in range(RNN_BLOCKS):
        sl = slice(n * RNN_BLOCK_W, (n + 1) * RNN_BLOCK_W)
        xn = xc[:, sl]
        xb = xn.astype(bf16)
        r = _sigmoid(jnp.dot(xb, wa_ref[n], preferred_element_type=f32) + ba[:, sl])
        i = _sigmoid(jnp.dot(xb, wi_ref[n], preferred_element_type=f32) + bi[:, sl])
        log_a = (-RG_C) * r * sp[:, sl]
        t = jnp.tanh(log_a)
        one_minus_a2 = (-2.0 * t) / (1.0 - t)
        a_parts.append(jnp.exp(log_a))
        u_parts.append(jnp.sqrt(one_minus_a2) * (i * xn))
    return a_parts, u_parts


def _mod_kernel(c_ref, w_ref, b_ref, o_ref):
    c = c_ref[...]
    s = c * _sigmoid(c)
    o_ref[...] = _dot(s, w_ref[...].astype(bf16)) + b_ref[...]


def _mod_call(c_all, w_ada, b_ada):
    rows = c_all.shape[0]
    ncol = w_ada.shape[1]
    bn = D_MODEL
    return pl.pallas_call(
        _mod_kernel,
        grid=(ncol // bn,),
        in_specs=[
            pl.BlockSpec((rows, D_MODEL), lambda j: (0, 0)),
            pl.BlockSpec((D_MODEL, bn), lambda j: (0, j)),
            pl.BlockSpec((1, bn), lambda j: (0, j)),
        ],
        out_specs=pl.BlockSpec((rows, bn), lambda j: (0, j)),
        out_shape=jax.ShapeDtypeStruct((rows, ncol), f32),
        compiler_params=pltpu.CompilerParams(dimension_semantics=("arbitrary",)),
        name="adaln_mod",
    )(c_all, w_ada, b_ada)


def _groups(v, c0=None, c1=None):
    if c0 is None:
        return [v[j * SUBLANES:(j + 1) * SUBLANES, :] for j in range(GROUPS)]
    return [v[j * SUBLANES:(j + 1) * SUBLANES, c0:c1] for j in range(GROUPS)]


def _shifted_groups(X, tail_row, sub0, max_shift):
    wrapped = {}
    for j in range(GROUPS - max_shift, GROUPS):
        wrapped[j] = jnp.where(sub0, tail_row(j), pltpu.roll(X[j], 1, axis=0))
    sh = {}
    for d in range(1, max_shift + 1):
        sh[d] = [X[j - d] if j >= d else wrapped[j - d + GROUPS] for j in range(GROUPS)]
    return sh


def _scan_sub(a, u, h_in, sub_iota):
    A = _groups(a)
    L = _groups(u)
    for j in range(1, GROUPS):
        L[j] = A[j] * L[j - 1] + L[j]
        A[j] = A[j] * A[j - 1]
    ae, le = A[GROUPS - 1], L[GROUPS - 1]
    for s_ in (1, 2, 4):
        ok = sub_iota >= s_
        a_sh = pltpu.roll(ae, s_, axis=0)
        l_sh = pltpu.roll(le, s_, axis=0)
        le = jnp.where(ok, ae * l_sh + le, le)
        ae = jnp.where(ok, ae * a_sh, ae)
    hend = le + ae * h_in
    hprev = jnp.where(sub_iota == 0, h_in, pltpu.roll(hend, 1, axis=0))
    h = jnp.concatenate([L[j] + A[j] * hprev for j in range(GROUPS)], axis=0)
    h_out = jnp.broadcast_to(hend[SUBLANES - 1:SUBLANES, :], hend.shape)
    return h, h_out


def _prompt_kernel(
    x_hbm, mod_ref, sinks_ref,
    g_pre_mix_ref, w_in_hbm, conv_w_ref, conv_b_ref, wa_ref, ba_ref, wi_ref, bi_ref, lam_ref,
    g_attn_ref, g_rnn_ref, w_out_hbm, g_post_mix_ref, g_pre_ffn_ref, w_up_hbm,
    fconv_w_ref, fconv_b_ref, w_down_hbm, g_post_ffn_ref,
    y_hbm, kwin_hbm, vwin_hbm, hlast_ref, convst_ref, ffnst_ref,
    w_in_bf_hbm, w_out_bf_hbm, w_up_bf_hbm, w_down_bf_hbm,
    xbuf, ybuf, sem_in, sem_out, sem_st, kprev, vprev, kst, vst, xr_tail, up_tail, hcar, bias_s,
    hff_s, w_in_ref, w_out_ref, w_up_ref, w_down_ref, wstage, sem_w, sem_wout,
):
    step = pl.program_id(0)
    nblocks = pl.num_programs(0) - 1
    slot = lax.rem(step, 2)
    yslot = lax.rem(step, 3)
    fslot = lax.rem(step + 2, 3)

    def x_copies(st, sl):
        return [pltpu.make_async_copy(
            x_hbm.at[st * NSUB + s, c, :, :],
            xbuf.at[sl, pl.ds(s * GROUPS, GROUPS), c, :],
            sem_in.at[sl]) for s in range(NSUB) for c in range(SUBLANES)]

    def y_copies(st, sl):
        return [pltpu.make_async_copy(
            ybuf.at[sl, pl.ds(s * GROUPS, GROUPS), c, :],
            y_hbm.at[st * NSUB + s, c, :, :],
            sem_out.at[sl]) for s in range(NSUB) for c in range(SUBLANES)]

    def state_copies():
        cps = []
        for src, dst in ((kst, kwin_hbm), (vst, vwin_hbm)):
            cps += [pltpu.make_async_copy(src.at[pl.ds(j * SUBLANES, SUBLANES), :],
                                          dst.at[:, j, :], sem_st) for j in range(GROUPS)]
        return cps

    def weight_out_copies():
        return [pltpu.make_async_copy(w_in_ref, w_in_bf_hbm, sem_wout),
                pltpu.make_async_copy(w_up_ref, w_up_bf_hbm, sem_wout),
                pltpu.make_async_copy(w_out_ref.at[:, pl.ds(0, D_MODEL)], w_out_bf_hbm, sem_wout),
                pltpu.make_async_copy(w_down_ref.at[:, pl.ds(0, D_MODEL)], w_down_bf_hbm, sem_wout)]

    def load_cast(w_hbm, w_s, stage, chunk_rows):
        tiled = len(stage.shape) == 4
        if tiled:
            n_rows, n_cols = w_hbm.shape[0] * SUBLANES, w_hbm.shape[2]
            assert stage.shape[1:] == (chunk_rows // SUBLANES, SUBLANES, n_cols)
        else:
            n_rows, n_cols = w_hbm.shape
        n = n_rows // chunk_rows
        depth = stage.shape[0]
        assert n >= depth

        def chunk(i, sl):
            if tiled:
                g = chunk_rows // SUBLANES
                return pltpu.make_async_copy(w_hbm.at[pl.ds(i * g, g), :, :], stage.at[sl],
                                             sem_w.at[sl])
            return pltpu.make_async_copy(
                w_hbm.at[pl.ds(i * chunk_rows, chunk_rows), :],
                stage.at[sl, pl.ds(0, chunk_rows), pl.ds(0, n_cols)], sem_w.at[sl])

        def staged(sl):
            if tiled:
                return stage[sl].reshape(chunk_rows, n_cols)
            return stage[sl, 0:chunk_rows, 0:n_cols]

        for i in range(depth):
            chunk(i, i).start()

        def body(i, carry):
            sl = lax.rem(i, depth)
            chunk(i, sl).wait()
            r0 = pl.multiple_of(i * chunk_rows, chunk_rows)
            w_s[pl.ds(r0, chunk_rows), 0:n_cols] = staged(sl).astype(bf16)

            @pl.when(i + depth < n)
            def _next():
                chunk(i + depth, sl).start()

            return carry

        lax.fori_loop(0, n, body, 0)

    @pl.when(step == 0)
    def _init():
        for cp in x_copies(0, 0):
            cp.start()
        load_cast(w_in_hbm, w_in_ref, wstage, WSTAGE_ROWS)
        load_cast(w_out_hbm, w_out_ref, ybuf, TOKEN_BLOCK)
        load_cast(w_up_hbm, w_up_ref, wstage, WSTAGE_ROWS)
        load_cast(w_down_hbm, w_down_ref, ybuf, TOKEN_BLOCK)
        for cp in weight_out_copies():
            cp.start()
        kprev[...] = jnp.zeros(kprev.shape, bf16)
        vprev[...] = jnp.zeros(vprev.shape, bf16)
        xr_tail[...] = jnp.zeros(xr_tail.shape, f32)
        up_tail[...] = jnp.zeros(up_tail.shape, f32)
        hcar[...] = jnp.zeros(hcar.shape, f32)
        hff_s[...] = jnp.zeros(hff_s.shape, bf16)
        ybuf[2] = jnp.zeros(ybuf.shape[1:], f32)
        rq = lax.broadcasted_iota(jnp.int32, (SUB_ROWS, 2 * SUB_ROWS), 0)
        ck = lax.broadcasted_iota(jnp.int32, (SUB_ROWS, 2 * SUB_ROWS), 1)
        rk = ck & (SUB_ROWS - 1)
        tq = (rq % SUBLANES) * GROUPS + rq // SUBLANES
        tk = (rk % SUBLANES) * GROUPS + rk // SUBLANES
        dist = tq - tk + jnp.where(ck < SUB_ROWS, WINDOW, 0)
        base = jnp.where((dist >= 0) & (dist < WINDOW), dist.astype(f32), jnp.inf)
        for h in range(N_HEADS):
            bias_s[h] = ALIBI_SLOPES[h] * base

    @pl.when(step + 1 < nblocks)
    def _prefetch():
        for cp in x_copies(step + 1, 1 - slot):
            cp.start()

    @pl.when(step >= 3)
    def _free_ybuf():
        for cp in y_copies(0, yslot):
            cp.wait()

    @pl.when(step < nblocks)
    def _wait_x():
        for cp in x_copies(step, slot):
            cp.wait()

    sh1 = mod_ref[0:1, 0 * D_MODEL:1 * D_MODEL]
    sc1 = mod_ref[0:1, 1 * D_MODEL:2 * D_MODEL]
    g1 = mod_ref[0:1, 2 * D_MODEL:3 * D_MODEL]
    sh2 = mod_ref[0:1, 3 * D_MODEL:4 * D_MODEL]
    sc2 = mod_ref[0:1, 4 * D_MODEL:5 * D_MODEL]
    g2 = mod_ref[0:1, 5 * D_MODEL:6 * D_MODEL]
    gs1 = g_pre_mix_ref[...] * (1.0 + sc1)
    gs2 = g_pre_ffn_ref[...] * (1.0 + sc2)
    sp = _softplus_neg(lam_ref[...])
    first_pen = jnp.where(step == 0, jnp.inf, 0.0)
    sub_iota = lax.broadcasted_iota(jnp.int32, (SUBLANES, D_RNN), 0)
    sub0_rnn = sub_iota == 0
    sub0_ffn = lax.broadcasted_iota(jnp.int32, (SUBLANES, FFN_CHUNK), 0) == 0
    conv_w = [conv_w_ref.at[jj:jj + 1, :] for jj in range(RNN_CONV_W)]

    sts = [dict() for _ in range(NSUB)]

    def rows(s):
        return pl.ds(s * SUB_ROWS, SUB_ROWS)

    def groups(s):
        return pl.ds(s * GROUPS, GROUPS)

    def stage_pre(s):
        x = xbuf[slot, groups(s)].reshape(SUB_ROWS, D_MODEL)
        sts[s]["hmix"] = (x * _rms_scale(x) * gs1 + sh1).astype(bf16)

    def stage_inproj(s):
        st = sts[s]
        hm = st["hmix"]
        st["q"] = (jnp.dot(hm, w_in_ref[:, 0:C_K], preferred_element_type=f32) * Q_SCALE).astype(bf16)
        kv = jnp.dot(hm, w_in_ref[:, C_K:C_XR], preferred_element_type=f32)
        st["kb"] = kv[:, 0:KV_WIDTH].astype(bf16)
        st["vb"] = kv[:, KV_WIDTH:2 * KV_WIDTH].astype(bf16)
        st["kv"] = kv
        st["xr"] = jnp.dot(hm, w_in_ref[:, C_XR:C_YR], preferred_element_type=f32)

    def stage_attn(s):
        st = sts[s]
        if s == 0:
            kp, vp = kprev[...], vprev[...]
        else:
            kp, vp = sts[s - 1]["kb"], sts[s - 1]["vb"]
        kw = jnp.concatenate([kp, st["kb"]], axis=0)
        vw = jnp.concatenate([vp, st["vb"]], axis=0)
        outs = []
        for h in range(N_HEADS):
            c = h // GQA_GROUP
            hs = slice(h * HEAD_DIM, (h + 1) * HEAD_DIM)
            cs = slice(c * HEAD_DIM, (c + 1) * HEAD_DIM)
            sc = _dot_nt(st["q"][:, hs], kw[:, cs])
            if s == 0:
                sc = jnp.concatenate([sc[:, 0:SUB_ROWS] - first_pen, sc[:, SUB_ROWS:]], axis=1)
            sc = sc - bias_s[h]
            sink = sinks_ref[h]
            m = jnp.maximum(jnp.max(sc, axis=-1, keepdims=True), sink)
            p = jnp.exp(sc - m)
            denom = jnp.sum(p, axis=-1, keepdims=True) + jnp.exp(sink - m)
            outs.append(jnp.dot(p.astype(bf16), vw[:, cs], preferred_element_type=f32) / denom)
        attn = jnp.concatenate(outs, axis=1)
        st["attn_n"] = _rms_norm(attn, g_attn_ref[...]).astype(bf16)

    def stage_rnn(s):
        st = sts[s]
        xr = st["xr"]
        X = _groups(xr)
        if s == 0:
            tail = lambda j: xr_tail[(j - (GROUPS - RNN_CONV_W + 1)) * SUBLANES + SUBLANES - 1:
                                     (j - (GROUPS - RNN_CONV_W + 1)) * SUBLANES + SUBLANES, :]
        else:
            pxr = sts[s - 1]["xr"]
            tail = lambda j: pxr[j * SUBLANES + SUBLANES - 1:(j + 1) * SUBLANES, :]
        sh = _shifted_groups(X, tail, sub0_rnn, RNN_CONV_W - 1)
        xcs = []
        for j in range(GROUPS):
            acc = conv_b_ref[...] + conv_w[RNN_CONV_W - 1][...] * X[j]
            for d in range(1, RNN_CONV_W):
                acc = acc + conv_w[RNN_CONV_W - 1 - d][...] * sh[d][j]
            xcs.append(acc)
        xc = jnp.concatenate(xcs, axis=0)
        a_parts, u_parts = _rglru_gates(xc, wa_ref, ba_ref[...], wi_ref, bi_ref[...], sp)
        a = jnp.concatenate(a_parts, axis=1)
        u = jnp.concatenate(u_parts, axis=1)
        h_in = hcar[...] if s == 0 else sts[s - 1]["h_out"]
        h, st["h_out"] = _scan_sub(a, u, h_in, sub_iota)
        yr = jnp.dot(st["hmix"], w_in_ref[:, C_YR:IN_COLS], preferred_element_type=f32)
        rnn = h * _gelu_tanh(yr)
        st["rnn_n"] = _rms_norm(rnn, g_rnn_ref[...]).astype(bf16)

    def stage_out(s):
        st = sts[s]
        mo = (jnp.dot(st["attn_n"], w_out_ref[0:ATTN_WIDTH, 0:D_MODEL], preferred_element_type=f32)
              + jnp.dot(st["rnn_n"], w_out_ref[ATTN_WIDTH:ATTN_WIDTH + D_RNN, 0:D_MODEL],
                        preferred_element_type=f32))
        x1 = (xbuf[slot, groups(s)].reshape(SUB_ROWS, D_MODEL)
              + g1 * _rms_norm(mo, g_post_mix_ref[...]))
        ybuf[yslot, groups(s)] = x1.reshape(GROUPS, SUBLANES, D_MODEL)
        hff_s[slot, rows(s), :] = (x1 * _rms_scale(x1) * gs2 + sh2).astype(bf16)

    tail_g0 = GROUPS - (FFN_CONV_W - 1)
    hff = hff_s[1 - slot]
    acts, tails = [], {}

    def conv_chunk(c0):
        c1 = c0 + FFN_CHUNK
        up = jnp.dot(hff, w_up_ref[:, c0:c1], preferred_element_type=f32)
        w = [fconv_w_ref.at[jj:jj + 1, c0:c1] for jj in range(FFN_CONV_W)]
        outs = []
        for s in range(NSUB):
            r0 = s * SUB_ROWS
            X = [up[r0 + j * SUBLANES:r0 + (j + 1) * SUBLANES, :] for j in range(GROUPS)]
            if s == 0:
                tail = lambda j: up_tail[(j - tail_g0) * SUBLANES + SUBLANES - 1:
                                         (j - tail_g0 + 1) * SUBLANES, c0:c1]
            else:
                tail = lambda j, p0=r0 - SUB_ROWS: up[p0 + j * SUBLANES + SUBLANES - 1:
                                                      p0 + (j + 1) * SUBLANES, :]
            sh = _shifted_groups(X, tail, sub0_ffn, FFN_CONV_W - 1)
            for j in range(GROUPS):
                acc = fconv_b_ref[:, c0:c1] + w[FFN_CONV_W - 1][...] * X[j]
                for d in range(1, FFN_CONV_W):
                    acc = acc + w[FFN_CONV_W - 1 - d][...] * sh[d][j]
                outs.append(acc)
        last0 = (NSUB - 1) * SUB_ROWS
        for t in range(FFN_CONV_W - 1):
            r = last0 + (tail_g0 + t) * SUBLANES + SUBLANES - 1
            ffnst_ref[t:t + 1, c0:c1] = up[r:r + 1, :]
        tails[c0] = up[last0 + tail_g0 * SUBLANES:, :]
        return jnp.concatenate(outs, axis=0)

    def ffn_chunk(cc):
        gate = conv_chunk(cc * FFN_CHUNK)
        val = conv_chunk(D_FF + cc * FFN_CHUNK)
        acts.append((_gelu_tanh(gate) * val).astype(bf16))

    def ffn_finish():
        for c0, tl in tails.items():
            up_tail[:, c0:c0 + FFN_CHUNK] = tl
        act = jnp.concatenate(acts, axis=1)
        f = jnp.dot(act, w_down_ref[:, 0:D_MODEL], preferred_element_type=f32)
        y = ybuf[fslot].reshape(TOKEN_BLOCK, D_MODEL) + g2 * _rms_norm(f, g_post_ffn_ref[...])
        ybuf[fslot] = y.reshape(NSUB * GROUPS, SUBLANES, D_MODEL)

    stages = (stage_pre, stage_inproj, stage_attn, stage_rnn, stage_out)
    n_times = len(stages) + (NSUB - 1) * STAGE_SKEW
    n_chunks = D_FF // FFN_CHUNK
    ffn_chunk(0)
    nxt = 1
    for t in range(n_times):
        for s in range(NSUB):
            k = t - s * STAGE_SKEW
            if 0 <= k < len(stages):
                stages[k](s)
                if nxt < n_chunks:
                    ffn_chunk(nxt)
                    nxt += 1
    while nxt < n_chunks:
        ffn_chunk(nxt)
        nxt += 1
    ffn_finish()

    lst = sts[NSUB - 1]
    kprev[...] = lst["kb"]
    vprev[...] = lst["vb"]
    xr_tail[...] = lst["xr"][(GROUPS - (RNN_CONV_W - 1)) * SUBLANES:, :]
    hcar[...] = lst["h_out"]

    @pl.when(step < nblocks)
    def _state():
        kst[...] = lst["kv"][:, 0:KV_WIDTH]
        vst[...] = lst["kv"][:, KV_WIDTH:2 * KV_WIDTH]
        for t in range(RNN_CONV_W - 1):
            r = (GROUPS - (RNN_CONV_W - 1) + t) * SUBLANES + SUBLANES - 1
            convst_ref[t:t + 1, :] = lst["xr"][r:r + 1, :]
        hlast_ref[...] = lst["h_out"][0:1, :]

    @pl.when(step >= 1)
    def _store_y():
        for cp in y_copies(step - 1, fslot):
            cp.start()

    @pl.when(step == nblocks)
    def _finish():
        for cp in state_copies():
            cp.start()
        for cp in state_copies() + weight_out_copies():
            cp.wait()
        for cp in y_copies(0, lax.rem(step + 1, 3)) + y_copies(0, fslot):
            cp.wait()


def _const_spec(shape):
    nd = len(shape)
    return pl.BlockSpec(shape, lambda i: (0,) * nd)


def _prompt_call(x, mod, mod_row0, sinks, params):
    T = x.shape[0]
    TB = TOKEN_BLOCK
    assert T // TB >= 3
    (g_pre_mix, w_in, conv_w, conv_b, wa, ba, wi, bi, lam, g_attn, g_rnn, w_out,
     g_post_mix, g_pre_ffn, w_up, fconv_w, fconv_b, w_down, g_post_ffn) = params
    x4 = x.reshape(T // SUB_ROWS, SUBLANES, GROUPS, D_MODEL)
    w_out3 = w_out.reshape(-1, SUBLANES, D_MODEL)
    w_down3 = w_down.reshape(-1, SUBLANES, D_MODEL)
    ins = [x4, mod, sinks, g_pre_mix, w_in, conv_w, conv_b, wa, ba, wi, bi, lam, g_attn, g_rnn,
           w_out3, g_post_mix, g_pre_ffn, w_up, fconv_w, fconv_b, w_down3, g_post_ffn]
    assert mod_row0 % MOD_PAD_ROWS == 0
    in_specs = [pl.BlockSpec(memory_space=pl.ANY),
                pl.BlockSpec((MOD_PAD_ROWS, mod.shape[1]), lambda i: (mod_row0 // MOD_PAD_ROWS, 0)),
                pl.BlockSpec(memory_space=pltpu.SMEM)]
    big = (w_in, w_out, w_up, w_down)
    assert all(w.dtype == f32 and w.shape[0] % TOKEN_BLOCK == 0 for w in big)
    in_specs += [pl.BlockSpec(memory_space=pl.ANY)
                 if any(a is w for w in (w_in, w_out3, w_up, w_down3))
                 else _const_spec(a.shape) for a in ins[3:]]
    out_shape = (
        jax.ShapeDtypeStruct(x4.shape, f32),
        jax.ShapeDtypeStruct((SUBLANES, GROUPS, KV_WIDTH), f32),
        jax.ShapeDtypeStruct((SUBLANES, GROUPS, KV_WIDTH), f32),
        jax.ShapeDtypeStruct((1, D_RNN), f32),
        jax.ShapeDtypeStruct((RNN_CONV_W - 1, D_RNN), f32),
        jax.ShapeDtypeStruct((FFN_CONV_W - 1, 2 * D_FF), f32),
    ) + tuple(jax.ShapeDtypeStruct(w.shape, bf16) for w in big)
    out_specs = (
        pl.BlockSpec(memory_space=pl.ANY),
        pl.BlockSpec(memory_space=pl.ANY),
        pl.BlockSpec(memory_space=pl.ANY),
        _const_spec((1, D_RNN)),
        _const_spec((RNN_CONV_W - 1, D_RNN)),
        _const_spec((FFN_CONV_W - 1, 2 * D_FF)),
    ) + (pl.BlockSpec(memory_space=pl.ANY),) * len(big)
    scratch = [
        pltpu.VMEM((2, TB // SUBLANES, SUBLANES, D_MODEL), f32),
        pltpu.VMEM((3, TB // SUBLANES, SUBLANES, D_MODEL), f32),
        pltpu.SemaphoreType.DMA((2,)),
        pltpu.SemaphoreType.DMA((3,)),
        pltpu.SemaphoreType.DMA(()),
        pltpu.VMEM((SUB_ROWS, KV_WIDTH), bf16),
        pltpu.VMEM((SUB_ROWS, KV_WIDTH), bf16),
        pltpu.VMEM((SUB_ROWS, KV_WIDTH), f32),
        pltpu.VMEM((SUB_ROWS, KV_WIDTH), f32),
        pltpu.VMEM(((RNN_CONV_W - 1) * SUBLANES, D_RNN), f32),
        pltpu.VMEM(((FFN_CONV_W - 1) * SUBLANES, 2 * D_FF), f32),
        pltpu.VMEM((SUBLANES, D_RNN), f32),
        pltpu.VMEM((N_HEADS, SUB_ROWS, 2 * SUB_ROWS), f32),
        pltpu.VMEM((2, TOKEN_BLOCK, D_MODEL), bf16),
        pltpu.VMEM(w_in.shape, bf16),
        pltpu.VMEM((w_out.shape[0], PADDED_OUT_COLS), bf16),
        pltpu.VMEM(w_up.shape, bf16),
        pltpu.VMEM((w_down.shape[0], PADDED_OUT_COLS), bf16),
        pltpu.VMEM((WSTAGE_SLOTS, WSTAGE_ROWS, w_up.shape[1]), f32),
        pltpu.SemaphoreType.DMA((WSTAGE_SLOTS,)),
        pltpu.SemaphoreType.DMA(()),
    ]
    assert w_in.shape[1] <= w_up.shape[1] and w_in.shape[0] % WSTAGE_ROWS == 0
    y4, kp, vp, hp, convp, ffnp, w_in_bf, w_out_bf, w_up_bf, w_down_bf = pl.pallas_call(
        _prompt_kernel,
        grid=(T // TB + 1,),
        in_specs=in_specs,
        out_specs=out_specs,
        out_shape=out_shape,
        scratch_shapes=scratch,
        compiler_params=pltpu.CompilerParams(
            dimension_semantics=("arbitrary",), vmem_limit_bytes=VMEM_LIMIT_BYTES),
        name="prompt_layer",
    )(*ins)
    return (y4.reshape(T, D_MODEL), kp.reshape(WINDOW, KV_WIDTH), vp.reshape(WINDOW, KV_WIDTH),
            hp, convp, ffnp, (w_in_bf, w_out_bf, w_up_bf, w_down_bf))


def _sample_pre_kernel(x_ref, mod_ref, g_pre_mix_ref, w_in_ref, q_ref, kv_ref, xr_ref, yr_ref):
    B = x_ref.shape[0]
    sh1 = mod_ref[0:B, 0 * D_MODEL:1 * D_MODEL]
    sc1 = mod_ref[0:B, 1 * D_MODEL:2 * D_MODEL]
    hmix = (_rms_norm(x_ref[:, 0, :], g_pre_mix_ref[...]) * (1.0 + sc1) + sh1).astype(bf16)
    q = jnp.dot(hmix, w_in_ref[:, 0:C_K], preferred_element_type=f32) * Q_SCALE
    for h in range(N_HEADS):
        q_ref[:, h, :] = q[:, h * HEAD_DIM:(h + 1) * HEAD_DIM]
    kv_ref[...] = jnp.dot(hmix, w_in_ref[:, C_K:C_XR], preferred_element_type=f32)
    xr_ref[...] = jnp.dot(hmix, w_in_ref[:, C_XR:C_YR], preferred_element_type=f32)
    yr_ref[...] = jnp.dot(hmix, w_in_ref[:, C_YR:IN_COLS], preferred_element_type=f32)


def _sample_pre_call(x, mod_s, g_pre_mix, w_in):
    B = x.shape[0]
    return pl.pallas_call(
        _sample_pre_kernel,
        out_shape=(
            jax.ShapeDtypeStruct((B, N_HEADS, HEAD_DIM), f32),
            jax.ShapeDtypeStruct((B, 2 * KV_WIDTH), f32),
            jax.ShapeDtypeStruct((B, D_RNN), f32),
            jax.ShapeDtypeStruct((B, D_RNN), f32),
        ),
        compiler_params=pltpu.CompilerParams(vmem_limit_bytes=VMEM_LIMIT_BYTES),
        name="sample_pre",
    )(x, mod_s, g_pre_mix, w_in)


def _sample_attn_kernel(q_ref, kv_ref, ck_ref, cv_ref, sinks_ref, o_ref, kwin_ref, vwin_ref):
    R = N_KV_HEADS * WINDOW
    hrow = lax.broadcasted_iota(jnp.int32, (N_HEADS, R), 0)
    rcol = lax.broadcasted_iota(jnp.int32, (N_HEADS, R), 1)
    slope = jnp.exp2(-8.0 * (hrow + 1).astype(f32) / N_HEADS)
    own = (rcol % N_KV_HEADS) == (hrow // GQA_GROUP)
    bias = jnp.where(own, slope * (WINDOW - 1 - rcol // N_KV_HEADS).astype(f32), jnp.inf)
    wrow = lax.broadcasted_iota(jnp.int32, (R, HEAD_DIM), 0)
    sink = sinks_ref[...]

    def shifted(cache, new_rows):
        out = pltpu.roll(cache, R - N_KV_HEADS, axis=0)
        for c in range(N_KV_HEADS):
            out = jnp.where(wrow == R - N_KV_HEADS + c, new_rows[c], out)
        return out

    scores = []
    for b in range(SAMPLE_CHUNK):
        knew = [kv_ref[b:b + 1, c * HEAD_DIM:(c + 1) * HEAD_DIM] for c in range(N_KV_HEADS)]
        kw = shifted(ck_ref[b], knew)
        kwin_ref[b] = kw
        scores.append(_dot_nt(q_ref[b].astype(bf16), kw.astype(bf16)) - bias)
    probs, denoms = [], []
    for b in range(SAMPLE_CHUNK):
        s = scores[b]
        m = jnp.maximum(jnp.max(s, axis=-1, keepdims=True), sink)
        p = jnp.exp(s - m)
        denoms.append(jnp.sum(p, axis=-1, keepdims=True) + jnp.exp(sink - m))
        probs.append(p.astype(bf16))
    for b in range(SAMPLE_CHUNK):
        vnew = [kv_ref[b:b + 1, KV_WIDTH + c * HEAD_DIM:KV_WIDTH + (c + 1) * HEAD_DIM]
                for c in range(N_KV_HEADS)]
        vw = shifted(cv_ref[b], vnew)
        vwin_ref[b] = vw
        o_ref[b] = (jnp.dot(probs[b], vw.astype(bf16), preferred_element_type=f32)
                    / denoms[b])


def _sample_attn_call(q, kv, ck, cv, sinks_col):
    B = q.shape[0]
    BC = SAMPLE_CHUNK
    return pl.pallas_call(
        _sample_attn_kernel,
        grid=(B // BC,),
        in_specs=[
            pl.BlockSpec((BC, N_HEADS, HEAD_DIM), lambda i: (i, 0, 0)),
            pl.BlockSpec((BC, 2 * KV_WIDTH), lambda i: (i, 0)),
            pl.BlockSpec((BC, N_KV_HEADS * WINDOW, HEAD_DIM), lambda i: (i, 0, 0)),
            pl.BlockSpec((BC, N_KV_HEADS * WINDOW, HEAD_DIM), lambda i: (i, 0, 0)),
            pl.BlockSpec((N_HEADS, 1), lambda i: (0, 0)),
        ],
        out_specs=(
            pl.BlockSpec((BC, N_HEADS, HEAD_DIM), lambda i: (i, 0, 0)),
            pl.BlockSpec((BC, N_KV_HEADS * WINDOW, HEAD_DIM), lambda i: (i, 0, 0)),
            pl.BlockSpec((BC, N_KV_HEADS * WINDOW, HEAD_DIM), lambda i: (i, 0, 0)),
        ),
        out_shape=(
            jax.ShapeDtypeStruct((B, N_HEADS, HEAD_DIM), f32),
            jax.ShapeDtypeStruct((B, N_KV_HEADS * WINDOW, HEAD_DIM), f32),
            jax.ShapeDtypeStruct((B, N_KV_HEADS * WINDOW, HEAD_DIM), f32),
        ),
        compiler_params=pltpu.CompilerParams(
            dimension_semantics=("arbitrary",), vmem_limit_bytes=SAMPLE_ATTN_VMEM_BYTES),
        name="sample_attn",
    )(q, kv, ck, cv, sinks_col)


def _sample_post_kernel(
    x_ref, mod_ref, attn_ref, xr_ref, yr_ref, h0_ref, cbuf_ref, fbuf_ref,
    conv_w_ref, conv_b_ref, wa_ref, ba_ref, wi_ref, bi_ref, lam_ref,
    g_attn_ref, g_rnn_ref, w_out_ref, g_post_mix_ref, g_pre_ffn_ref, w_up_ref,
    fconv_w_ref, fconv_b_ref, w_down_ref, g_post_ffn_ref,
    y_ref, h_ref, cst_ref, fst_ref,
):
    B = x_ref.shape[0]
    g1 = mod_ref[0:B, 2 * D_MODEL:3 * D_MODEL]
    sh2 = mod_ref[0:B, 3 * D_MODEL:4 * D_MODEL]
    sc2 = mod_ref[0:B, 4 * D_MODEL:5 * D_MODEL]
    g2 = mod_ref[0:B, 5 * D_MODEL:6 * D_MODEL]
    x = x_ref[:, 0, :]
    xr = xr_ref[...]

    xc = conv_b_ref[...] + conv_w_ref[RNN_CONV_W - 1:RNN_CONV_W, :] * xr
    for jj in range(RNN_CONV_W - 1):
        xc = xc + conv_w_ref[jj:jj + 1, :] * cbuf_ref[jj]
    for jj in range(RNN_CONV_W - 2):
        cst_ref[jj] = cbuf_ref[jj + 1]
    cst_ref[RNN_CONV_W - 2] = xr

    sp = _softplus_neg(lam_ref[...])
    a_parts, u_parts = _rglru_gates(xc, wa_ref, ba_ref[...], wi_ref, bi_ref[...], sp)
    a = jnp.concatenate(a_parts, axis=1)
    u = jnp.concatenate(u_parts, axis=1)
    h = a * h0_ref[...] + u
    h_ref[...] = h
    rnn = h * _gelu_tanh(yr_ref[...])

    attn = jnp.concatenate([attn_ref[:, h, :] for h in range(N_HEADS)], axis=1)
    attn_n = _rms_norm(attn, g_attn_ref[...])
    rnn_n = _rms_norm(rnn, g_rnn_ref[...])
    mo = (_dot(attn_n, w_out_ref[0:ATTN_WIDTH, 0:D_MODEL])
          + _dot(rnn_n, w_out_ref[ATTN_WIDTH:ATTN_WIDTH + D_RNN, 0:D_MODEL]))
    x1 = x + g1 * _rms_norm(mo, g_post_mix_ref[...])

    hff = (_rms_norm(x1, g_pre_ffn_ref[...]) * (1.0 + sc2) + sh2).astype(bf16)
    up_pre = jnp.dot(hff, w_up_ref[...], preferred_element_type=f32)
    up = fconv_b_ref[...] + fconv_w_ref[FFN_CONV_W - 1:FFN_CONV_W, :] * up_pre
    for jj in range(FFN_CONV_W - 1):
        up = up + fconv_w_ref[jj:jj + 1, :] * fbuf_ref[:, jj, :]
    for jj in range(FFN_CONV_W - 2):
        fst_ref[:, jj, :] = fbuf_ref[:, jj + 1, :]
    fst_ref[:, FFN_CONV_W - 2, :] = up_pre

    act = (_gelu_tanh(up[:, 0:D_FF]) * up[:, D_FF:2 * D_FF]).astype(bf16)
    f = jnp.dot(act, w_down_ref[:, 0:D_MODEL], preferred_element_type=f32)
    y_ref[:, 0, :] = x1 + g2 * _rms_norm(f, g_post_ffn_ref[...])


def _sample_post_call(x, mod_s, attn, xr, yr, h0, cbuf, fbuf, params):
    B = x.shape[0]
    (_, _, conv_w, conv_b, wa, ba, wi, bi, lam, g_attn, g_rnn, w_out,
     g_post_mix, g_pre_ffn, w_up, fconv_w, fconv_b, w_down, g_post_ffn) = params
    return pl.pallas_call(
        _sample_post_kernel,
        out_shape=(
            jax.ShapeDtypeStruct((B, 1, D_MODEL), f32),
            jax.ShapeDtypeStruct((B, D_RNN), f32),
            jax.ShapeDtypeStruct((RNN_CONV_W - 1, B, D_RNN), f32),
            jax.ShapeDtypeStruct((B, FFN_CONV_W - 1, 2 * D_FF), f32),
        ),
        compiler_params=pltpu.CompilerParams(vmem_limit_bytes=VMEM_LIMIT_BYTES),
        name="sample_post",
    )(x, mod_s, attn, xr, yr, h0, cbuf, fbuf, conv_w, conv_b, wa, ba, wi, bi, lam,
      g_attn, g_rnn, w_out, g_post_mix, g_pre_ffn, w_up, fconv_w, fconv_b, w_down, g_post_ffn)


def kernel(x_prompt, x_sample, cache_k, cache_v, state_h, state_conv, state_ffn_conv, c_prompt, c_sample, w_ada, b_ada, g_pre_mix, w_in, conv_w, conv_b, w_a, b_a, w_i, b_i, lam, sinks, g_attn_out, g_rnn_out, w_out, g_post_mix, g_pre_ffn, w_up, ffn_conv_w, ffn_conv_b, w_down, g_post_ffn):
    depth = w_in.shape[0]
    assert depth == 1 and x_prompt.shape[0] == 1 and x_sample.shape[1] == 1
    T = x_prompt.shape[1]
    B = x_sample.shape[0]
    W = cache_k.shape[2]
    assert W == WINDOW and T % TOKEN_BLOCK == 0 and B % SAMPLE_CHUNK == 0

    row = lambda a: a[0].reshape(1, -1)
    params = (
        row(g_pre_mix), w_in[0], conv_w[0], row(conv_b),
        w_a[0].astype(bf16), row(b_a), w_i[0].astype(bf16), row(b_i), row(lam),
        row(g_attn_out), row(g_rnn_out), w_out[0], row(g_post_mix),
        row(g_pre_ffn),
        w_up[0], ffn_conv_w[0], row(ffn_conv_b), w_down[0],
        row(g_post_ffn),
    )

    c_all = jnp.concatenate(
        [c_sample, jnp.broadcast_to(c_prompt, (MOD_PAD_ROWS, D_MODEL))], axis=0)
    mod = _mod_call(c_all, w_ada[0], b_ada[0].reshape(1, -1))
    mod_s = mod

    yp, kp, vp, hp, convp, ffnp, (w_in_bf, w_out_bf, w_up_bf, w_down_bf) = _prompt_call(
        x_prompt[0], mod, B, sinks[0], params)
    params = list(params)
    params[1], params[11], params[14], params[17] = w_in_bf, w_out_bf, w_up_bf, w_down_bf
    assert params[11].shape == w_out.shape[1:] and params[17].shape == w_down.shape[1:]

    xs = x_sample
    q, kv, xr, yr = _sample_pre_call(xs, mod_s, params[0], params[1])
    ck = cache_k.reshape(B, W * N_KV_HEADS, HEAD_DIM)
    cv = cache_v.reshape(B, W * N_KV_HEADS, HEAD_DIM)
    attn3, kwin, vwin = _sample_attn_call(
        q, kv, ck, cv, sinks[0].reshape(N_HEADS, 1))
    attn = attn3
    ys, hs, convs, ffns = _sample_post_call(
        xs, mod_s, attn, xr, yr, state_h[0],
        jnp.transpose(state_conv[0], (1, 0, 2)), state_ffn_conv[0],
        params)
    convs = jnp.transpose(convs, (1, 0, 2))

    kv_shape = (1, 1, W, N_KV_HEADS, HEAD_DIM)
    kvs_shape = (1, B, W, N_KV_HEADS, HEAD_DIM)
    return (
        yp[None], ys,
        kp.reshape(kv_shape), vp.reshape(kv_shape), hp[None], convp[None, None], ffnp[None, None],
        kwin.reshape(kvs_shape), vwin.reshape(kvs_shape), hs[None], convs[None], ffns[None],
    )
```

```python
import math

import jax
import jax.numpy as jnp
from jax import lax
from jax.experimental import pallas as pl
from jax.experimental.pallas import tpu as pltpu

D_MODEL = 1024
N_HEADS = 8
N_KV_HEADS = 2
HEAD_DIM = 128
GQA_GROUP = N_HEADS // N_KV_HEADS
ATTN_WIDTH = N_HEADS * HEAD_DIM
KV_WIDTH = N_KV_HEADS * HEAD_DIM
WINDOW = 128
D_RNN = D_MODEL
RNN_BLOCKS = 8
RNN_BLOCK_W = D_RNN // RNN_BLOCKS
RG_C = 8.0
RNN_CONV_W = 4
D_FF = 2816
FFN_CONV_W = 3
RMS_EPS = 1e-6

C_K = ATTN_WIDTH
C_V = C_K + KV_WIDTH
C_XR = C_V + KV_WIDTH
C_YR = C_XR + D_RNN
IN_COLS = C_YR + D_RNN

SUBLANES = 8
SUB_ROWS = WINDOW
GROUPS = SUB_ROWS // SUBLANES
TOKEN_BLOCK = 256
NSUB = TOKEN_BLOCK // SUB_ROWS
FFN_CHUNK = 256
STAGE_SKEW = 3
WSTAGE_ROWS = 64
WSTAGE_SLOTS = 4
SAMPLE_CHUNK = 16
SAMPLE_ATTN_VMEM_BYTES = (2 * 2 * 2 * SAMPLE_CHUNK * N_KV_HEADS * WINDOW * HEAD_DIM * 4
                          + 8 * 1024 * 1024)
MOD_PAD_ROWS = 8
VMEM_LIMIT_BYTES = 56 * 1024 * 1024

ALIBI_SLOPES = tuple(2.0 ** (-8.0 * (h + 1) / N_HEADS) for h in range(N_HEADS))
Q_SCALE = HEAD_DIM ** -0.5
SQRT_2_OVER_PI = math.sqrt(2.0 / math.pi)
LOG2_E = 1.0 / math.log(2.0)
PADDED_OUT_COLS = D_MODEL + 128

bf16 = jnp.bfloat16
f32 = jnp.float32


def _rms_scale(x):
    return lax.rsqrt(jnp.mean(x * x, axis=-1, keepdims=True) + RMS_EPS)


def _rms_norm(x, g):
    return x * _rms_scale(x) * g


def _gelu_tanh(x):
    k1 = -2.0 * LOG2_E * SQRT_2_OVER_PI
    k3 = k1 * 0.044715
    return x / (1.0 + jnp.exp2(x * (k1 + k3 * (x * x))))


def _sigmoid(x):
    return 1.0 / (1.0 + jnp.exp2(x * (-LOG2_E)))


def _softplus_neg(lam):
    return jnp.maximum(-lam, 0.0) + jnp.log1p(jnp.exp(-jnp.abs(lam)))


def _dot(a, b):
    return jnp.dot(a.astype(bf16), b, preferred_element_type=f32)


def _dot_nt(a, b):
    return lax.dot_general(a, b, (((1,), (1,)), ((), ())), preferred_element_type=f32)


def _rglru_gates(xc, wa_ref, ba, wi_ref, bi, sp):
    a_parts, u_parts = [], []
    for n in range(RNN_BLOCKS):
        sl = slice(n * RNN_BLOCK_W, (n + 1) * RNN_BLOCK_W)
        xn = xc[:, sl]
        xb = xn.astype(bf16)
        r = _sigmoid(jnp.dot(xb, wa_ref[n], preferred_element_type=f32) + ba[:, sl])
        i = _sigmoid(jnp.dot(xb, wi_ref[n], preferred_element_type=f32) + bi[:, sl])
        log_a = (-RG_C) * r * sp[:, sl]
        t = jnp.tanh(log_a)
        one_minus_a2 = (-2.0 * t) / (1.0 - t)
        a_parts.append(jnp.exp(log_a))
        u_parts.append(jnp.sqrt(one_minus_a2) * (i * xn))
    return a_parts, u_parts


def _mod_kernel(c_ref, w_ref, b_ref, o_ref):
    c = c_ref[...]
    s = c * _sigmoid(c)
    o_ref[...] = _dot(s, w_ref[...].astype(bf16)) + b_ref[...]


def _mod_call(c_all, w_ada, b_ada):
    rows = c_all.shape[0]
    ncol = w_ada.shape[1]
    bn = D_MODEL
    return pl.pallas_call(
        _mod_kernel,
        grid=(ncol // bn,),
        in_specs=[
            pl.BlockSpec((rows, D_MODEL), lambda j: (0, 0)),
            pl.BlockSpec((D_MODEL, bn), lambda j: (0, j)),
            pl.BlockSpec((1, bn), lambda j: (0, j)),
        ],
        out_specs=pl.BlockSpec((rows, bn), lambda j: (0, j)),
        out_shape=jax.ShapeDtypeStruct((rows, ncol), f32),
        compiler_params=pltpu.CompilerParams(dimension_semantics=("arbitrary",)),
        name="adaln_mod",
    )(c_all, w_ada, b_ada)


def _groups(v, c0=None, c1=None):
    if c0 is None:
        return [v[j * SUBLANES:(j + 1) * SUBLANES, :] for j in range(GROUPS)]
    return [v[j * SUBLANES:(j + 1) * SUBLANES, c0:c1] for j in range(GROUPS)]


def _shifted_groups(X, tail_row, sub0, max_shift):
    wrapped = {}
    for j in range(GROUPS - max_shift, GROUPS):
        wrapped[j] = jnp.where(sub0, tail_row(j), pltpu.roll(X[j], 1, axis=0))
    sh = {}
    for d in range(1, max_shift + 1):
        sh[d] = [X[j - d] if j >= d else wrapped[j - d + GROUPS] for j in range(GROUPS)]
    return sh


def _scan_sub(a, u, h_in, sub_iota):
    A = _groups(a)
    L = _groups(u)
    for j in range(1, GROUPS):
        L[j] = A[j] * L[j - 1] + L[j]
        A[j] = A[j] * A[j - 1]
    ae, le = A[GROUPS - 1], L[GROUPS - 1]
    for s_ in (1, 2, 4):
        ok = sub_iota >= s_
        a_sh = pltpu.roll(ae, s_, axis=0)
        l_sh = pltpu.roll(le, s_, axis=0)
        le = jnp.where(ok, ae * l_sh + le, le)
        ae = jnp.where(ok, ae * a_sh, ae)
    hend = le + ae * h_in
    hprev = jnp.where(sub_iota == 0, h_in, pltpu.roll(hend, 1, axis=0))
    h = jnp.concatenate([L[j] + A[j] * hprev for j in range(GROUPS)], axis=0)
    h_out = jnp.broadcast_to(hend[SUBLANES - 1:SUBLANES, :], hend.shape)
    return h, h_out


def _prompt_kernel(
    x_hbm, mod_ref, sinks_ref,
    g_pre_mix_ref, w_in_hbm, conv_w_ref, conv_b_ref, wa_ref, ba_ref, wi_ref, bi_ref, lam_ref,
    g_attn_ref, g_rnn_ref, w_out_hbm, g_post_mix_ref, g_pre_ffn_ref, w_up_hbm,
    fconv_w_ref, fconv_b_ref, w_down_hbm, g_post_ffn_ref,
    y_hbm, kwin_hbm, vwin_hbm, hlast_ref, convst_ref, ffnst_ref,
    w_in_bf_hbm, w_out_bf_hbm, w_up_bf_hbm, w_down_bf_hbm,
    xbuf, ybuf, sem_in, sem_out, sem_st, kprev, vprev, kst, vst, xr_tail, up_tail, hcar, bias_s,
    hff_s, w_in_ref, w_out_ref, w_up_ref, w_down_ref, wstage, sem_w, sem_wout,
):
    step = pl.program_id(0)
    nblocks = pl.num_programs(0) - 1
    slot = lax.rem(step, 2)
    yslot = lax.rem(step, 3)
    fslot = lax.rem(step + 2, 3)

    def x_copies(st, sl):
        return [pltpu.make_async_copy(
            x_hbm.at[st * NSUB + s, c, :, :],
            xbuf.at[sl, pl.ds(s * GROUPS, GROUPS), c, :],
            sem_in.at[sl]) for s in range(NSUB) for c in range(SUBLANES)]

    def y_copies(st, sl):
        return [pltpu.make_async_copy(
            ybuf.at[sl, pl.ds(s * GROUPS, GROUPS), c, :],
            y_hbm.at[st * NSUB + s, c, :, :],
            sem_out.at[sl]) for s in range(NSUB) for c in range(SUBLANES)]

    def state_copies():
        cps = []
        for src, dst in ((kst, kwin_hbm), (vst, vwin_hbm)):
            cps += [pltpu.make_async_copy(src.at[pl.ds(j * SUBLANES, SUBLANES), :],
                                          dst.at[:, j, :], sem_st) for j in range(GROUPS)]
        return cps

    def weight_out_copies():
        return [pltpu.make_async_copy(w_in_ref, w_in_bf_hbm, sem_wout),
                pltpu.make_async_copy(w_up_ref, w_up_bf_hbm, sem_wout),
                pltpu.make_async_copy(w_out_ref.at[:, pl.ds(0, D_MODEL)], w_out_bf_hbm, sem_wout),
                pltpu.make_async_copy(w_down_ref.at[:, pl.ds(0, D_MODEL)], w_down_bf_hbm, sem_wout)]

    def load_cast(w_hbm, w_s, stage, chunk_rows):
        tiled = len(stage.shape) == 4
        if tiled:
            n_rows, n_cols = w_hbm.shape[0] * SUBLANES, w_hbm.shape[2]
            assert stage.shape[1:] == (chunk_rows // SUBLANES, SUBLANES, n_cols)
        else:
            n_rows, n_cols = w_hbm.shape
        n = n_rows // chunk_rows
        depth = stage.shape[0]
        assert n >= depth

        def chunk(i, sl):
            if tiled:
                g = chunk_rows // SUBLANES
                return pltpu.make_async_copy(w_hbm.at[pl.ds(i * g, g), :, :], stage.at[sl],
                                             sem_w.at[sl])
            return pltpu.make_async_copy(
                w_hbm.at[pl.ds(i * chunk_rows, chunk_rows), :],
                stage.at[sl, pl.ds(0, chunk_rows), pl.ds(0, n_cols)], sem_w.at[sl])

        def staged(sl):
            if tiled:
                return stage[sl].reshape(chunk_rows, n_cols)
            return stage[sl, 0:chunk_rows, 0:n_cols]

        for i in range(depth):
            chunk(i, i).start()

        def body(i, carry):
            sl = lax.rem(i, depth)
            chunk(i, sl).wait()
            r0 = pl.multiple_of(i * chunk_rows, chunk_rows)
            w_s[pl.ds(r0, chunk_rows), 0:n_cols] = staged(sl).astype(bf16)

            @pl.when(i + depth < n)
            def _next():
                chunk(i + depth, sl).start()

            return carry

        lax.fori_loop(0, n, body, 0)

    @pl.when(step == 0)
    def _init():
        for cp in x_copies(0, 0):
            cp.start()
        load_cast(w_in_hbm, w_in_ref, wstage, WSTAGE_ROWS)
        load_cast(w_out_hbm, w_out_ref, ybuf, TOKEN_BLOCK)
        load_cast(w_up_hbm, w_up_ref, wstage, WSTAGE_ROWS)
        load_cast(w_down_hbm, w_down_ref, ybuf, TOKEN_BLOCK)
        for cp in weight_out_copies():
            cp.start()
        kprev[...] = jnp.zeros(kprev.shape, bf16)
        vprev[...] = jnp.zeros(vprev.shape, bf16)
        xr_tail[...] = jnp.zeros(xr_tail.shape, f32)
        up_tail[...] = jnp.zeros(up_tail.shape, f32)
        hcar[...] = jnp.zeros(hcar.shape, f32)
        hff_s[...] = jnp.zeros(hff_s.shape, bf16)
        ybuf[2] = jnp.zeros(ybuf.shape[1:], f32)
        rq = lax.broadcasted_iota(jnp.int32, (SUB_ROWS, 2 * SUB_ROWS), 0)
        ck = lax.broadcasted_iota(jnp.int32, (SUB_ROWS, 2 * SUB_ROWS), 1)
        rk = ck & (SUB_ROWS - 1)
        tq = (rq % SUBLANES) * GROUPS + rq // SUBLANES
        tk = (rk % SUBLANES) * GROUPS + rk // SUBLANES
        dist = tq - tk + jnp.where(ck < SUB_ROWS, WINDOW, 0)
        base = jnp.where((dist >= 0) & (dist < WINDOW), dist.astype(f32), jnp.inf)
        for h in range(N_HEADS):
            bias_s[h] = ALIBI_SLOPES[h] * base

    @pl.when(step + 1 < nblocks)
    def _prefetch():
        for cp in x_copies(step + 1, 1 - slot):
            cp.start()

    @pl.when(step >= 3)
    def _free_ybuf():
        for cp in y_copies(0, yslot):
            cp.wait()

    @pl.when(step < nblocks)
    def _wait_x():
        for cp in x_copies(step, slot):
            cp.wait()

    sh1 = mod_ref[0:1, 0 * D_MODEL:1 * D_MODEL]
    sc1 = mod_ref[0:1, 1 * D_MODEL:2 * D_MODEL]
    g1 = mod_ref[0:1, 2 * D_MODEL:3 * D_MODEL]
    sh2 = mod_ref[0:1, 3 * D_MODEL:4 * D_MODEL]
    sc2 = mod_ref[0:1, 4 * D_MODEL:5 * D_MODEL]
    g2 = mod_ref[0:1, 5 * D_MODEL:6 * D_MODEL]
    gs1 = g_pre_mix_ref[...] * (1.0 + sc1)
    gs2 = g_pre_ffn_ref[...] * (1.0 + sc2)
    sp = _softplus_neg(lam_ref[...])
    first_pen = jnp.where(step == 0, jnp.inf, 0.0)
    sub_iota = lax.broadcasted_iota(jnp.int32, (SUBLANES, D_RNN), 0)
    sub0_rnn = sub_iota == 0
    sub0_ffn = lax.broadcasted_iota(jnp.int32, (SUBLANES, FFN_CHUNK), 0) == 0
    conv_w = [conv_w_ref.at[jj:jj + 1, :] for jj in range(RNN_CONV_W)]

    sts = [dict() for _ in range(NSUB)]

    def rows(s):
        return pl.ds(s * SUB_ROWS, SUB_ROWS)

    def groups(s):
        return pl.ds(s * GROUPS, GROUPS)

    def stage_pre(s):
        x = xbuf[slot, groups(s)].reshape(SUB_ROWS, D_MODEL)
        sts[s]["hmix"] = (x * _rms_scale(x) * gs1 + sh1).astype(bf16)

    def stage_inproj(s):
        st = sts[s]
        hm = st["hmix"]
        st["q"] = (jnp.dot(hm, w_in_ref[:, 0:C_K], preferred_element_type=f32) * Q_SCALE).astype(bf16)
        kv = jnp.dot(hm, w_in_ref[:, C_K:C_XR], preferred_element_type=f32)
        st["kb"] = kv[:, 0:KV_WIDTH].astype(bf16)
        st["vb"] = kv[:, KV_WIDTH:2 * KV_WIDTH].astype(bf16)
        st["kv"] = kv
        st["xr"] = jnp.dot(hm, w_in_ref[:, C_XR:C_YR], preferred_element_type=f32)

    def stage_attn(s):
        st = sts[s]
        if s == 0:
            kp, vp = kprev[...], vprev[...]
        else:
            kp, vp = sts[s - 1]["kb"], sts[s - 1]["vb"]
        kw = jnp.concatenate([kp, st["kb"]], axis=0)
        vw = jnp.concatenate([vp, st["vb"]], axis=0)
        outs = []
        for h in range(N_HEADS):
            c = h // GQA_GROUP
            hs = slice(h * HEAD_DIM, (h + 1) * HEAD_DIM)
            cs = slice(c * HEAD_DIM, (c + 1) * HEAD_DIM)
            sc = _dot_nt(st["q"][:, hs], kw[:, cs])
            if s == 0:
                sc = jnp.concatenate([sc[:, 0:SUB_ROWS] - first_pen, sc[:, SUB_ROWS:]], axis=1)
            sc = sc - bias_s[h]
            sink = sinks_ref[h]
            m = jnp.maximum(jnp.max(sc, axis=-1, keepdims=True), sink)
            p = jnp.exp(sc - m)
            denom = jnp.sum(p, axis=-1, keepdims=True) + jnp.exp(sink - m)
            outs.append(jnp.dot(p.astype(bf16), vw[:, cs], preferred_element_type=f32) / denom)
        attn = jnp.concatenate(outs, axis=1)
        st["attn_n"] = _rms_norm(attn, g_attn_ref[...]).astype(bf16)

    def stage_rnn(s):
        st = sts[s]
        xr = st["xr"]
        X = _groups(xr)
        if s == 0:
            tail = lambda j: xr_tail[(j - (GROUPS - RNN_CONV_W + 1)) * SUBLANES + SUBLANES - 1:
                                     (j - (GROUPS - RNN_CONV_W + 1)) * SUBLANES + SUBLANES, :]
        else:
            pxr = sts[s - 1]["xr"]
            tail = lambda j: pxr[j * SUBLANES + SUBLANES - 1:(j + 1) * SUBLANES, :]
        sh = _shifted_groups(X, tail, sub0_rnn, RNN_CONV_W - 1)
        xcs = []
        for j in range(GROUPS):
            acc = conv_b_ref[...] + conv_w[RNN_CONV_W - 1][...] * X[j]
            for d in range(1, RNN_CONV_W):
                acc = acc + conv_w[RNN_CONV_W - 1 - d][...] * sh[d][j]
            xcs.append(acc)
        xc = jnp.concatenate(xcs, axis=0)
        a_parts, u_parts = _rglru_gates(xc, wa_ref, ba_ref[...], wi_ref, bi_ref[...], sp)
        a = jnp.concatenate(a_parts, axis=1)
        u = jnp.concatenate(u_parts, axis=1)
        h_in = hcar[...] if s == 0 else sts[s - 1]["h_out"]
        h, st["h_out"] = _scan_sub(a, u, h_in, sub_iota)
        yr = jnp.dot(st["hmix"], w_in_ref[:, C_YR:IN_COLS], preferred_element_type=f32)
        rnn = h * _gelu_tanh(yr)
        st["rnn_n"] = _rms_norm(rnn, g_rnn_ref[...]).astype(bf16)

    def stage_out(s):
        st = sts[s]
        mo = (jnp.dot(st["attn_n"], w_out_ref[0:ATTN_WIDTH, 0:D_MODEL], preferred_element_type=f32)
              + jnp.dot(st["rnn_n"], w_out_ref[ATTN_WIDTH:ATTN_WIDTH + D_RNN, 0:D_MODEL],
                        preferred_element_type=f32))
        x1 = (xbuf[slot, groups(s)].reshape(SUB_ROWS, D_MODEL)
              + g1 * _rms_norm(mo, g_post_mix_ref[...]))
        ybuf[yslot, groups(s)] = x1.reshape(GROUPS, SUBLANES, D_MODEL)
        hff_s[slot, rows(s), :] = (x1 * _rms_scale(x1) * gs2 + sh2).astype(bf16)

    tail_g0 = GROUPS - (FFN_CONV_W - 1)
    hff = hff_s[1 - slot]
    acts, tails = [], {}

    def conv_chunk(c0):
        c1 = c0 + FFN_CHUNK
        up = jnp.dot(hff, w_up_ref[:, c0:c1], preferred_element_type=f32)
        w = [fconv_w_ref.at[jj:jj + 1, c0:c1] for jj in range(FFN_CONV_W)]
        outs = []
        for s in range(NSUB):
            r0 = s * SUB_ROWS
            X = [up[r0 + j * SUBLANES:r0 + (j + 1) * SUBLANES, :] for j in range(GROUPS)]
            if s == 0:
                tail = lambda j: up_tail[(j - tail_g0) * SUBLANES + SUBLANES - 1:
                                         (j - tail_g0 + 1) * SUBLANES, c0:c1]
            else:
                tail = lambda j, p0=r0 - SUB_ROWS: up[p0 + j * SUBLANES + SUBLANES - 1:
                                                      p0 + (j + 1) * SUBLANES, :]
            sh = _shifted_groups(X, tail, sub0_ffn, FFN_CONV_W - 1)
            for j in range(GROUPS):
                acc = fconv_b_ref[:, c0:c1] + w[FFN_CONV_W - 1][...] * X[j]
                for d in range(1, FFN_CONV_W):
                    acc = acc + w[FFN_CONV_W - 1 - d][...] * sh[d][j]
                outs.append(acc)
        last0 = (NSUB - 1) * SUB_ROWS
        for t in range(FFN_CONV_W - 1):
            r = last0 + (tail_g0 + t) * SUBLANES + SUBLANES - 1
            ffnst_ref[t:t + 1, c0:c1] = up[r:r + 1, :]
        tails[c0] = up[last0 + tail_g0 * SUBLANES:, :]
        return jnp.concatenate(outs, axis=0)

    def ffn_chunk(cc):
        gate = conv_chunk(cc * FFN_CHUNK)
        val = conv_chunk(D_FF + cc * FFN_CHUNK)
        acts.append((_gelu_tanh(gate) * val).astype(bf16))

    def ffn_finish():
        for c0, tl in tails.items():
            up_tail[:, c0:c0 + FFN_CHUNK] = tl
        act = jnp.concatenate(acts, axis=1)
        f = jnp.dot(act, w_down_ref[:, 0:D_MODEL], preferred_element_type=f32)
        y = ybuf[fslot].reshape(TOKEN_BLOCK, D_MODEL) + g2 * _rms_norm(f, g_post_ffn_ref[...])
        ybuf[fslot] = y.reshape(NSUB * GROUPS, SUBLANES, D_MODEL)

    stages = (stage_pre, stage_inproj, stage_attn, stage_rnn, stage_out)
    n_times = len(stages) + (NSUB - 1) * STAGE_SKEW
    n_chunks = D_FF // FFN_CHUNK
    ffn_chunk(0)
    nxt = 1
    for t in range(n_times):
        for s in range(NSUB):
            k = t - s * STAGE_SKEW
            if 0 <= k < len(stages):
                stages[k](s)
                if nxt < n_chunks:
                    ffn_chunk(nxt)
                    nxt += 1
    while nxt < n_chunks:
        ffn_chunk(nxt)
        nxt += 1
    ffn_finish()

    lst = sts[NSUB - 1]
    kprev[...] = lst["kb"]
    vprev[...] = lst["vb"]
    xr_tail[...] = lst["xr"][(GROUPS - (RNN_CONV_W - 1)) * SUBLANES:, :]
    hcar[...] = lst["h_out"]

    @pl.when(step < nblocks)
    def _state():
        kst[...] = lst["kv"][:, 0:KV_WIDTH]
        vst[...] = lst["kv"][:, KV_WIDTH:2 * KV_WIDTH]
        for t in range(RNN_CONV_W - 1):
            r = (GROUPS - (RNN_CONV_W - 1) + t) * SUBLANES + SUBLANES - 1
            convst_ref[t:t + 1, :] = lst["xr"][r:r + 1, :]
        hlast_ref[...] = lst["h_out"][0:1, :]

    @pl.when(step >= 1)
    def _store_y():
        for cp in y_copies(step - 1, fslot):
            cp.start()

    @pl.when(step == nblocks)
    def _finish():
        for cp in state_copies():
            cp.start()
        for cp in state_copies() + weight_out_copies():
            cp.wait()
        for cp in y_copies(0, lax.rem(step + 1, 3)) + y_copies(0, fslot):
            cp.wait()


def _const_spec(shape):
    nd = len(shape)
    return pl.BlockSpec(shape, lambda i: (0,) * nd)


def _prompt_call(x, mod, mod_row0, sinks, params):
    T = x.shape[0]
    TB = TOKEN_BLOCK
    assert T // TB >= 3
    (g_pre_mix, w_in, conv_w, conv_b, wa, ba, wi, bi, lam, g_attn, g_rnn, w_out,
     g_post_mix, g_pre_ffn, w_up, fconv_w, fconv_b, w_down, g_post_ffn) = params
    x4 = x.reshape(T // SUB_ROWS, SUBLANES, GROUPS, D_MODEL)
    w_out3 = w_out.reshape(-1, SUBLANES, D_MODEL)
    w_down3 = w_down.reshape(-1, SUBLANES, D_MODEL)
    ins = [x4, mod, sinks, g_pre_mix, w_in, conv_w, conv_b, wa, ba, wi, bi, lam, g_attn, g_rnn,
           w_out3, g_post_mix, g_pre_ffn, w_up, fconv_w, fconv_b, w_down3, g_post_ffn]
    assert mod_row0 % MOD_PAD_ROWS == 0
    in_specs = [pl.BlockSpec(memory_space=pl.ANY),
                pl.BlockSpec((MOD_PAD_ROWS, mod.shape[1]), lambda i: (mod_row0 // MOD_PAD_ROWS, 0)),
                pl.BlockSpec(memory_space=pltpu.SMEM)]
    big = (w_in, w_out, w_up, w_down)
    assert all(w.dtype == f32 and w.shape[0] % TOKEN_BLOCK == 0 for w in big)
    in_specs += [pl.BlockSpec(memory_space=pl.ANY)
                 if any(a is w for w in (w_in, w_out3, w_up, w_down3))
                 else _const_spec(a.shape) for a in ins[3:]]
    out_shape = (
        jax.ShapeDtypeStruct(x4.shape, f32),
        jax.ShapeDtypeStruct((SUBLANES, GROUPS, KV_WIDTH), f32),
        jax.ShapeDtypeStruct((SUBLANES, GROUPS, KV_WIDTH), f32),
        jax.ShapeDtypeStruct((1, D_RNN), f32),
        jax.ShapeDtypeStruct((RNN_CONV_W - 1, D_RNN), f32),
        jax.ShapeDtypeStruct((FFN_CONV_W - 1, 2 * D_FF), f32),
    ) + tuple(jax.ShapeDtypeStruct(w.shape, bf16) for w in big)
    out_specs = (
        pl.BlockSpec(memory_space=pl.ANY),
        pl.BlockSpec(memory_space=pl.ANY),
        pl.BlockSpec(memory_space=pl.ANY),
        _const_spec((1, D_RNN)),
        _const_spec((RNN_CONV_W - 1, D_RNN)),
        _const_spec((FFN_CONV_W - 1, 2 * D_FF)),
    ) + (pl.BlockSpec(memory_space=pl.ANY),) * len(big)
    scratch = [
        pltpu.VMEM((2, TB // SUBLANES, SUBLANES, D_MODEL), f32),
        pltpu.VMEM((3, TB // SUBLANES, SUBLANES, D_MODEL), f32),
        pltpu.SemaphoreType.DMA((2,)),
        pltpu.SemaphoreType.DMA((3,)),
        pltpu.SemaphoreType.DMA(()),
        pltpu.VMEM((SUB_ROWS, KV_WIDTH), bf16),
        pltpu.VMEM((SUB_ROWS, KV_WIDTH), bf16),
        pltpu.VMEM((SUB_ROWS, KV_WIDTH), f32),
        pltpu.VMEM((SUB_ROWS, KV_WIDTH), f32),
        pltpu.VMEM(((RNN_CONV_W - 1) * SUBLANES, D_RNN), f32),
        pltpu.VMEM(((FFN_CONV_W - 1) * SUBLANES, 2 * D_FF), f32),
        pltpu.VMEM((SUBLANES, D_RNN), f32),
        pltpu.VMEM((N_HEADS, SUB_ROWS, 2 * SUB_ROWS), f32),
        pltpu.VMEM((2, TOKEN_BLOCK, D_MODEL), bf16),
        pltpu.VMEM(w_in.shape, bf16),
        pltpu.VMEM((w_out.shape[0], PADDED_OUT_COLS), bf16),
        pltpu.VMEM(w_up.shape, bf16),
        pltpu.VMEM((w_down.shape[0], PADDED_OUT_COLS), bf16),
        pltpu.VMEM((WSTAGE_SLOTS, WSTAGE_ROWS, w_up.shape[1]), f32),
        pltpu.SemaphoreType.DMA((WSTAGE_SLOTS,)),
        pltpu.SemaphoreType.DMA(()),
    ]
    assert w_in.shape[1] <= w_up.shape[1] and w_in.shape[0] % WSTAGE_ROWS == 0
    y4, kp, vp, hp, convp, ffnp, w_in_bf, w_out_bf, w_up_bf, w_down_bf = pl.pallas_call(
        _prompt_kernel,
        grid=(T // TB + 1,),
        in_specs=in_specs,
        out_specs=out_specs,
        out_shape=out_shape,
        scratch_shapes=scratch,
        compiler_params=pltpu.CompilerParams(
            dimension_semantics=("arbitrary",), vmem_limit_bytes=VMEM_LIMIT_BYTES),
        name="prompt_layer",
    )(*ins)
    return (y4.reshape(T, D_MODEL), kp.reshape(WINDOW, KV_WIDTH), vp.reshape(WINDOW, KV_WIDTH),
            hp, convp, ffnp, (w_in_bf, w_out_bf, w_up_bf, w_down_bf))


def _sample_pre_kernel(x_ref, mod_ref, g_pre_mix_ref, w_in_ref, q_ref, kv_ref, xr_ref, yr_ref):
    B = x_ref.shape[0]
    sh1 = mod_ref[0:B, 0 * D_MODEL:1 * D_MODEL]
    sc1 = mod_ref[0:B, 1 * D_MODEL:2 * D_MODEL]
    hmix = (_rms_norm(x_ref[:, 0, :], g_pre_mix_ref[...]) * (1.0 + sc1) + sh1).astype(bf16)
    q = jnp.dot(hmix, w_in_ref[:, 0:C_K], preferred_element_type=f32) * Q_SCALE
    for h in range(N_HEADS):
        q_ref[:, h, :] = q[:, h * HEAD_DIM:(h + 1) * HEAD_DIM]
    kv_ref[...] = jnp.dot(hmix, w_in_ref[:, C_K:C_XR], preferred_element_type=f32)
    xr_ref[...] = jnp.dot(hmix, w_in_ref[:, C_XR:C_YR], preferred_element_type=f32)
    yr_ref[...] = jnp.dot(hmix, w_in_ref[:, C_YR:IN_COLS], preferred_element_type=f32)


def _sample_pre_call(x, mod_s, g_pre_mix, w_in):
    B = x.shape[0]
    return pl.pallas_call(
        _sample_pre_kernel,
        out_shape=(
            jax.ShapeDtypeStruct((B, N_HEADS, HEAD_DIM), f32),
            jax.ShapeDtypeStruct((B, 2 * KV_WIDTH), f32),
            jax.ShapeDtypeStruct((B, D_RNN), f32),
            jax.ShapeDtypeStruct((B, D_RNN), f32),
        ),
        compiler_params=pltpu.CompilerParams(vmem_limit_bytes=VMEM_LIMIT_BYTES),
        name="sample_pre",
    )(x, mod_s, g_pre_mix, w_in)


def _sample_attn_kernel(q_ref, kv_ref, ck_ref, cv_ref, sinks_ref, o_ref, kwin_ref, vwin_ref):
    R = N_KV_HEADS * WINDOW
    hrow = lax.broadcasted_iota(jnp.int32, (N_HEADS, R), 0)
    rcol = lax.broadcasted_iota(jnp.int32, (N_HEADS, R), 1)
    slope = jnp.exp2(-8.0 * (hrow + 1).astype(f32) / N_HEADS)
    own = (rcol % N_KV_HEADS) == (hrow // GQA_GROUP)
    bias = jnp.where(own, slope * (WINDOW - 1 - rcol // N_KV_HEADS).astype(f32), jnp.inf)
    wrow = lax.broadcasted_iota(jnp.int32, (R, HEAD_DIM), 0)
    sink = sinks_ref[...]

    def shifted(cache, new_rows):
        out = pltpu.roll(cache, R - N_KV_HEADS, axis=0)
        for c in range(N_KV_HEADS):
            out = jnp.where(wrow == R - N_KV_HEADS + c, new_rows[c], out)
        return out

    scores = []
    for b in range(SAMPLE_CHUNK):
        knew = [kv_ref[b:b + 1, c * HEAD_DIM:(c + 1) * HEAD_DIM] for c in range(N_KV_HEADS)]
        kw = shifted(ck_ref[b], knew)
        kwin_ref[b] = kw
        scores.append(_dot_nt(q_ref[b].astype(bf16), kw.astype(bf16)) - bias)
    probs, denoms = [], []
    for b in range(SAMPLE_CHUNK):
        s = scores[b]
        m = jnp.maximum(jnp.max(s, axis=-1, keepdims=True), sink)
        p = jnp.exp(s - m)
        denoms.append(jnp.sum(p, axis=-1, keepdims=True) + jnp.exp(sink - m))
        probs.append(p.astype(bf16))
    for b in range(SAMPLE_CHUNK):
        vnew = [kv_ref[b:b + 1, KV_WIDTH + c * HEAD_DIM:KV_WIDTH + (c + 1) * HEAD_DIM]
                for c in range(N_KV_HEADS)]
        vw = shifted(cv_ref[b], vnew)
        vwin_ref[b] = vw
        o_ref[b] = (jnp.dot(probs[b], vw.astype(bf16), preferred_element_type=f32)
                    / denoms[b])


def _sample_attn_call(q, kv, ck, cv, sinks_col):
    B = q.shape[0]
    BC = SAMPLE_CHUNK
    return pl.pallas_call(
        _sample_attn_kernel,
        grid=(B // BC,),
        in_specs=[
            pl.BlockSpec((BC, N_HEADS, HEAD_DIM), lambda i: (i, 0, 0)),
            pl.BlockSpec((BC, 2 * KV_WIDTH), lambda i: (i, 0)),
            pl.BlockSpec((BC, N_KV_HEADS * WINDOW, HEAD_DIM), lambda i: (i, 0, 0)),
            pl.BlockSpec((BC, N_KV_HEADS * WINDOW, HEAD_DIM), lambda i: (i, 0, 0)),
            pl.BlockSpec((N_HEADS, 1), lambda i: (0, 0)),
        ],
        out_specs=(
            pl.BlockSpec((BC, N_HEADS, HEAD_DIM), lambda i: (i, 0, 0)),
            pl.BlockSpec((BC, N_KV_HEADS * WINDOW, HEAD_DIM), lambda i: (i, 0, 0)),
            pl.BlockSpec((BC, N_KV_HEADS * WINDOW, HEAD_DIM), lambda i: (i, 0, 0)),
        ),
        out_shape=(
            jax.ShapeDtypeStruct((B, N_HEADS, HEAD_DIM), f32),
            jax.ShapeDtypeStruct((B, N_KV_HEADS * WINDOW, HEAD_DIM), f32),
            jax.ShapeDtypeStruct((B, N_KV_HEADS * WINDOW, HEAD_DIM), f32),
        ),
        compiler_params=pltpu.CompilerParams(
            dimension_semantics=("arbitrary",), vmem_limit_bytes=SAMPLE_ATTN_VMEM_BYTES),
        name="sample_attn",
    )(q, kv, ck, cv, sinks_col)


def _sample_post_kernel(
    x_ref, mod_ref, attn_ref, xr_ref, yr_ref, h0_ref, cbuf_ref, fbuf_ref,
    conv_w_ref, conv_b_ref, wa_ref, ba_ref, wi_ref, bi_ref, lam_ref,
    g_attn_ref, g_rnn_ref, w_out_ref, g_post_mix_ref, g_pre_ffn_ref, w_up_hbm,
    fconv_w_ref, fconv_b_ref, w_down_hbm, g_post_ffn_ref,
    y_ref, h_ref, cst_ref, fst_ref,
    w_up_ref, w_down_ref, sem_ffn,
):
    up_copy = pltpu.make_async_copy(w_up_hbm, w_up_ref, sem_ffn.at[0])
    down_copy = pltpu.make_async_copy(w_down_hbm, w_down_ref, sem_ffn.at[1])
    up_copy.start()
    down_copy.start()

    B = x_ref.shape[0]
    g1 = mod_ref[0:B, 2 * D_MODEL:3 * D_MODEL]
    sh2 = mod_ref[0:B, 3 * D_MODEL:4 * D_MODEL]
    sc2 = mod_ref[0:B, 4 * D_MODEL:5 * D_MODEL]
    g2 = mod_ref[0:B, 5 * D_MODEL:6 * D_MODEL]
    x = x_ref[:, 0, :]
    xr = xr_ref[...]

    xc = conv_b_ref[...] + conv_w_ref[RNN_CONV_W - 1:RNN_CONV_W, :] * xr
    for jj in range(RNN_CONV_W - 1):
        xc = xc + conv_w_ref[jj:jj + 1, :] * cbuf_ref[jj]
    for jj in range(RNN_CONV_W - 2):
        cst_ref[jj] = cbuf_ref[jj + 1]
    cst_ref[RNN_CONV_W - 2] = xr

    sp = _softplus_neg(lam_ref[...])
    a_parts, u_parts = _rglru_gates(xc, wa_ref, ba_ref[...], wi_ref, bi_ref[...], sp)
    a = jnp.concatenate(a_parts, axis=1)
    u = jnp.concatenate(u_parts, axis=1)
    h = a * h0_ref[...] + u
    h_ref[...] = h
    rnn = h * _gelu_tanh(yr_ref[...])

    attn = jnp.concatenate([attn_ref[:, h, :] for h in range(N_HEADS)], axis=1)
    attn_n = _rms_norm(attn, g_attn_ref[...])
    rnn_n = _rms_norm(rnn, g_rnn_ref[...])
    mo = (_dot(attn_n, w_out_ref[0:ATTN_WIDTH, 0:D_MODEL])
          + _dot(rnn_n, w_out_ref[ATTN_WIDTH:ATTN_WIDTH + D_RNN, 0:D_MODEL]))
    x1 = x + g1 * _rms_norm(mo, g_post_mix_ref[...])

    hff = (_rms_norm(x1, g_pre_ffn_ref[...]) * (1.0 + sc2) + sh2).astype(bf16)
    up_copy.wait()
    up_pre = jnp.dot(hff, w_up_ref[...], preferred_element_type=f32)
    up = fconv_b_ref[...] + fconv_w_ref[FFN_CONV_W - 1:FFN_CONV_W, :] * up_pre
    for jj in range(FFN_CONV_W - 1):
        up = up + fconv_w_ref[jj:jj + 1, :] * fbuf_ref[:, jj, :]
    for jj in range(FFN_CONV_W - 2):
        fst_ref[:, jj, :] = fbuf_ref[:, jj + 1, :]
    fst_ref[:, FFN_CONV_W - 2, :] = up_pre

    act = (_gelu_tanh(up[:, 0:D_FF]) * up[:, D_FF:2 * D_FF]).astype(bf16)
    down_copy.wait()
    f = jnp.dot(act, w_down_ref[:, 0:D_MODEL], preferred_element_type=f32)
    y_ref[:, 0, :] = x1 + g2 * _rms_norm(f, g_post_ffn_ref[...])


def _sample_post_call(x, mod_s, attn, xr, yr, h0, cbuf, fbuf, params):
    B = x.shape[0]
    (_, _, conv_w, conv_b, wa, ba, wi, bi, lam, g_attn, g_rnn, w_out,
     g_post_mix, g_pre_ffn, w_up, fconv_w, fconv_b, w_down, g_post_ffn) = params
    ins = (x, mod_s, attn, xr, yr, h0, cbuf, fbuf, conv_w, conv_b, wa, ba, wi, bi, lam,
           g_attn, g_rnn, w_out, g_post_mix, g_pre_ffn, w_up, fconv_w, fconv_b, w_down, g_post_ffn)
    return pl.pallas_call(
        _sample_post_kernel,
        in_specs=[pl.BlockSpec(memory_space=pl.ANY) if (a is w_up or a is w_down)
                  else pl.BlockSpec(memory_space=pltpu.VMEM) for a in ins],
        out_shape=(
            jax.ShapeDtypeStruct((B, 1, D_MODEL), f32),
            jax.ShapeDtypeStruct((B, D_RNN), f32),
            jax.ShapeDtypeStruct((RNN_CONV_W - 1, B, D_RNN), f32),
            jax.ShapeDtypeStruct((B, FFN_CONV_W - 1, 2 * D_FF), f32),
        ),
        scratch_shapes=[pltpu.VMEM(w_up.shape, w_up.dtype), pltpu.VMEM(w_down.shape, w_down.dtype),
                        pltpu.SemaphoreType.DMA((2,))],
        compiler_params=pltpu.CompilerParams(vmem_limit_bytes=VMEM_LIMIT_BYTES),
        name="sample_post",
    )(*ins)


def kernel(x_prompt, x_sample, cache_k, cache_v, state_h, state_conv, state_ffn_conv, c_prompt, c_sample, w_ada, b_ada, g_pre_mix, w_in, conv_w, conv_b, w_a, b_a, w_i, b_i, lam, sinks, g_attn_out, g_rnn_out, w_out, g_post_mix, g_pre_ffn, w_up, ffn_conv_w, ffn_conv_b, w_down, g_post_ffn):
    depth = w_in.shape[0]
    assert depth == 1 and x_prompt.shape[0] == 1 and x_sample.shape[1] == 1
    T = x_prompt.shape[1]
    B = x_sample.shape[0]
    W = cache_k.shape[2]
    assert W == WINDOW and T % TOKEN_BLOCK == 0 and B % SAMPLE_CHUNK == 0

    row = lambda a: a[0].reshape(1, -1)
    params = (
        row(g_pre_mix), w_in[0], conv_w[0], row(conv_b),
        w_a[0].astype(bf16), row(b_a), w_i[0].astype(bf16), row(b_i), row(lam),
        row(g_attn_out), row(g_rnn_out), w_out[0], row(g_post_mix),
        row(g_pre_ffn),
        w_up[0], ffn_conv_w[0], row(ffn_conv_b), w_down[0],
        row(g_post_ffn),
    )

    c_all = jnp.concatenate(
        [c_sample, jnp.broadcast_to(c_prompt, (MOD_PAD_ROWS, D_MODEL))], axis=0)
    mod = _mod_call(c_all, w_ada[0], b_ada[0].reshape(1, -1))
    mod_s = mod

    yp, kp, vp, hp, convp, ffnp, (w_in_bf, w_out_bf, w_up_bf, w_down_bf) = _prompt_call(
        x_prompt[0], mod, B, sinks[0], params)
    params = list(params)
    params[1], params[11], params[14], params[17] = w_in_bf, w_out_bf, w_up_bf, w_down_bf
    assert params[11].shape == w_out.shape[1:] and params[17].shape == w_down.shape[1:]

    xs = x_sample
    q, kv, xr, yr = _sample_pre_call(xs, mod_s, params[0], params[1])
    ck = cache_k.reshape(B, W * N_KV_HEADS, HEAD_DIM)
    cv = cache_v.reshape(B, W * N_KV_HEADS, HEAD_DIM)
    attn3, kwin, vwin = _sample_attn_call(
        q, kv, ck, cv, sinks[0].reshape(N_HEADS, 1))
    attn = attn3
    ys, hs, convs, ffns = _sample_post_call(
        xs, mod_s, attn, xr, yr, state_h[0],
        jnp.transpose(state_conv[0], (1, 0, 2)), state_ffn_conv[0],
        params)
    convs = jnp.transpose(convs, (1, 0, 2))

    kv_shape = (1, 1, W, N_KV_HEADS, HEAD_DIM)
    kvs_shape = (1, B, W, N_KV_HEADS, HEAD_DIM)
    return (
        yp[None], ys,
        kp.reshape(kv_shape), vp.reshape(kv_shape), hp[None], convp[None, None], ffnp[None, None],
        kwin.reshape(kvs_shape), vwin.reshape(kvs_shape), hs[None], convs[None], ffns[None],
    )
```

```python
import math

import jax
import jax.numpy as jnp
from jax import lax
from jax.experimental import pallas as pl
from jax.experimental.pallas import tpu as pltpu

D_MODEL = 1024
N_HEADS = 8
N_KV_HEADS = 2
HEAD_DIM = 128
GQA_GROUP = N_HEADS // N_KV_HEADS
ATTN_WIDTH = N_HEADS * HEAD_DIM
KV_WIDTH = N_KV_HEADS * HEAD_DIM
WINDOW = 128
D_RNN = D_MODEL
RNN_BLOCKS = 8
RNN_BLOCK_W = D_RNN // RNN_BLOCKS
RG_C = 8.0
RNN_CONV_W = 4
D_FF = 2816
FFN_CONV_W = 3
RMS_EPS = 1e-6

C_K = ATTN_WIDTH
C_V = C_K + KV_WIDTH
C_XR = C_V + KV_WIDTH
C_YR = C_XR + D_RNN
IN_COLS = C_YR + D_RNN

SUBLANES = 8
SUB_ROWS = WINDOW
GROUPS = SUB_ROWS // SUBLANES
TOKEN_BLOCK = 256
NSUB = TOKEN_BLOCK // SUB_ROWS
FFN_CHUNK = 256
STAGE_SKEW = 3
WSTAGE_ROWS = 64
WSTAGE_SLOTS = 4
SAMPLE_CHUNK = 16
SAMPLE_ATTN_VMEM_BYTES = (2 * 2 * 2 * SAMPLE_CHUNK * N_KV_HEADS * WINDOW * HEAD_DIM * 4
                          + 8 * 1024 * 1024)
MOD_PAD_ROWS = 8
VMEM_LIMIT_BYTES = 56 * 1024 * 1024

ALIBI_SLOPES = tuple(2.0 ** (-8.0 * (h + 1) / N_HEADS) for h in range(N_HEADS))
Q_SCALE = HEAD_DIM ** -0.5
SQRT_2_OVER_PI = math.sqrt(2.0 / math.pi)
LOG2_E = 1.0 / math.log(2.0)
PADDED_OUT_COLS = D_MODEL + 128

bf16 = jnp.bfloat16
f32 = jnp.float32


def _rms_scale(x):
    return lax.rsqrt(jnp.mean(x * x, axis=-1, keepdims=True) + RMS_EPS)


def _rms_norm(x, g):
    return x * _rms_scale(x) * g


def _gelu_tanh(x):
    k1 = -2.0 * LOG2_E * SQRT_2_OVER_PI
    k3 = k1 * 0.044715
    return x / (1.0 + jnp.exp2(x * (k1 + k3 * (x * x))))


def _sigmoid(x):
    return 1.0 / (1.0 + jnp.exp2(x * (-LOG2_E)))


def _softplus_neg(lam):
    return jnp.maximum(-lam, 0.0) + jnp.log1p(jnp.exp(-jnp.abs(lam)))


def _dot(a, b):
    return jnp.dot(a.astype(bf16), b, preferred_element_type=f32)


def _dot_nt(a, b):
    return lax.dot_general(a, b, (((1,), (1,)), ((), ())), preferred_element_type=f32)


def _rglru_gates(xc, wa_ref, ba, wi_ref, bi, sp):
    a_parts, u_parts = [], []
    for n in range(RNN_BLOCKS):
        sl = slice(n * RNN_BLOCK_W, (n + 1) * RNN_BLOCK_W)
        xn = xc[:, sl]
        xb = xn.astype(bf16)
        r = _sigmoid(jnp.dot(xb, wa_ref[n], preferred_element_type=f32) + ba[:, sl])
        i = _sigmoid(jnp.dot(xb, wi_ref[n], preferred_element_type=f32) + bi[:, sl])
        log_a = (-RG_C) * r * sp[:, sl]
        t = jnp.tanh(log_a)
        one_minus_a2 = (-2.0 * t) / (1.0 - t)
        a_parts.append(jnp.exp(log_a))
        u_parts.append(jnp.sqrt(one_minus_a2) * (i * xn))
    return a_parts, u_parts


def _mod_kernel(c_ref, w_hbm, b_ref, o_ref, wbuf, sem):
    nblk, _, bn = wbuf.shape
    copies = [pltpu.make_async_copy(w_hbm.at[:, pl.ds(j * bn, bn)], wbuf.at[j], sem.at[j])
              for j in range(nblk)]
    for cp in copies:
        cp.start()
    c = c_ref[...]
    s = (c * _sigmoid(c)).astype(bf16)
    for j in range(nblk):
        copies[j].wait()
        cols = slice(j * bn, (j + 1) * bn)
        o_ref[:, cols] = (jnp.dot(s, wbuf[j].astype(bf16), preferred_element_type=f32)
                          + b_ref[:, cols])


def _mod_call(c_all, w_ada, b_ada):
    rows = c_all.shape[0]
    k, ncol = w_ada.shape
    bn = D_MODEL
    vmem = pl.BlockSpec(memory_space=pltpu.VMEM)
    return pl.pallas_call(
        _mod_kernel,
        in_specs=[vmem, pl.BlockSpec(memory_space=pl.ANY), vmem],
        out_shape=jax.ShapeDtypeStruct((rows, ncol), f32),
        scratch_shapes=[pltpu.VMEM((ncol // bn, k, bn), f32),
                        pltpu.SemaphoreType.DMA((ncol // bn,))],
        compiler_params=pltpu.CompilerParams(
            vmem_limit_bytes=k * ncol * 4 + rows * ncol * 4 + 8 * 1024 * 1024),
        name="adaln_mod",
    )(c_all, w_ada, b_ada)


def _groups(v, c0=None, c1=None):
    if c0 is None:
        return [v[j * SUBLANES:(j + 1) * SUBLANES, :] for j in range(GROUPS)]
    return [v[j * SUBLANES:(j + 1) * SUBLANES, c0:c1] for j in range(GROUPS)]


def _shifted_groups(X, tail_row, sub0, max_shift):
    wrapped = {}
    for j in range(GROUPS - max_shift, GROUPS):
        wrapped[j] = jnp.where(sub0, tail_row(j), pltpu.roll(X[j], 1, axis=0))
    sh = {}
    for d in range(1, max_shift + 1):
        sh[d] = [X[j - d] if j >= d else wrapped[j - d + GROUPS] for j in range(GROUPS)]
    return sh


def _scan_sub(a, u, h_in, sub_iota):
    A = _groups(a)
    L = _groups(u)
    for j in range(1, GROUPS):
        L[j] = A[j] * L[j - 1] + L[j]
        A[j] = A[j] * A[j - 1]
    ae, le = A[GROUPS - 1], L[GROUPS - 1]
    for s_ in (1, 2, 4):
        ok = sub_iota >= s_
        a_sh = pltpu.roll(ae, s_, axis=0)
        l_sh = pltpu.roll(le, s_, axis=0)
        le = jnp.where(ok, ae * l_sh + le, le)
        ae = jnp.where(ok, ae * a_sh, ae)
    hend = le + ae * h_in
    hprev = jnp.where(sub_iota == 0, h_in, pltpu.roll(hend, 1, axis=0))
    h = jnp.concatenate([L[j] + A[j] * hprev for j in range(GROUPS)], axis=0)
    h_out = jnp.broadcast_to(hend[SUBLANES - 1:SUBLANES, :], hend.shape)
    return h, h_out


def _prompt_kernel(
    x_hbm, mod_ref, sinks_ref,
    g_pre_mix_ref, w_in_hbm, conv_w_ref, conv_b_ref, wa_ref, ba_ref, wi_ref, bi_ref, lam_ref,
    g_attn_ref, g_rnn_ref, w_out_hbm, g_post_mix_ref, g_pre_ffn_ref, w_up_hbm,
    fconv_w_ref, fconv_b_ref, w_down_hbm, g_post_ffn_ref,
    y_hbm, kwin_hbm, vwin_hbm, hlast_ref, convst_ref, ffnst_ref,
    w_in_bf_hbm, w_out_bf_hbm, w_up_bf_hbm, w_down_bf_hbm,
    xbuf, ybuf, sem_in, sem_out, sem_st, kprev, vprev, kst, vst, xr_tail, up_tail, hcar, bias_s,
    hff_s, w_in_ref, w_out_ref, w_up_ref, w_down_ref, wstage, sem_w, sem_wout,
):
    step = pl.program_id(0)
    nblocks = pl.num_programs(0) - 1
    slot = lax.rem(step, 2)
    yslot = lax.rem(step, 3)
    fslot = lax.rem(step + 2, 3)

    def x_copies(st, sl):
        return [pltpu.make_async_copy(
            x_hbm.at[st * NSUB + s, c, :, :],
            xbuf.at[sl, pl.ds(s * GROUPS, GROUPS), c, :],
            sem_in.at[sl]) for s in range(NSUB) for c in range(SUBLANES)]

    def y_copies(st, sl):
        return [pltpu.make_async_copy(
            ybuf.at[sl, pl.ds(s * GROUPS, GROUPS), c, :],
            y_hbm.at[st * NSUB + s, c, :, :],
            sem_out.at[sl]) for s in range(NSUB) for c in range(SUBLANES)]

    def state_copies():
        cps = []
        for src, dst in ((kst, kwin_hbm), (vst, vwin_hbm)):
            cps += [pltpu.make_async_copy(src.at[pl.ds(j * SUBLANES, SUBLANES), :],
                                          dst.at[:, j, :], sem_st) for j in range(GROUPS)]
        return cps

    def weight_out_copies():
        return [pltpu.make_async_copy(w_in_ref, w_in_bf_hbm, sem_wout),
                pltpu.make_async_copy(w_up_ref, w_up_bf_hbm, sem_wout),
                pltpu.make_async_copy(w_out_ref.at[:, pl.ds(0, D_MODEL)], w_out_bf_hbm, sem_wout),
                pltpu.make_async_copy(w_down_ref.at[:, pl.ds(0, D_MODEL)], w_down_bf_hbm, sem_wout)]

    def load_cast(w_hbm, w_s, stage, chunk_rows):
        tiled = len(stage.shape) == 4
        if tiled:
            n_rows, n_cols = w_hbm.shape[0] * SUBLANES, w_hbm.shape[2]
            assert stage.shape[1:] == (chunk_rows // SUBLANES, SUBLANES, n_cols)
        else:
            n_rows, n_cols = w_hbm.shape
        n = n_rows // chunk_rows
        depth = stage.shape[0]
        assert n >= depth

        def chunk(i, sl):
            if tiled:
                g = chunk_rows // SUBLANES
                return pltpu.make_async_copy(w_hbm.at[pl.ds(i * g, g), :, :], stage.at[sl],
                                             sem_w.at[sl])
            return pltpu.make_async_copy(
                w_hbm.at[pl.ds(i * chunk_rows, chunk_rows), :],
                stage.at[sl, pl.ds(0, chunk_rows), pl.ds(0, n_cols)], sem_w.at[sl])

        def staged(sl):
            if tiled:
                return stage[sl].reshape(chunk_rows, n_cols)
            return stage[sl, 0:chunk_rows, 0:n_cols]

        for i in range(depth):
            chunk(i, i).start()

        def body(i, carry):
            sl = lax.rem(i, depth)
            chunk(i, sl).wait()
            r0 = pl.multiple_of(i * chunk_rows, chunk_rows)
            w_s[pl.ds(r0, chunk_rows), 0:n_cols] = staged(sl).astype(bf16)

            @pl.when(i + depth < n)
            def _next():
                chunk(i + depth, sl).start()

            return carry

        lax.fori_loop(0, n, body, 0)

    @pl.when(step == 0)
    def _init():
        for cp in x_copies(0, 0):
            cp.start()
        load_cast(w_in_hbm, w_in_ref, wstage, WSTAGE_ROWS)
        load_cast(w_out_hbm, w_out_ref, ybuf, TOKEN_BLOCK)
        load_cast(w_up_hbm, w_up_ref, wstage, WSTAGE_ROWS)
        load_cast(w_down_hbm, w_down_ref, ybuf, TOKEN_BLOCK)
        for cp in weight_out_copies():
            cp.start()
        kprev[...] = jnp.zeros(kprev.shape, bf16)
        vprev[...] = jnp.zeros(vprev.shape, bf16)
        xr_tail[...] = jnp.zeros(xr_tail.shape, f32)
        up_tail[...] = jnp.zeros(up_tail.shape, f32)
        hcar[...] = jnp.zeros(hcar.shape, f32)
        hff_s[...] = jnp.zeros(hff_s.shape, bf16)
        ybuf[2] = jnp.zeros(ybuf.shape[1:], f32)
        rq = lax.broadcasted_iota(jnp.int32, (SUB_ROWS, 2 * SUB_ROWS), 0)
        ck = lax.broadcasted_iota(jnp.int32, (SUB_ROWS, 2 * SUB_ROWS), 1)
        rk = ck & (SUB_ROWS - 1)
        tq = (rq % SUBLANES) * GROUPS + rq // SUBLANES
        tk = (rk % SUBLANES) * GROUPS + rk // SUBLANES
        dist = tq - tk + jnp.where(ck < SUB_ROWS, WINDOW, 0)
        base = jnp.where((dist >= 0) & (dist < WINDOW), dist.astype(f32), jnp.inf)
        for h in range(N_HEADS):
            bias_s[h] = ALIBI_SLOPES[h] * base

    @pl.when(step + 1 < nblocks)
    def _prefetch():
        for cp in x_copies(step + 1, 1 - slot):
            cp.start()

    @pl.when(step >= 3)
    def _free_ybuf():
        for cp in y_copies(0, yslot):
            cp.wait()

    @pl.when(step < nblocks)
    def _wait_x():
        for cp in x_copies(step, slot):
            cp.wait()

    sh1 = mod_ref[0:1, 0 * D_MODEL:1 * D_MODEL]
    sc1 = mod_ref[0:1, 1 * D_MODEL:2 * D_MODEL]
    g1 = mod_ref[0:1, 2 * D_MODEL:3 * D_MODEL]
    sh2 = mod_ref[0:1, 3 * D_MODEL:4 * D_MODEL]
    sc2 = mod_ref[0:1, 4 * D_MODEL:5 * D_MODEL]
    g2 = mod_ref[0:1, 5 * D_MODEL:6 * D_MODEL]
    gs1 = g_pre_mix_ref[...] * (1.0 + sc1)
    gs2 = g_pre_ffn_ref[...] * (1.0 + sc2)
    sp = _softplus_neg(lam_ref[...])
    first_pen = jnp.where(step == 0, jnp.inf, 0.0)
    sub_iota = lax.broadcasted_iota(jnp.int32, (SUBLANES, D_RNN), 0)
    sub0_rnn = sub_iota == 0
    sub0_ffn = lax.broadcasted_iota(jnp.int32, (SUBLANES, FFN_CHUNK), 0) == 0
    conv_w = [conv_w_ref.at[jj:jj + 1, :] for jj in range(RNN_CONV_W)]

    sts = [dict() for _ in range(NSUB)]

    def rows(s):
        return pl.ds(s * SUB_ROWS, SUB_ROWS)

    def groups(s):
        return pl.ds(s * GROUPS, GROUPS)

    def stage_pre(s):
        x = xbuf[slot, groups(s)].reshape(SUB_ROWS, D_MODEL)
        sts[s]["hmix"] = (x * _rms_scale(x) * gs1 + sh1).astype(bf16)

    def stage_inproj(s):
        st = sts[s]
        hm = st["hmix"]
        st["q"] = (jnp.dot(hm, w_in_ref[:, 0:C_K], preferred_element_type=f32) * Q_SCALE).astype(bf16)
        kv = jnp.dot(hm, w_in_ref[:, C_K:C_XR], preferred_element_type=f32)
        st["kb"] = kv[:, 0:KV_WIDTH].astype(bf16)
        st["vb"] = kv[:, KV_WIDTH:2 * KV_WIDTH].astype(bf16)
        st["kv"] = kv
        st["xr"] = jnp.dot(hm, w_in_ref[:, C_XR:C_YR], preferred_element_type=f32)

    def stage_attn(s):
        st = sts[s]
        if s == 0:
            kp, vp = kprev[...], vprev[...]
        else:
            kp, vp = sts[s - 1]["kb"], sts[s - 1]["vb"]
        kw = jnp.concatenate([kp, st["kb"]], axis=0)
        vw = jnp.concatenate([vp, st["vb"]], axis=0)
        outs = []
        for h in range(N_HEADS):
            c = h // GQA_GROUP
            hs = slice(h * HEAD_DIM, (h + 1) * HEAD_DIM)
            cs = slice(c * HEAD_DIM, (c + 1) * HEAD_DIM)
            sc = _dot_nt(st["q"][:, hs], kw[:, cs])
            if s == 0:
                sc = jnp.concatenate([sc[:, 0:SUB_ROWS] - first_pen, sc[:, SUB_ROWS:]], axis=1)
            sc = sc - bias_s[h]
            sink = sinks_ref[h]
            m = jnp.maximum(jnp.max(sc, axis=-1, keepdims=True), sink)
            p = jnp.exp(sc - m)
            denom = jnp.sum(p, axis=-1, keepdims=True) + jnp.exp(sink - m)
            outs.append(jnp.dot(p.astype(bf16), vw[:, cs], preferred_element_type=f32) / denom)
        attn = jnp.concatenate(outs, axis=1)
        st["attn_n"] = _rms_norm(attn, g_attn_ref[...]).astype(bf16)

    def stage_rnn(s):
        st = sts[s]
        xr = st["xr"]
        X = _groups(xr)
        if s == 0:
            tail = lambda j: xr_tail[(j - (GROUPS - RNN_CONV_W + 1)) * SUBLANES + SUBLANES - 1:
                                     (j - (GROUPS - RNN_CONV_W + 1)) * SUBLANES + SUBLANES, :]
        else:
            pxr = sts[s - 1]["xr"]
            tail = lambda j: pxr[j * SUBLANES + SUBLANES - 1:(j + 1) * SUBLANES, :]
        sh = _shifted_groups(X, tail, sub0_rnn, RNN_CONV_W - 1)
        xcs = []
        for j in range(GROUPS):
            acc = conv_b_ref[...] + conv_w[RNN_CONV_W - 1][...] * X[j]
            for d in range(1, RNN_CONV_W):
                acc = acc + conv_w[RNN_CONV_W - 1 - d][...] * sh[d][j]
            xcs.append(acc)
        xc = jnp.concatenate(xcs, axis=0)
        a_parts, u_parts = _rglru_gates(xc, wa_ref, ba_ref[...], wi_ref, bi_ref[...], sp)
        a = jnp.concatenate(a_parts, axis=1)
        u = jnp.concatenate(u_parts, axis=1)
        h_in = hcar[...] if s == 0 else sts[s - 1]["h_out"]
        h, st["h_out"] = _scan_sub(a, u, h_in, sub_iota)
        yr = jnp.dot(st["hmix"], w_in_ref[:, C_YR:IN_COLS], preferred_element_type=f32)
        rnn = h * _gelu_tanh(yr)
        st["rnn_n"] = _rms_norm(rnn, g_rnn_ref[...]).astype(bf16)

    def stage_out(s):
        st = sts[s]
        mo = (jnp.dot(st["attn_n"], w_out_ref[0:ATTN_WIDTH, 0:D_MODEL], preferred_element_type=f32)
              + jnp.dot(st["rnn_n"], w_out_ref[ATTN_WIDTH:ATTN_WIDTH + D_RNN, 0:D_MODEL],
                        preferred_element_type=f32))
        x1 = (xbuf[slot, groups(s)].reshape(SUB_ROWS, D_MODEL)
              + g1 * _rms_norm(mo, g_post_mix_ref[...]))
        ybuf[yslot, groups(s)] = x1.reshape(GROUPS, SUBLANES, D_MODEL)
        hff_s[slot, rows(s), :] = (x1 * _rms_scale(x1) * gs2 + sh2).astype(bf16)

    tail_g0 = GROUPS - (FFN_CONV_W - 1)
    hff = hff_s[1 - slot]
    acts, tails = [], {}

    def conv_chunk(c0):
        c1 = c0 + FFN_CHUNK
        up = jnp.dot(hff, w_up_ref[:, c0:c1], preferred_element_type=f32)
        w = [fconv_w_ref.at[jj:jj + 1, c0:c1] for jj in range(FFN_CONV_W)]
        outs = []
        for s in range(NSUB):
            r0 = s * SUB_ROWS
            X = [up[r0 + j * SUBLANES:r0 + (j + 1) * SUBLANES, :] for j in range(GROUPS)]
            if s == 0:
                tail = lambda j: up_tail[(j - tail_g0) * SUBLANES + SUBLANES - 1:
                                         (j - tail_g0 + 1) * SUBLANES, c0:c1]
            else:
                tail = lambda j, p0=r0 - SUB_ROWS: up[p0 + j * SUBLANES + SUBLANES - 1:
                                                      p0 + (j + 1) * SUBLANES, :]
            sh = _shifted_groups(X, tail, sub0_ffn, FFN_CONV_W - 1)
            for j in range(GROUPS):
                acc = fconv_b_ref[:, c0:c1] + w[FFN_CONV_W - 1][...] * X[j]
                for d in range(1, FFN_CONV_W):
                    acc = acc + w[FFN_CONV_W - 1 - d][...] * sh[d][j]
                outs.append(acc)
        last0 = (NSUB - 1) * SUB_ROWS
        for t in range(FFN_CONV_W - 1):
            r = last0 + (tail_g0 + t) * SUBLANES + SUBLANES - 1
            ffnst_ref[t:t + 1, c0:c1] = up[r:r + 1, :]
        tails[c0] = up[last0 + tail_g0 * SUBLANES:, :]
        return jnp.concatenate(outs, axis=0)

    def ffn_chunk(cc):
        gate = conv_chunk(cc * FFN_CHUNK)
        val = conv_chunk(D_FF + cc * FFN_CHUNK)
        acts.append((_gelu_tanh(gate) * val).astype(bf16))

    def ffn_finish():
        for c0, tl in tails.items():
            up_tail[:, c0:c0 + FFN_CHUNK] = tl
        act = jnp.concatenate(acts, axis=1)
        f = jnp.dot(act, w_down_ref[:, 0:D_MODEL], preferred_element_type=f32)
        y = ybuf[fslot].reshape(TOKEN_BLOCK, D_MODEL) + g2 * _rms_norm(f, g_post_ffn_ref[...])
        ybuf[fslot] = y.reshape(NSUB * GROUPS, SUBLANES, D_MODEL)

    stages = (stage_pre, stage_inproj, stage_attn, stage_rnn, stage_out)
    n_times = len(stages) + (NSUB - 1) * STAGE_SKEW
    n_chunks = D_FF // FFN_CHUNK
    ffn_chunk(0)
    nxt = 1
    for t in range(n_times):
        for s in range(NSUB):
            k = t - s * STAGE_SKEW
            if 0 <= k < len(stages):
                stages[k](s)
                if nxt < n_chunks:
                    ffn_chunk(nxt)
                    nxt += 1
    while nxt < n_chunks:
        ffn_chunk(nxt)
        nxt += 1
    ffn_finish()

    lst = sts[NSUB - 1]
    kprev[...] = lst["kb"]
    vprev[...] = lst["vb"]
    xr_tail[...] = lst["xr"][(GROUPS - (RNN_CONV_W - 1)) * SUBLANES:, :]
    hcar[...] = lst["h_out"]

    @pl.when(step < nblocks)
    def _state():
        kst[...] = lst["kv"][:, 0:KV_WIDTH]
        vst[...] = lst["kv"][:, KV_WIDTH:2 * KV_WIDTH]
        for t in range(RNN_CONV_W - 1):
            r = (GROUPS - (RNN_CONV_W - 1) + t) * SUBLANES + SUBLANES - 1
            convst_ref[t:t + 1, :] = lst["xr"][r:r + 1, :]
        hlast_ref[...] = lst["h_out"][0:1, :]

    @pl.when(step >= 1)
    def _store_y():
        for cp in y_copies(step - 1, fslot):
            cp.start()

    @pl.when(step == nblocks)
    def _finish():
        for cp in state_copies():
            cp.start()
        for cp in state_copies() + weight_out_copies():
            cp.wait()
        for cp in y_copies(0, lax.rem(step + 1, 3)) + y_copies(0, fslot):
            cp.wait()


def _const_spec(shape):
    nd = len(shape)
    return pl.BlockSpec(shape, lambda i: (0,) * nd)


def _prompt_call(x, mod, mod_row0, sinks, params):
    T = x.shape[0]
    TB = TOKEN_BLOCK
    assert T // TB >= 3
    (g_pre_mix, w_in, conv_w, conv_b, wa, ba, wi, bi, lam, g_attn, g_rnn, w_out,
     g_post_mix, g_pre_ffn, w_up, fconv_w, fconv_b, w_down, g_post_ffn) = params
    x4 = x.reshape(T // SUB_ROWS, SUBLANES, GROUPS, D_MODEL)
    w_out3 = w_out.reshape(-1, SUBLANES, D_MODEL)
    w_down3 = w_down.reshape(-1, SUBLANES, D_MODEL)
    ins = [x4, mod, sinks, g_pre_mix, w_in, conv_w, conv_b, wa, ba, wi, bi, lam, g_attn, g_rnn,
           w_out3, g_post_mix, g_pre_ffn, w_up, fconv_w, fconv_b, w_down3, g_post_ffn]
    assert mod_row0 % MOD_PAD_ROWS == 0
    in_specs = [pl.BlockSpec(memory_space=pl.ANY),
                pl.BlockSpec((MOD_PAD_ROWS, mod.shape[1]), lambda i: (mod_row0 // MOD_PAD_ROWS, 0)),
                pl.BlockSpec(memory_space=pltpu.SMEM)]
    big = (w_in, w_out, w_up, w_down)
    assert all(w.dtype == f32 and w.shape[0] % TOKEN_BLOCK == 0 for w in big)
    in_specs += [pl.BlockSpec(memory_space=pl.ANY)
                 if any(a is w for w in (w_in, w_out3, w_up, w_down3))
                 else _const_spec(a.shape) for a in ins[3:]]
    out_shape = (
        jax.ShapeDtypeStruct(x4.shape, f32),
        jax.ShapeDtypeStruct((SUBLANES, GROUPS, KV_WIDTH), f32),
        jax.ShapeDtypeStruct((SUBLANES, GROUPS, KV_WIDTH), f32),
        jax.ShapeDtypeStruct((1, D_RNN), f32),
        jax.ShapeDtypeStruct((RNN_CONV_W - 1, D_RNN), f32),
        jax.ShapeDtypeStruct((FFN_CONV_W - 1, 2 * D_FF), f32),
    ) + tuple(jax.ShapeDtypeStruct(w.shape, bf16) for w in big)
    out_specs = (
        pl.BlockSpec(memory_space=pl.ANY),
        pl.BlockSpec(memory_space=pl.ANY),
        pl.BlockSpec(memory_space=pl.ANY),
        _const_spec((1, D_RNN)),
        _const_spec((RNN_CONV_W - 1, D_RNN)),
        _const_spec((FFN_CONV_W - 1, 2 * D_FF)),
    ) + (pl.BlockSpec(memory_space=pl.ANY),) * len(big)
    scratch = [
        pltpu.VMEM((2, TB // SUBLANES, SUBLANES, D_MODEL), f32),
        pltpu.VMEM((3, TB // SUBLANES, SUBLANES, D_MODEL), f32),
        pltpu.SemaphoreType.DMA((2,)),
        pltpu.SemaphoreType.DMA((3,)),
        pltpu.SemaphoreType.DMA(()),
        pltpu.VMEM((SUB_ROWS, KV_WIDTH), bf16),
        pltpu.VMEM((SUB_ROWS, KV_WIDTH), bf16),
        pltpu.VMEM((SUB_ROWS, KV_WIDTH), f32),
        pltpu.VMEM((SUB_ROWS, KV_WIDTH), f32),
        pltpu.VMEM(((RNN_CONV_W - 1) * SUBLANES, D_RNN), f32),
        pltpu.VMEM(((FFN_CONV_W - 1) * SUBLANES, 2 * D_FF), f32),
        pltpu.VMEM((SUBLANES, D_RNN), f32),
        pltpu.VMEM((N_HEADS, SUB_ROWS, 2 * SUB_ROWS), f32),
        pltpu.VMEM((2, TOKEN_BLOCK, D_MODEL), bf16),
        pltpu.VMEM(w_in.shape, bf16),
        pltpu.VMEM((w_out.shape[0], PADDED_OUT_COLS), bf16),
        pltpu.VMEM(w_up.shape, bf16),
        pltpu.VMEM((w_down.shape[0], PADDED_OUT_COLS), bf16),
        pltpu.VMEM((WSTAGE_SLOTS, WSTAGE_ROWS, w_up.shape[1]), f32),
        pltpu.SemaphoreType.DMA((WSTAGE_SLOTS,)),
        pltpu.SemaphoreType.DMA(()),
    ]
    assert w_in.shape[1] <= w_up.shape[1] and w_in.shape[0] % WSTAGE_ROWS == 0
    y4, kp, vp, hp, convp, ffnp, w_in_bf, w_out_bf, w_up_bf, w_down_bf = pl.pallas_call(
        _prompt_kernel,
        grid=(T // TB + 1,),
        in_specs=in_specs,
        out_specs=out_specs,
        out_shape=out_shape,
        scratch_shapes=scratch,
        compiler_params=pltpu.CompilerParams(
            dimension_semantics=("arbitrary",), vmem_limit_bytes=VMEM_LIMIT_BYTES),
        name="prompt_layer",
    )(*ins)
    return (y4.reshape(T, D_MODEL), kp.reshape(WINDOW, KV_WIDTH), vp.reshape(WINDOW, KV_WIDTH),
            hp, convp, ffnp, (w_in_bf, w_out_bf, w_up_bf, w_down_bf))


def _sample_pre_kernel(x_ref, mod_ref, g_pre_mix_ref, w_in_ref, q_ref, kv_ref, xr_ref, yr_ref):
    B = x_ref.shape[0]
    sh1 = mod_ref[0:B, 0 * D_MODEL:1 * D_MODEL]
    sc1 = mod_ref[0:B, 1 * D_MODEL:2 * D_MODEL]
    hmix = (_rms_norm(x_ref[:, 0, :], g_pre_mix_ref[...]) * (1.0 + sc1) + sh1).astype(bf16)
    q = jnp.dot(hmix, w_in_ref[:, 0:C_K], preferred_element_type=f32) * Q_SCALE
    for h in range(N_HEADS):
        q_ref[:, h, :] = q[:, h * HEAD_DIM:(h + 1) * HEAD_DIM]
    kv_ref[...] = jnp.dot(hmix, w_in_ref[:, C_K:C_XR], preferred_element_type=f32)
    xr_ref[...] = jnp.dot(hmix, w_in_ref[:, C_XR:C_YR], preferred_element_type=f32)
    yr_ref[...] = jnp.dot(hmix, w_in_ref[:, C_YR:IN_COLS], preferred_element_type=f32)


def _sample_pre_call(x, mod_s, g_pre_mix, w_in):
    B = x.shape[0]
    return pl.pallas_call(
        _sample_pre_kernel,
        out_shape=(
            jax.ShapeDtypeStruct((B, N_HEADS, HEAD_DIM), f32),
            jax.ShapeDtypeStruct((B, 2 * KV_WIDTH), f32),
            jax.ShapeDtypeStruct((B, D_RNN), f32),
            jax.ShapeDtypeStruct((B, D_RNN), f32),
        ),
        compiler_params=pltpu.CompilerParams(vmem_limit_bytes=VMEM_LIMIT_BYTES),
        name="sample_pre",
    )(x, mod_s, g_pre_mix, w_in)


def _sample_attn_kernel(q_ref, kv_ref, ck_ref, cv_ref, sinks_ref, o_ref, kwin_ref, vwin_ref):
    R = N_KV_HEADS * WINDOW
    hrow = lax.broadcasted_iota(jnp.int32, (N_HEADS, R), 0)
    rcol = lax.broadcasted_iota(jnp.int32, (N_HEADS, R), 1)
    slope = jnp.exp2(-8.0 * (hrow + 1).astype(f32) / N_HEADS)
    own = (rcol % N_KV_HEADS) == (hrow // GQA_GROUP)
    bias = jnp.where(own, slope * (WINDOW - 1 - rcol // N_KV_HEADS).astype(f32), jnp.inf)
    wrow = lax.broadcasted_iota(jnp.int32, (R, HEAD_DIM), 0)
    sink = sinks_ref[...]

    def shifted(cache, new_rows):
        out = pltpu.roll(cache, R - N_KV_HEADS, axis=0)
        for c in range(N_KV_HEADS):
            out = jnp.where(wrow == R - N_KV_HEADS + c, new_rows[c], out)
        return out

    scores = []
    for b in range(SAMPLE_CHUNK):
        knew = [kv_ref[b:b + 1, c * HEAD_DIM:(c + 1) * HEAD_DIM] for c in range(N_KV_HEADS)]
        kw = shifted(ck_ref[b], knew)
        kwin_ref[b] = kw
        scores.append(_dot_nt(q_ref[b].astype(bf16), kw.astype(bf16)) - bias)
    probs, denoms = [], []
    for b in range(SAMPLE_CHUNK):
        s = scores[b]
        m = jnp.maximum(jnp.max(s, axis=-1, keepdims=True), sink)
        p = jnp.exp(s - m)
        denoms.append(jnp.sum(p, axis=-1, keepdims=True) + jnp.exp(sink - m))
        probs.append(p.astype(bf16))
    for b in range(SAMPLE_CHUNK):
        vnew = [kv_ref[b:b + 1, KV_WIDTH + c * HEAD_DIM:KV_WIDTH + (c + 1) * HEAD_DIM]
                for c in range(N_KV_HEADS)]
        vw = shifted(cv_ref[b], vnew)
        vwin_ref[b] = vw
        o_ref[b] = (jnp.dot(probs[b], vw.astype(bf16), preferred_element_type=f32)
                    / denoms[b])


def _sample_attn_call(q, kv, ck, cv, sinks_col):
    B = q.shape[0]
    BC = SAMPLE_CHUNK
    return pl.pallas_call(
        _sample_attn_kernel,
        grid=(B // BC,),
        in_specs=[
            pl.BlockSpec((BC, N_HEADS, HEAD_DIM), lambda i: (i, 0, 0)),
            pl.BlockSpec((BC, 2 * KV_WIDTH), lambda i: (i, 0)),
            pl.BlockSpec((BC, N_KV_HEADS * WINDOW, HEAD_DIM), lambda i: (i, 0, 0)),
            pl.BlockSpec((BC, N_KV_HEADS * WINDOW, HEAD_DIM), lambda i: (i, 0, 0)),
            pl.BlockSpec((N_HEADS, 1), lambda i: (0, 0)),
        ],
        out_specs=(
            pl.BlockSpec((BC, N_HEADS, HEAD_DIM), lambda i: (i, 0, 0)),
            pl.BlockSpec((BC, N_KV_HEADS * WINDOW, HEAD_DIM), lambda i: (i, 0, 0)),
            pl.BlockSpec((BC, N_KV_HEADS * WINDOW, HEAD_DIM), lambda i: (i, 0, 0)),
        ),
        out_shape=(
            jax.ShapeDtypeStruct((B, N_HEADS, HEAD_DIM), f32),
            jax.ShapeDtypeStruct((B, N_KV_HEADS * WINDOW, HEAD_DIM), f32),
            jax.ShapeDtypeStruct((B, N_KV_HEADS * WINDOW, HEAD_DIM), f32),
        ),
        compiler_params=pltpu.CompilerParams(
            dimension_semantics=("arbitrary",), vmem_limit_bytes=SAMPLE_ATTN_VMEM_BYTES),
        name="sample_attn",
    )(q, kv, ck, cv, sinks_col)


def _sample_post_kernel(
    x_ref, mod_ref, attn_ref, xr_ref, yr_ref, h0_ref, cbuf_ref, fbuf_ref,
    conv_w_ref, conv_b_ref, wa_ref, ba_ref, wi_ref, bi_ref, lam_ref,
    g_attn_ref, g_rnn_ref, w_out_ref, g_post_mix_ref, g_pre_ffn_ref, w_up_hbm,
    fconv_w_ref, fconv_b_ref, w_down_hbm, g_post_ffn_ref,
    y_ref, h_ref, cst_ref, fst_ref,
    w_up_ref, w_down_ref, sem_ffn,
):
    up_copy = pltpu.make_async_copy(w_up_hbm, w_up_ref, sem_ffn.at[0])
    down_copy = pltpu.make_async_copy(w_down_hbm, w_down_ref, sem_ffn.at[1])
    up_copy.start()
    down_copy.start()

    B = x_ref.shape[0]
    g1 = mod_ref[0:B, 2 * D_MODEL:3 * D_MODEL]
    sh2 = mod_ref[0:B, 3 * D_MODEL:4 * D_MODEL]
    sc2 = mod_ref[0:B, 4 * D_MODEL:5 * D_MODEL]
    g2 = mod_ref[0:B, 5 * D_MODEL:6 * D_MODEL]
    x = x_ref[:, 0, :]
    xr = xr_ref[...]

    xc = conv_b_ref[...] + conv_w_ref[RNN_CONV_W - 1:RNN_CONV_W, :] * xr
    for jj in range(RNN_CONV_W - 1):
        xc = xc + conv_w_ref[jj:jj + 1, :] * cbuf_ref[jj]
    for jj in range(RNN_CONV_W - 2):
        cst_ref[jj] = cbuf_ref[jj + 1]
    cst_ref[RNN_CONV_W - 2] = xr

    sp = _softplus_neg(lam_ref[...])
    a_parts, u_parts = _rglru_gates(xc, wa_ref, ba_ref[...], wi_ref, bi_ref[...], sp)
    a = jnp.concatenate(a_parts, axis=1)
    u = jnp.concatenate(u_parts, axis=1)
    h = a * h0_ref[...] + u
    h_ref[...] = h
    rnn = h * _gelu_tanh(yr_ref[...])

    attn = jnp.concatenate([attn_ref[:, h, :] for h in range(N_HEADS)], axis=1)
    attn_n = _rms_norm(attn, g_attn_ref[...])
    rnn_n = _rms_norm(rnn, g_rnn_ref[...])
    mo = (_dot(attn_n, w_out_ref[0:ATTN_WIDTH, 0:D_MODEL])
          + _dot(rnn_n, w_out_ref[ATTN_WIDTH:ATTN_WIDTH + D_RNN, 0:D_MODEL]))
    x1 = x + g1 * _rms_norm(mo, g_post_mix_ref[...])

    hff = (_rms_norm(x1, g_pre_ffn_ref[...]) * (1.0 + sc2) + sh2).astype(bf16)
    up_copy.wait()
    up_pre = jnp.dot(hff, w_up_ref[...], preferred_element_type=f32)
    up = fconv_b_ref[...] + fconv_w_ref[FFN_CONV_W - 1:FFN_CONV_W, :] * up_pre
    for jj in range(FFN_CONV_W - 1):
        up = up + fconv_w_ref[jj:jj + 1, :] * fbuf_ref[:, jj, :]
    for jj in range(FFN_CONV_W - 2):
        fst_ref[:, jj, :] = fbuf_ref[:, jj + 1, :]
    fst_ref[:, FFN_CONV_W - 2, :] = up_pre

    act = (_gelu_tanh(up[:, 0:D_FF]) * up[:, D_FF:2 * D_FF]).astype(bf16)
    down_copy.wait()
    f = jnp.dot(act, w_down_ref[:, 0:D_MODEL], preferred_element_type=f32)
    y_ref[:, 0, :] = x1 + g2 * _rms_norm(f, g_post_ffn_ref[...])


def _sample_post_call(x, mod_s, attn, xr, yr, h0, cbuf, fbuf, params):
    B = x.shape[0]
    (_, _, conv_w, conv_b, wa, ba, wi, bi, lam, g_attn, g_rnn, w_out,
     g_post_mix, g_pre_ffn, w_up, fconv_w, fconv_b, w_down, g_post_ffn) = params
    ins = (x, mod_s, attn, xr, yr, h0, cbuf, fbuf, conv_w, conv_b, wa, ba, wi, bi, lam,
           g_attn, g_rnn, w_out, g_post_mix, g_pre_ffn, w_up, fconv_w, fconv_b, w_down, g_post_ffn)
    return pl.pallas_call(
        _sample_post_kernel,
        in_specs=[pl.BlockSpec(memory_space=pl.ANY) if (a is w_up or a is w_down)
                  else pl.BlockSpec(memory_space=pltpu.VMEM) for a in ins],
        out_shape=(
            jax.ShapeDtypeStruct((B, 1, D_MODEL), f32),
            jax.ShapeDtypeStruct((B, D_RNN), f32),
            jax.ShapeDtypeStruct((RNN_CONV_W - 1, B, D_RNN), f32),
            jax.ShapeDtypeStruct((B, FFN_CONV_W - 1, 2 * D_FF), f32),
        ),
        scratch_shapes=[pltpu.VMEM(w_up.shape, w_up.dtype), pltpu.VMEM(w_down.shape, w_down.dtype),
                        pltpu.SemaphoreType.DMA((2,))],
        compiler_params=pltpu.CompilerParams(vmem_limit_bytes=VMEM_LIMIT_BYTES),
        name="sample_post",
    )(*ins)


def kernel(x_prompt, x_sample, cache_k, cache_v, state_h, state_conv, state_ffn_conv, c_prompt, c_sample, w_ada, b_ada, g_pre_mix, w_in, conv_w, conv_b, w_a, b_a, w_i, b_i, lam, sinks, g_attn_out, g_rnn_out, w_out, g_post_mix, g_pre_ffn, w_up, ffn_conv_w, ffn_conv_b, w_down, g_post_ffn):
    depth = w_in.shape[0]
    assert depth == 1 and x_prompt.shape[0] == 1 and x_sample.shape[1] == 1
    T = x_prompt.shape[1]
    B = x_sample.shape[0]
    W = cache_k.shape[2]
    assert W == WINDOW and T % TOKEN_BLOCK == 0 and B % SAMPLE_CHUNK == 0

    row = lambda a: a[0].reshape(1, -1)
    params = (
        row(g_pre_mix), w_in[0], conv_w[0], row(conv_b),
        w_a[0].astype(bf16), row(b_a), w_i[0].astype(bf16), row(b_i), row(lam),
        row(g_attn_out), row(g_rnn_out), w_out[0], row(g_post_mix),
        row(g_pre_ffn),
        w_up[0], ffn_conv_w[0], row(ffn_conv_b), w_down[0],
        row(g_post_ffn),
    )

    c_all = jnp.concatenate(
        [c_sample, jnp.broadcast_to(c_prompt, (MOD_PAD_ROWS, D_MODEL))], axis=0)
    mod = _mod_call(c_all, w_ada[0], b_ada[0].reshape(1, -1))
    mod_s = mod

    yp, kp, vp, hp, convp, ffnp, (w_in_bf, w_out_bf, w_up_bf, w_down_bf) = _prompt_call(
        x_prompt[0], mod, B, sinks[0], params)
    params = list(params)
    params[1], params[11], params[14], params[17] = w_in_bf, w_out_bf, w_up_bf, w_down_bf
    assert params[11].shape == w_out.shape[1:] and params[17].shape == w_down.shape[1:]

    xs = x_sample
    q, kv, xr, yr = _sample_pre_call(xs, mod_s, params[0], params[1])
    ck = cache_k.reshape(B, W * N_KV_HEADS, HEAD_DIM)
    cv = cache_v.reshape(B, W * N_KV_HEADS, HEAD_DIM)
    attn3, kwin, vwin = _sample_attn_call(
        q, kv, ck, cv, sinks[0].reshape(N_HEADS, 1))
    attn = attn3
    ys, hs, convs, ffns = _sample_post_call(
        xs, mod_s, attn, xr, yr, state_h[0],
        jnp.transpose(state_conv[0], (1, 0, 2)), state_ffn_conv[0],
        params)
    convs = jnp.transpose(convs, (1, 0, 2))

    kv_shape = (1, 1, W, N_KV_HEADS, HEAD_DIM)
    kvs_shape = (1, B, W, N_KV_HEADS, HEAD_DIM)
    return (
        yp[None], ys,
        kp.reshape(kv_shape), vp.reshape(kv_shape), hp[None], convp[None, None], ffnp[None, None],
        kwin.reshape(kvs_shape), vwin.reshape(kvs_shape), hs[None], convs[None], ffns[None],
    )
```

```python
import math

import jax
import jax.numpy as jnp
from jax import lax
from jax.experimental import pallas as pl
from jax.experimental.pallas import tpu as pltpu

D_MODEL = 1024
N_HEADS = 8
N_KV_HEADS = 2
HEAD_DIM = 128
GQA_GROUP = N_HEADS // N_KV_HEADS
ATTN_WIDTH = N_HEADS * HEAD_DIM
KV_WIDTH = N_KV_HEADS * HEAD_DIM
WINDOW = 128
D_RNN = D_MODEL
RNN_BLOCKS = 8
RNN_BLOCK_W = D_RNN // RNN_BLOCKS
RG_C = 8.0
RNN_CONV_W = 4
D_FF = 2816
FFN_CONV_W = 3
RMS_EPS = 1e-6

C_K = ATTN_WIDTH
C_V = C_K + KV_WIDTH
C_XR = C_V + KV_WIDTH
C_YR = C_XR + D_RNN
IN_COLS = C_YR + D_RNN

SUBLANES = 8
SUB_ROWS = WINDOW
GROUPS = SUB_ROWS // SUBLANES
TOKEN_BLOCK = 256
NSUB = TOKEN_BLOCK // SUB_ROWS
FFN_CHUNK = 256
STAGE_SKEW = 3
WSTAGE_ROWS = 64
WSTAGE_SLOTS = 4
SAMPLE_CHUNK = 16
SAMPLE_ATTN_VMEM_BYTES = (2 * 2 * 2 * SAMPLE_CHUNK * N_KV_HEADS * WINDOW * HEAD_DIM * 4
                          + 8 * 1024 * 1024)
MOD_PAD_ROWS = 8
VMEM_LIMIT_BYTES = 56 * 1024 * 1024

ALIBI_SLOPES = tuple(2.0 ** (-8.0 * (h + 1) / N_HEADS) for h in range(N_HEADS))
Q_SCALE = HEAD_DIM ** -0.5
SQRT_2_OVER_PI = math.sqrt(2.0 / math.pi)
LOG2_E = 1.0 / math.log(2.0)
PADDED_OUT_COLS = D_MODEL + 128

bf16 = jnp.bfloat16
f32 = jnp.float32


def _rms_scale(x):
    return lax.rsqrt(jnp.mean(x * x, axis=-1, keepdims=True) + RMS_EPS)


def _rms_norm(x, g):
    return x * _rms_scale(x) * g


def _gelu_tanh(x):
    k1 = -2.0 * LOG2_E * SQRT_2_OVER_PI
    k3 = k1 * 0.044715
    return x / (1.0 + jnp.exp2(x * (k1 + k3 * (x * x))))


def _sigmoid(x):
    return 1.0 / (1.0 + jnp.exp2(x * (-LOG2_E)))


def _softplus_neg(lam):
    return jnp.maximum(-lam, 0.0) + jnp.log1p(jnp.exp(-jnp.abs(lam)))


def _dot(a, b):
    return jnp.dot(a.astype(bf16), b, preferred_element_type=f32)


def _dot_nt(a, b):
    return lax.dot_general(a, b, (((1,), (1,)), ((), ())), preferred_element_type=f32)


def _rglru_gates(xc, wa_ref, ba, wi_ref, bi, sp):
    a_parts, u_parts = [], []
    for n in range(RNN_BLOCKS):
        sl = slice(n * RNN_BLOCK_W, (n + 1) * RNN_BLOCK_W)
        xn = xc[:, sl]
        xb = xn.astype(bf16)
        r = _sigmoid(jnp.dot(xb, wa_ref[n], preferred_element_type=f32) + ba[:, sl])
        i = _sigmoid(jnp.dot(xb, wi_ref[n], preferred_element_type=f32) + bi[:, sl])
        log_a = (-RG_C) * r * sp[:, sl]
        t = jnp.tanh(log_a)
        one_minus_a2 = (-2.0 * t) / (1.0 - t)
        a_parts.append(jnp.exp(log_a))
        u_parts.append(jnp.sqrt(one_minus_a2) * (i * xn))
    return a_parts, u_parts


def _mod_kernel(c_ref, w_hbm, b_ref, o_ref, wbuf, sem):
    nblk, _, bn = wbuf.shape
    copies = [pltpu.make_async_copy(w_hbm.at[:, pl.ds(j * bn, bn)], wbuf.at[j], sem.at[j])
              for j in range(nblk)]
    for cp in copies:
        cp.start()
    c = c_ref[...]
    s = (c * _sigmoid(c)).astype(bf16)
    for j in range(nblk):
        copies[j].wait()
        cols = slice(j * bn, (j + 1) * bn)
        o_ref[:, cols] = (jnp.dot(s, wbuf[j].astype(bf16), preferred_element_type=f32)
                          + b_ref[:, cols])


def _mod_call(c_all, w_ada, b_ada):
    rows = c_all.shape[0]
    k, ncol = w_ada.shape
    bn = D_MODEL
    vmem = pl.BlockSpec(memory_space=pltpu.VMEM)
    return pl.pallas_call(
        _mod_kernel,
        in_specs=[vmem, pl.BlockSpec(memory_space=pl.ANY), vmem],
        out_shape=jax.ShapeDtypeStruct((rows, ncol), f32),
        scratch_shapes=[pltpu.VMEM((ncol // bn, k, bn), f32),
                        pltpu.SemaphoreType.DMA((ncol // bn,))],
        compiler_params=pltpu.CompilerParams(
            vmem_limit_bytes=k * ncol * 4 + rows * ncol * 4 + 8 * 1024 * 1024),
        name="adaln_mod",
    )(c_all, w_ada, b_ada)


def _groups(v, c0=None, c1=None):
    if c0 is None:
        return [v[j * SUBLANES:(j + 1) * SUBLANES, :] for j in range(GROUPS)]
    return [v[j * SUBLANES:(j + 1) * SUBLANES, c0:c1] for j in range(GROUPS)]


def _shifted_groups(X, tail_row, sub0, max_shift):
    wrapped = {}
    for j in range(GROUPS - max_shift, GROUPS):
        wrapped[j] = jnp.where(sub0, tail_row(j), pltpu.roll(X[j], 1, axis=0))
    sh = {}
    for d in range(1, max_shift + 1):
        sh[d] = [X[j - d] if j >= d else wrapped[j - d + GROUPS] for j in range(GROUPS)]
    return sh


def _scan_sub(a, u, h_in, sub_iota):
    A = _groups(a)
    L = _groups(u)
    for j in range(1, GROUPS):
        L[j] = A[j] * L[j - 1] + L[j]
        A[j] = A[j] * A[j - 1]
    ae, le = A[GROUPS - 1], L[GROUPS - 1]
    for s_ in (1, 2, 4):
        ok = sub_iota >= s_
        a_sh = pltpu.roll(ae, s_, axis=0)
        l_sh = pltpu.roll(le, s_, axis=0)
        le = jnp.where(ok, ae * l_sh + le, le)
        ae = jnp.where(ok, ae * a_sh, ae)
    hend = le + ae * h_in
    hprev = jnp.where(sub_iota == 0, h_in, pltpu.roll(hend, 1, axis=0))
    h = jnp.concatenate([L[j] + A[j] * hprev for j in range(GROUPS)], axis=0)
    h_out = jnp.broadcast_to(hend[SUBLANES - 1:SUBLANES, :], hend.shape)
    return h, h_out


def _prompt_kernel(
    x_hbm, mod_ref, sinks_ref,
    g_pre_mix_ref, w_in_hbm, conv_w_ref, conv_b_ref, wa_ref, ba_ref, wi_ref, bi_ref, lam_ref,
    g_attn_ref, g_rnn_ref, w_out_hbm, g_post_mix_ref, g_pre_ffn_ref, w_up_hbm,
    fconv_w_ref, fconv_b_ref, w_down_hbm, g_post_ffn_ref,
    y_hbm, kwin_hbm, vwin_hbm, hlast_ref, convst_ref, ffnst_ref,
    w_in_bf_hbm, w_out_bf_hbm, w_up_bf_hbm, w_down_bf_hbm,
    xbuf, ybuf, sem_in, sem_out, sem_st, kprev, vprev, kst, vst, xr_tail, up_tail, hcar, bias_s,
    hff_s, w_in_ref, w_out_ref, w_up_ref, w_down_ref, wstage, sem_w, sem_wout,
):
    step = pl.program_id(0)
    nblocks = pl.num_programs(0) - 1
    slot = lax.rem(step, 2)
    yslot = lax.rem(step, 3)
    fslot = lax.rem(step + 2, 3)

    def x_copies(st, sl):
        return [pltpu.make_async_copy(
            x_hbm.at[st * NSUB + s, c, :, :],
            xbuf.at[sl, pl.ds(s * GROUPS, GROUPS), c, :],
            sem_in.at[sl]) for s in range(NSUB) for c in range(SUBLANES)]

    def y_copies(st, sl):
        return [pltpu.make_async_copy(
            ybuf.at[sl, pl.ds(s * GROUPS, GROUPS), c, :],
            y_hbm.at[st * NSUB + s, c, :, :],
            sem_out.at[sl]) for s in range(NSUB) for c in range(SUBLANES)]

    def state_copies():
        cps = []
        for src, dst in ((kst, kwin_hbm), (vst, vwin_hbm)):
            cps += [pltpu.make_async_copy(src.at[pl.ds(j * SUBLANES, SUBLANES), :],
                                          dst.at[:, j, :], sem_st) for j in range(GROUPS)]
        return cps

    def weight_out_copies():
        return [pltpu.make_async_copy(w_in_ref, w_in_bf_hbm, sem_wout),
                pltpu.make_async_copy(w_up_ref, w_up_bf_hbm, sem_wout),
                pltpu.make_async_copy(w_out_ref.at[:, pl.ds(0, D_MODEL)], w_out_bf_hbm, sem_wout),
                pltpu.make_async_copy(w_down_ref.at[:, pl.ds(0, D_MODEL)], w_down_bf_hbm, sem_wout)]

    def load_cast(w_hbm, w_s, stage, chunk_rows):
        tiled = len(stage.shape) == 4
        if tiled:
            n_rows, n_cols = w_hbm.shape[0] * SUBLANES, w_hbm.shape[2]
            assert stage.shape[1:] == (chunk_rows // SUBLANES, SUBLANES, n_cols)
        else:
            n_rows, n_cols = w_hbm.shape
        n = n_rows // chunk_rows
        depth = stage.shape[0]
        assert n >= depth

        def chunk(i, sl):
            if tiled:
                g = chunk_rows // SUBLANES
                return pltpu.make_async_copy(w_hbm.at[pl.ds(i * g, g), :, :], stage.at[sl],
                                             sem_w.at[sl])
            return pltpu.make_async_copy(
                w_hbm.at[pl.ds(i * chunk_rows, chunk_rows), :],
                stage.at[sl, pl.ds(0, chunk_rows), pl.ds(0, n_cols)], sem_w.at[sl])

        def staged(sl):
            if tiled:
                return stage[sl].reshape(chunk_rows, n_cols)
            return stage[sl, 0:chunk_rows, 0:n_cols]

        for i in range(depth):
            chunk(i, i).start()

        def body(i, carry):
            sl = lax.rem(i, depth)
            chunk(i, sl).wait()
            r0 = pl.multiple_of(i * chunk_rows, chunk_rows)
            w_s[pl.ds(r0, chunk_rows), 0:n_cols] = staged(sl).astype(bf16)

            @pl.when(i + depth < n)
            def _next():
                chunk(i + depth, sl).start()

            return carry

        lax.fori_loop(0, n, body, 0)

    @pl.when(step == 0)
    def _init():
        for cp in x_copies(0, 0):
            cp.start()
        load_cast(w_in_hbm, w_in_ref, wstage, WSTAGE_ROWS)
        load_cast(w_out_hbm, w_out_ref, ybuf, TOKEN_BLOCK)
        load_cast(w_up_hbm, w_up_ref, wstage, WSTAGE_ROWS)
        load_cast(w_down_hbm, w_down_ref, ybuf, TOKEN_BLOCK)
        for cp in weight_out_copies():
            cp.start()
        kprev[...] = jnp.zeros(kprev.shape, bf16)
        vprev[...] = jnp.zeros(vprev.shape, bf16)
        xr_tail[...] = jnp.zeros(xr_tail.shape, f32)
        up_tail[...] = jnp.zeros(up_tail.shape, f32)
        hcar[...] = jnp.zeros(hcar.shape, f32)
        hff_s[...] = jnp.zeros(hff_s.shape, bf16)
        ybuf[2] = jnp.zeros(ybuf.shape[1:], f32)
        rq = lax.broadcasted_iota(jnp.int32, (SUB_ROWS, 2 * SUB_ROWS), 0)
        ck = lax.broadcasted_iota(jnp.int32, (SUB_ROWS, 2 * SUB_ROWS), 1)
        rk = ck & (SUB_ROWS - 1)
        tq = (rq % SUBLANES) * GROUPS + rq // SUBLANES
        tk = (rk % SUBLANES) * GROUPS + rk // SUBLANES
        dist = tq - tk + jnp.where(ck < SUB_ROWS, WINDOW, 0)
        base = jnp.where((dist >= 0) & (dist < WINDOW), dist.astype(f32), jnp.inf)
        for h in range(N_HEADS):
            bias_s[h] = ALIBI_SLOPES[h] * base

    @pl.when(step + 1 < nblocks)
    def _prefetch():
        for cp in x_copies(step + 1, 1 - slot):
            cp.start()

    @pl.when(step >= 3)
    def _free_ybuf():
        for cp in y_copies(0, yslot):
            cp.wait()

    @pl.when(step < nblocks)
    def _wait_x():
        for cp in x_copies(step, slot):
            cp.wait()

    sh1 = mod_ref[0:1, 0 * D_MODEL:1 * D_MODEL]
    sc1 = mod_ref[0:1, 1 * D_MODEL:2 * D_MODEL]
    g1 = mod_ref[0:1, 2 * D_MODEL:3 * D_MODEL]
    sh2 = mod_ref[0:1, 3 * D_MODEL:4 * D_MODEL]
    sc2 = mod_ref[0:1, 4 * D_MODEL:5 * D_MODEL]
    g2 = mod_ref[0:1, 5 * D_MODEL:6 * D_MODEL]
    gs1 = g_pre_mix_ref[...] * (1.0 + sc1)
    gs2 = g_pre_ffn_ref[...] * (1.0 + sc2)
    sp = _softplus_neg(lam_ref[...])
    first_pen = jnp.where(step == 0, jnp.inf, 0.0)
    sub_iota = lax.broadcasted_iota(jnp.int32, (SUBLANES, D_RNN), 0)
    sub0_rnn = sub_iota == 0
    sub0_ffn = lax.broadcasted_iota(jnp.int32, (SUBLANES, FFN_CHUNK), 0) == 0
    conv_w = [conv_w_ref.at[jj:jj + 1, :] for jj in range(RNN_CONV_W)]

    sts = [dict() for _ in range(NSUB)]

    def rows(s):
        return pl.ds(s * SUB_ROWS, SUB_ROWS)

    def groups(s):
        return pl.ds(s * GROUPS, GROUPS)

    def stage_pre(s):
        x = xbuf[slot, groups(s)].reshape(SUB_ROWS, D_MODEL)
        sts[s]["hmix"] = (x * _rms_scale(x) * gs1 + sh1).astype(bf16)

    def stage_inproj(s):
        st = sts[s]
        hm = st["hmix"]
        st["q"] = (jnp.dot(hm, w_in_ref[:, 0:C_K], preferred_element_type=f32) * Q_SCALE).astype(bf16)
        kv = jnp.dot(hm, w_in_ref[:, C_K:C_XR], preferred_element_type=f32)
        st["kb"] = kv[:, 0:KV_WIDTH].astype(bf16)
        st["vb"] = kv[:, KV_WIDTH:2 * KV_WIDTH].astype(bf16)
        st["kv"] = kv
        st["xr"] = jnp.dot(hm, w_in_ref[:, C_XR:C_YR], preferred_element_type=f32)

    def stage_attn(s):
        st = sts[s]
        if s == 0:
            kp, vp = kprev[...], vprev[...]
        else:
            kp, vp = sts[s - 1]["kb"], sts[s - 1]["vb"]
        kw = jnp.concatenate([kp, st["kb"]], axis=0)
        vw = jnp.concatenate([vp, st["vb"]], axis=0)
        outs = []
        for h in range(N_HEADS):
            c = h // GQA_GROUP
            hs = slice(h * HEAD_DIM, (h + 1) * HEAD_DIM)
            cs = slice(c * HEAD_DIM, (c + 1) * HEAD_DIM)
            sc = _dot_nt(st["q"][:, hs], kw[:, cs])
            if s == 0:
                sc = jnp.concatenate([sc[:, 0:SUB_ROWS] - first_pen, sc[:, SUB_ROWS:]], axis=1)
            sc = sc - bias_s[h]
            sink = sinks_ref[h]
            m = jnp.maximum(jnp.max(sc, axis=-1, keepdims=True), sink)
            p = jnp.exp(sc - m)
            denom = jnp.sum(p, axis=-1, keepdims=True) + jnp.exp(sink - m)
            outs.append(jnp.dot(p.astype(bf16), vw[:, cs], preferred_element_type=f32) / denom)
        attn = jnp.concatenate(outs, axis=1)
        st["attn_n"] = _rms_norm(attn, g_attn_ref[...]).astype(bf16)

    def stage_rnn(s):
        st = sts[s]
        xr = st["xr"]
        X = _groups(xr)
        if s == 0:
            tail = lambda j: xr_tail[(j - (GROUPS - RNN_CONV_W + 1)) * SUBLANES + SUBLANES - 1:
                                     (j - (GROUPS - RNN_CONV_W + 1)) * SUBLANES + SUBLANES, :]
        else:
            pxr = sts[s - 1]["xr"]
            tail = lambda j: pxr[j * SUBLANES + SUBLANES - 1:(j + 1) * SUBLANES, :]
        sh = _shifted_groups(X, tail, sub0_rnn, RNN_CONV_W - 1)
        xcs = []
        for j in range(GROUPS):
            acc = conv_b_ref[...] + conv_w[RNN_CONV_W - 1][...] * X[j]
            for d in range(1, RNN_CONV_W):
                acc = acc + conv_w[RNN_CONV_W - 1 - d][...] * sh[d][j]
            xcs.append(acc)
        xc = jnp.concatenate(xcs, axis=0)
        a_parts, u_parts = _rglru_gates(xc, wa_ref, ba_ref[...], wi_ref, bi_ref[...], sp)
        a = jnp.concatenate(a_parts, axis=1)
        u = jnp.concatenate(u_parts, axis=1)
        h_in = hcar[...] if s == 0 else sts[s - 1]["h_out"]
        h, st["h_out"] = _scan_sub(a, u, h_in, sub_iota)
        yr = jnp.dot(st["hmix"], w_in_ref[:, C_YR:IN_COLS], preferred_element_type=f32)
        rnn = h * _gelu_tanh(yr)
        st["rnn_n"] = _rms_norm(rnn, g_rnn_ref[...]).astype(bf16)

    def stage_out(s):
        st = sts[s]
        mo = (jnp.dot(st["attn_n"], w_out_ref[0:ATTN_WIDTH, 0:D_MODEL], preferred_element_type=f32)
              + jnp.dot(st["rnn_n"], w_out_ref[ATTN_WIDTH:ATTN_WIDTH + D_RNN, 0:D_MODEL],
                        preferred_element_type=f32))
        x1 = (xbuf[slot, groups(s)].reshape(SUB_ROWS, D_MODEL)
              + g1 * _rms_norm(mo, g_post_mix_ref[...]))
        ybuf[yslot, groups(s)] = x1.reshape(GROUPS, SUBLANES, D_MODEL)
        hff_s[slot, rows(s), :] = (x1 * _rms_scale(x1) * gs2 + sh2).astype(bf16)

    tail_g0 = GROUPS - (FFN_CONV_W - 1)
    hff = hff_s[1 - slot]
    acts, tails = [], {}

    def conv_chunk(c0):
        c1 = c0 + FFN_CHUNK
        up = jnp.dot(hff, w_up_ref[:, c0:c1], preferred_element_type=f32)
        w = [fconv_w_ref.at[jj:jj + 1, c0:c1] for jj in range(FFN_CONV_W)]
        outs = []
        for s in range(NSUB):
            r0 = s * SUB_ROWS
            X = [up[r0 + j * SUBLANES:r0 + (j + 1) * SUBLANES, :] for j in range(GROUPS)]
            if s == 0:
                tail = lambda j: up_tail[(j - tail_g0) * SUBLANES + SUBLANES - 1:
                                         (j - tail_g0 + 1) * SUBLANES, c0:c1]
            else:
                tail = lambda j, p0=r0 - SUB_ROWS: up[p0 + j * SUBLANES + SUBLANES - 1:
                                                      p0 + (j + 1) * SUBLANES, :]
            sh = _shifted_groups(X, tail, sub0_ffn, FFN_CONV_W - 1)
            for j in range(GROUPS):
                acc = fconv_b_ref[:, c0:c1] + w[FFN_CONV_W - 1][...] * X[j]
                for d in range(1, FFN_CONV_W):
                    acc = acc + w[FFN_CONV_W - 1 - d][...] * sh[d][j]
                outs.append(acc)
        last0 = (NSUB - 1) * SUB_ROWS
        for t in range(FFN_CONV_W - 1):
            r = last0 + (tail_g0 + t) * SUBLANES + SUBLANES - 1
            ffnst_ref[t:t + 1, c0:c1] = up[r:r + 1, :]
        tails[c0] = up[last0 + tail_g0 * SUBLANES:, :]
        return jnp.concatenate(outs, axis=0)

    def ffn_chunk(cc):
        gate = conv_chunk(cc * FFN_CHUNK)
        val = conv_chunk(D_FF + cc * FFN_CHUNK)
        acts.append((_gelu_tanh(gate) * val).astype(bf16))

    def ffn_finish():
        for c0, tl in tails.items():
            up_tail[:, c0:c0 + FFN_CHUNK] = tl
        act = jnp.concatenate(acts, axis=1)
        f = jnp.dot(act, w_down_ref[:, 0:D_MODEL], preferred_element_type=f32)
        y = ybuf[fslot].reshape(TOKEN_BLOCK, D_MODEL) + g2 * _rms_norm(f, g_post_ffn_ref[...])
        ybuf[fslot] = y.reshape(NSUB * GROUPS, SUBLANES, D_MODEL)

    stages = (stage_pre, stage_inproj, stage_attn, stage_rnn, stage_out)
    n_times = len(stages) + (NSUB - 1) * STAGE_SKEW
    n_chunks = D_FF // FFN_CHUNK
    ffn_chunk(0)
    nxt = 1
    for t in range(n_times):
        for s in range(NSUB):
            k = t - s * STAGE_SKEW
            if 0 <= k < len(stages):
                stages[k](s)
                if nxt < n_chunks:
                    ffn_chunk(nxt)
                    nxt += 1
    while nxt < n_chunks:
        ffn_chunk(nxt)
        nxt += 1
    ffn_finish()

    lst = sts[NSUB - 1]
    kprev[...] = lst["kb"]
    vprev[...] = lst["vb"]
    xr_tail[...] = lst["xr"][(GROUPS - (RNN_CONV_W - 1)) * SUBLANES:, :]
    hcar[...] = lst["h_out"]

    @pl.when(step < nblocks)
    def _state():
        kst[...] = lst["kv"][:, 0:KV_WIDTH]
        vst[...] = lst["kv"][:, KV_WIDTH:2 * KV_WIDTH]
        for t in range(RNN_CONV_W - 1):
            r = (GROUPS - (RNN_CONV_W - 1) + t) * SUBLANES + SUBLANES - 1
            convst_ref[t:t + 1, :] = lst["xr"][r:r + 1, :]
        hlast_ref[...] = lst["h_out"][0:1, :]

    @pl.when(step >= 1)
    def _store_y():
        for cp in y_copies(step - 1, fslot):
            cp.start()

    @pl.when(step == nblocks)
    def _finish():
        for cp in state_copies():
            cp.start()
        for cp in state_copies() + weight_out_copies():
            cp.wait()
        for cp in y_copies(0, lax.rem(step + 1, 3)) + y_copies(0, fslot):
            cp.wait()


def _const_spec(shape):
    nd = len(shape)
    return pl.BlockSpec(shape, lambda i: (0,) * nd)


def _prompt_call(x, mod, mod_row0, sinks, params):
    T = x.shape[0]
    TB = TOKEN_BLOCK
    assert T // TB >= 3
    (g_pre_mix, w_in, conv_w, conv_b, wa, ba, wi, bi, lam, g_attn, g_rnn, w_out,
     g_post_mix, g_pre_ffn, w_up, fconv_w, fconv_b, w_down, g_post_ffn) = params
    x4 = x.reshape(T // SUB_ROWS, SUBLANES, GROUPS, D_MODEL)
    w_out3 = w_out.reshape(-1, SUBLANES, D_MODEL)
    w_down3 = w_down.reshape(-1, SUBLANES, D_MODEL)
    ins = [x4, mod, sinks, g_pre_mix, w_in, conv_w, conv_b, wa, ba, wi, bi, lam, g_attn, g_rnn,
           w_out3, g_post_mix, g_pre_ffn, w_up, fconv_w, fconv_b, w_down3, g_post_ffn]
    assert mod_row0 % MOD_PAD_ROWS == 0
    in_specs = [pl.BlockSpec(memory_space=pl.ANY),
                pl.BlockSpec((MOD_PAD_ROWS, mod.shape[1]), lambda i: (mod_row0 // MOD_PAD_ROWS, 0)),
                pl.BlockSpec(memory_space=pltpu.SMEM)]
    big = (w_in, w_out, w_up, w_down)
    assert all(w.dtype == f32 and w.shape[0] % TOKEN_BLOCK == 0 for w in big)
    in_specs += [pl.BlockSpec(memory_space=pl.ANY)
                 if any(a is w for w in (w_in, w_out3, w_up, w_down3))
                 else _const_spec(a.shape) for a in ins[3:]]
    out_shape = (
        jax.ShapeDtypeStruct(x4.shape, f32),
        jax.ShapeDtypeStruct((SUBLANES, GROUPS, KV_WIDTH), f32),
        jax.ShapeDtypeStruct((SUBLANES, GROUPS, KV_WIDTH), f32),
        jax.ShapeDtypeStruct((1, D_RNN), f32),
        jax.ShapeDtypeStruct((RNN_CONV_W - 1, D_RNN), f32),
        jax.ShapeDtypeStruct((FFN_CONV_W - 1, 2 * D_FF), f32),
    ) + tuple(jax.ShapeDtypeStruct(w.shape, bf16) for w in big)
    out_specs = (
        pl.BlockSpec(memory_space=pl.ANY),
        pl.BlockSpec(memory_space=pl.ANY),
        pl.BlockSpec(memory_space=pl.ANY),
        _const_spec((1, D_RNN)),
        _const_spec((RNN_CONV_W - 1, D_RNN)),
        _const_spec((FFN_CONV_W - 1, 2 * D_FF)),
    ) + (pl.BlockSpec(memory_space=pl.ANY),) * len(big)
    scratch = [
        pltpu.VMEM((2, TB // SUBLANES, SUBLANES, D_MODEL), f32),
        pltpu.VMEM((3, TB // SUBLANES, SUBLANES, D_MODEL), f32),
        pltpu.SemaphoreType.DMA((2,)),
        pltpu.SemaphoreType.DMA((3,)),
        pltpu.SemaphoreType.DMA(()),
        pltpu.VMEM((SUB_ROWS, KV_WIDTH), bf16),
        pltpu.VMEM((SUB_ROWS, KV_WIDTH), bf16),
        pltpu.VMEM((SUB_ROWS, KV_WIDTH), f32),
        pltpu.VMEM((SUB_ROWS, KV_WIDTH), f32),
        pltpu.VMEM(((RNN_CONV_W - 1) * SUBLANES, D_RNN), f32),
        pltpu.VMEM(((FFN_CONV_W - 1) * SUBLANES, 2 * D_FF), f32),
        pltpu.VMEM((SUBLANES, D_RNN), f32),
        pltpu.VMEM((N_HEADS, SUB_ROWS, 2 * SUB_ROWS), f32),
        pltpu.VMEM((2, TOKEN_BLOCK, D_MODEL), bf16),
        pltpu.VMEM(w_in.shape, bf16),
        pltpu.VMEM((w_out.shape[0], PADDED_OUT_COLS), bf16),
        pltpu.VMEM(w_up.shape, bf16),
        pltpu.VMEM((w_down.shape[0], PADDED_OUT_COLS), bf16),
        pltpu.VMEM((WSTAGE_SLOTS, WSTAGE_ROWS, w_up.shape[1]), f32),
        pltpu.SemaphoreType.DMA((WSTAGE_SLOTS,)),
        pltpu.SemaphoreType.DMA(()),
    ]
    assert w_in.shape[1] <= w_up.shape[1] and w_in.shape[0] % WSTAGE_ROWS == 0
    y4, kp, vp, hp, convp, ffnp, w_in_bf, w_out_bf, w_up_bf, w_down_bf = pl.pallas_call(
        _prompt_kernel,
        grid=(T // TB + 1,),
        in_specs=in_specs,
        out_specs=out_specs,
        out_shape=out_shape,
        scratch_shapes=scratch,
        compiler_params=pltpu.CompilerParams(
            dimension_semantics=("arbitrary",), vmem_limit_bytes=VMEM_LIMIT_BYTES),
        name="prompt_layer",
    )(*ins)
    return (y4.reshape(T, D_MODEL), kp.reshape(WINDOW, KV_WIDTH), vp.reshape(WINDOW, KV_WIDTH),
            hp, convp, ffnp, (w_in_bf, w_out_bf, w_up_bf, w_down_bf))


def _sample_pre_kernel(x_ref, mod_ref, g_pre_mix_ref, w_in_hbm, q_ref, kv_ref, xr_ref, yr_ref,
                       w_in_ref, sem):
    halves = [pltpu.make_async_copy(w_in_hbm.at[:, pl.ds(c0, c1 - c0)],
                                    w_in_ref.at[:, pl.ds(c0, c1 - c0)], sem.at[i])
              for i, (c0, c1) in enumerate(((0, C_XR), (C_XR, IN_COLS)))]
    for cp in halves:
        cp.start()
    B = x_ref.shape[0]
    sh1 = mod_ref[0:B, 0 * D_MODEL:1 * D_MODEL]
    sc1 = mod_ref[0:B, 1 * D_MODEL:2 * D_MODEL]
    hmix = (_rms_norm(x_ref[:, 0, :], g_pre_mix_ref[...]) * (1.0 + sc1) + sh1).astype(bf16)
    halves[0].wait()
    q = jnp.dot(hmix, w_in_ref[:, 0:C_K], preferred_element_type=f32) * Q_SCALE
    for h in range(N_HEADS):
        q_ref[:, h, :] = q[:, h * HEAD_DIM:(h + 1) * HEAD_DIM]
    kv_ref[...] = jnp.dot(hmix, w_in_ref[:, C_K:C_XR], preferred_element_type=f32)
    halves[1].wait()
    xr_ref[...] = jnp.dot(hmix, w_in_ref[:, C_XR:C_YR], preferred_element_type=f32)
    yr_ref[...] = jnp.dot(hmix, w_in_ref[:, C_YR:IN_COLS], preferred_element_type=f32)


def _sample_pre_call(x, mod_s, g_pre_mix, w_in):
    B = x.shape[0]
    vmem = pl.BlockSpec(memory_space=pltpu.VMEM)
    return pl.pallas_call(
        _sample_pre_kernel,
        in_specs=[vmem, vmem, vmem, pl.BlockSpec(memory_space=pl.ANY)],
        out_shape=(
            jax.ShapeDtypeStruct((B, N_HEADS, HEAD_DIM), f32),
            jax.ShapeDtypeStruct((B, 2 * KV_WIDTH), f32),
            jax.ShapeDtypeStruct((B, D_RNN), f32),
            jax.ShapeDtypeStruct((B, D_RNN), f32),
        ),
        scratch_shapes=[pltpu.VMEM(w_in.shape, w_in.dtype), pltpu.SemaphoreType.DMA((2,))],
        compiler_params=pltpu.CompilerParams(vmem_limit_bytes=VMEM_LIMIT_BYTES),
        name="sample_pre",
    )(x, mod_s, g_pre_mix, w_in)


def _sample_attn_kernel(q_ref, kv_ref, ck_ref, cv_ref, sinks_ref, o_ref, kwin_ref, vwin_ref):
    R = N_KV_HEADS * WINDOW
    hrow = lax.broadcasted_iota(jnp.int32, (N_HEADS, R), 0)
    rcol = lax.broadcasted_iota(jnp.int32, (N_HEADS, R), 1)
    slope = jnp.exp2(-8.0 * (hrow + 1).astype(f32) / N_HEADS)
    own = (rcol % N_KV_HEADS) == (hrow // GQA_GROUP)
    bias = jnp.where(own, slope * (WINDOW - 1 - rcol // N_KV_HEADS).astype(f32), jnp.inf)
    wrow = lax.broadcasted_iota(jnp.int32, (R, HEAD_DIM), 0)
    sink = sinks_ref[...]

    def shifted(cache, new_rows):
        out = pltpu.roll(cache, R - N_KV_HEADS, axis=0)
        for c in range(N_KV_HEADS):
            out = jnp.where(wrow == R - N_KV_HEADS + c, new_rows[c], out)
        return out

    scores = []
    for b in range(SAMPLE_CHUNK):
        knew = [kv_ref[b:b + 1, c * HEAD_DIM:(c + 1) * HEAD_DIM] for c in range(N_KV_HEADS)]
        kw = shifted(ck_ref[b], knew)
        kwin_ref[b] = kw
        scores.append(_dot_nt(q_ref[b].astype(bf16), kw.astype(bf16)) - bias)
    probs, denoms = [], []
    for b in range(SAMPLE_CHUNK):
        s = scores[b]
        m = jnp.maximum(jnp.max(s, axis=-1, keepdims=True), sink)
        p = jnp.exp(s - m)
        denoms.append(jnp.sum(p, axis=-1, keepdims=True) + jnp.exp(sink - m))
        probs.append(p.astype(bf16))
    for b in range(SAMPLE_CHUNK):
        vnew = [kv_ref[b:b + 1, KV_WIDTH + c * HEAD_DIM:KV_WIDTH + (c + 1) * HEAD_DIM]
                for c in range(N_KV_HEADS)]
        vw = shifted(cv_ref[b], vnew)
        vwin_ref[b] = vw
        o_ref[b] = (jnp.dot(probs[b], vw.astype(bf16), preferred_element_type=f32)
                    / denoms[b])


def _sample_attn_call(q, kv, ck, cv, sinks_col):
    B = q.shape[0]
    BC = SAMPLE_CHUNK
    return pl.pallas_call(
        _sample_attn_kernel,
        grid=(B // BC,),
        in_specs=[
            pl.BlockSpec((BC, N_HEADS, HEAD_DIM), lambda i: (i, 0, 0)),
            pl.BlockSpec((BC, 2 * KV_WIDTH), lambda i: (i, 0)),
            pl.BlockSpec((BC, N_KV_HEADS * WINDOW, HEAD_DIM), lambda i: (i, 0, 0)),
            pl.BlockSpec((BC, N_KV_HEADS * WINDOW, HEAD_DIM), lambda i: (i, 0, 0)),
            pl.BlockSpec((N_HEADS, 1), lambda i: (0, 0)),
        ],
        out_specs=(
            pl.BlockSpec((BC, N_HEADS, HEAD_DIM), lambda i: (i, 0, 0)),
            pl.BlockSpec((BC, N_KV_HEADS * WINDOW, HEAD_DIM), lambda i: (i, 0, 0)),
            pl.BlockSpec((BC, N_KV_HEADS * WINDOW, HEAD_DIM), lambda i: (i, 0, 0)),
        ),
        out_shape=(
            jax.ShapeDtypeStruct((B, N_HEADS, HEAD_DIM), f32),
            jax.ShapeDtypeStruct((B, N_KV_HEADS * WINDOW, HEAD_DIM), f32),
            jax.ShapeDtypeStruct((B, N_KV_HEADS * WINDOW, HEAD_DIM), f32),
        ),
        compiler_params=pltpu.CompilerParams(
            dimension_semantics=("arbitrary",), vmem_limit_bytes=SAMPLE_ATTN_VMEM_BYTES),
        name="sample_attn",
    )(q, kv, ck, cv, sinks_col)


def _sample_post_kernel(
    x_ref, mod_ref, attn_ref, xr_ref, yr_ref, h0_ref, cbuf_ref, fbuf_hbm,
    conv_w_ref, conv_b_ref, wa_ref, ba_ref, wi_ref, bi_ref, lam_ref,
    g_attn_ref, g_rnn_ref, w_out_ref, g_post_mix_ref, g_pre_ffn_ref, w_up_hbm,
    fconv_w_ref, fconv_b_ref, w_down_hbm, g_post_ffn_ref,
    y_ref, h_ref, cst_ref, fst_ref,
    w_up_ref, w_down_ref, fbuf_ref, sem_ffn,
):
    up_copy = pltpu.make_async_copy(w_up_hbm, w_up_ref, sem_ffn.at[0])
    down_copy = pltpu.make_async_copy(w_down_hbm, w_down_ref, sem_ffn.at[1])
    fbuf_copy = pltpu.make_async_copy(fbuf_hbm, fbuf_ref, sem_ffn.at[2])
    up_copy.start()
    fbuf_copy.start()
    down_copy.start()

    B = x_ref.shape[0]
    g1 = mod_ref[0:B, 2 * D_MODEL:3 * D_MODEL]
    sh2 = mod_ref[0:B, 3 * D_MODEL:4 * D_MODEL]
    sc2 = mod_ref[0:B, 4 * D_MODEL:5 * D_MODEL]
    g2 = mod_ref[0:B, 5 * D_MODEL:6 * D_MODEL]
    x = x_ref[:, 0, :]
    xr = xr_ref[...]

    xc = conv_b_ref[...] + conv_w_ref[RNN_CONV_W - 1:RNN_CONV_W, :] * xr
    for jj in range(RNN_CONV_W - 1):
        xc = xc + conv_w_ref[jj:jj + 1, :] * cbuf_ref[jj]
    for jj in range(RNN_CONV_W - 2):
        cst_ref[jj] = cbuf_ref[jj + 1]
    cst_ref[RNN_CONV_W - 2] = xr

    sp = _softplus_neg(lam_ref[...])
    a_parts, u_parts = _rglru_gates(xc, wa_ref, ba_ref[...], wi_ref, bi_ref[...], sp)
    a = jnp.concatenate(a_parts, axis=1)
    u = jnp.concatenate(u_parts, axis=1)
    h = a * h0_ref[...] + u
    h_ref[...] = h
    rnn = h * _gelu_tanh(yr_ref[...])

    attn = jnp.concatenate([attn_ref[:, h, :] for h in range(N_HEADS)], axis=1)
    attn_n = _rms_norm(attn, g_attn_ref[...])
    rnn_n = _rms_norm(rnn, g_rnn_ref[...])
    mo = (_dot(attn_n, w_out_ref[0:ATTN_WIDTH, 0:D_MODEL])
          + _dot(rnn_n, w_out_ref[ATTN_WIDTH:ATTN_WIDTH + D_RNN, 0:D_MODEL]))
    x1 = x + g1 * _rms_norm(mo, g_post_mix_ref[...])

    hff = (_rms_norm(x1, g_pre_ffn_ref[...]) * (1.0 + sc2) + sh2).astype(bf16)
    up_copy.wait()
    up_pre = jnp.dot(hff, w_up_ref[...], preferred_element_type=f32)
    fbuf_copy.wait()
    up = fconv_b_ref[...] + fconv_w_ref[FFN_CONV_W - 1:FFN_CONV_W, :] * up_pre
    for jj in range(FFN_CONV_W - 1):
        up = up + fconv_w_ref[jj:jj + 1, :] * fbuf_ref[:, jj, :]
    for jj in range(FFN_CONV_W - 2):
        fst_ref[:, jj, :] = fbuf_ref[:, jj + 1, :]
    fst_ref[:, FFN_CONV_W - 2, :] = up_pre

    act = (_gelu_tanh(up[:, 0:D_FF]) * up[:, D_FF:2 * D_FF]).astype(bf16)
    down_copy.wait()
    f = jnp.dot(act, w_down_ref[:, 0:D_MODEL], preferred_element_type=f32)
    y_ref[:, 0, :] = x1 + g2 * _rms_norm(f, g_post_ffn_ref[...])


def _sample_post_call(x, mod_s, attn, xr, yr, h0, cbuf, fbuf, params):
    B = x.shape[0]
    (_, _, conv_w, conv_b, wa, ba, wi, bi, lam, g_attn, g_rnn, w_out,
     g_post_mix, g_pre_ffn, w_up, fconv_w, fconv_b, w_down, g_post_ffn) = params
    ins = (x, mod_s, attn, xr, yr, h0, cbuf, fbuf, conv_w, conv_b, wa, ba, wi, bi, lam,
           g_attn, g_rnn, w_out, g_post_mix, g_pre_ffn, w_up, fconv_w, fconv_b, w_down, g_post_ffn)
    return pl.pallas_call(
        _sample_post_kernel,
        in_specs=[pl.BlockSpec(memory_space=pl.ANY) if (a is w_up or a is w_down or a is fbuf)
                  else pl.BlockSpec(memory_space=pltpu.VMEM) for a in ins],
        out_shape=(
            jax.ShapeDtypeStruct((B, 1, D_MODEL), f32),
            jax.ShapeDtypeStruct((B, D_RNN), f32),
            jax.ShapeDtypeStruct((RNN_CONV_W - 1, B, D_RNN), f32),
            jax.ShapeDtypeStruct((B, FFN_CONV_W - 1, 2 * D_FF), f32),
        ),
        scratch_shapes=[pltpu.VMEM(w_up.shape, w_up.dtype), pltpu.VMEM(w_down.shape, w_down.dtype),
                        pltpu.VMEM(fbuf.shape, fbuf.dtype), pltpu.SemaphoreType.DMA((3,))],
        compiler_params=pltpu.CompilerParams(vmem_limit_bytes=VMEM_LIMIT_BYTES),
        name="sample_post",
    )(*ins)


def kernel(x_prompt, x_sample, cache_k, cache_v, state_h, state_conv, state_ffn_conv, c_prompt, c_sample, w_ada, b_ada, g_pre_mix, w_in, conv_w, conv_b, w_a, b_a, w_i, b_i, lam, sinks, g_attn_out, g_rnn_out, w_out, g_post_mix, g_pre_ffn, w_up, ffn_conv_w, ffn_conv_b, w_down, g_post_ffn):
    depth = w_in.shape[0]
    assert depth == 1 and x_prompt.shape[0] == 1 and x_sample.shape[1] == 1
    T = x_prompt.shape[1]
    B = x_sample.shape[0]
    W = cache_k.shape[2]
    assert W == WINDOW and T % TOKEN_BLOCK == 0 and B % SAMPLE_CHUNK == 0

    row = lambda a: a[0].reshape(1, -1)
    params = (
        row(g_pre_mix), w_in[0], conv_w[0], row(conv_b),
        w_a[0].astype(bf16), row(b_a), w_i[0].astype(bf16), row(b_i), row(lam),
        row(g_attn_out), row(g_rnn_out), w_out[0], row(g_post_mix),
        row(g_pre_ffn),
        w_up[0], ffn_conv_w[0], row(ffn_conv_b), w_down[0],
        row(g_post_ffn),
    )

    c_all = jnp.concatenate(
        [c_sample, jnp.broadcast_to(c_prompt, (MOD_PAD_ROWS, D_MODEL))], axis=0)
    mod = _mod_call(c_all, w_ada[0], b_ada[0].reshape(1, -1))
    mod_s = mod

    yp, kp, vp, hp, convp, ffnp, (w_in_bf, w_out_bf, w_up_bf, w_down_bf) = _prompt_call(
        x_prompt[0], mod, B, sinks[0], params)
    params = list(params)
    params[1], params[11], params[14], params[17] = w_in_bf, w_out_bf, w_up_bf, w_down_bf
    assert params[11].shape == w_out.shape[1:] and params[17].shape == w_down.shape[1:]

    xs = x_sample
    q, kv, xr, yr = _sample_pre_call(xs, mod_s, params[0], params[1])
    ck = cache_k.reshape(B, W * N_KV_HEADS, HEAD_DIM)
    cv = cache_v.reshape(B, W * N_KV_HEADS, HEAD_DIM)
    attn3, kwin, vwin = _sample_attn_call(
        q, kv, ck, cv, sinks[0].reshape(N_HEADS, 1))
    attn = attn3
    ys, hs, convs, ffns = _sample_post_call(
        xs, mod_s, attn, xr, yr, state_h[0],
        jnp.transpose(state_conv[0], (1, 0, 2)), state_ffn_conv[0],
        params)
    convs = jnp.transpose(convs, (1, 0, 2))

    kv_shape = (1, 1, W, N_KV_HEADS, HEAD_DIM)
    kvs_shape = (1, B, W, N_KV_HEADS, HEAD_DIM)
    return (
        yp[None], ys,
        kp.reshape(kv_shape), vp.reshape(kv_shape), hp[None], convp[None, None], ffnp[None, None],
        kwin.reshape(kvs_shape), vwin.reshape(kvs_shape), hs[None], convs[None], ffns[None],
    )
```

```python
import math

import jax
import jax.numpy as jnp
from jax import lax
from jax.experimental import pallas as pl
from jax.experimental.pallas import tpu as pltpu

D_MODEL = 1024
N_HEADS = 8
N_KV_HEADS = 2
HEAD_DIM = 128
GQA_GROUP = N_HEADS // N_KV_HEADS
ATTN_WIDTH = N_HEADS * HEAD_DIM
KV_WIDTH = N_KV_HEADS * HEAD_DIM
WINDOW = 128
D_RNN = D_MODEL
RNN_BLOCKS = 8
RNN_BLOCK_W = D_RNN // RNN_BLOCKS
RG_C = 8.0
RNN_CONV_W = 4
D_FF = 2816
FFN_CONV_W = 3
RMS_EPS = 1e-6

C_K = ATTN_WIDTH
C_V = C_K + KV_WIDTH
C_XR = C_V + KV_WIDTH
C_YR = C_XR + D_RNN
IN_COLS = C_YR + D_RNN

SUBLANES = 8
SUB_ROWS = WINDOW
GROUPS = SUB_ROWS // SUBLANES
TOKEN_BLOCK = 256
NSUB = TOKEN_BLOCK // SUB_ROWS
FFN_CHUNK = 256
STAGE_SKEW = 3
WSTAGE_ROWS = 64
WSTAGE_SLOTS = 4
SAMPLE_CHUNK = 16
SAMPLE_ATTN_VMEM_BYTES = (2 * 2 * 2 * SAMPLE_CHUNK * N_KV_HEADS * WINDOW * HEAD_DIM * 4
                          + 8 * 1024 * 1024)
MOD_PAD_ROWS = 8
VMEM_LIMIT_BYTES = 56 * 1024 * 1024

ALIBI_SLOPES = tuple(2.0 ** (-8.0 * (h + 1) / N_HEADS) for h in range(N_HEADS))
Q_SCALE = HEAD_DIM ** -0.5
SQRT_2_OVER_PI = math.sqrt(2.0 / math.pi)
LOG2_E = 1.0 / math.log(2.0)
PADDED_OUT_COLS = D_MODEL + 128

bf16 = jnp.bfloat16
f32 = jnp.float32


def _rms_scale(x):
    return lax.rsqrt(jnp.mean(x * x, axis=-1, keepdims=True) + RMS_EPS)


def _rms_norm(x, g):
    return x * _rms_scale(x) * g


def _gelu_tanh(x):
    k1 = -2.0 * LOG2_E * SQRT_2_OVER_PI
    k3 = k1 * 0.044715
    return x / (1.0 + jnp.exp2(x * (k1 + k3 * (x * x))))


def _sigmoid(x):
    return 1.0 / (1.0 + jnp.exp2(x * (-LOG2_E)))


def _softplus_neg(lam):
    return jnp.maximum(-lam, 0.0) + jnp.log1p(jnp.exp(-jnp.abs(lam)))


def _dot(a, b):
    return jnp.dot(a.astype(bf16), b, preferred_element_type=f32)


def _dot_nt(a, b):
    return lax.dot_general(a, b, (((1,), (1,)), ((), ())), preferred_element_type=f32)


def _rglru_gates(xc, wa_ref, ba, wi_ref, bi, sp):
    a_parts, u_parts = [], []
    for n in range(RNN_BLOCKS):
        sl = slice(n * RNN_BLOCK_W, (n + 1) * RNN_BLOCK_W)
        xn = xc[:, sl]
        xb = xn.astype(bf16)
        r = _sigmoid(jnp.dot(xb, wa_ref[n], preferred_element_type=f32) + ba[:, sl])
        i = _sigmoid(jnp.dot(xb, wi_ref[n], preferred_element_type=f32) + bi[:, sl])
        log_a = (-RG_C) * r * sp[:, sl]
        t = jnp.tanh(log_a)
        one_minus_a2 = (-2.0 * t) / (1.0 - t)
        a_parts.append(jnp.exp(log_a))
        u_parts.append(jnp.sqrt(one_minus_a2) * (i * xn))
    return a_parts, u_parts


def _mod_kernel(c_ref, w_hbm, b_ref, o_ref, wbuf, sem):
    nblk, _, bn = wbuf.shape
    copies = [pltpu.make_async_copy(w_hbm.at[:, pl.ds(j * bn, bn)], wbuf.at[j], sem.at[j])
              for j in range(nblk)]
    for cp in copies:
        cp.start()
    c = c_ref[...]
    s = (c * _sigmoid(c)).astype(bf16)
    for j in range(nblk):
        copies[j].wait()
        cols = slice(j * bn, (j + 1) * bn)
        o_ref[:, cols] = (jnp.dot(s, wbuf[j].astype(bf16), preferred_element_type=f32)
                          + b_ref[:, cols])


def _mod_call(c_all, w_ada, b_ada):
    rows = c_all.shape[0]
    k, ncol = w_ada.shape
    bn = D_MODEL
    vmem = pl.BlockSpec(memory_space=pltpu.VMEM)
    return pl.pallas_call(
        _mod_kernel,
        in_specs=[vmem, pl.BlockSpec(memory_space=pl.ANY), vmem],
        out_shape=jax.ShapeDtypeStruct((rows, ncol), f32),
        scratch_shapes=[pltpu.VMEM((ncol // bn, k, bn), f32),
                        pltpu.SemaphoreType.DMA((ncol // bn,))],
        compiler_params=pltpu.CompilerParams(
            vmem_limit_bytes=k * ncol * 4 + rows * ncol * 4 + 8 * 1024 * 1024),
        name="adaln_mod",
    )(c_all, w_ada, b_ada)


def _groups(v, c0=None, c1=None):
    if c0 is None:
        return [v[j * SUBLANES:(j + 1) * SUBLANES, :] for j in range(GROUPS)]
    return [v[j * SUBLANES:(j + 1) * SUBLANES, c0:c1] for j in range(GROUPS)]


def _shifted_groups(X, tail_row, sub0, max_shift):
    wrapped = {}
    for j in range(GROUPS - max_shift, GROUPS):
        wrapped[j] = jnp.where(sub0, tail_row(j), pltpu.roll(X[j], 1, axis=0))
    sh = {}
    for d in range(1, max_shift + 1):
        sh[d] = [X[j - d] if j >= d else wrapped[j - d + GROUPS] for j in range(GROUPS)]
    return sh


def _scan_sub(a, u, h_in, sub_iota):
    A = _groups(a)
    L = _groups(u)
    for j in range(1, GROUPS):
        L[j] = A[j] * L[j - 1] + L[j]
        A[j] = A[j] * A[j - 1]
    ae, le = A[GROUPS - 1], L[GROUPS - 1]
    for s_ in (1, 2, 4):
        ok = sub_iota >= s_
        a_sh = pltpu.roll(ae, s_, axis=0)
        l_sh = pltpu.roll(le, s_, axis=0)
        le = jnp.where(ok, ae * l_sh + le, le)
        ae = jnp.where(ok, ae * a_sh, ae)
    hend = le + ae * h_in
    hprev = jnp.where(sub_iota == 0, h_in, pltpu.roll(hend, 1, axis=0))
    h = jnp.concatenate([L[j] + A[j] * hprev for j in range(GROUPS)], axis=0)
    h_out = jnp.broadcast_to(hend[SUBLANES - 1:SUBLANES, :], hend.shape)
    return h, h_out


def _prompt_kernel(
    x_hbm, mod_ref, sinks_ref,
    g_pre_mix_ref, w_in_hbm, conv_w_ref, conv_b_ref, wa_ref, ba_ref, wi_ref, bi_ref, lam_ref,
    g_attn_ref, g_rnn_ref, w_out_hbm, g_post_mix_ref, g_pre_ffn_ref, w_up_hbm,
    fconv_w_ref, fconv_b_ref, w_down_hbm, g_post_ffn_ref,
    y_hbm, kwin_hbm, vwin_hbm, hlast_ref, convst_ref, ffnst_ref,
    w_in_bf_hbm, w_out_bf_hbm, w_up_bf_hbm, w_down_bf_hbm,
    xbuf, ybuf, sem_in, sem_out, sem_st, kprev, vprev, kst, vst, xr_tail, up_tail, hcar, bias_s,
    hff_s, w_in_ref, w_out_ref, w_up_ref, w_down_ref, wstage, sem_w, sem_wout,
):
    step = pl.program_id(0)
    nblocks = pl.num_programs(0) - 1
    slot = lax.rem(step, 2)
    yslot = lax.rem(step, 3)
    fslot = lax.rem(step + 2, 3)

    def x_copies(st, sl):
        return [pltpu.make_async_copy(
            x_hbm.at[st * NSUB + s, c, :, :],
            xbuf.at[sl, pl.ds(s * GROUPS, GROUPS), c, :],
            sem_in.at[sl]) for s in range(NSUB) for c in range(SUBLANES)]

    def y_copies(st, sl):
        return [pltpu.make_async_copy(
            ybuf.at[sl, pl.ds(s * GROUPS, GROUPS), c, :],
            y_hbm.at[st * NSUB + s, c, :, :],
            sem_out.at[sl]) for s in range(NSUB) for c in range(SUBLANES)]

    def state_copies():
        cps = []
        for src, dst in ((kst, kwin_hbm), (vst, vwin_hbm)):
            cps += [pltpu.make_async_copy(src.at[pl.ds(j * SUBLANES, SUBLANES), :],
                                          dst.at[:, j, :], sem_st) for j in range(GROUPS)]
        return cps

    def weight_out_copies():
        return [pltpu.make_async_copy(w_in_ref, w_in_bf_hbm, sem_wout),
                pltpu.make_async_copy(w_up_ref, w_up_bf_hbm, sem_wout),
                pltpu.make_async_copy(w_out_ref.at[:, pl.ds(0, D_MODEL)], w_out_bf_hbm, sem_wout),
                pltpu.make_async_copy(w_down_ref.at[:, pl.ds(0, D_MODEL)], w_down_bf_hbm, sem_wout)]

    def load_cast(w_hbm, w_s, stage, chunk_rows):
        tiled = len(stage.shape) == 4
        if tiled:
            n_rows, n_cols = w_hbm.shape[0] * SUBLANES, w_hbm.shape[2]
            assert stage.shape[1:] == (chunk_rows // SUBLANES, SUBLANES, n_cols)
        else:
            n_rows, n_cols = w_hbm.shape
        n = n_rows // chunk_rows
        depth = stage.shape[0]
        assert n >= depth

        def chunk(i, sl):
            if tiled:
                g = chunk_rows // SUBLANES
                return pltpu.make_async_copy(w_hbm.at[pl.ds(i * g, g), :, :], stage.at[sl],
                                             sem_w.at[sl])
            return pltpu.make_async_copy(
                w_hbm.at[pl.ds(i * chunk_rows, chunk_rows), :],
                stage.at[sl, pl.ds(0, chunk_rows), pl.ds(0, n_cols)], sem_w.at[sl])

        def staged(sl):
            if tiled:
                return stage[sl].reshape(chunk_rows, n_cols)
            return stage[sl, 0:chunk_rows, 0:n_cols]

        for i in range(depth):
            chunk(i, i).start()

        def body(i, carry):
            sl = lax.rem(i, depth)
            chunk(i, sl).wait()
            r0 = pl.multiple_of(i * chunk_rows, chunk_rows)
            w_s[pl.ds(r0, chunk_rows), 0:n_cols] = staged(sl).astype(bf16)

            @pl.when(i + depth < n)
            def _next():
                chunk(i + depth, sl).start()

            return carry

        lax.fori_loop(0, n, body, 0)

    @pl.when(step == 0)
    def _init():
        for cp in x_copies(0, 0):
            cp.start()
        load_cast(w_in_hbm, w_in_ref, wstage, WSTAGE_ROWS)
        load_cast(w_out_hbm, w_out_ref, ybuf, TOKEN_BLOCK)
        load_cast(w_up_hbm, w_up_ref, wstage, WSTAGE_ROWS)
        load_cast(w_down_hbm, w_down_ref, ybuf, TOKEN_BLOCK)
        for cp in weight_out_copies():
            cp.start()
        kprev[...] = jnp.zeros(kprev.shape, bf16)
        vprev[...] = jnp.zeros(vprev.shape, bf16)
        xr_tail[...] = jnp.zeros(xr_tail.shape, f32)
        up_tail[...] = jnp.zeros(up_tail.shape, f32)
        hcar[...] = jnp.zeros(hcar.shape, f32)
        hff_s[...] = jnp.zeros(hff_s.shape, bf16)
        ybuf[2] = jnp.zeros(ybuf.shape[1:], f32)
        rq = lax.broadcasted_iota(jnp.int32, (SUB_ROWS, 2 * SUB_ROWS), 0)
        ck = lax.broadcasted_iota(jnp.int32, (SUB_ROWS, 2 * SUB_ROWS), 1)
        rk = ck & (SUB_ROWS - 1)
        tq = (rq % SUBLANES) * GROUPS + rq // SUBLANES
        tk = (rk % SUBLANES) * GROUPS + rk // SUBLANES
        dist = tq - tk + jnp.where(ck < SUB_ROWS, WINDOW, 0)
        base = jnp.where((dist >= 0) & (dist < WINDOW), dist.astype(f32), jnp.inf)
        for h in range(N_HEADS):
            bias_s[h] = ALIBI_SLOPES[h] * base

    @pl.when(step + 1 < nblocks)
    def _prefetch():
        for cp in x_copies(step + 1, 1 - slot):
            cp.start()

    @pl.when(step >= 3)
    def _free_ybuf():
        for cp in y_copies(0, yslot):
            cp.wait()

    @pl.when(step < nblocks)
    def _wait_x():
        for cp in x_copies(step, slot):
            cp.wait()

    sh1 = mod_ref[0:1, 0 * D_MODEL:1 * D_MODEL]
    sc1 = mod_ref[0:1, 1 * D_MODEL:2 * D_MODEL]
    g1 = mod_ref[0:1, 2 * D_MODEL:3 * D_MODEL]
    sh2 = mod_ref[0:1, 3 * D_MODEL:4 * D_MODEL]
    sc2 = mod_ref[0:1, 4 * D_MODEL:5 * D_MODEL]
    g2 = mod_ref[0:1, 5 * D_MODEL:6 * D_MODEL]
    gs1 = g_pre_mix_ref[...] * (1.0 + sc1)
    gs2 = g_pre_ffn_ref[...] * (1.0 + sc2)
    gg1 = g1 * g_post_mix_ref[...]
    gg2 = g2 * g_post_ffn_ref[...]
    sp = _softplus_neg(lam_ref[...])
    first_pen = jnp.where(step == 0, jnp.inf, 0.0)
    sub_iota = lax.broadcasted_iota(jnp.int32, (SUBLANES, D_RNN), 0)
    sub0_rnn = sub_iota == 0
    sub0_ffn = lax.broadcasted_iota(jnp.int32, (SUBLANES, FFN_CHUNK), 0) == 0
    conv_w = [conv_w_ref.at[jj:jj + 1, :] for jj in range(RNN_CONV_W)]

    sts = [dict() for _ in range(NSUB)]

    def rows(s):
        return pl.ds(s * SUB_ROWS, SUB_ROWS)

    def groups(s):
        return pl.ds(s * GROUPS, GROUPS)

    def stage_pre(s):
        x = xbuf[slot, groups(s)].reshape(SUB_ROWS, D_MODEL)
        sts[s]["hmix"] = (x * _rms_scale(x) * gs1 + sh1).astype(bf16)

    def stage_inproj(s):
        st = sts[s]
        hm = st["hmix"]
        st["q"] = (jnp.dot(hm, w_in_ref[:, 0:C_K], preferred_element_type=f32) * Q_SCALE).astype(bf16)
        kv = jnp.dot(hm, w_in_ref[:, C_K:C_XR], preferred_element_type=f32)
        st["kb"] = kv[:, 0:KV_WIDTH].astype(bf16)
        st["vb"] = kv[:, KV_WIDTH:2 * KV_WIDTH].astype(bf16)
        st["kv"] = kv
        st["xr"] = jnp.dot(hm, w_in_ref[:, C_XR:C_YR], preferred_element_type=f32)

    def stage_attn(s):
        st = sts[s]
        if s == 0:
            kp, vp = kprev[...], vprev[...]
        else:
            kp, vp = sts[s - 1]["kb"], sts[s - 1]["vb"]
        kw = jnp.concatenate([kp, st["kb"]], axis=0)
        vw = jnp.concatenate([vp, st["vb"]], axis=0)
        outs = []
        for h in range(N_HEADS):
            c = h // GQA_GROUP
            hs = slice(h * HEAD_DIM, (h + 1) * HEAD_DIM)
            cs = slice(c * HEAD_DIM, (c + 1) * HEAD_DIM)
            sc = _dot_nt(st["q"][:, hs], kw[:, cs])
            if s == 0:
                sc = jnp.concatenate([sc[:, 0:SUB_ROWS] - first_pen, sc[:, SUB_ROWS:]], axis=1)
            sc = sc - bias_s[h]
            sink = sinks_ref[h]
            m = jnp.maximum(jnp.max(sc, axis=-1, keepdims=True), sink)
            p = jnp.exp(sc - m)
            denom = jnp.sum(p, axis=-1, keepdims=True) + jnp.exp(sink - m)
            outs.append(jnp.dot(p.astype(bf16), vw[:, cs], preferred_element_type=f32) / denom)
        attn = jnp.concatenate(outs, axis=1)
        st["attn_n"] = _rms_norm(attn, g_attn_ref[...]).astype(bf16)

    def stage_rnn(s):
        st = sts[s]
        xr = st["xr"]
        X = _groups(xr)
        if s == 0:
            tail = lambda j: xr_tail[(j - (GROUPS - RNN_CONV_W + 1)) * SUBLANES + SUBLANES - 1:
                                     (j - (GROUPS - RNN_CONV_W + 1)) * SUBLANES + SUBLANES, :]
        else:
            pxr = sts[s - 1]["xr"]
            tail = lambda j: pxr[j * SUBLANES + SUBLANES - 1:(j + 1) * SUBLANES, :]
        sh = _shifted_groups(X, tail, sub0_rnn, RNN_CONV_W - 1)
        xcs = []
        for j in range(GROUPS):
            acc = conv_b_ref[...] + conv_w[RNN_CONV_W - 1][...] * X[j]
            for d in range(1, RNN_CONV_W):
                acc = acc + conv_w[RNN_CONV_W - 1 - d][...] * sh[d][j]
            xcs.append(acc)
        xc = jnp.concatenate(xcs, axis=0)
        a_parts, u_parts = _rglru_gates(xc, wa_ref, ba_ref[...], wi_ref, bi_ref[...], sp)
        a = jnp.concatenate(a_parts, axis=1)
        u = jnp.concatenate(u_parts, axis=1)
        h_in = hcar[...] if s == 0 else sts[s - 1]["h_out"]
        h, st["h_out"] = _scan_sub(a, u, h_in, sub_iota)
        yr = jnp.dot(st["hmix"], w_in_ref[:, C_YR:IN_COLS], preferred_element_type=f32)
        rnn = h * _gelu_tanh(yr)
        st["rnn_n"] = _rms_norm(rnn, g_rnn_ref[...]).astype(bf16)

    def stage_out(s):
        st = sts[s]
        mo = (jnp.dot(st["attn_n"], w_out_ref[0:ATTN_WIDTH, 0:D_MODEL], preferred_element_type=f32)
              + jnp.dot(st["rnn_n"], w_out_ref[ATTN_WIDTH:ATTN_WIDTH + D_RNN, 0:D_MODEL],
                        preferred_element_type=f32))
        x1 = (xbuf[slot, groups(s)].reshape(SUB_ROWS, D_MODEL)
              + mo * _rms_scale(mo) * gg1)
        ybuf[yslot, groups(s)] = x1.reshape(GROUPS, SUBLANES, D_MODEL)
        hff_s[slot, rows(s), :] = (x1 * _rms_scale(x1) * gs2 + sh2).astype(bf16)

    tail_g0 = GROUPS - (FFN_CONV_W - 1)
    hff = hff_s[1 - slot]
    acts, tails = [], {}

    def conv_chunk(c0):
        c1 = c0 + FFN_CHUNK
        up = jnp.dot(hff, w_up_ref[:, c0:c1], preferred_element_type=f32)
        w = [fconv_w_ref.at[jj:jj + 1, c0:c1] for jj in range(FFN_CONV_W)]
        outs = []
        for s in range(NSUB):
            r0 = s * SUB_ROWS
            X = [up[r0 + j * SUBLANES:r0 + (j + 1) * SUBLANES, :] for j in range(GROUPS)]
            if s == 0:
                tail = lambda j: up_tail[(j - tail_g0) * SUBLANES + SUBLANES - 1:
                                         (j - tail_g0 + 1) * SUBLANES, c0:c1]
            else:
                tail = lambda j, p0=r0 - SUB_ROWS: up[p0 + j * SUBLANES + SUBLANES - 1:
                                                      p0 + (j + 1) * SUBLANES, :]
            sh = _shifted_groups(X, tail, sub0_ffn, FFN_CONV_W - 1)
            for j in range(GROUPS):
                acc = fconv_b_ref[:, c0:c1] + w[FFN_CONV_W - 1][...] * X[j]
                for d in range(1, FFN_CONV_W):
                    acc = acc + w[FFN_CONV_W - 1 - d][...] * sh[d][j]
                outs.append(acc)
        last0 = (NSUB - 1) * SUB_ROWS
        for t in range(FFN_CONV_W - 1):
            r = last0 + (tail_g0 + t) * SUBLANES + SUBLANES - 1
            ffnst_ref[t:t + 1, c0:c1] = up[r:r + 1, :]
        tails[c0] = up[last0 + tail_g0 * SUBLANES:, :]
        return jnp.concatenate(outs, axis=0)

    def ffn_chunk(cc):
        gate = conv_chunk(cc * FFN_CHUNK)
        val = conv_chunk(D_FF + cc * FFN_CHUNK)
        acts.append((_gelu_tanh(gate) * val).astype(bf16))

    def ffn_finish():
        for c0, tl in tails.items():
            up_tail[:, c0:c0 + FFN_CHUNK] = tl
        act = jnp.concatenate(acts, axis=1)
        f = jnp.dot(act, w_down_ref[:, 0:D_MODEL], preferred_element_type=f32)
        y = ybuf[fslot].reshape(TOKEN_BLOCK, D_MODEL) + f * _rms_scale(f) * gg2
        ybuf[fslot] = y.reshape(NSUB * GROUPS, SUBLANES, D_MODEL)

    stages = (stage_pre, stage_inproj, stage_attn, stage_rnn, stage_out)
    n_times = len(stages) + (NSUB - 1) * STAGE_SKEW
    n_chunks = D_FF // FFN_CHUNK
    ffn_chunk(0)
    nxt = 1
    for t in range(n_times):
        for s in range(NSUB):
            k = t - s * STAGE_SKEW
            if 0 <= k < len(stages):
                stages[k](s)
                if nxt < n_chunks:
                    ffn_chunk(nxt)
                    nxt += 1
    while nxt < n_chunks:
        ffn_chunk(nxt)
        nxt += 1
    ffn_finish()

    lst = sts[NSUB - 1]
    kprev[...] = lst["kb"]
    vprev[...] = lst["vb"]
    xr_tail[...] = lst["xr"][(GROUPS - (RNN_CONV_W - 1)) * SUBLANES:, :]
    hcar[...] = lst["h_out"]

    @pl.when(step < nblocks)
    def _state():
        kst[...] = lst["kv"][:, 0:KV_WIDTH]
        vst[...] = lst["kv"][:, KV_WIDTH:2 * KV_WIDTH]
        for t in range(RNN_CONV_W - 1):
            r = (GROUPS - (RNN_CONV_W - 1) + t) * SUBLANES + SUBLANES - 1
            convst_ref[t:t + 1, :] = lst["xr"][r:r + 1, :]
        hlast_ref[...] = lst["h_out"][0:1, :]

    @pl.when(step >= 1)
    def _store_y():
        for cp in y_copies(step - 1, fslot):
            cp.start()

    @pl.when(step == nblocks)
    def _finish():
        for cp in state_copies():
            cp.start()
        for cp in state_copies() + weight_out_copies():
            cp.wait()
        for cp in y_copies(0, lax.rem(step + 1, 3)) + y_copies(0, fslot):
            cp.wait()


def _const_spec(shape):
    nd = len(shape)
    return pl.BlockSpec(shape, lambda i: (0,) * nd)


def _prompt_call(x, mod, mod_row0, sinks, params):
    T = x.shape[0]
    TB = TOKEN_BLOCK
    assert T // TB >= 3
    (g_pre_mix, w_in, conv_w, conv_b, wa, ba, wi, bi, lam, g_attn, g_rnn, w_out,
     g_post_mix, g_pre_ffn, w_up, fconv_w, fconv_b, w_down, g_post_ffn) = params
    x4 = x.reshape(T // SUB_ROWS, SUBLANES, GROUPS, D_MODEL)
    w_out3 = w_out.reshape(-1, SUBLANES, D_MODEL)
    w_down3 = w_down.reshape(-1, SUBLANES, D_MODEL)
    ins = [x4, mod, sinks, g_pre_mix, w_in, conv_w, conv_b, wa, ba, wi, bi, lam, g_attn, g_rnn,
           w_out3, g_post_mix, g_pre_ffn, w_up, fconv_w, fconv_b, w_down3, g_post_ffn]
    assert mod_row0 % MOD_PAD_ROWS == 0
    in_specs = [pl.BlockSpec(memory_space=pl.ANY),
                pl.BlockSpec((MOD_PAD_ROWS, mod.shape[1]), lambda i: (mod_row0 // MOD_PAD_ROWS, 0)),
                pl.BlockSpec(memory_space=pltpu.SMEM)]
    big = (w_in, w_out, w_up, w_down)
    assert all(w.dtype == f32 and w.shape[0] % TOKEN_BLOCK == 0 for w in big)
    in_specs += [pl.BlockSpec(memory_space=pl.ANY)
                 if any(a is w for w in (w_in, w_out3, w_up, w_down3))
                 else _const_spec(a.shape) for a in ins[3:]]
    out_shape = (
        jax.ShapeDtypeStruct(x4.shape, f32),
        jax.ShapeDtypeStruct((SUBLANES, GROUPS, KV_WIDTH), f32),
        jax.ShapeDtypeStruct((SUBLANES, GROUPS, KV_WIDTH), f32),
        jax.ShapeDtypeStruct((1, D_RNN), f32),
        jax.ShapeDtypeStruct((RNN_CONV_W - 1, D_RNN), f32),
        jax.ShapeDtypeStruct((FFN_CONV_W - 1, 2 * D_FF), f32),
    ) + tuple(jax.ShapeDtypeStruct(w.shape, bf16) for w in big)
    out_specs = (
        pl.BlockSpec(memory_space=pl.ANY),
        pl.BlockSpec(memory_space=pl.ANY),
        pl.BlockSpec(memory_space=pl.ANY),
        _const_spec((1, D_RNN)),
        _const_spec((RNN_CONV_W - 1, D_RNN)),
        _const_spec((FFN_CONV_W - 1, 2 * D_FF)),
    ) + (pl.BlockSpec(memory_space=pl.ANY),) * len(big)
    scratch = [
        pltpu.VMEM((2, TB // SUBLANES, SUBLANES, D_MODEL), f32),
        pltpu.VMEM((3, TB // SUBLANES, SUBLANES, D_MODEL), f32),
        pltpu.SemaphoreType.DMA((2,)),
        pltpu.SemaphoreType.DMA((3,)),
        pltpu.SemaphoreType.DMA(()),
        pltpu.VMEM((SUB_ROWS, KV_WIDTH), bf16),
        pltpu.VMEM((SUB_ROWS, KV_WIDTH), bf16),
        pltpu.VMEM((SUB_ROWS, KV_WIDTH), f32),
        pltpu.VMEM((SUB_ROWS, KV_WIDTH), f32),
        pltpu.VMEM(((RNN_CONV_W - 1) * SUBLANES, D_RNN), f32),
        pltpu.VMEM(((FFN_CONV_W - 1) * SUBLANES, 2 * D_FF), f32),
        pltpu.VMEM((SUBLANES, D_RNN), f32),
        pltpu.VMEM((N_HEADS, SUB_ROWS, 2 * SUB_ROWS), f32),
        pltpu.VMEM((2, TOKEN_BLOCK, D_MODEL), bf16),
        pltpu.VMEM(w_in.shape, bf16),
        pltpu.VMEM((w_out.shape[0], PADDED_OUT_COLS), bf16),
        pltpu.VMEM(w_up.shape, bf16),
        pltpu.VMEM((w_down.shape[0], PADDED_OUT_COLS), bf16),
        pltpu.VMEM((WSTAGE_SLOTS, WSTAGE_ROWS, w_up.shape[1]), f32),
        pltpu.SemaphoreType.DMA((WSTAGE_SLOTS,)),
        pltpu.SemaphoreType.DMA(()),
    ]
    assert w_in.shape[1] <= w_up.shape[1] and w_in.shape[0] % WSTAGE_ROWS == 0
    y4, kp, vp, hp, convp, ffnp, w_in_bf, w_out_bf, w_up_bf, w_down_bf = pl.pallas_call(
        _prompt_kernel,
        grid=(T // TB + 1,),
        in_specs=in_specs,
        out_specs=out_specs,
        out_shape=out_shape,
        scratch_shapes=scratch,
        compiler_params=pltpu.CompilerParams(
            dimension_semantics=("arbitrary",), vmem_limit_bytes=VMEM_LIMIT_BYTES),
        name="prompt_layer",
    )(*ins)
    return (y4.reshape(T, D_MODEL), kp.reshape(WINDOW, KV_WIDTH), vp.reshape(WINDOW, KV_WIDTH),
            hp, convp, ffnp, (w_in_bf, w_out_bf, w_up_bf, w_down_bf))


def _sample_pre_kernel(x_ref, mod_ref, g_pre_mix_ref, w_in_hbm, q_ref, kv_ref, xr_ref, yr_ref,
                       w_in_ref, sem):
    halves = [pltpu.make_async_copy(w_in_hbm.at[:, pl.ds(c0, c1 - c0)],
                                    w_in_ref.at[:, pl.ds(c0, c1 - c0)], sem.at[i])
              for i, (c0, c1) in enumerate(((0, C_XR), (C_XR, IN_COLS)))]
    for cp in halves:
        cp.start()
    B = x_ref.shape[0]
    sh1 = mod_ref[0:B, 0 * D_MODEL:1 * D_MODEL]
    sc1 = mod_ref[0:B, 1 * D_MODEL:2 * D_MODEL]
    hmix = (_rms_norm(x_ref[:, 0, :], g_pre_mix_ref[...]) * (1.0 + sc1) + sh1).astype(bf16)
    halves[0].wait()
    q = jnp.dot(hmix, w_in_ref[:, 0:C_K], preferred_element_type=f32) * Q_SCALE
    for h in range(N_HEADS):
        q_ref[:, h, :] = q[:, h * HEAD_DIM:(h + 1) * HEAD_DIM]
    kv_ref[...] = jnp.dot(hmix, w_in_ref[:, C_K:C_XR], preferred_element_type=f32)
    halves[1].wait()
    xr_ref[...] = jnp.dot(hmix, w_in_ref[:, C_XR:C_YR], preferred_element_type=f32)
    yr_ref[...] = jnp.dot(hmix, w_in_ref[:, C_YR:IN_COLS], preferred_element_type=f32)


def _sample_pre_call(x, mod_s, g_pre_mix, w_in):
    B = x.shape[0]
    vmem = pl.BlockSpec(memory_space=pltpu.VMEM)
    return pl.pallas_call(
        _sample_pre_kernel,
        in_specs=[vmem, vmem, vmem, pl.BlockSpec(memory_space=pl.ANY)],
        out_shape=(
            jax.ShapeDtypeStruct((B, N_HEADS, HEAD_DIM), f32),
            jax.ShapeDtypeStruct((B, 2 * KV_WIDTH), f32),
            jax.ShapeDtypeStruct((B, D_RNN), f32),
            jax.ShapeDtypeStruct((B, D_RNN), f32),
        ),
        scratch_shapes=[pltpu.VMEM(w_in.shape, w_in.dtype), pltpu.SemaphoreType.DMA((2,))],
        compiler_params=pltpu.CompilerParams(vmem_limit_bytes=VMEM_LIMIT_BYTES),
        name="sample_pre",
    )(x, mod_s, g_pre_mix, w_in)


def _sample_attn_kernel(q_ref, kv_ref, ck_ref, cv_ref, sinks_ref, o_ref, kwin_ref, vwin_ref):
    R = N_KV_HEADS * WINDOW
    hrow = lax.broadcasted_iota(jnp.int32, (N_HEADS, R), 0)
    rcol = lax.broadcasted_iota(jnp.int32, (N_HEADS, R), 1)
    slope = jnp.exp2(-8.0 * (hrow + 1).astype(f32) / N_HEADS)
    own = (rcol % N_KV_HEADS) == (hrow // GQA_GROUP)
    bias = jnp.where(own, slope * (WINDOW - 1 - rcol // N_KV_HEADS).astype(f32), jnp.inf)
    wrow = lax.broadcasted_iota(jnp.int32, (R, HEAD_DIM), 0)
    sink = sinks_ref[...]

    def shifted(cache, new_rows):
        out = pltpu.roll(cache, R - N_KV_HEADS, axis=0)
        for c in range(N_KV_HEADS):
            out = jnp.where(wrow == R - N_KV_HEADS + c, new_rows[c], out)
        return out

    scores = []
    for b in range(SAMPLE_CHUNK):
        knew = [kv_ref[b:b + 1, c * HEAD_DIM:(c + 1) * HEAD_DIM] for c in range(N_KV_HEADS)]
        kw = shifted(ck_ref[b], knew)
        kwin_ref[b] = kw
        scores.append(_dot_nt(q_ref[b].astype(bf16), kw.astype(bf16)) - bias)
    probs, denoms = [], []
    for b in range(SAMPLE_CHUNK):
        s = scores[b]
        m = jnp.maximum(jnp.max(s, axis=-1, keepdims=True), sink)
        p = jnp.exp(s - m)
        denoms.append(jnp.sum(p, axis=-1, keepdims=True) + jnp.exp(sink - m))
        probs.append(p.astype(bf16))
    for b in range(SAMPLE_CHUNK):
        vnew = [kv_ref[b:b + 1, KV_WIDTH + c * HEAD_DIM:KV_WIDTH + (c + 1) * HEAD_DIM]
                for c in range(N_KV_HEADS)]
        vw = shifted(cv_ref[b], vnew)
        vwin_ref[b] = vw
        o_ref[b] = (jnp.dot(probs[b], vw.astype(bf16), preferred_element_type=f32)
                    / denoms[b])


def _sample_attn_call(q, kv, ck, cv, sinks_col):
    B = q.shape[0]
    BC = SAMPLE_CHUNK
    return pl.pallas_call(
        _sample_attn_kernel,
        grid=(B // BC,),
        in_specs=[
            pl.BlockSpec((BC, N_HEADS, HEAD_DIM), lambda i: (i, 0, 0)),
            pl.BlockSpec((BC, 2 * KV_WIDTH), lambda i: (i, 0)),
            pl.BlockSpec((BC, N_KV_HEADS * WINDOW, HEAD_DIM), lambda i: (i, 0, 0)),
            pl.BlockSpec((BC, N_KV_HEADS * WINDOW, HEAD_DIM), lambda i: (i, 0, 0)),
            pl.BlockSpec((N_HEADS, 1), lambda i: (0, 0)),
        ],
        out_specs=(
            pl.BlockSpec((BC, N_HEADS, HEAD_DIM), lambda i: (i, 0, 0)),
            pl.BlockSpec((BC, N_KV_HEADS * WINDOW, HEAD_DIM), lambda i: (i, 0, 0)),
            pl.BlockSpec((BC, N_KV_HEADS * WINDOW, HEAD_DIM), lambda i: (i, 0, 0)),
        ),
        out_shape=(
            jax.ShapeDtypeStruct((B, N_HEADS, HEAD_DIM), f32),
            jax.ShapeDtypeStruct((B, N_KV_HEADS * WINDOW, HEAD_DIM), f32),
            jax.ShapeDtypeStruct((B, N_KV_HEADS * WINDOW, HEAD_DIM), f32),
        ),
        compiler_params=pltpu.CompilerParams(
            dimension_semantics=("arbitrary",), vmem_limit_bytes=SAMPLE_ATTN_VMEM_BYTES),
        name="sample_attn",
    )(q, kv, ck, cv, sinks_col)


def _sample_post_kernel(
    x_ref, mod_ref, attn_ref, xr_ref, yr_ref, h0_ref, cbuf_ref, fbuf_hbm,
    conv_w_ref, conv_b_ref, wa_ref, ba_ref, wi_ref, bi_ref, lam_ref,
    g_attn_ref, g_rnn_ref, w_out_ref, g_post_mix_ref, g_pre_ffn_ref, w_up_hbm,
    fconv_w_ref, fconv_b_ref, w_down_hbm, g_post_ffn_ref,
    y_ref, h_ref, cst_ref, fst_ref,
    w_up_ref, w_down_ref, fbuf_ref, sem_ffn,
):
    up_copy = pltpu.make_async_copy(w_up_hbm, w_up_ref, sem_ffn.at[0])
    down_copy = pltpu.make_async_copy(w_down_hbm, w_down_ref, sem_ffn.at[1])
    fbuf_copy = pltpu.make_async_copy(fbuf_hbm, fbuf_ref, sem_ffn.at[2])
    up_copy.start()
    fbuf_copy.start()
    down_copy.start()

    B = x_ref.shape[0]
    g1 = mod_ref[0:B, 2 * D_MODEL:3 * D_MODEL]
    sh2 = mod_ref[0:B, 3 * D_MODEL:4 * D_MODEL]
    sc2 = mod_ref[0:B, 4 * D_MODEL:5 * D_MODEL]
    g2 = mod_ref[0:B, 5 * D_MODEL:6 * D_MODEL]
    x = x_ref[:, 0, :]
    xr = xr_ref[...]

    xc = conv_b_ref[...] + conv_w_ref[RNN_CONV_W - 1:RNN_CONV_W, :] * xr
    for jj in range(RNN_CONV_W - 1):
        xc = xc + conv_w_ref[jj:jj + 1, :] * cbuf_ref[jj]
    for jj in range(RNN_CONV_W - 2):
        cst_ref[jj] = cbuf_ref[jj + 1]
    cst_ref[RNN_CONV_W - 2] = xr

    sp = _softplus_neg(lam_ref[...])
    a_parts, u_parts = _rglru_gates(xc, wa_ref, ba_ref[...], wi_ref, bi_ref[...], sp)
    a = jnp.concatenate(a_parts, axis=1)
    u = jnp.concatenate(u_parts, axis=1)
    h = a * h0_ref[...] + u
    h_ref[...] = h
    rnn = h * _gelu_tanh(yr_ref[...])

    attn = jnp.concatenate([attn_ref[:, h, :] for h in range(N_HEADS)], axis=1)
    attn_n = _rms_norm(attn, g_attn_ref[...])
    rnn_n = _rms_norm(rnn, g_rnn_ref[...])
    mo = (_dot(attn_n, w_out_ref[0:ATTN_WIDTH, 0:D_MODEL])
          + _dot(rnn_n, w_out_ref[ATTN_WIDTH:ATTN_WIDTH + D_RNN, 0:D_MODEL]))
    x1 = x + g1 * _rms_norm(mo, g_post_mix_ref[...])

    hff = (_rms_norm(x1, g_pre_ffn_ref[...]) * (1.0 + sc2) + sh2).astype(bf16)
    up_copy.wait()
    up_pre = jnp.dot(hff, w_up_ref[...], preferred_element_type=f32)
    fbuf_copy.wait()
    up = fconv_b_ref[...] + fconv_w_ref[FFN_CONV_W - 1:FFN_CONV_W, :] * up_pre
    for jj in range(FFN_CONV_W - 1):
        up = up + fconv_w_ref[jj:jj + 1, :] * fbuf_ref[:, jj, :]
    for jj in range(FFN_CONV_W - 2):
        fst_ref[:, jj, :] = fbuf_ref[:, jj + 1, :]
    fst_ref[:, FFN_CONV_W - 2, :] = up_pre

    act = (_gelu_tanh(up[:, 0:D_FF]) * up[:, D_FF:2 * D_FF]).astype(bf16)
    down_copy.wait()
    f = jnp.dot(act, w_down_ref[:, 0:D_MODEL], preferred_element_type=f32)
    y_ref[:, 0, :] = x1 + g2 * _rms_norm(f, g_post_ffn_ref[...])


def _sample_post_call(x, mod_s, attn, xr, yr, h0, cbuf, fbuf, params):
    B = x.shape[0]
    (_, _, conv_w, conv_b, wa, ba, wi, bi, lam, g_attn, g_rnn, w_out,
     g_post_mix, g_pre_ffn, w_up, fconv_w, fconv_b, w_down, g_post_ffn) = params
    ins = (x, mod_s, attn, xr, yr, h0, cbuf, fbuf, conv_w, conv_b, wa, ba, wi, bi, lam,
           g_attn, g_rnn, w_out, g_post_mix, g_pre_ffn, w_up, fconv_w, fconv_b, w_down, g_post_ffn)
    return pl.pallas_call(
        _sample_post_kernel,
        in_specs=[pl.BlockSpec(memory_space=pl.ANY) if (a is w_up or a is w_down or a is fbuf)
                  else pl.BlockSpec(memory_space=pltpu.VMEM) for a in ins],
        out_shape=(
            jax.ShapeDtypeStruct((B, 1, D_MODEL), f32),
            jax.ShapeDtypeStruct((B, D_RNN), f32),
            jax.ShapeDtypeStruct((RNN_CONV_W - 1, B, D_RNN), f32),
            jax.ShapeDtypeStruct((B, FFN_CONV_W - 1, 2 * D_FF), f32),
        ),
        scratch_shapes=[pltpu.VMEM(w_up.shape, w_up.dtype), pltpu.VMEM(w_down.shape, w_down.dtype),
                        pltpu.VMEM(fbuf.shape, fbuf.dtype), pltpu.SemaphoreType.DMA((3,))],
        compiler_params=pltpu.CompilerParams(vmem_limit_bytes=VMEM_LIMIT_BYTES),
        name="sample_post",
    )(*ins)


def kernel(x_prompt, x_sample, cache_k, cache_v, state_h, state_conv, state_ffn_conv, c_prompt, c_sample, w_ada, b_ada, g_pre_mix, w_in, conv_w, conv_b, w_a, b_a, w_i, b_i, lam, sinks, g_attn_out, g_rnn_out, w_out, g_post_mix, g_pre_ffn, w_up, ffn_conv_w, ffn_conv_b, w_down, g_post_ffn):
    depth = w_in.shape[0]
    assert depth == 1 and x_prompt.shape[0] == 1 and x_sample.shape[1] == 1
    T = x_prompt.shape[1]
    B = x_sample.shape[0]
    W = cache_k.shape[2]
    assert W == WINDOW and T % TOKEN_BLOCK == 0 and B % SAMPLE_CHUNK == 0

    row = lambda a: a[0].reshape(1, -1)
    params = (
        row(g_pre_mix), w_in[0], conv_w[0], row(conv_b),
        w_a[0].astype(bf16), row(b_a), w_i[0].astype(bf16), row(b_i), row(lam),
        row(g_attn_out), row(g_rnn_out), w_out[0], row(g_post_mix),
        row(g_pre_ffn),
        w_up[0], ffn_conv_w[0], row(ffn_conv_b), w_down[0],
        row(g_post_ffn),
    )

    c_all = jnp.concatenate(
        [c_sample, jnp.broadcast_to(c_prompt, (MOD_PAD_ROWS, D_MODEL))], axis=0)
    mod = _mod_call(c_all, w_ada[0], b_ada[0].reshape(1, -1))
    mod_s = mod

    yp, kp, vp, hp, convp, ffnp, (w_in_bf, w_out_bf, w_up_bf, w_down_bf) = _prompt_call(
        x_prompt[0], mod, B, sinks[0], params)
    params = list(params)
    params[1], params[11], params[14], params[17] = w_in_bf, w_out_bf, w_up_bf, w_down_bf
    assert params[11].shape == w_out.shape[1:] and params[17].shape == w_down.shape[1:]

    xs = x_sample
    q, kv, xr, yr = _sample_pre_call(xs, mod_s, params[0], params[1])
    ck = cache_k.reshape(B, W * N_KV_HEADS, HEAD_DIM)
    cv = cache_v.reshape(B, W * N_KV_HEADS, HEAD_DIM)
    attn3, kwin, vwin = _sample_attn_call(
        q, kv, ck, cv, sinks[0].reshape(N_HEADS, 1))
    attn = attn3
    ys, hs, convs, ffns = _sample_post_call(
        xs, mod_s, attn, xr, yr, state_h[0],
        jnp.transpose(state_conv[0], (1, 0, 2)), state_ffn_conv[0],
        params)
    convs = jnp.transpose(convs, (1, 0, 2))

    kv_shape = (1, 1, W, N_KV_HEADS, HEAD_DIM)
    kvs_shape = (1, B, W, N_KV_HEADS, HEAD_DIM)
    return (
        yp[None], ys,
        kp.reshape(kv_shape), vp.reshape(kv_shape), hp[None], convp[None, None], ffnp[None, None],
        kwin.reshape(kvs_shape), vwin.reshape(kvs_shape), hs[None], convs[None], ffns[None],
    )
```

```python
import math

import jax
import jax.numpy as jnp
from jax import lax
from jax.experimental import pallas as pl
from jax.experimental.pallas import tpu as pltpu

D_MODEL = 1024
N_HEADS = 8
N_KV_HEADS = 2
HEAD_DIM = 128
GQA_GROUP = N_HEADS // N_KV_HEADS
ATTN_WIDTH = N_HEADS * HEAD_DIM
KV_WIDTH = N_KV_HEADS * HEAD_DIM
WINDOW = 128
D_RNN = D_MODEL
RNN_BLOCKS = 8
RNN_BLOCK_W = D_RNN // RNN_BLOCKS
RG_C = 8.0
RNN_CONV_W = 4
D_FF = 2816
FFN_CONV_W = 3
RMS_EPS = 1e-6

C_K = ATTN_WIDTH
C_V = C_K + KV_WIDTH
C_XR = C_V + KV_WIDTH
C_YR = C_XR + D_RNN
IN_COLS = C_YR + D_RNN

SUBLANES = 8
SUB_ROWS = WINDOW
GROUPS = SUB_ROWS // SUBLANES
TOKEN_BLOCK = 256
NSUB = TOKEN_BLOCK // SUB_ROWS
FFN_CHUNK = 256
STAGE_SKEW = 3
WSTAGE_ROWS = 64
WSTAGE_SLOTS = 4
SAMPLE_CHUNK = 16
SAMPLE_ATTN_VMEM_BYTES = (2 * 2 * 2 * SAMPLE_CHUNK * N_KV_HEADS * WINDOW * HEAD_DIM * 4
                          + 8 * 1024 * 1024)
MOD_PAD_ROWS = 8
VMEM_LIMIT_BYTES = 56 * 1024 * 1024

ALIBI_SLOPES = tuple(2.0 ** (-8.0 * (h + 1) / N_HEADS) for h in range(N_HEADS))
Q_SCALE = HEAD_DIM ** -0.5
SQRT_2_OVER_PI = math.sqrt(2.0 / math.pi)
LOG2_E = 1.0 / math.log(2.0)
PADDED_OUT_COLS = D_MODEL + 128

bf16 = jnp.bfloat16
f32 = jnp.float32


def _rms_scale(x):
    return lax.rsqrt(jnp.mean(x * x, axis=-1, keepdims=True) + RMS_EPS)


def _rms_norm(x, g):
    return x * _rms_scale(x) * g


def _gelu_tanh(x):
    k1 = -2.0 * LOG2_E * SQRT_2_OVER_PI
    k3 = k1 * 0.044715
    return x / (1.0 + jnp.exp2(x * (k1 + k3 * (x * x))))


def _sigmoid(x):
    return 1.0 / (1.0 + jnp.exp2(x * (-LOG2_E)))


def _softplus_neg(lam):
    return jnp.maximum(-lam, 0.0) + jnp.log1p(jnp.exp(-jnp.abs(lam)))


def _dot(a, b):
    return jnp.dot(a.astype(bf16), b, preferred_element_type=f32)


def _dot_nt(a, b):
    return lax.dot_general(a, b, (((1,), (1,)), ((), ())), preferred_element_type=f32)


def _rglru_gates(xc, wa_ref, ba, wi_ref, bi, neg_c_sp):
    a_parts, u_parts = [], []
    for n in range(RNN_BLOCKS):
        sl = slice(n * RNN_BLOCK_W, (n + 1) * RNN_BLOCK_W)
        xn = xc[:, sl]
        xb = xn.astype(bf16)
        r = _sigmoid(jnp.dot(xb, wa_ref[n], preferred_element_type=f32) + ba[:, sl])
        i = _sigmoid(jnp.dot(xb, wi_ref[n], preferred_element_type=f32) + bi[:, sl])
        log_a = r * neg_c_sp[:, sl]
        t = jnp.tanh(log_a)
        one_minus_a2 = (-2.0 * t) / (1.0 - t)
        a_parts.append(jnp.exp(log_a))
        u_parts.append(jnp.sqrt(one_minus_a2) * (i * xn))
    return a_parts, u_parts


def _mod_kernel(c_ref, w_hbm, b_ref, o_ref, wbuf, sem):
    nblk, _, bn = wbuf.shape
    copies = [pltpu.make_async_copy(w_hbm.at[:, pl.ds(j * bn, bn)], wbuf.at[j], sem.at[j])
              for j in range(nblk)]
    for cp in copies:
        cp.start()
    c = c_ref[...]
    s = (c * _sigmoid(c)).astype(bf16)
    for j in range(nblk):
        copies[j].wait()
        cols = slice(j * bn, (j + 1) * bn)
        o_ref[:, cols] = (jnp.dot(s, wbuf[j].astype(bf16), preferred_element_type=f32)
                          + b_ref[:, cols])


def _mod_call(c_all, w_ada, b_ada):
    rows = c_all.shape[0]
    k, ncol = w_ada.shape
    bn = D_MODEL
    vmem = pl.BlockSpec(memory_space=pltpu.VMEM)
    return pl.pallas_call(
        _mod_kernel,
        in_specs=[vmem, pl.BlockSpec(memory_space=pl.ANY), vmem],
        out_shape=jax.ShapeDtypeStruct((rows, ncol), f32),
        scratch_shapes=[pltpu.VMEM((ncol // bn, k, bn), f32),
                        pltpu.SemaphoreType.DMA((ncol // bn,))],
        compiler_params=pltpu.CompilerParams(
            vmem_limit_bytes=k * ncol * 4 + rows * ncol * 4 + 8 * 1024 * 1024),
        name="adaln_mod",
    )(c_all, w_ada, b_ada)


def _groups(v, c0=None, c1=None):
    if c0 is None:
        return [v[j * SUBLANES:(j + 1) * SUBLANES, :] for j in range(GROUPS)]
    return [v[j * SUBLANES:(j + 1) * SUBLANES, c0:c1] for j in range(GROUPS)]


def _shifted_groups(X, tail_row, sub0, max_shift):
    wrapped = {}
    for j in range(GROUPS - max_shift, GROUPS):
        wrapped[j] = jnp.where(sub0, tail_row(j), pltpu.roll(X[j], 1, axis=0))
    sh = {}
    for d in range(1, max_shift + 1):
        sh[d] = [X[j - d] if j >= d else wrapped[j - d + GROUPS] for j in range(GROUPS)]
    return sh


def _scan_sub(a, u, h_in, sub_iota):
    A = _groups(a)
    L = _groups(u)
    for j in range(1, GROUPS):
        L[j] = A[j] * L[j - 1] + L[j]
        A[j] = A[j] * A[j - 1]
    ae, le = A[GROUPS - 1], L[GROUPS - 1]
    for s_ in (1, 2, 4):
        ok = sub_iota >= s_
        a_sh = pltpu.roll(ae, s_, axis=0)
        l_sh = pltpu.roll(le, s_, axis=0)
        le = jnp.where(ok, ae * l_sh + le, le)
        ae = jnp.where(ok, ae * a_sh, ae)
    hend = le + ae * h_in
    hprev = jnp.where(sub_iota == 0, h_in, pltpu.roll(hend, 1, axis=0))
    h = jnp.concatenate([L[j] + A[j] * hprev for j in range(GROUPS)], axis=0)
    h_out = jnp.broadcast_to(hend[SUBLANES - 1:SUBLANES, :], hend.shape)
    return h, h_out


def _prompt_kernel(
    x_hbm, mod_ref, sinks_ref,
    g_pre_mix_ref, w_in_hbm, conv_w_ref, conv_b_ref, wa_ref, ba_ref, wi_ref, bi_ref, lam_ref,
    g_attn_ref, g_rnn_ref, w_out_hbm, g_post_mix_ref, g_pre_ffn_ref, w_up_hbm,
    fconv_w_ref, fconv_b_ref, w_down_hbm, g_post_ffn_ref,
    y_hbm, kwin_hbm, vwin_hbm, hlast_ref, convst_ref, ffnst_ref,
    w_in_bf_hbm, w_out_bf_hbm, w_up_bf_hbm, w_down_bf_hbm,
    xbuf, ybuf, sem_in, sem_out, sem_st, kprev, vprev, kst, vst, xr_tail, up_tail, hcar, bias_s,
    hff_s, w_in_ref, w_out_ref, w_up_ref, w_down_ref, wstage, sem_w, sem_wout,
):
    step = pl.program_id(0)
    nblocks = pl.num_programs(0) - 1
    slot = lax.rem(step, 2)
    yslot = lax.rem(step, 3)
    fslot = lax.rem(step + 2, 3)

    def x_copies(st, sl):
        return [pltpu.make_async_copy(
            x_hbm.at[st * NSUB + s, c, :, :],
            xbuf.at[sl, pl.ds(s * GROUPS, GROUPS), c, :],
            sem_in.at[sl]) for s in range(NSUB) for c in range(SUBLANES)]

    def y_copies(st, sl):
        return [pltpu.make_async_copy(
            ybuf.at[sl, pl.ds(s * GROUPS, GROUPS), c, :],
            y_hbm.at[st * NSUB + s, c, :, :],
            sem_out.at[sl]) for s in range(NSUB) for c in range(SUBLANES)]

    def state_copies():
        cps = []
        for src, dst in ((kst, kwin_hbm), (vst, vwin_hbm)):
            cps += [pltpu.make_async_copy(src.at[pl.ds(j * SUBLANES, SUBLANES), :],
                                          dst.at[:, j, :], sem_st) for j in range(GROUPS)]
        return cps

    def weight_out_copies():
        return [pltpu.make_async_copy(w_in_ref, w_in_bf_hbm, sem_wout),
                pltpu.make_async_copy(w_up_ref, w_up_bf_hbm, sem_wout),
                pltpu.make_async_copy(w_out_ref.at[:, pl.ds(0, D_MODEL)], w_out_bf_hbm, sem_wout),
                pltpu.make_async_copy(w_down_ref.at[:, pl.ds(0, D_MODEL)], w_down_bf_hbm, sem_wout)]

    def load_cast(w_hbm, w_s, stage, chunk_rows):
        tiled = len(stage.shape) == 4
        if tiled:
            n_rows, n_cols = w_hbm.shape[0] * SUBLANES, w_hbm.shape[2]
            assert stage.shape[1:] == (chunk_rows // SUBLANES, SUBLANES, n_cols)
        else:
            n_rows, n_cols = w_hbm.shape
        n = n_rows // chunk_rows
        depth = stage.shape[0]
        assert n >= depth

        def chunk(i, sl):
            if tiled:
                g = chunk_rows // SUBLANES
                return pltpu.make_async_copy(w_hbm.at[pl.ds(i * g, g), :, :], stage.at[sl],
                                             sem_w.at[sl])
            return pltpu.make_async_copy(
                w_hbm.at[pl.ds(i * chunk_rows, chunk_rows), :],
                stage.at[sl, pl.ds(0, chunk_rows), pl.ds(0, n_cols)], sem_w.at[sl])

        def staged(sl):
            if tiled:
                return stage[sl].reshape(chunk_rows, n_cols)
            return stage[sl, 0:chunk_rows, 0:n_cols]

        for i in range(depth):
            chunk(i, i).start()

        def body(i, carry):
            sl = lax.rem(i, depth)
            chunk(i, sl).wait()
            r0 = pl.multiple_of(i * chunk_rows, chunk_rows)
            w_s[pl.ds(r0, chunk_rows), 0:n_cols] = staged(sl).astype(bf16)

            @pl.when(i + depth < n)
            def _next():
                chunk(i + depth, sl).start()

            return carry

        lax.fori_loop(0, n, body, 0)

    @pl.when(step == 0)
    def _init():
        for cp in x_copies(0, 0):
            cp.start()
        load_cast(w_in_hbm, w_in_ref, wstage, WSTAGE_ROWS)
        load_cast(w_out_hbm, w_out_ref, ybuf, TOKEN_BLOCK)
        load_cast(w_up_hbm, w_up_ref, wstage, WSTAGE_ROWS)
        load_cast(w_down_hbm, w_down_ref, ybuf, TOKEN_BLOCK)
        for cp in weight_out_copies():
            cp.start()
        kprev[...] = jnp.zeros(kprev.shape, bf16)
        vprev[...] = jnp.zeros(vprev.shape, bf16)
        xr_tail[...] = jnp.zeros(xr_tail.shape, f32)
        up_tail[...] = jnp.zeros(up_tail.shape, f32)
        hcar[...] = jnp.zeros(hcar.shape, f32)
        hff_s[...] = jnp.zeros(hff_s.shape, bf16)
        ybuf[2] = jnp.zeros(ybuf.shape[1:], f32)
        rq = lax.broadcasted_iota(jnp.int32, (SUB_ROWS, 2 * SUB_ROWS), 0)
        ck = lax.broadcasted_iota(jnp.int32, (SUB_ROWS, 2 * SUB_ROWS), 1)
        rk = ck & (SUB_ROWS - 1)
        tq = (rq % SUBLANES) * GROUPS + rq // SUBLANES
        tk = (rk % SUBLANES) * GROUPS + rk // SUBLANES
        dist = tq - tk + jnp.where(ck < SUB_ROWS, WINDOW, 0)
        base = jnp.where((dist >= 0) & (dist < WINDOW), dist.astype(f32), jnp.inf)
        for h in range(N_HEADS):
            bias_s[h] = (ALIBI_SLOPES[h] * LOG2_E) * base

    @pl.when(step + 1 < nblocks)
    def _prefetch():
        for cp in x_copies(step + 1, 1 - slot):
            cp.start()

    @pl.when(step >= 3)
    def _free_ybuf():
        for cp in y_copies(0, yslot):
            cp.wait()

    @pl.when(step < nblocks)
    def _wait_x():
        for cp in x_copies(step, slot):
            cp.wait()

    sh1 = mod_ref[0:1, 0 * D_MODEL:1 * D_MODEL]
    sc1 = mod_ref[0:1, 1 * D_MODEL:2 * D_MODEL]
    g1 = mod_ref[0:1, 2 * D_MODEL:3 * D_MODEL]
    sh2 = mod_ref[0:1, 3 * D_MODEL:4 * D_MODEL]
    sc2 = mod_ref[0:1, 4 * D_MODEL:5 * D_MODEL]
    g2 = mod_ref[0:1, 5 * D_MODEL:6 * D_MODEL]
    gs1 = g_pre_mix_ref[...] * (1.0 + sc1)
    gs2 = g_pre_ffn_ref[...] * (1.0 + sc2)
    gg1 = g1 * g_post_mix_ref[...]
    gg2 = g2 * g_post_ffn_ref[...]
    sp = (-RG_C) * _softplus_neg(lam_ref[...])
    first_pen = jnp.where(step == 0, jnp.inf, 0.0)
    sub_iota = lax.broadcasted_iota(jnp.int32, (SUBLANES, D_RNN), 0)
    sub0_rnn = sub_iota == 0
    sub0_ffn = lax.broadcasted_iota(jnp.int32, (SUBLANES, FFN_CHUNK), 0) == 0
    conv_w = [conv_w_ref.at[jj:jj + 1, :] for jj in range(RNN_CONV_W)]

    sts = [dict() for _ in range(NSUB)]

    def rows(s):
        return pl.ds(s * SUB_ROWS, SUB_ROWS)

    def groups(s):
        return pl.ds(s * GROUPS, GROUPS)

    def stage_pre(s):
        x = xbuf[slot, groups(s)].reshape(SUB_ROWS, D_MODEL)
        sts[s]["hmix"] = (x * _rms_scale(x) * gs1 + sh1).astype(bf16)

    def stage_inproj(s):
        st = sts[s]
        hm = st["hmix"]
        st["q"] = (jnp.dot(hm, w_in_ref[:, 0:C_K], preferred_element_type=f32)
                   * (Q_SCALE * LOG2_E)).astype(bf16)
        kv = jnp.dot(hm, w_in_ref[:, C_K:C_XR], preferred_element_type=f32)
        st["kb"] = kv[:, 0:KV_WIDTH].astype(bf16)
        st["vb"] = kv[:, KV_WIDTH:2 * KV_WIDTH].astype(bf16)
        st["kv"] = kv
        st["xr"] = jnp.dot(hm, w_in_ref[:, C_XR:C_YR], preferred_element_type=f32)

    def stage_attn(s):
        st = sts[s]
        if s == 0:
            kp, vp = kprev[...], vprev[...]
        else:
            kp, vp = sts[s - 1]["kb"], sts[s - 1]["vb"]
        kw = jnp.concatenate([kp, st["kb"]], axis=0)
        vw = jnp.concatenate([vp, st["vb"]], axis=0)
        outs = []
        for h in range(N_HEADS):
            c = h // GQA_GROUP
            hs = slice(h * HEAD_DIM, (h + 1) * HEAD_DIM)
            cs = slice(c * HEAD_DIM, (c + 1) * HEAD_DIM)
            sc = _dot_nt(st["q"][:, hs], kw[:, cs])
            if s == 0:
                sc = jnp.concatenate([sc[:, 0:SUB_ROWS] - first_pen, sc[:, SUB_ROWS:]], axis=1)
            sc = sc - bias_s[h]
            sink = sinks_ref[h] * LOG2_E
            m = jnp.maximum(jnp.max(sc, axis=-1, keepdims=True), sink)
            p = jnp.exp2(sc - m)
            denom = jnp.sum(p, axis=-1, keepdims=True) + jnp.exp2(sink - m)
            outs.append(jnp.dot(p.astype(bf16), vw[:, cs], preferred_element_type=f32) / denom)
        attn = jnp.concatenate(outs, axis=1)
        st["attn_n"] = _rms_norm(attn, g_attn_ref[...]).astype(bf16)

    def stage_rnn(s):
        st = sts[s]
        xr = st["xr"]
        X = _groups(xr)
        if s == 0:
            tail = lambda j: xr_tail[(j - (GROUPS - RNN_CONV_W + 1)) * SUBLANES + SUBLANES - 1:
                                     (j - (GROUPS - RNN_CONV_W + 1)) * SUBLANES + SUBLANES, :]
        else:
            pxr = sts[s - 1]["xr"]
            tail = lambda j: pxr[j * SUBLANES + SUBLANES - 1:(j + 1) * SUBLANES, :]
        sh = _shifted_groups(X, tail, sub0_rnn, RNN_CONV_W - 1)
        xcs = []
        for j in range(GROUPS):
            acc = conv_b_ref[...] + conv_w[RNN_CONV_W - 1][...] * X[j]
            for d in range(1, RNN_CONV_W):
                acc = acc + conv_w[RNN_CONV_W - 1 - d][...] * sh[d][j]
            xcs.append(acc)
        xc = jnp.concatenate(xcs, axis=0)
        a_parts, u_parts = _rglru_gates(xc, wa_ref, ba_ref[...], wi_ref, bi_ref[...], sp)
        a = jnp.concatenate(a_parts, axis=1)
        u = jnp.concatenate(u_parts, axis=1)
        h_in = hcar[...] if s == 0 else sts[s - 1]["h_out"]
        h, st["h_out"] = _scan_sub(a, u, h_in, sub_iota)
        yr = jnp.dot(st["hmix"], w_in_ref[:, C_YR:IN_COLS], preferred_element_type=f32)
        rnn = h * _gelu_tanh(yr)
        st["rnn_n"] = _rms_norm(rnn, g_rnn_ref[...]).astype(bf16)

    def stage_out(s):
        st = sts[s]
        mo = (jnp.dot(st["attn_n"], w_out_ref[0:ATTN_WIDTH, 0:D_MODEL], preferred_element_type=f32)
              + jnp.dot(st["rnn_n"], w_out_ref[ATTN_WIDTH:ATTN_WIDTH + D_RNN, 0:D_MODEL],
                        preferred_element_type=f32))
        x1 = (xbuf[slot, groups(s)].reshape(SUB_ROWS, D_MODEL)
              + mo * _rms_scale(mo) * gg1)
        ybuf[yslot, groups(s)] = x1.reshape(GROUPS, SUBLANES, D_MODEL)
        hff_s[slot, rows(s), :] = (x1 * _rms_scale(x1) * gs2 + sh2).astype(bf16)

    tail_g0 = GROUPS - (FFN_CONV_W - 1)
    hff = hff_s[1 - slot]
    acts, tails = [], {}

    def conv_chunk(c0):
        c1 = c0 + FFN_CHUNK
        up = jnp.dot(hff, w_up_ref[:, c0:c1], preferred_element_type=f32)
        w = [fconv_w_ref.at[jj:jj + 1, c0:c1] for jj in range(FFN_CONV_W)]
        outs = []
        for s in range(NSUB):
            r0 = s * SUB_ROWS
            X = [up[r0 + j * SUBLANES:r0 + (j + 1) * SUBLANES, :] for j in range(GROUPS)]
            if s == 0:
                tail = lambda j: up_tail[(j - tail_g0) * SUBLANES + SUBLANES - 1:
                                         (j - tail_g0 + 1) * SUBLANES, c0:c1]
            else:
                tail = lambda j, p0=r0 - SUB_ROWS: up[p0 + j * SUBLANES + SUBLANES - 1:
                                                      p0 + (j + 1) * SUBLANES, :]
            sh = _shifted_groups(X, tail, sub0_ffn, FFN_CONV_W - 1)
            for j in range(GROUPS):
                acc = fconv_b_ref[:, c0:c1] + w[FFN_CONV_W - 1][...] * X[j]
                for d in range(1, FFN_CONV_W):
                    acc = acc + w[FFN_CONV_W - 1 - d][...] * sh[d][j]
                outs.append(acc)
        last0 = (NSUB - 1) * SUB_ROWS
        for t in range(FFN_CONV_W - 1):
            r = last0 + (tail_g0 + t) * SUBLANES + SUBLANES - 1
            ffnst_ref[t:t + 1, c0:c1] = up[r:r + 1, :]
        tails[c0] = up[last0 + tail_g0 * SUBLANES:, :]
        return jnp.concatenate(outs, axis=0)

    def ffn_chunk(cc):
        gate = conv_chunk(cc * FFN_CHUNK)
        val = conv_chunk(D_FF + cc * FFN_CHUNK)
        acts.append((_gelu_tanh(gate) * val).astype(bf16))

    def ffn_finish():
        for c0, tl in tails.items():
            up_tail[:, c0:c0 + FFN_CHUNK] = tl
        act = jnp.concatenate(acts, axis=1)
        f = jnp.dot(act, w_down_ref[:, 0:D_MODEL], preferred_element_type=f32)
        y = ybuf[fslot].reshape(TOKEN_BLOCK, D_MODEL) + f * _rms_scale(f) * gg2
        ybuf[fslot] = y.reshape(NSUB * GROUPS, SUBLANES, D_MODEL)

    stages = (stage_pre, stage_inproj, stage_attn, stage_rnn, stage_out)
    n_times = len(stages) + (NSUB - 1) * STAGE_SKEW
    n_chunks = D_FF // FFN_CHUNK
    ffn_chunk(0)
    nxt = 1
    for t in range(n_times):
        for s in range(NSUB):
            k = t - s * STAGE_SKEW
            if 0 <= k < len(stages):
                stages[k](s)
                if nxt < n_chunks:
                    ffn_chunk(nxt)
                    nxt += 1
    while nxt < n_chunks:
        ffn_chunk(nxt)
        nxt += 1
    ffn_finish()

    lst = sts[NSUB - 1]
    kprev[...] = lst["kb"]
    vprev[...] = lst["vb"]
    xr_tail[...] = lst["xr"][(GROUPS - (RNN_CONV_W - 1)) * SUBLANES:, :]
    hcar[...] = lst["h_out"]

    @pl.when(step < nblocks)
    def _state():
        kst[...] = lst["kv"][:, 0:KV_WIDTH]
        vst[...] = lst["kv"][:, KV_WIDTH:2 * KV_WIDTH]
        for t in range(RNN_CONV_W - 1):
            r = (GROUPS - (RNN_CONV_W - 1) + t) * SUBLANES + SUBLANES - 1
            convst_ref[t:t + 1, :] = lst["xr"][r:r + 1, :]
        hlast_ref[...] = lst["h_out"][0:1, :]

    @pl.when(step >= 1)
    def _store_y():
        for cp in y_copies(step - 1, fslot):
            cp.start()

    @pl.when(step == nblocks)
    def _finish():
        for cp in state_copies():
            cp.start()
        for cp in state_copies() + weight_out_copies():
            cp.wait()
        for cp in y_copies(0, lax.rem(step + 1, 3)) + y_copies(0, fslot):
            cp.wait()


def _const_spec(shape):
    nd = len(shape)
    return pl.BlockSpec(shape, lambda i: (0,) * nd)


def _prompt_call(x, mod, mod_row0, sinks, params):
    T = x.shape[0]
    TB = TOKEN_BLOCK
    assert T // TB >= 3
    (g_pre_mix, w_in, conv_w, conv_b, wa, ba, wi, bi, lam, g_attn, g_rnn, w_out,
     g_post_mix, g_pre_ffn, w_up, fconv_w, fconv_b, w_down, g_post_ffn) = params
    x4 = x.reshape(T // SUB_ROWS, SUBLANES, GROUPS, D_MODEL)
    w_out3 = w_out.reshape(-1, SUBLANES, D_MODEL)
    w_down3 = w_down.reshape(-1, SUBLANES, D_MODEL)
    ins = [x4, mod, sinks, g_pre_mix, w_in, conv_w, conv_b, wa, ba, wi, bi, lam, g_attn, g_rnn,
           w_out3, g_post_mix, g_pre_ffn, w_up, fconv_w, fconv_b, w_down3, g_post_ffn]
    assert mod_row0 % MOD_PAD_ROWS == 0
    in_specs = [pl.BlockSpec(memory_space=pl.ANY),
                pl.BlockSpec((MOD_PAD_ROWS, mod.shape[1]), lambda i: (mod_row0 // MOD_PAD_ROWS, 0)),
                pl.BlockSpec(memory_space=pltpu.SMEM)]
    big = (w_in, w_out, w_up, w_down)
    assert all(w.dtype == f32 and w.shape[0] % TOKEN_BLOCK == 0 for w in big)
    in_specs += [pl.BlockSpec(memory_space=pl.ANY)
                 if any(a is w for w in (w_in, w_out3, w_up, w_down3))
                 else _const_spec(a.shape) for a in ins[3:]]
    out_shape = (
        jax.ShapeDtypeStruct(x4.shape, f32),
        jax.ShapeDtypeStruct((SUBLANES, GROUPS, KV_WIDTH), f32),
        jax.ShapeDtypeStruct((SUBLANES, GROUPS, KV_WIDTH), f32),
        jax.ShapeDtypeStruct((1, D_RNN), f32),
        jax.ShapeDtypeStruct((RNN_CONV_W - 1, D_RNN), f32),
        jax.ShapeDtypeStruct((FFN_CONV_W - 1, 2 * D_FF), f32),
    ) + tuple(jax.ShapeDtypeStruct(w.shape, bf16) for w in big)
    out_specs = (
        pl.BlockSpec(memory_space=pl.ANY),
        pl.BlockSpec(memory_space=pl.ANY),
        pl.BlockSpec(memory_space=pl.ANY),
        _const_spec((1, D_RNN)),
        _const_spec((RNN_CONV_W - 1, D_RNN)),
        _const_spec((FFN_CONV_W - 1, 2 * D_FF)),
    ) + (pl.BlockSpec(memory_space=pl.ANY),) * len(big)
    scratch = [
        pltpu.VMEM((2, TB // SUBLANES, SUBLANES, D_MODEL), f32),
        pltpu.VMEM((3, TB // SUBLANES, SUBLANES, D_MODEL), f32),
        pltpu.SemaphoreType.DMA((2,)),
        pltpu.SemaphoreType.DMA((3,)),
        pltpu.SemaphoreType.DMA(()),
        pltpu.VMEM((SUB_ROWS, KV_WIDTH), bf16),
        pltpu.VMEM((SUB_ROWS, KV_WIDTH), bf16),
        pltpu.VMEM((SUB_ROWS, KV_WIDTH), f32),
        pltpu.VMEM((SUB_ROWS, KV_WIDTH), f32),
        pltpu.VMEM(((RNN_CONV_W - 1) * SUBLANES, D_RNN), f32),
        pltpu.VMEM(((FFN_CONV_W - 1) * SUBLANES, 2 * D_FF), f32),
        pltpu.VMEM((SUBLANES, D_RNN), f32),
        pltpu.VMEM((N_HEADS, SUB_ROWS, 2 * SUB_ROWS), f32),
        pltpu.VMEM((2, TOKEN_BLOCK, D_MODEL), bf16),
        pltpu.VMEM(w_in.shape, bf16),
        pltpu.VMEM((w_out.shape[0], PADDED_OUT_COLS), bf16),
        pltpu.VMEM(w_up.shape, bf16),
        pltpu.VMEM((w_down.shape[0], PADDED_OUT_COLS), bf16),
        pltpu.VMEM((WSTAGE_SLOTS, WSTAGE_ROWS, w_up.shape[1]), f32),
        pltpu.SemaphoreType.DMA((WSTAGE_SLOTS,)),
        pltpu.SemaphoreType.DMA(()),
    ]
    assert w_in.shape[1] <= w_up.shape[1] and w_in.shape[0] % WSTAGE_ROWS == 0
    y4, kp, vp, hp, convp, ffnp, w_in_bf, w_out_bf, w_up_bf, w_down_bf = pl.pallas_call(
        _prompt_kernel,
        grid=(T // TB + 1,),
        in_specs=in_specs,
        out_specs=out_specs,
        out_shape=out_shape,
        scratch_shapes=scratch,
        compiler_params=pltpu.CompilerParams(
            dimension_semantics=("arbitrary",), vmem_limit_bytes=VMEM_LIMIT_BYTES),
        name="prompt_layer",
    )(*ins)
    return (y4.reshape(T, D_MODEL), kp.reshape(WINDOW, KV_WIDTH), vp.reshape(WINDOW, KV_WIDTH),
            hp, convp, ffnp, (w_in_bf, w_out_bf, w_up_bf, w_down_bf))


def _sample_pre_kernel(x_ref, mod_ref, g_pre_mix_ref, w_in_hbm, q_ref, kv_ref, xr_ref, yr_ref,
                       w_in_ref, sem):
    halves = [pltpu.make_async_copy(w_in_hbm.at[:, pl.ds(c0, c1 - c0)],
                                    w_in_ref.at[:, pl.ds(c0, c1 - c0)], sem.at[i])
              for i, (c0, c1) in enumerate(((0, C_XR), (C_XR, IN_COLS)))]
    for cp in halves:
        cp.start()
    B = x_ref.shape[0]
    sh1 = mod_ref[0:B, 0 * D_MODEL:1 * D_MODEL]
    sc1 = mod_ref[0:B, 1 * D_MODEL:2 * D_MODEL]
    hmix = (_rms_norm(x_ref[:, 0, :], g_pre_mix_ref[...]) * (1.0 + sc1) + sh1).astype(bf16)
    halves[0].wait()
    q = jnp.dot(hmix, w_in_ref[:, 0:C_K], preferred_element_type=f32) * Q_SCALE
    for h in range(N_HEADS):
        q_ref[:, h, :] = q[:, h * HEAD_DIM:(h + 1) * HEAD_DIM]
    kv_ref[...] = jnp.dot(hmix, w_in_ref[:, C_K:C_XR], preferred_element_type=f32)
    halves[1].wait()
    xr_ref[...] = jnp.dot(hmix, w_in_ref[:, C_XR:C_YR], preferred_element_type=f32)
    yr_ref[...] = jnp.dot(hmix, w_in_ref[:, C_YR:IN_COLS], preferred_element_type=f32)


def _sample_pre_call(x, mod_s, g_pre_mix, w_in):
    B = x.shape[0]
    vmem = pl.BlockSpec(memory_space=pltpu.VMEM)
    return pl.pallas_call(
        _sample_pre_kernel,
        in_specs=[vmem, vmem, vmem, pl.BlockSpec(memory_space=pl.ANY)],
        out_shape=(
            jax.ShapeDtypeStruct((B, N_HEADS, HEAD_DIM), f32),
            jax.ShapeDtypeStruct((B, 2 * KV_WIDTH), f32),
            jax.ShapeDtypeStruct((B, D_RNN), f32),
            jax.ShapeDtypeStruct((B, D_RNN), f32),
        ),
        scratch_shapes=[pltpu.VMEM(w_in.shape, w_in.dtype), pltpu.SemaphoreType.DMA((2,))],
        compiler_params=pltpu.CompilerParams(vmem_limit_bytes=VMEM_LIMIT_BYTES),
        name="sample_pre",
    )(x, mod_s, g_pre_mix, w_in)


def _sample_attn_kernel(q_ref, kv_ref, ck_ref, cv_ref, sinks_ref, o_ref, kwin_ref, vwin_ref):
    R = N_KV_HEADS * WINDOW
    hrow = lax.broadcasted_iota(jnp.int32, (N_HEADS, R), 0)
    rcol = lax.broadcasted_iota(jnp.int32, (N_HEADS, R), 1)
    slope = jnp.exp2(-8.0 * (hrow + 1).astype(f32) / N_HEADS)
    own = (rcol % N_KV_HEADS) == (hrow // GQA_GROUP)
    bias = jnp.where(own, slope * (WINDOW - 1 - rcol // N_KV_HEADS).astype(f32), jnp.inf)
    wrow = lax.broadcasted_iota(jnp.int32, (R, HEAD_DIM), 0)
    sink = sinks_ref[...]

    def shifted(cache, new_rows):
        out = pltpu.roll(cache, R - N_KV_HEADS, axis=0)
        for c in range(N_KV_HEADS):
            out = jnp.where(wrow == R - N_KV_HEADS + c, new_rows[c], out)
        return out

    scores = []
    for b in range(SAMPLE_CHUNK):
        knew = [kv_ref[b:b + 1, c * HEAD_DIM:(c + 1) * HEAD_DIM] for c in range(N_KV_HEADS)]
        kw = shifted(ck_ref[b], knew)
        kwin_ref[b] = kw
        scores.append(_dot_nt(q_ref[b].astype(bf16), kw.astype(bf16)) - bias)
    probs, denoms = [], []
    for b in range(SAMPLE_CHUNK):
        s = scores[b]
        m = jnp.maximum(jnp.max(s, axis=-1, keepdims=True), sink)
        p = jnp.exp(s - m)
        denoms.append(jnp.sum(p, axis=-1, keepdims=True) + jnp.exp(sink - m))
        probs.append(p.astype(bf16))
    for b in range(SAMPLE_CHUNK):
        vnew = [kv_ref[b:b + 1, KV_WIDTH + c * HEAD_DIM:KV_WIDTH + (c + 1) * HEAD_DIM]
                for c in range(N_KV_HEADS)]
        vw = shifted(cv_ref[b], vnew)
        vwin_ref[b] = vw
        o_ref[b] = (jnp.dot(probs[b], vw.astype(bf16), preferred_element_type=f32)
                    / denoms[b])


def _sample_attn_call(q, kv, ck, cv, sinks_col):
    B = q.shape[0]
    BC = SAMPLE_CHUNK
    return pl.pallas_call(
        _sample_attn_kernel,
        grid=(B // BC,),
        in_specs=[
            pl.BlockSpec((BC, N_HEADS, HEAD_DIM), lambda i: (i, 0, 0)),
            pl.BlockSpec((BC, 2 * KV_WIDTH), lambda i: (i, 0)),
            pl.BlockSpec((BC, N_KV_HEADS * WINDOW, HEAD_DIM), lambda i: (i, 0, 0)),
            pl.BlockSpec((BC, N_KV_HEADS * WINDOW, HEAD_DIM), lambda i: (i, 0, 0)),
            pl.BlockSpec((N_HEADS, 1), lambda i: (0, 0)),
        ],
        out_specs=(
            pl.BlockSpec((BC, N_HEADS, HEAD_DIM), lambda i: (i, 0, 0)),
            pl.BlockSpec((BC, N_KV_HEADS * WINDOW, HEAD_DIM), lambda i: (i, 0, 0)),
            pl.BlockSpec((BC, N_KV_HEADS * WINDOW, HEAD_DIM), lambda i: (i, 0, 0)),
        ),
        out_shape=(
            jax.ShapeDtypeStruct((B, N_HEADS, HEAD_DIM), f32),
            jax.ShapeDtypeStruct((B, N_KV_HEADS * WINDOW, HEAD_DIM), f32),
            jax.ShapeDtypeStruct((B, N_KV_HEADS * WINDOW, HEAD_DIM), f32),
        ),
        compiler_params=pltpu.CompilerParams(
            dimension_semantics=("arbitrary",), vmem_limit_bytes=SAMPLE_ATTN_VMEM_BYTES),
        name="sample_attn",
    )(q, kv, ck, cv, sinks_col)


def _sample_post_kernel(
    x_ref, mod_ref, attn_ref, xr_ref, yr_ref, h0_ref, cbuf_ref, fbuf_hbm,
    conv_w_ref, conv_b_ref, wa_ref, ba_ref, wi_ref, bi_ref, lam_ref,
    g_attn_ref, g_rnn_ref, w_out_ref, g_post_mix_ref, g_pre_ffn_ref, w_up_hbm,
    fconv_w_ref, fconv_b_ref, w_down_hbm, g_post_ffn_ref,
    y_ref, h_ref, cst_ref, fst_ref,
    w_up_ref, w_down_ref, fbuf_ref, sem_ffn,
):
    up_copy = pltpu.make_async_copy(w_up_hbm, w_up_ref, sem_ffn.at[0])
    down_copy = pltpu.make_async_copy(w_down_hbm, w_down_ref, sem_ffn.at[1])
    fbuf_copy = pltpu.make_async_copy(fbuf_hbm, fbuf_ref, sem_ffn.at[2])
    up_copy.start()
    fbuf_copy.start()
    down_copy.start()

    B = x_ref.shape[0]
    g1 = mod_ref[0:B, 2 * D_MODEL:3 * D_MODEL]
    sh2 = mod_ref[0:B, 3 * D_MODEL:4 * D_MODEL]
    sc2 = mod_ref[0:B, 4 * D_MODEL:5 * D_MODEL]
    g2 = mod_ref[0:B, 5 * D_MODEL:6 * D_MODEL]
    x = x_ref[:, 0, :]
    xr = xr_ref[...]

    xc = conv_b_ref[...] + conv_w_ref[RNN_CONV_W - 1:RNN_CONV_W, :] * xr
    for jj in range(RNN_CONV_W - 1):
        xc = xc + conv_w_ref[jj:jj + 1, :] * cbuf_ref[jj]
    for jj in range(RNN_CONV_W - 2):
        cst_ref[jj] = cbuf_ref[jj + 1]
    cst_ref[RNN_CONV_W - 2] = xr

    sp = (-RG_C) * _softplus_neg(lam_ref[...])
    a_parts, u_parts = _rglru_gates(xc, wa_ref, ba_ref[...], wi_ref, bi_ref[...], sp)
    a = jnp.concatenate(a_parts, axis=1)
    u = jnp.concatenate(u_parts, axis=1)
    h = a * h0_ref[...] + u
    h_ref[...] = h
    rnn = h * _gelu_tanh(yr_ref[...])

    attn = jnp.concatenate([attn_ref[:, h, :] for h in range(N_HEADS)], axis=1)
    attn_n = _rms_norm(attn, g_attn_ref[...])
    rnn_n = _rms_norm(rnn, g_rnn_ref[...])
    mo = (_dot(attn_n, w_out_ref[0:ATTN_WIDTH, 0:D_MODEL])
          + _dot(rnn_n, w_out_ref[ATTN_WIDTH:ATTN_WIDTH + D_RNN, 0:D_MODEL]))
    x1 = x + g1 * _rms_norm(mo, g_post_mix_ref[...])

    hff = (_rms_norm(x1, g_pre_ffn_ref[...]) * (1.0 + sc2) + sh2).astype(bf16)
    up_copy.wait()
    up_pre = jnp.dot(hff, w_up_ref[...], preferred_element_type=f32)
    fbuf_copy.wait()
    up = fconv_b_ref[...] + fconv_w_ref[FFN_CONV_W - 1:FFN_CONV_W, :] * up_pre
    for jj in range(FFN_CONV_W - 1):
        up = up + fconv_w_ref[jj:jj + 1, :] * fbuf_ref[:, jj, :]
    for jj in range(FFN_CONV_W - 2):
        fst_ref[:, jj, :] = fbuf_ref[:, jj + 1, :]
    fst_ref[:, FFN_CONV_W - 2, :] = up_pre

    act = (_gelu_tanh(up[:, 0:D_FF]) * up[:, D_FF:2 * D_FF]).astype(bf16)
    down_copy.wait()
    f = jnp.dot(act, w_down_ref[:, 0:D_MODEL], preferred_element_type=f32)
    y_ref[:, 0, :] = x1 + g2 * _rms_norm(f, g_post_ffn_ref[...])


def _sample_post_call(x, mod_s, attn, xr, yr, h0, cbuf, fbuf, params):
    B = x.shape[0]
    (_, _, conv_w, conv_b, wa, ba, wi, bi, lam, g_attn, g_rnn, w_out,
     g_post_mix, g_pre_ffn, w_up, fconv_w, fconv_b, w_down, g_post_ffn) = params
    ins = (x, mod_s, attn, xr, yr, h0, cbuf, fbuf, conv_w, conv_b, wa, ba, wi, bi, lam,
           g_attn, g_rnn, w_out, g_post_mix, g_pre_ffn, w_up, fconv_w, fconv_b, w_down, g_post_ffn)
    return pl.pallas_call(
        _sample_post_kernel,
        in_specs=[pl.BlockSpec(memory_space=pl.ANY) if (a is w_up or a is w_down or a is fbuf)
                  else pl.BlockSpec(memory_space=pltpu.VMEM) for a in ins],
        out_shape=(
            jax.ShapeDtypeStruct((B, 1, D_MODEL), f32),
            jax.ShapeDtypeStruct((B, D_RNN), f32),
            jax.ShapeDtypeStruct((RNN_CONV_W - 1, B, D_RNN), f32),
            jax.ShapeDtypeStruct((B, FFN_CONV_W - 1, 2 * D_FF), f32),
        ),
        scratch_shapes=[pltpu.VMEM(w_up.shape, w_up.dtype), pltpu.VMEM(w_down.shape, w_down.dtype),
                        pltpu.VMEM(fbuf.shape, fbuf.dtype), pltpu.SemaphoreType.DMA((3,))],
        compiler_params=pltpu.CompilerParams(vmem_limit_bytes=VMEM_LIMIT_BYTES),
        name="sample_post",
    )(*ins)


def kernel(x_prompt, x_sample, cache_k, cache_v, state_h, state_conv, state_ffn_conv, c_prompt, c_sample, w_ada, b_ada, g_pre_mix, w_in, conv_w, conv_b, w_a, b_a, w_i, b_i, lam, sinks, g_attn_out, g_rnn_out, w_out, g_post_mix, g_pre_ffn, w_up, ffn_conv_w, ffn_conv_b, w_down, g_post_ffn):
    depth = w_in.shape[0]
    assert depth == 1 and x_prompt.shape[0] == 1 and x_sample.shape[1] == 1
    T = x_prompt.shape[1]
    B = x_sample.shape[0]
    W = cache_k.shape[2]
    assert W == WINDOW and T % TOKEN_BLOCK == 0 and B % SAMPLE_CHUNK == 0

    row = lambda a: a[0].reshape(1, -1)
    params = (
        row(g_pre_mix), w_in[0], conv_w[0], row(conv_b),
        w_a[0].astype(bf16), row(b_a), w_i[0].astype(bf16), row(b_i), row(lam),
        row(g_attn_out), row(g_rnn_out), w_out[0], row(g_post_mix),
        row(g_pre_ffn),
        w_up[0], ffn_conv_w[0], row(ffn_conv_b), w_down[0],
        row(g_post_ffn),
    )

    c_all = jnp.concatenate(
        [c_sample, jnp.broadcast_to(c_prompt, (MOD_PAD_ROWS, D_MODEL))], axis=0)
    mod = _mod_call(c_all, w_ada[0], b_ada[0].reshape(1, -1))
    mod_s = mod

    yp, kp, vp, hp, convp, ffnp, (w_in_bf, w_out_bf, w_up_bf, w_down_bf) = _prompt_call(
        x_prompt[0], mod, B, sinks[0], params)
    params = list(params)
    params[1], params[11], params[14], params[17] = w_in_bf, w_out_bf, w_up_bf, w_down_bf
    assert params[11].shape == w_out.shape[1:] and params[17].shape == w_down.shape[1:]

    xs = x_sample
    q, kv, xr, yr = _sample_pre_call(xs, mod_s, params[0], params[1])
    ck = cache_k.reshape(B, W * N_KV_HEADS, HEAD_DIM)
    cv = cache_v.reshape(B, W * N_KV_HEADS, HEAD_DIM)
    attn3, kwin, vwin = _sample_attn_call(
        q, kv, ck, cv, sinks[0].reshape(N_HEADS, 1))
    attn = attn3
    ys, hs, convs, ffns = _sample_post_call(
        xs, mod_s, attn, xr, yr, state_h[0],
        jnp.transpose(state_conv[0], (1, 0, 2)), state_ffn_conv[0],
        params)
    convs = jnp.transpose(convs, (1, 0, 2))

    kv_shape = (1, 1, W, N_KV_HEADS, HEAD_DIM)
    kvs_shape = (1, B, W, N_KV_HEADS, HEAD_DIM)
    return (
        yp[None], ys,
        kp.reshape(kv_shape), vp.reshape(kv_shape), hp[None], convp[None, None], ffnp[None, None],
        kwin.reshape(kvs_shape), vwin.reshape(kvs_shape), hs[None], convs[None], ffns[None],
    )
```

```python
import math

import jax
import jax.numpy as jnp
from jax import lax
from jax.experimental import pallas as pl
from jax.experimental.pallas import tpu as pltpu

D_MODEL = 1024
N_HEADS = 8
N_KV_HEADS = 2
HEAD_DIM = 128
GQA_GROUP = N_HEADS // N_KV_HEADS
ATTN_WIDTH = N_HEADS * HEAD_DIM
KV_WIDTH = N_KV_HEADS * HEAD_DIM
WINDOW = 128
D_RNN = D_MODEL
RNN_BLOCKS = 8
RNN_BLOCK_W = D_RNN // RNN_BLOCKS
RG_C = 8.0
RNN_CONV_W = 4
D_FF = 2816
FFN_CONV_W = 3
RMS_EPS = 1e-6

C_K = ATTN_WIDTH
C_V = C_K + KV_WIDTH
C_XR = C_V + KV_WIDTH
C_YR = C_XR + D_RNN
IN_COLS = C_YR + D_RNN

SUBLANES = 8
SUB_ROWS = WINDOW
GROUPS = SUB_ROWS // SUBLANES
TOKEN_BLOCK = 256
NSUB = TOKEN_BLOCK // SUB_ROWS
FFN_CHUNK = 256
STAGE_SKEW = 3
WSTAGE_ROWS = 64
WSTAGE_SLOTS = 4
SAMPLE_CHUNK = 16
SAMPLE_ATTN_VMEM_BYTES = (2 * 2 * 2 * SAMPLE_CHUNK * N_KV_HEADS * WINDOW * HEAD_DIM * 4
                          + 8 * 1024 * 1024)
MOD_PAD_ROWS = 8
VMEM_LIMIT_BYTES = 56 * 1024 * 1024

ALIBI_SLOPES = tuple(2.0 ** (-8.0 * (h + 1) / N_HEADS) for h in range(N_HEADS))
Q_SCALE = HEAD_DIM ** -0.5
SQRT_2_OVER_PI = math.sqrt(2.0 / math.pi)
LOG2_E = 1.0 / math.log(2.0)
PADDED_OUT_COLS = D_MODEL + 128

bf16 = jnp.bfloat16
f32 = jnp.float32


def _rms_scale(x):
    return lax.rsqrt(jnp.mean(x * x, axis=-1, keepdims=True) + RMS_EPS)


def _rms_norm(x, g):
    return x * _rms_scale(x) * g


def _gelu_tanh(x):
    k1 = -2.0 * LOG2_E * SQRT_2_OVER_PI
    k3 = k1 * 0.044715
    return x / (1.0 + jnp.exp2(x * (k1 + k3 * (x * x))))


def _sigmoid(x):
    return 1.0 / (1.0 + jnp.exp2(x * (-LOG2_E)))


def _softplus_neg(lam):
    return jnp.maximum(-lam, 0.0) + jnp.log1p(jnp.exp(-jnp.abs(lam)))


def _dot(a, b):
    return jnp.dot(a.astype(bf16), b, preferred_element_type=f32)


def _dot_nt(a, b):
    return lax.dot_general(a, b, (((1,), (1,)), ((), ())), preferred_element_type=f32)


def _rglru_gates(xc, wa_ref, ba, wi_ref, bi, neg_c_sp):
    a_parts, u_parts = [], []
    for n in range(RNN_BLOCKS):
        sl = slice(n * RNN_BLOCK_W, (n + 1) * RNN_BLOCK_W)
        xn = xc[:, sl]
        xb = xn.astype(bf16)
        r = _sigmoid(jnp.dot(xb, wa_ref[n], preferred_element_type=f32) + ba[:, sl])
        i = _sigmoid(jnp.dot(xb, wi_ref[n], preferred_element_type=f32) + bi[:, sl])
        log_a = r * neg_c_sp[:, sl]
        t = jnp.tanh(log_a)
        one_minus_a2 = (-2.0 * t) / (1.0 - t)
        a_parts.append(jnp.exp(log_a))
        u_parts.append(jnp.sqrt(one_minus_a2) * (i * xn))
    return a_parts, u_parts


def _mod_kernel(c_ref, w_hbm, b_ref, o_ref, wbuf, sem):
    nblk, _, bn = wbuf.shape
    copies = [pltpu.make_async_copy(w_hbm.at[:, pl.ds(j * bn, bn)], wbuf.at[j], sem.at[j])
              for j in range(nblk)]
    for cp in copies:
        cp.start()
    c = c_ref[...]
    s = (c * _sigmoid(c)).astype(bf16)
    for j in range(nblk):
        copies[j].wait()
        cols = slice(j * bn, (j + 1) * bn)
        o_ref[:, cols] = (jnp.dot(s, wbuf[j].astype(bf16), preferred_element_type=f32)
                          + b_ref[:, cols])


def _mod_call(c_all, w_ada, b_ada):
    rows = c_all.shape[0]
    k, ncol = w_ada.shape
    bn = D_MODEL
    vmem = pl.BlockSpec(memory_space=pltpu.VMEM)
    return pl.pallas_call(
        _mod_kernel,
        in_specs=[vmem, pl.BlockSpec(memory_space=pl.ANY), vmem],
        out_shape=jax.ShapeDtypeStruct((rows, ncol), f32),
        scratch_shapes=[pltpu.VMEM((ncol // bn, k, bn), f32),
                        pltpu.SemaphoreType.DMA((ncol // bn,))],
        compiler_params=pltpu.CompilerParams(
            vmem_limit_bytes=k * ncol * 4 + rows * ncol * 4 + 8 * 1024 * 1024),
        name="adaln_mod",
    )(c_all, w_ada, b_ada)


def _groups(v, c0=None, c1=None):
    if c0 is None:
        return [v[j * SUBLANES:(j + 1) * SUBLANES, :] for j in range(GROUPS)]
    return [v[j * SUBLANES:(j + 1) * SUBLANES, c0:c1] for j in range(GROUPS)]


def _shifted_groups(X, tail_row, sub0, max_shift):
    wrapped = {}
    for j in range(GROUPS - max_shift, GROUPS):
        wrapped[j] = jnp.where(sub0, tail_row(j), pltpu.roll(X[j], 1, axis=0))
    sh = {}
    for d in range(1, max_shift + 1):
        sh[d] = [X[j - d] if j >= d else wrapped[j - d + GROUPS] for j in range(GROUPS)]
    return sh


def _scan_sub(a, u, h_in, sub_iota):
    A = _groups(a)
    L = _groups(u)
    for j in range(1, GROUPS):
        L[j] = A[j] * L[j - 1] + L[j]
        A[j] = A[j] * A[j - 1]
    ae, le = A[GROUPS - 1], L[GROUPS - 1]
    for s_ in (1, 2, 4):
        ok = sub_iota >= s_
        a_sh = pltpu.roll(ae, s_, axis=0)
        l_sh = pltpu.roll(le, s_, axis=0)
        le = jnp.where(ok, ae * l_sh + le, le)
        ae = jnp.where(ok, ae * a_sh, ae)
    hend = le + ae * h_in
    hprev = jnp.where(sub_iota == 0, h_in, pltpu.roll(hend, 1, axis=0))
    h = jnp.concatenate([L[j] + A[j] * hprev for j in range(GROUPS)], axis=0)
    h_out = jnp.broadcast_to(hend[SUBLANES - 1:SUBLANES, :], hend.shape)
    return h, h_out


def _prompt_kernel(
    x_hbm, mod_ref, sinks_ref,
    g_pre_mix_ref, w_in_hbm, conv_w_ref, conv_b_ref, wa_ref, ba_ref, wi_ref, bi_ref, lam_ref,
    g_attn_ref, g_rnn_ref, w_out_hbm, g_post_mix_ref, g_pre_ffn_ref, w_up_hbm,
    fconv_w_ref, fconv_b_ref, w_down_hbm, g_post_ffn_ref,
    y_hbm, kwin_hbm, vwin_hbm, hlast_ref, convst_ref, ffnst_ref,
    w_in_bf_hbm, w_out_bf_hbm, w_up_bf_hbm, w_down_bf_hbm,
    xbuf, ybuf, sem_in, sem_out, sem_st, kprev, vprev, kst, vst, xr_tail, up_tail, hcar, bias_s,
    hff_s, w_in_ref, w_out_ref, w_up_ref, w_down_ref, wstage, sem_w, sem_wout,
):
    step = pl.program_id(0)
    nblocks = pl.num_programs(0) - 1
    slot = lax.rem(step, 2)
    yslot = lax.rem(step, 3)
    fslot = lax.rem(step + 2, 3)

    def x_copies(st, sl):
        return [pltpu.make_async_copy(
            x_hbm.at[st * NSUB + s, c, :, :],
            xbuf.at[sl, pl.ds(s * GROUPS, GROUPS), c, :],
            sem_in.at[sl]) for s in range(NSUB) for c in range(SUBLANES)]

    def y_copies(st, sl):
        return [pltpu.make_async_copy(
            ybuf.at[sl, pl.ds(s * GROUPS, GROUPS), c, :],
            y_hbm.at[st * NSUB + s, c, :, :],
            sem_out.at[sl]) for s in range(NSUB) for c in range(SUBLANES)]

    def state_copies():
        cps = []
        for src, dst in ((kst, kwin_hbm), (vst, vwin_hbm)):
            cps += [pltpu.make_async_copy(src.at[pl.ds(j * SUBLANES, SUBLANES), :],
                                          dst.at[:, j, :], sem_st) for j in range(GROUPS)]
        return cps

    def weight_out_copies():
        return [pltpu.make_async_copy(w_in_ref, w_in_bf_hbm, sem_wout),
                pltpu.make_async_copy(w_up_ref, w_up_bf_hbm, sem_wout),
                pltpu.make_async_copy(w_out_ref.at[:, pl.ds(0, D_MODEL)], w_out_bf_hbm, sem_wout),
                pltpu.make_async_copy(w_down_ref.at[:, pl.ds(0, D_MODEL)], w_down_bf_hbm, sem_wout)]

    def load_cast(w_hbm, w_s, stage, chunk_rows):
        tiled = len(stage.shape) == 4
        if tiled:
            n_rows, n_cols = w_hbm.shape[0] * SUBLANES, w_hbm.shape[2]
            assert stage.shape[1:] == (chunk_rows // SUBLANES, SUBLANES, n_cols)
        else:
            n_rows, n_cols = w_hbm.shape
        n = n_rows // chunk_rows
        depth = stage.shape[0]
        assert n >= depth

        def chunk(i, sl):
            if tiled:
                g = chunk_rows // SUBLANES
                return pltpu.make_async_copy(w_hbm.at[pl.ds(i * g, g), :, :], stage.at[sl],
                                             sem_w.at[sl])
            return pltpu.make_async_copy(
                w_hbm.at[pl.ds(i * chunk_rows, chunk_rows), :],
                stage.at[sl, pl.ds(0, chunk_rows), pl.ds(0, n_cols)], sem_w.at[sl])

        def staged(sl):
            if tiled:
                return stage[sl].reshape(chunk_rows, n_cols)
            return stage[sl, 0:chunk_rows, 0:n_cols]

        for i in range(depth):
            chunk(i, i).start()

        def body(i, carry):
            sl = lax.rem(i, depth)
            chunk(i, sl).wait()
            r0 = pl.multiple_of(i * chunk_rows, chunk_rows)
            w_s[pl.ds(r0, chunk_rows), 0:n_cols] = staged(sl).astype(bf16)

            @pl.when(i + depth < n)
            def _next():
                chunk(i + depth, sl).start()

            return carry

        lax.fori_loop(0, n, body, 0)

    @pl.when(step == 0)
    def _init():
        for cp in x_copies(0, 0):
            cp.start()
        load_cast(w_in_hbm, w_in_ref, wstage, WSTAGE_ROWS)
        load_cast(w_out_hbm, w_out_ref, ybuf, TOKEN_BLOCK)
        load_cast(w_up_hbm, w_up_ref, wstage, WSTAGE_ROWS)
        load_cast(w_down_hbm, w_down_ref, ybuf, TOKEN_BLOCK)
        for cp in weight_out_copies():
            cp.start()
        kprev[...] = jnp.zeros(kprev.shape, bf16)
        vprev[...] = jnp.zeros(vprev.shape, bf16)
        xr_tail[...] = jnp.zeros(xr_tail.shape, f32)
        up_tail[...] = jnp.zeros(up_tail.shape, f32)
        hcar[...] = jnp.zeros(hcar.shape, f32)
        hff_s[...] = jnp.zeros(hff_s.shape, bf16)
        ybuf[2] = jnp.zeros(ybuf.shape[1:], f32)
        rq = lax.broadcasted_iota(jnp.int32, (SUB_ROWS, 2 * SUB_ROWS), 0)
        ck = lax.broadcasted_iota(jnp.int32, (SUB_ROWS, 2 * SUB_ROWS), 1)
        rk = ck & (SUB_ROWS - 1)
        tq = (rq % SUBLANES) * GROUPS + rq // SUBLANES
        tk = (rk % SUBLANES) * GROUPS + rk // SUBLANES
        dist = tq - tk + jnp.where(ck < SUB_ROWS, WINDOW, 0)
        base = jnp.where((dist >= 0) & (dist < WINDOW), dist.astype(f32), jnp.inf)
        base_first = jnp.where(ck < SUB_ROWS, jnp.inf, base)
        for h in range(N_HEADS):
            bias_s[0, h] = (ALIBI_SLOPES[h] * LOG2_E) * base
            bias_s[1, h] = (ALIBI_SLOPES[h] * LOG2_E) * base_first

    @pl.when(step + 1 < nblocks)
    def _prefetch():
        for cp in x_copies(step + 1, 1 - slot):
            cp.start()

    @pl.when(step >= 3)
    def _free_ybuf():
        for cp in y_copies(0, yslot):
            cp.wait()

    @pl.when(step < nblocks)
    def _wait_x():
        for cp in x_copies(step, slot):
            cp.wait()

    sh1 = mod_ref[0:1, 0 * D_MODEL:1 * D_MODEL]
    sc1 = mod_ref[0:1, 1 * D_MODEL:2 * D_MODEL]
    g1 = mod_ref[0:1, 2 * D_MODEL:3 * D_MODEL]
    sh2 = mod_ref[0:1, 3 * D_MODEL:4 * D_MODEL]
    sc2 = mod_ref[0:1, 4 * D_MODEL:5 * D_MODEL]
    g2 = mod_ref[0:1, 5 * D_MODEL:6 * D_MODEL]
    gs1 = g_pre_mix_ref[...] * (1.0 + sc1)
    gs2 = g_pre_ffn_ref[...] * (1.0 + sc2)
    gg1 = g1 * g_post_mix_ref[...]
    gg2 = g2 * g_post_ffn_ref[...]
    sp = (-RG_C) * _softplus_neg(lam_ref[...])
    first_blk = jnp.where(step == 0, 1, 0)
    sub_iota = lax.broadcasted_iota(jnp.int32, (SUBLANES, D_RNN), 0)
    sub0_rnn = sub_iota == 0
    sub0_ffn = lax.broadcasted_iota(jnp.int32, (SUBLANES, FFN_CHUNK), 0) == 0
    conv_w = [conv_w_ref.at[jj:jj + 1, :] for jj in range(RNN_CONV_W)]

    sts = [dict() for _ in range(NSUB)]

    def rows(s):
        return pl.ds(s * SUB_ROWS, SUB_ROWS)

    def groups(s):
        return pl.ds(s * GROUPS, GROUPS)

    def stage_pre(s):
        x = xbuf[slot, groups(s)].reshape(SUB_ROWS, D_MODEL)
        sts[s]["hmix"] = (x * _rms_scale(x) * gs1 + sh1).astype(bf16)

    def stage_inproj(s):
        st = sts[s]
        hm = st["hmix"]
        st["q"] = (jnp.dot(hm, w_in_ref[:, 0:C_K], preferred_element_type=f32)
                   * (Q_SCALE * LOG2_E)).astype(bf16)
        kv = jnp.dot(hm, w_in_ref[:, C_K:C_XR], preferred_element_type=f32)
        st["kb"] = kv[:, 0:KV_WIDTH].astype(bf16)
        st["vb"] = kv[:, KV_WIDTH:2 * KV_WIDTH].astype(bf16)
        st["kv"] = kv
        st["xr"] = jnp.dot(hm, w_in_ref[:, C_XR:C_YR], preferred_element_type=f32)

    def stage_attn(s):
        st = sts[s]
        if s == 0:
            kp, vp = kprev[...], vprev[...]
        else:
            kp, vp = sts[s - 1]["kb"], sts[s - 1]["vb"]
        kw = jnp.concatenate([kp, st["kb"]], axis=0)
        vw = jnp.concatenate([vp, st["vb"]], axis=0)
        outs = []
        for h in range(N_HEADS):
            c = h // GQA_GROUP
            hs = slice(h * HEAD_DIM, (h + 1) * HEAD_DIM)
            cs = slice(c * HEAD_DIM, (c + 1) * HEAD_DIM)
            sc = _dot_nt(st["q"][:, hs], kw[:, cs])
            sc = sc - (bias_s[first_blk, h] if s == 0 else bias_s[0, h])
            sink = sinks_ref[h] * LOG2_E
            m = jnp.maximum(jnp.max(sc, axis=-1, keepdims=True), sink)
            p = jnp.exp2(sc - m)
            denom = jnp.sum(p, axis=-1, keepdims=True) + jnp.exp2(sink - m)
            outs.append(jnp.dot(p.astype(bf16), vw[:, cs], preferred_element_type=f32) / denom)
        attn = jnp.concatenate(outs, axis=1)
        st["attn_n"] = _rms_norm(attn, g_attn_ref[...]).astype(bf16)

    def stage_rnn(s):
        st = sts[s]
        xr = st["xr"]
        X = _groups(xr)
        if s == 0:
            tail = lambda j: xr_tail[(j - (GROUPS - RNN_CONV_W + 1)) * SUBLANES + SUBLANES - 1:
                                     (j - (GROUPS - RNN_CONV_W + 1)) * SUBLANES + SUBLANES, :]
        else:
            pxr = sts[s - 1]["xr"]
            tail = lambda j: pxr[j * SUBLANES + SUBLANES - 1:(j + 1) * SUBLANES, :]
        sh = _shifted_groups(X, tail, sub0_rnn, RNN_CONV_W - 1)
        xcs = []
        for j in range(GROUPS):
            acc = conv_b_ref[...] + conv_w[RNN_CONV_W - 1][...] * X[j]
            for d in range(1, RNN_CONV_W):
                acc = acc + conv_w[RNN_CONV_W - 1 - d][...] * sh[d][j]
            xcs.append(acc)
        xc = jnp.concatenate(xcs, axis=0)
        a_parts, u_parts = _rglru_gates(xc, wa_ref, ba_ref[...], wi_ref, bi_ref[...], sp)
        a = jnp.concatenate(a_parts, axis=1)
        u = jnp.concatenate(u_parts, axis=1)
        h_in = hcar[...] if s == 0 else sts[s - 1]["h_out"]
        h, st["h_out"] = _scan_sub(a, u, h_in, sub_iota)
        yr = jnp.dot(st["hmix"], w_in_ref[:, C_YR:IN_COLS], preferred_element_type=f32)
        rnn = h * _gelu_tanh(yr)
        st["rnn_n"] = _rms_norm(rnn, g_rnn_ref[...]).astype(bf16)

    def stage_out(s):
        st = sts[s]
        mo = (jnp.dot(st["attn_n"], w_out_ref[0:ATTN_WIDTH, 0:D_MODEL], preferred_element_type=f32)
              + jnp.dot(st["rnn_n"], w_out_ref[ATTN_WIDTH:ATTN_WIDTH + D_RNN, 0:D_MODEL],
                        preferred_element_type=f32))
        x1 = (xbuf[slot, groups(s)].reshape(SUB_ROWS, D_MODEL)
              + mo * _rms_scale(mo) * gg1)
        ybuf[yslot, groups(s)] = x1.reshape(GROUPS, SUBLANES, D_MODEL)
        hff_s[slot, rows(s), :] = (x1 * _rms_scale(x1) * gs2 + sh2).astype(bf16)

    tail_g0 = GROUPS - (FFN_CONV_W - 1)
    hff = hff_s[1 - slot]
    acts, tails = [], {}

    def conv_chunk(c0):
        c1 = c0 + FFN_CHUNK
        up = jnp.dot(hff, w_up_ref[:, c0:c1], preferred_element_type=f32)
        w = [fconv_w_ref.at[jj:jj + 1, c0:c1] for jj in range(FFN_CONV_W)]
        outs = []
        for s in range(NSUB):
            r0 = s * SUB_ROWS
            X = [up[r0 + j * SUBLANES:r0 + (j + 1) * SUBLANES, :] for j in range(GROUPS)]
            if s == 0:
                tail = lambda j: up_tail[(j - tail_g0) * SUBLANES + SUBLANES - 1:
                                         (j - tail_g0 + 1) * SUBLANES, c0:c1]
            else:
                tail = lambda j, p0=r0 - SUB_ROWS: up[p0 + j * SUBLANES + SUBLANES - 1:
                                                      p0 + (j + 1) * SUBLANES, :]
            sh = _shifted_groups(X, tail, sub0_ffn, FFN_CONV_W - 1)
            for j in range(GROUPS):
                acc = fconv_b_ref[:, c0:c1] + w[FFN_CONV_W - 1][...] * X[j]
                for d in range(1, FFN_CONV_W):
                    acc = acc + w[FFN_CONV_W - 1 - d][...] * sh[d][j]
                outs.append(acc)
        last0 = (NSUB - 1) * SUB_ROWS
        for t in range(FFN_CONV_W - 1):
            r = last0 + (tail_g0 + t) * SUBLANES + SUBLANES - 1
            ffnst_ref[t:t + 1, c0:c1] = up[r:r + 1, :]
        tails[c0] = up[last0 + tail_g0 * SUBLANES:, :]
        return jnp.concatenate(outs, axis=0)

    def ffn_chunk(cc):
        gate = conv_chunk(cc * FFN_CHUNK)
        val = conv_chunk(D_FF + cc * FFN_CHUNK)
        acts.append((_gelu_tanh(gate) * val).astype(bf16))

    def ffn_finish():
        for c0, tl in tails.items():
            up_tail[:, c0:c0 + FFN_CHUNK] = tl
        act = jnp.concatenate(acts, axis=1)
        f = jnp.dot(act, w_down_ref[:, 0:D_MODEL], preferred_element_type=f32)
        y = ybuf[fslot].reshape(TOKEN_BLOCK, D_MODEL) + f * _rms_scale(f) * gg2
        ybuf[fslot] = y.reshape(NSUB * GROUPS, SUBLANES, D_MODEL)

    stages = (stage_pre, stage_inproj, stage_attn, stage_rnn, stage_out)
    n_times = len(stages) + (NSUB - 1) * STAGE_SKEW
    n_chunks = D_FF // FFN_CHUNK
    ffn_chunk(0)
    nxt = 1
    for t in range(n_times):
        for s in range(NSUB):
            k = t - s * STAGE_SKEW
            if 0 <= k < len(stages):
                stages[k](s)
                if nxt < n_chunks:
                    ffn_chunk(nxt)
                    nxt += 1
    while nxt < n_chunks:
        ffn_chunk(nxt)
        nxt += 1
    ffn_finish()

    lst = sts[NSUB - 1]
    kprev[...] = lst["kb"]
    vprev[...] = lst["vb"]
    xr_tail[...] = lst["xr"][(GROUPS - (RNN_CONV_W - 1)) * SUBLANES:, :]
    hcar[...] = lst["h_out"]

    @pl.when(step < nblocks)
    def _state():
        kst[...] = lst["kv"][:, 0:KV_WIDTH]
        vst[...] = lst["kv"][:, KV_WIDTH:2 * KV_WIDTH]
        for t in range(RNN_CONV_W - 1):
            r = (GROUPS - (RNN_CONV_W - 1) + t) * SUBLANES + SUBLANES - 1
            convst_ref[t:t + 1, :] = lst["xr"][r:r + 1, :]
        hlast_ref[...] = lst["h_out"][0:1, :]

    @pl.when(step >= 1)
    def _store_y():
        for cp in y_copies(step - 1, fslot):
            cp.start()

    @pl.when(step == nblocks)
    def _finish():
        for cp in state_copies():
            cp.start()
        for cp in state_copies() + weight_out_copies():
            cp.wait()
        for cp in y_copies(0, lax.rem(step + 1, 3)) + y_copies(0, fslot):
            cp.wait()


def _const_spec(shape):
    nd = len(shape)
    return pl.BlockSpec(shape, lambda i: (0,) * nd)


def _prompt_call(x, mod, mod_row0, sinks, params):
    T = x.shape[0]
    TB = TOKEN_BLOCK
    assert T // TB >= 3
    (g_pre_mix, w_in, conv_w, conv_b, wa, ba, wi, bi, lam, g_attn, g_rnn, w_out,
     g_post_mix, g_pre_ffn, w_up, fconv_w, fconv_b, w_down, g_post_ffn) = params
    x4 = x.reshape(T // SUB_ROWS, SUBLANES, GROUPS, D_MODEL)
    w_out3 = w_out.reshape(-1, SUBLANES, D_MODEL)
    w_down3 = w_down.reshape(-1, SUBLANES, D_MODEL)
    ins = [x4, mod, sinks, g_pre_mix, w_in, conv_w, conv_b, wa, ba, wi, bi, lam, g_attn, g_rnn,
           w_out3, g_post_mix, g_pre_ffn, w_up, fconv_w, fconv_b, w_down3, g_post_ffn]
    assert mod_row0 % MOD_PAD_ROWS == 0
    in_specs = [pl.BlockSpec(memory_space=pl.ANY),
                pl.BlockSpec((MOD_PAD_ROWS, mod.shape[1]), lambda i: (mod_row0 // MOD_PAD_ROWS, 0)),
                pl.BlockSpec(memory_space=pltpu.SMEM)]
    big = (w_in, w_out, w_up, w_down)
    assert all(w.dtype == f32 and w.shape[0] % TOKEN_BLOCK == 0 for w in big)
    in_specs += [pl.BlockSpec(memory_space=pl.ANY)
                 if any(a is w for w in (w_in, w_out3, w_up, w_down3))
                 else _const_spec(a.shape) for a in ins[3:]]
    out_shape = (
        jax.ShapeDtypeStruct(x4.shape, f32),
        jax.ShapeDtypeStruct((SUBLANES, GROUPS, KV_WIDTH), f32),
        jax.ShapeDtypeStruct((SUBLANES, GROUPS, KV_WIDTH), f32),
        jax.ShapeDtypeStruct((1, D_RNN), f32),
        jax.ShapeDtypeStruct((RNN_CONV_W - 1, D_RNN), f32),
        jax.ShapeDtypeStruct((FFN_CONV_W - 1, 2 * D_FF), f32),
    ) + tuple(jax.ShapeDtypeStruct(w.shape, bf16) for w in big)
    out_specs = (
        pl.BlockSpec(memory_space=pl.ANY),
        pl.BlockSpec(memory_space=pl.ANY),
        pl.BlockSpec(memory_space=pl.ANY),
        _const_spec((1, D_RNN)),
        _const_spec((RNN_CONV_W - 1, D_RNN)),
        _const_spec((FFN_CONV_W - 1, 2 * D_FF)),
    ) + (pl.BlockSpec(memory_space=pl.ANY),) * len(big)
    scratch = [
        pltpu.VMEM((2, TB // SUBLANES, SUBLANES, D_MODEL), f32),
        pltpu.VMEM((3, TB // SUBLANES, SUBLANES, D_MODEL), f32),
        pltpu.SemaphoreType.DMA((2,)),
        pltpu.SemaphoreType.DMA((3,)),
        pltpu.SemaphoreType.DMA(()),
        pltpu.VMEM((SUB_ROWS, KV_WIDTH), bf16),
        pltpu.VMEM((SUB_ROWS, KV_WIDTH), bf16),
        pltpu.VMEM((SUB_ROWS, KV_WIDTH), f32),
        pltpu.VMEM((SUB_ROWS, KV_WIDTH), f32),
        pltpu.VMEM(((RNN_CONV_W - 1) * SUBLANES, D_RNN), f32),
        pltpu.VMEM(((FFN_CONV_W - 1) * SUBLANES, 2 * D_FF), f32),
        pltpu.VMEM((SUBLANES, D_RNN), f32),
        pltpu.VMEM((2, N_HEADS, SUB_ROWS, 2 * SUB_ROWS), f32),
        pltpu.VMEM((2, TOKEN_BLOCK, D_MODEL), bf16),
        pltpu.VMEM(w_in.shape, bf16),
        pltpu.VMEM((w_out.shape[0], PADDED_OUT_COLS), bf16),
        pltpu.VMEM(w_up.shape, bf16),
        pltpu.VMEM((w_down.shape[0], PADDED_OUT_COLS), bf16),
        pltpu.VMEM((WSTAGE_SLOTS, WSTAGE_ROWS, w_up.shape[1]), f32),
        pltpu.SemaphoreType.DMA((WSTAGE_SLOTS,)),
        pltpu.SemaphoreType.DMA(()),
    ]
    assert w_in.shape[1] <= w_up.shape[1] and w_in.shape[0] % WSTAGE_ROWS == 0
    y4, kp, vp, hp, convp, ffnp, w_in_bf, w_out_bf, w_up_bf, w_down_bf = pl.pallas_call(
        _prompt_kernel,
        grid=(T // TB + 1,),
        in_specs=in_specs,
        out_specs=out_specs,
        out_shape=out_shape,
        scratch_shapes=scratch,
        compiler_params=pltpu.CompilerParams(
            dimension_semantics=("arbitrary",), vmem_limit_bytes=VMEM_LIMIT_BYTES),
        name="prompt_layer",
    )(*ins)
    return (y4.reshape(T, D_MODEL), kp.reshape(WINDOW, KV_WIDTH), vp.reshape(WINDOW, KV_WIDTH),
            hp, convp, ffnp, (w_in_bf, w_out_bf, w_up_bf, w_down_bf))


def _sample_pre_kernel(x_ref, mod_ref, g_pre_mix_ref, w_in_hbm, q_ref, kv_ref, xr_ref, yr_ref,
                       w_in_ref, sem):
    halves = [pltpu.make_async_copy(w_in_hbm.at[:, pl.ds(c0, c1 - c0)],
                                    w_in_ref.at[:, pl.ds(c0, c1 - c0)], sem.at[i])
              for i, (c0, c1) in enumerate(((0, C_XR), (C_XR, IN_COLS)))]
    for cp in halves:
        cp.start()
    B = x_ref.shape[0]
    sh1 = mod_ref[0:B, 0 * D_MODEL:1 * D_MODEL]
    sc1 = mod_ref[0:B, 1 * D_MODEL:2 * D_MODEL]
    hmix = (_rms_norm(x_ref[:, 0, :], g_pre_mix_ref[...]) * (1.0 + sc1) + sh1).astype(bf16)
    halves[0].wait()
    q = jnp.dot(hmix, w_in_ref[:, 0:C_K], preferred_element_type=f32) * Q_SCALE
    for h in range(N_HEADS):
        q_ref[:, h, :] = q[:, h * HEAD_DIM:(h + 1) * HEAD_DIM]
    kv_ref[...] = jnp.dot(hmix, w_in_ref[:, C_K:C_XR], preferred_element_type=f32)
    halves[1].wait()
    xr_ref[...] = jnp.dot(hmix, w_in_ref[:, C_XR:C_YR], preferred_element_type=f32)
    yr_ref[...] = jnp.dot(hmix, w_in_ref[:, C_YR:IN_COLS], preferred_element_type=f32)


def _sample_pre_call(x, mod_s, g_pre_mix, w_in):
    B = x.shape[0]
    vmem = pl.BlockSpec(memory_space=pltpu.VMEM)
    return pl.pallas_call(
        _sample_pre_kernel,
        in_specs=[vmem, vmem, vmem, pl.BlockSpec(memory_space=pl.ANY)],
        out_shape=(
            jax.ShapeDtypeStruct((B, N_HEADS, HEAD_DIM), f32),
            jax.ShapeDtypeStruct((B, 2 * KV_WIDTH), f32),
            jax.ShapeDtypeStruct((B, D_RNN), f32),
            jax.ShapeDtypeStruct((B, D_RNN), f32),
        ),
        scratch_shapes=[pltpu.VMEM(w_in.shape, w_in.dtype), pltpu.SemaphoreType.DMA((2,))],
        compiler_params=pltpu.CompilerParams(vmem_limit_bytes=VMEM_LIMIT_BYTES),
        name="sample_pre",
    )(x, mod_s, g_pre_mix, w_in)


def _sample_attn_kernel(q_ref, kv_ref, ck_ref, cv_ref, sinks_ref, o_ref, kwin_ref, vwin_ref):
    R = N_KV_HEADS * WINDOW
    hrow = lax.broadcasted_iota(jnp.int32, (N_HEADS, R), 0)
    rcol = lax.broadcasted_iota(jnp.int32, (N_HEADS, R), 1)
    slope = jnp.exp2(-8.0 * (hrow + 1).astype(f32) / N_HEADS)
    own = (rcol % N_KV_HEADS) == (hrow // GQA_GROUP)
    bias = jnp.where(own, slope * (WINDOW - 1 - rcol // N_KV_HEADS).astype(f32), jnp.inf)
    wrow = lax.broadcasted_iota(jnp.int32, (R, HEAD_DIM), 0)
    sink = sinks_ref[...]

    def shifted(cache, new_rows):
        out = pltpu.roll(cache, R - N_KV_HEADS, axis=0)
        for c in range(N_KV_HEADS):
            out = jnp.where(wrow == R - N_KV_HEADS + c, new_rows[c], out)
        return out

    scores = []
    for b in range(SAMPLE_CHUNK):
        knew = [kv_ref[b:b + 1, c * HEAD_DIM:(c + 1) * HEAD_DIM] for c in range(N_KV_HEADS)]
        kw = shifted(ck_ref[b], knew)
        kwin_ref[b] = kw
        scores.append(_dot_nt(q_ref[b].astype(bf16), kw.astype(bf16)) - bias)
    probs, denoms = [], []
    for b in range(SAMPLE_CHUNK):
        s = scores[b]
        m = jnp.maximum(jnp.max(s, axis=-1, keepdims=True), sink)
        p = jnp.exp(s - m)
        denoms.append(jnp.sum(p, axis=-1, keepdims=True) + jnp.exp(sink - m))
        probs.append(p.astype(bf16))
    for b in range(SAMPLE_CHUNK):
        vnew = [kv_ref[b:b + 1, KV_WIDTH + c * HEAD_DIM:KV_WIDTH + (c + 1) * HEAD_DIM]
                for c in range(N_KV_HEADS)]
        vw = shifted(cv_ref[b], vnew)
        vwin_ref[b] = vw
        o_ref[b] = (jnp.dot(probs[b], vw.astype(bf16), preferred_element_type=f32)
                    / denoms[b])


def _sample_attn_call(q, kv, ck, cv, sinks_col):
    B = q.shape[0]
    BC = SAMPLE_CHUNK
    return pl.pallas_call(
        _sample_attn_kernel,
        grid=(B // BC,),
        in_specs=[
            pl.BlockSpec((BC, N_HEADS, HEAD_DIM), lambda i: (i, 0, 0)),
            pl.BlockSpec((BC, 2 * KV_WIDTH), lambda i: (i, 0)),
            pl.BlockSpec((BC, N_KV_HEADS * WINDOW, HEAD_DIM), lambda i: (i, 0, 0)),
            pl.BlockSpec((BC, N_KV_HEADS * WINDOW, HEAD_DIM), lambda i: (i, 0, 0)),
            pl.BlockSpec((N_HEADS, 1), lambda i: (0, 0)),
        ],
        out_specs=(
            pl.BlockSpec((BC, N_HEADS, HEAD_DIM), lambda i: (i, 0, 0)),
            pl.BlockSpec((BC, N_KV_HEADS * WINDOW, HEAD_DIM), lambda i: (i, 0, 0)),
            pl.BlockSpec((BC, N_KV_HEADS * WINDOW, HEAD_DIM), lambda i: (i, 0, 0)),
        ),
        out_shape=(
            jax.ShapeDtypeStruct((B, N_HEADS, HEAD_DIM), f32),
            jax.ShapeDtypeStruct((B, N_KV_HEADS * WINDOW, HEAD_DIM), f32),
            jax.ShapeDtypeStruct((B, N_KV_HEADS * WINDOW, HEAD_DIM), f32),
        ),
        compiler_params=pltpu.CompilerParams(
            dimension_semantics=("arbitrary",), vmem_limit_bytes=SAMPLE_ATTN_VMEM_BYTES),
        name="sample_attn",
    )(q, kv, ck, cv, sinks_col)


def _sample_post_kernel(
    x_ref, mod_ref, attn_ref, xr_ref, yr_ref, h0_ref, cbuf_ref, fbuf_hbm,
    conv_w_ref, conv_b_ref, wa_ref, ba_ref, wi_ref, bi_ref, lam_ref,
    g_attn_ref, g_rnn_ref, w_out_ref, g_post_mix_ref, g_pre_ffn_ref, w_up_hbm,
    fconv_w_ref, fconv_b_ref, w_down_hbm, g_post_ffn_ref,
    y_ref, h_ref, cst_ref, fst_ref,
    w_up_ref, w_down_ref, fbuf_ref, sem_ffn,
):
    up_copy = pltpu.make_async_copy(w_up_hbm, w_up_ref, sem_ffn.at[0])
    down_copy = pltpu.make_async_copy(w_down_hbm, w_down_ref, sem_ffn.at[1])
    fbuf_copy = pltpu.make_async_copy(fbuf_hbm, fbuf_ref, sem_ffn.at[2])
    up_copy.start()
    fbuf_copy.start()
    down_copy.start()

    B = x_ref.shape[0]
    g1 = mod_ref[0:B, 2 * D_MODEL:3 * D_MODEL]
    sh2 = mod_ref[0:B, 3 * D_MODEL:4 * D_MODEL]
    sc2 = mod_ref[0:B, 4 * D_MODEL:5 * D_MODEL]
    g2 = mod_ref[0:B, 5 * D_MODEL:6 * D_MODEL]
    x = x_ref[:, 0, :]
    xr = xr_ref[...]

    xc = conv_b_ref[...] + conv_w_ref[RNN_CONV_W - 1:RNN_CONV_W, :] * xr
    for jj in range(RNN_CONV_W - 1):
        xc = xc + conv_w_ref[jj:jj + 1, :] * cbuf_ref[jj]
    for jj in range(RNN_CONV_W - 2):
        cst_ref[jj] = cbuf_ref[jj + 1]
    cst_ref[RNN_CONV_W - 2] = xr

    sp = (-RG_C) * _softplus_neg(lam_ref[...])
    a_parts, u_parts = _rglru_gates(xc, wa_ref, ba_ref[...], wi_ref, bi_ref[...], sp)
    a = jnp.concatenate(a_parts, axis=1)
    u = jnp.concatenate(u_parts, axis=1)
    h = a * h0_ref[...] + u
    h_ref[...] = h
    rnn = h * _gelu_tanh(yr_ref[...])

    attn = jnp.concatenate([attn_ref[:, h, :] for h in range(N_HEADS)], axis=1)
    attn_n = _rms_norm(attn, g_attn_ref[...])
    rnn_n = _rms_norm(rnn, g_rnn_ref[...])
    mo = (_dot(attn_n, w_out_ref[0:ATTN_WIDTH, 0:D_MODEL])
          + _dot(rnn_n, w_out_ref[ATTN_WIDTH:ATTN_WIDTH + D_RNN, 0:D_MODEL]))
    x1 = x + g1 * _rms_norm(mo, g_post_mix_ref[...])

    hff = (_rms_norm(x1, g_pre_ffn_ref[...]) * (1.0 + sc2) + sh2).astype(bf16)
    up_copy.wait()
    up_pre = jnp.dot(hff, w_up_ref[...], preferred_element_type=f32)
    fbuf_copy.wait()
    up = fconv_b_ref[...] + fconv_w_ref[FFN_CONV_W - 1:FFN_CONV_W, :] * up_pre
    for jj in range(FFN_CONV_W - 1):
        up = up + fconv_w_ref[jj:jj + 1, :] * fbuf_ref[:, jj, :]
    for jj in range(FFN_CONV_W - 2):
        fst_ref[:, jj, :] = fbuf_ref[:, jj + 1, :]
    fst_ref[:, FFN_CONV_W - 2, :] = up_pre

    act = (_gelu_tanh(up[:, 0:D_FF]) * up[:, D_FF:2 * D_FF]).astype(bf16)
    down_copy.wait()
    f = jnp.dot(act, w_down_ref[:, 0:D_MODEL], preferred_element_type=f32)
    y_ref[:, 0, :] = x1 + g2 * _rms_norm(f, g_post_ffn_ref[...])


def _sample_post_call(x, mod_s, attn, xr, yr, h0, cbuf, fbuf, params):
    B = x.shape[0]
    (_, _, conv_w, conv_b, wa, ba, wi, bi, lam, g_attn, g_rnn, w_out,
     g_post_mix, g_pre_ffn, w_up, fconv_w, fconv_b, w_down, g_post_ffn) = params
    ins = (x, mod_s, attn, xr, yr, h0, cbuf, fbuf, conv_w, conv_b, wa, ba, wi, bi, lam,
           g_attn, g_rnn, w_out, g_post_mix, g_pre_ffn, w_up, fconv_w, fconv_b, w_down, g_post_ffn)
    return pl.pallas_call(
        _sample_post_kernel,
        in_specs=[pl.BlockSpec(memory_space=pl.ANY) if (a is w_up or a is w_down or a is fbuf)
                  else pl.BlockSpec(memory_space=pltpu.VMEM) for a in ins],
        out_shape=(
            jax.ShapeDtypeStruct((B, 1, D_MODEL), f32),
            jax.ShapeDtypeStruct((B, D_RNN), f32),
            jax.ShapeDtypeStruct((RNN_CONV_W - 1, B, D_RNN), f32),
            jax.ShapeDtypeStruct((B, FFN_CONV_W - 1, 2 * D_FF), f32),
        ),
        scratch_shapes=[pltpu.VMEM(w_up.shape, w_up.dtype), pltpu.VMEM(w_down.shape, w_down.dtype),
                        pltpu.VMEM(fbuf.shape, fbuf.dtype), pltpu.SemaphoreType.DMA((3,))],
        compiler_params=pltpu.CompilerParams(vmem_limit_bytes=VMEM_LIMIT_BYTES),
        name="sample_post",
    )(*ins)


def kernel(x_prompt, x_sample, cache_k, cache_v, state_h, state_conv, state_ffn_conv, c_prompt, c_sample, w_ada, b_ada, g_pre_mix, w_in, conv_w, conv_b, w_a, b_a, w_i, b_i, lam, sinks, g_attn_out, g_rnn_out, w_out, g_post_mix, g_pre_ffn, w_up, ffn_conv_w, ffn_conv_b, w_down, g_post_ffn):
    depth = w_in.shape[0]
    assert depth == 1 and x_prompt.shape[0] == 1 and x_sample.shape[1] == 1
    T = x_prompt.shape[1]
    B = x_sample.shape[0]
    W = cache_k.shape[2]
    assert W == WINDOW and T % TOKEN_BLOCK == 0 and B % SAMPLE_CHUNK == 0

    row = lambda a: a[0].reshape(1, -1)
    params = (
        row(g_pre_mix), w_in[0], conv_w[0], row(conv_b),
        w_a[0].astype(bf16), row(b_a), w_i[0].astype(bf16), row(b_i), row(lam),
        row(g_attn_out), row(g_rnn_out), w_out[0], row(g_post_mix),
        row(g_pre_ffn),
        w_up[0], ffn_conv_w[0], row(ffn_conv_b), w_down[0],
        row(g_post_ffn),
    )

    c_all = jnp.concatenate(
        [c_sample, jnp.broadcast_to(c_prompt, (MOD_PAD_ROWS, D_MODEL))], axis=0)
    mod = _mod_call(c_all, w_ada[0], b_ada[0].reshape(1, -1))
    mod_s = mod

    yp, kp, vp, hp, convp, ffnp, (w_in_bf, w_out_bf, w_up_bf, w_down_bf) = _prompt_call(
        x_prompt[0], mod, B, sinks[0], params)
    params = list(params)
    params[1], params[11], params[14], params[17] = w_in_bf, w_out_bf, w_up_bf, w_down_bf
    assert params[11].shape == w_out.shape[1:] and params[17].shape == w_down.shape[1:]

    xs = x_sample
    q, kv, xr, yr = _sample_pre_call(xs, mod_s, params[0], params[1])
    ck = cache_k.reshape(B, W * N_KV_HEADS, HEAD_DIM)
    cv = cache_v.reshape(B, W * N_KV_HEADS, HEAD_DIM)
    attn3, kwin, vwin = _sample_attn_call(
        q, kv, ck, cv, sinks[0].reshape(N_HEADS, 1))
    attn = attn3
    ys, hs, convs, ffns = _sample_post_call(
        xs, mod_s, attn, xr, yr, state_h[0],
        jnp.transpose(state_conv[0], (1, 0, 2)), state_ffn_conv[0],
        params)
    convs = jnp.transpose(convs, (1, 0, 2))

    kv_shape = (1, 1, W, N_KV_HEADS, HEAD_DIM)
    kvs_shape = (1, B, W, N_KV_HEADS, HEAD_DIM)
    return (
        yp[None], ys,
        kp.reshape(kv_shape), vp.reshape(kv_shape), hp[None], convp[None, None], ffnp[None, None],
        kwin.reshape(kvs_shape), vwin.reshape(kvs_shape), hs[None], convs[None], ffns[None],
    )
```

```python
import math

import jax
import jax.numpy as jnp
from jax import lax
from jax.experimental import pallas as pl
from jax.experimental.pallas import tpu as pltpu

D_MODEL = 1024
N_HEADS = 8
N_KV_HEADS = 2
HEAD_DIM = 128
GQA_GROUP = N_HEADS // N_KV_HEADS
ATTN_WIDTH = N_HEADS * HEAD_DIM
KV_WIDTH = N_KV_HEADS * HEAD_DIM
WINDOW = 128
D_RNN = D_MODEL
RNN_BLOCKS = 8
RNN_BLOCK_W = D_RNN // RNN_BLOCKS
RG_C = 8.0
RNN_CONV_W = 4
D_FF = 2816
FFN_CONV_W = 3
RMS_EPS = 1e-6

C_K = ATTN_WIDTH
C_V = C_K + KV_WIDTH
C_XR = C_V + KV_WIDTH
C_YR = C_XR + D_RNN
IN_COLS = C_YR + D_RNN

SUBLANES = 8
LANES = 128
SUB_ROWS = WINDOW
GROUPS = SUB_ROWS // SUBLANES
TOKEN_BLOCK = 256
NSUB = TOKEN_BLOCK // SUB_ROWS
FFN_CHUNK = 256
STAGE_SKEW = 3
WSTAGE_ROWS = 64
WSTAGE_SLOTS = 4
SAMPLE_CHUNK = 16
SAMPLE_ATTN_VMEM_BYTES = (2 * 2 * 2 * SAMPLE_CHUNK * N_KV_HEADS * WINDOW * HEAD_DIM * 4
                          + 8 * 1024 * 1024)
MOD_PAD_ROWS = 8
VMEM_LIMIT_BYTES = 56 * 1024 * 1024

ALIBI_SLOPES = tuple(2.0 ** (-8.0 * (h + 1) / N_HEADS) for h in range(N_HEADS))
Q_SCALE = HEAD_DIM ** -0.5
SQRT_2_OVER_PI = math.sqrt(2.0 / math.pi)
LOG2_E = 1.0 / math.log(2.0)
PADDED_OUT_COLS = D_MODEL + 128

bf16 = jnp.bfloat16
f32 = jnp.float32


def _rms_scale(x):
    n = x.shape[-1]
    ssq = jnp.dot((x * x).astype(bf16), jnp.ones((n, LANES), bf16), preferred_element_type=f32)
    r = lax.rsqrt(ssq * (1.0 / n) + RMS_EPS)
    return jnp.concatenate([r] * (n // LANES), axis=-1)


def _rms_norm(x, g):
    return x * _rms_scale(x) * g


def _gelu_tanh(x):
    k1 = -2.0 * LOG2_E * SQRT_2_OVER_PI
    k3 = k1 * 0.044715
    return x / (1.0 + jnp.exp2(x * (k1 + k3 * (x * x))))


def _sigmoid(x):
    return 1.0 / (1.0 + jnp.exp2(x * (-LOG2_E)))


def _softplus_neg(lam):
    return jnp.maximum(-lam, 0.0) + jnp.log1p(jnp.exp(-jnp.abs(lam)))


def _dot(a, b):
    return jnp.dot(a.astype(bf16), b, preferred_element_type=f32)


def _dot_nt(a, b):
    return lax.dot_general(a, b, (((1,), (1,)), ((), ())), preferred_element_type=f32)


def _rglru_gates(xc, wa_ref, ba, wi_ref, bi, neg_c_sp):
    a_parts, u_parts = [], []
    for n in range(RNN_BLOCKS):
        sl = slice(n * RNN_BLOCK_W, (n + 1) * RNN_BLOCK_W)
        xn = xc[:, sl]
        xb = xn.astype(bf16)
        r = _sigmoid(jnp.dot(xb, wa_ref[n], preferred_element_type=f32) + ba[:, sl])
        i = _sigmoid(jnp.dot(xb, wi_ref[n], preferred_element_type=f32) + bi[:, sl])
        log_a = r * neg_c_sp[:, sl]
        t = jnp.tanh(log_a)
        one_minus_a2 = (-2.0 * t) / (1.0 - t)
        a_parts.append(jnp.exp(log_a))
        u_parts.append(jnp.sqrt(one_minus_a2) * (i * xn))
    return a_parts, u_parts


def _mod_kernel(c_ref, w_hbm, b_ref, o_ref, wbuf, sem):
    nblk, _, bn = wbuf.shape
    copies = [pltpu.make_async_copy(w_hbm.at[:, pl.ds(j * bn, bn)], wbuf.at[j], sem.at[j])
              for j in range(nblk)]
    for cp in copies:
        cp.start()
    c = c_ref[...]
    s = (c * _sigmoid(c)).astype(bf16)
    for j in range(nblk):
        copies[j].wait()
        cols = slice(j * bn, (j + 1) * bn)
        o_ref[:, cols] = (jnp.dot(s, wbuf[j].astype(bf16), preferred_element_type=f32)
                          + b_ref[:, cols])


def _mod_call(c_all, w_ada, b_ada):
    rows = c_all.shape[0]
    k, ncol = w_ada.shape
    bn = D_MODEL
    vmem = pl.BlockSpec(memory_space=pltpu.VMEM)
    return pl.pallas_call(
        _mod_kernel,
        in_specs=[vmem, pl.BlockSpec(memory_space=pl.ANY), vmem],
        out_shape=jax.ShapeDtypeStruct((rows, ncol), f32),
        scratch_shapes=[pltpu.VMEM((ncol // bn, k, bn), f32),
                        pltpu.SemaphoreType.DMA((ncol // bn,))],
        compiler_params=pltpu.CompilerParams(
            vmem_limit_bytes=k * ncol * 4 + rows * ncol * 4 + 8 * 1024 * 1024),
        name="adaln_mod",
    )(c_all, w_ada, b_ada)


def _groups(v, c0=None, c1=None):
    if c0 is None:
        return [v[j * SUBLANES:(j + 1) * SUBLANES, :] for j in range(GROUPS)]
    return [v[j * SUBLANES:(j + 1) * SUBLANES, c0:c1] for j in range(GROUPS)]


def _shifted_groups(X, tail_row, sub0, max_shift):
    wrapped = {}
    for j in range(GROUPS - max_shift, GROUPS):
        wrapped[j] = jnp.where(sub0, tail_row(j), pltpu.roll(X[j], 1, axis=0))
    sh = {}
    for d in range(1, max_shift + 1):
        sh[d] = [X[j - d] if j >= d else wrapped[j - d + GROUPS] for j in range(GROUPS)]
    return sh


def _scan_sub(a, u, h_in, sub_iota):
    A = _groups(a)
    L = _groups(u)
    for j in range(1, GROUPS):
        L[j] = A[j] * L[j - 1] + L[j]
        A[j] = A[j] * A[j - 1]
    ae, le = A[GROUPS - 1], L[GROUPS - 1]
    for s_ in (1, 2, 4):
        ok = sub_iota >= s_
        a_sh = pltpu.roll(ae, s_, axis=0)
        l_sh = pltpu.roll(le, s_, axis=0)
        le = jnp.where(ok, ae * l_sh + le, le)
        ae = jnp.where(ok, ae * a_sh, ae)
    hend = le + ae * h_in
    hprev = jnp.where(sub_iota == 0, h_in, pltpu.roll(hend, 1, axis=0))
    h = jnp.concatenate([L[j] + A[j] * hprev for j in range(GROUPS)], axis=0)
    h_out = jnp.broadcast_to(hend[SUBLANES - 1:SUBLANES, :], hend.shape)
    return h, h_out


def _prompt_kernel(
    x_hbm, mod_ref, sinks_ref,
    g_pre_mix_ref, w_in_hbm, conv_w_ref, conv_b_ref, wa_ref, ba_ref, wi_ref, bi_ref, lam_ref,
    g_attn_ref, g_rnn_ref, w_out_hbm, g_post_mix_ref, g_pre_ffn_ref, w_up_hbm,
    fconv_w_ref, fconv_b_ref, w_down_hbm, g_post_ffn_ref,
    y_hbm, kwin_hbm, vwin_hbm, hlast_ref, convst_ref, ffnst_ref,
    w_in_bf_hbm, w_out_bf_hbm, w_up_bf_hbm, w_down_bf_hbm,
    xbuf, ybuf, sem_in, sem_out, sem_st, kprev, vprev, kst, vst, xr_tail, up_tail, hcar, bias_s,
    hff_s, w_in_ref, w_out_ref, w_up_ref, w_down_ref, wstage, sem_w, sem_wout,
):
    step = pl.program_id(0)
    nblocks = pl.num_programs(0) - 1
    slot = lax.rem(step, 2)
    yslot = lax.rem(step, 3)
    fslot = lax.rem(step + 2, 3)

    def x_copies(st, sl):
        return [pltpu.make_async_copy(
            x_hbm.at[st * NSUB + s, c, :, :],
            xbuf.at[sl, pl.ds(s * GROUPS, GROUPS), c, :],
            sem_in.at[sl]) for s in range(NSUB) for c in range(SUBLANES)]

    def y_copies(st, sl):
        return [pltpu.make_async_copy(
            ybuf.at[sl, pl.ds(s * GROUPS, GROUPS), c, :],
            y_hbm.at[st * NSUB + s, c, :, :],
            sem_out.at[sl]) for s in range(NSUB) for c in range(SUBLANES)]

    def state_copies():
        cps = []
        for src, dst in ((kst, kwin_hbm), (vst, vwin_hbm)):
            cps += [pltpu.make_async_copy(src.at[pl.ds(j * SUBLANES, SUBLANES), :],
                                          dst.at[:, j, :], sem_st) for j in range(GROUPS)]
        return cps

    def weight_out_copies():
        return [pltpu.make_async_copy(w_in_ref, w_in_bf_hbm, sem_wout),
                pltpu.make_async_copy(w_up_ref, w_up_bf_hbm, sem_wout),
                pltpu.make_async_copy(w_out_ref.at[:, pl.ds(0, D_MODEL)], w_out_bf_hbm, sem_wout),
                pltpu.make_async_copy(w_down_ref.at[:, pl.ds(0, D_MODEL)], w_down_bf_hbm, sem_wout)]

    def load_cast(w_hbm, w_s, stage, chunk_rows):
        tiled = len(stage.shape) == 4
        if tiled:
            n_rows, n_cols = w_hbm.shape[0] * SUBLANES, w_hbm.shape[2]
            assert stage.shape[1:] == (chunk_rows // SUBLANES, SUBLANES, n_cols)
        else:
            n_rows, n_cols = w_hbm.shape
        n = n_rows // chunk_rows
        depth = stage.shape[0]
        assert n >= depth

        def chunk(i, sl):
            if tiled:
                g = chunk_rows // SUBLANES
                return pltpu.make_async_copy(w_hbm.at[pl.ds(i * g, g), :, :], stage.at[sl],
                                             sem_w.at[sl])
            return pltpu.make_async_copy(
                w_hbm.at[pl.ds(i * chunk_rows, chunk_rows), :],
                stage.at[sl, pl.ds(0, chunk_rows), pl.ds(0, n_cols)], sem_w.at[sl])

        def staged(sl):
            if tiled:
                return stage[sl].reshape(chunk_rows, n_cols)
            return stage[sl, 0:chunk_rows, 0:n_cols]

        for i in range(depth):
            chunk(i, i).start()

        def body(i, carry):
            sl = lax.rem(i, depth)
            chunk(i, sl).wait()
            r0 = pl.multiple_of(i * chunk_rows, chunk_rows)
            w_s[pl.ds(r0, chunk_rows), 0:n_cols] = staged(sl).astype(bf16)

            @pl.when(i + depth < n)
            def _next():
                chunk(i + depth, sl).start()

            return carry

        lax.fori_loop(0, n, body, 0)

    @pl.when(step == 0)
    def _init():
        for cp in x_copies(0, 0):
            cp.start()
        load_cast(w_in_hbm, w_in_ref, wstage, WSTAGE_ROWS)
        load_cast(w_out_hbm, w_out_ref, ybuf, TOKEN_BLOCK)
        load_cast(w_up_hbm, w_up_ref, wstage, WSTAGE_ROWS)
        load_cast(w_down_hbm, w_down_ref, ybuf, TOKEN_BLOCK)
        for cp in weight_out_copies():
            cp.start()
        kprev[...] = jnp.zeros(kprev.shape, bf16)
        vprev[...] = jnp.zeros(vprev.shape, bf16)
        xr_tail[...] = jnp.zeros(xr_tail.shape, f32)
        up_tail[...] = jnp.zeros(up_tail.shape, f32)
        hcar[...] = jnp.zeros(hcar.shape, f32)
        hff_s[...] = jnp.zeros(hff_s.shape, bf16)
        ybuf[2] = jnp.zeros(ybuf.shape[1:], f32)
        rq = lax.broadcasted_iota(jnp.int32, (SUB_ROWS, 2 * SUB_ROWS), 0)
        ck = lax.broadcasted_iota(jnp.int32, (SUB_ROWS, 2 * SUB_ROWS), 1)
        rk = ck & (SUB_ROWS - 1)
        tq = (rq % SUBLANES) * GROUPS + rq // SUBLANES
        tk = (rk % SUBLANES) * GROUPS + rk // SUBLANES
        dist = tq - tk + jnp.where(ck < SUB_ROWS, WINDOW, 0)
        base = jnp.where((dist >= 0) & (dist < WINDOW), dist.astype(f32), jnp.inf)
        for h in range(N_HEADS):
            bias_s[h] = (ALIBI_SLOPES[h] * LOG2_E) * base

    @pl.when(step + 1 < nblocks)
    def _prefetch():
        for cp in x_copies(step + 1, 1 - slot):
            cp.start()

    @pl.when(step >= 3)
    def _free_ybuf():
        for cp in y_copies(0, yslot):
            cp.wait()

    @pl.when(step < nblocks)
    def _wait_x():
        for cp in x_copies(step, slot):
            cp.wait()

    sh1 = mod_ref[0:1, 0 * D_MODEL:1 * D_MODEL]
    sc1 = mod_ref[0:1, 1 * D_MODEL:2 * D_MODEL]
    g1 = mod_ref[0:1, 2 * D_MODEL:3 * D_MODEL]
    sh2 = mod_ref[0:1, 3 * D_MODEL:4 * D_MODEL]
    sc2 = mod_ref[0:1, 4 * D_MODEL:5 * D_MODEL]
    g2 = mod_ref[0:1, 5 * D_MODEL:6 * D_MODEL]
    gs1 = g_pre_mix_ref[...] * (1.0 + sc1)
    gs2 = g_pre_ffn_ref[...] * (1.0 + sc2)
    gg1 = g1 * g_post_mix_ref[...]
    gg2 = g2 * g_post_ffn_ref[...]
    sp = (-RG_C) * _softplus_neg(lam_ref[...])
    first_pen = jnp.where(step == 0, jnp.inf, 0.0)
    sub_iota = lax.broadcasted_iota(jnp.int32, (SUBLANES, D_RNN), 0)
    sub0_rnn = sub_iota == 0
    sub0_ffn = lax.broadcasted_iota(jnp.int32, (SUBLANES, FFN_CHUNK), 0) == 0
    conv_w = [conv_w_ref.at[jj:jj + 1, :] for jj in range(RNN_CONV_W)]

    sts = [dict() for _ in range(NSUB)]

    def rows(s):
        return pl.ds(s * SUB_ROWS, SUB_ROWS)

    def groups(s):
        return pl.ds(s * GROUPS, GROUPS)

    def stage_pre(s):
        x = xbuf[slot, groups(s)].reshape(SUB_ROWS, D_MODEL)
        sts[s]["hmix"] = (x * _rms_scale(x) * gs1 + sh1).astype(bf16)

    def stage_inproj(s):
        st = sts[s]
        hm = st["hmix"]
        st["q"] = (jnp.dot(hm, w_in_ref[:, 0:C_K], preferred_element_type=f32)
                   * (Q_SCALE * LOG2_E)).astype(bf16)
        kv = jnp.dot(hm, w_in_ref[:, C_K:C_XR], preferred_element_type=f32)
        st["kb"] = kv[:, 0:KV_WIDTH].astype(bf16)
        st["vb"] = kv[:, KV_WIDTH:2 * KV_WIDTH].astype(bf16)
        st["kv"] = kv
        st["xr"] = jnp.dot(hm, w_in_ref[:, C_XR:C_YR], preferred_element_type=f32)

    def stage_attn(s):
        st = sts[s]
        if s == 0:
            kp, vp = kprev[...], vprev[...]
        else:
            kp, vp = sts[s - 1]["kb"], sts[s - 1]["vb"]
        kw = jnp.concatenate([kp, st["kb"]], axis=0)
        vw = jnp.concatenate([vp, st["vb"]], axis=0)
        outs = []
        for h in range(N_HEADS):
            c = h // GQA_GROUP
            hs = slice(h * HEAD_DIM, (h + 1) * HEAD_DIM)
            cs = slice(c * HEAD_DIM, (c + 1) * HEAD_DIM)
            sc = _dot_nt(st["q"][:, hs], kw[:, cs])
            if s == 0:
                sc = jnp.concatenate([sc[:, 0:SUB_ROWS] - first_pen, sc[:, SUB_ROWS:]], axis=1)
            sc = sc - bias_s[h]
            sink = sinks_ref[h] * LOG2_E
            m = jnp.maximum(jnp.max(sc, axis=-1, keepdims=True), sink)
            p = jnp.exp2(sc - m)
            denom = jnp.sum(p, axis=-1, keepdims=True) + jnp.exp2(sink - m)
            outs.append(jnp.dot(p.astype(bf16), vw[:, cs], preferred_element_type=f32) / denom)
        attn = jnp.concatenate(outs, axis=1)
        st["attn_n"] = _rms_norm(attn, g_attn_ref[...]).astype(bf16)

    def stage_rnn(s):
        st = sts[s]
        xr = st["xr"]
        X = _groups(xr)
        if s == 0:
            tail = lambda j: xr_tail[(j - (GROUPS - RNN_CONV_W + 1)) * SUBLANES + SUBLANES - 1:
                                     (j - (GROUPS - RNN_CONV_W + 1)) * SUBLANES + SUBLANES, :]
        else:
            pxr = sts[s - 1]["xr"]
            tail = lambda j: pxr[j * SUBLANES + SUBLANES - 1:(j + 1) * SUBLANES, :]
        sh = _shifted_groups(X, tail, sub0_rnn, RNN_CONV_W - 1)
        xcs = []
        for j in range(GROUPS):
            acc = conv_b_ref[...] + conv_w[RNN_CONV_W - 1][...] * X[j]
            for d in range(1, RNN_CONV_W):
                acc = acc + conv_w[RNN_CONV_W - 1 - d][...] * sh[d][j]
            xcs.append(acc)
        xc = jnp.concatenate(xcs, axis=0)
        a_parts, u_parts = _rglru_gates(xc, wa_ref, ba_ref[...], wi_ref, bi_ref[...], sp)
        a = jnp.concatenate(a_parts, axis=1)
        u = jnp.concatenate(u_parts, axis=1)
        h_in = hcar[...] if s == 0 else sts[s - 1]["h_out"]
        h, st["h_out"] = _scan_sub(a, u, h_in, sub_iota)
        yr = jnp.dot(st["hmix"], w_in_ref[:, C_YR:IN_COLS], preferred_element_type=f32)
        rnn = h * _gelu_tanh(yr)
        st["rnn_n"] = _rms_norm(rnn, g_rnn_ref[...]).astype(bf16)

    def stage_out(s):
        st = sts[s]
        mo = (jnp.dot(st["attn_n"], w_out_ref[0:ATTN_WIDTH, 0:D_MODEL], preferred_element_type=f32)
              + jnp.dot(st["rnn_n"], w_out_ref[ATTN_WIDTH:ATTN_WIDTH + D_RNN, 0:D_MODEL],
                        preferred_element_type=f32))
        x1 = (xbuf[slot, groups(s)].reshape(SUB_ROWS, D_MODEL)
              + mo * _rms_scale(mo) * gg1)
        ybuf[yslot, groups(s)] = x1.reshape(GROUPS, SUBLANES, D_MODEL)
        hff_s[slot, rows(s), :] = (x1 * _rms_scale(x1) * gs2 + sh2).astype(bf16)

    tail_g0 = GROUPS - (FFN_CONV_W - 1)
    hff = hff_s[1 - slot]
    acts, tails = [], {}

    def conv_chunk(c0):
        c1 = c0 + FFN_CHUNK
        up = jnp.dot(hff, w_up_ref[:, c0:c1], preferred_element_type=f32)
        w = [fconv_w_ref.at[jj:jj + 1, c0:c1] for jj in range(FFN_CONV_W)]
        outs = []
        for s in range(NSUB):
            r0 = s * SUB_ROWS
            X = [up[r0 + j * SUBLANES:r0 + (j + 1) * SUBLANES, :] for j in range(GROUPS)]
            if s == 0:
                tail = lambda j: up_tail[(j - tail_g0) * SUBLANES + SUBLANES - 1:
                                         (j - tail_g0 + 1) * SUBLANES, c0:c1]
            else:
                tail = lambda j, p0=r0 - SUB_ROWS: up[p0 + j * SUBLANES + SUBLANES - 1:
                                                      p0 + (j + 1) * SUBLANES, :]
            sh = _shifted_groups(X, tail, sub0_ffn, FFN_CONV_W - 1)
            for j in range(GROUPS):
                acc = fconv_b_ref[:, c0:c1] + w[FFN_CONV_W - 1][...] * X[j]
                for d in range(1, FFN_CONV_W):
                    acc = acc + w[FFN_CONV_W - 1 - d][...] * sh[d][j]
                outs.append(acc)
        last0 = (NSUB - 1) * SUB_ROWS
        for t in range(FFN_CONV_W - 1):
            r = last0 + (tail_g0 + t) * SUBLANES + SUBLANES - 1
            ffnst_ref[t:t + 1, c0:c1] = up[r:r + 1, :]
        tails[c0] = up[last0 + tail_g0 * SUBLANES:, :]
        return jnp.concatenate(outs, axis=0)

    def ffn_chunk(cc):
        gate = conv_chunk(cc * FFN_CHUNK)
        val = conv_chunk(D_FF + cc * FFN_CHUNK)
        acts.append((_gelu_tanh(gate) * val).astype(bf16))

    def ffn_finish():
        for c0, tl in tails.items():
            up_tail[:, c0:c0 + FFN_CHUNK] = tl
        act = jnp.concatenate(acts, axis=1)
        f = jnp.dot(act, w_down_ref[:, 0:D_MODEL], preferred_element_type=f32)
        y = ybuf[fslot].reshape(TOKEN_BLOCK, D_MODEL) + f * _rms_scale(f) * gg2
        ybuf[fslot] = y.reshape(NSUB * GROUPS, SUBLANES, D_MODEL)

    stages = (stage_pre, stage_inproj, stage_attn, stage_rnn, stage_out)
    n_times = len(stages) + (NSUB - 1) * STAGE_SKEW
    n_chunks = D_FF // FFN_CHUNK
    ffn_chunk(0)
    nxt = 1
    for t in range(n_times):
        for s in range(NSUB):
            k = t - s * STAGE_SKEW
            if 0 <= k < len(stages):
                stages[k](s)
                if nxt < n_chunks:
                    ffn_chunk(nxt)
                    nxt += 1
    while nxt < n_chunks:
        ffn_chunk(nxt)
        nxt += 1
    ffn_finish()

    lst = sts[NSUB - 1]
    kprev[...] = lst["kb"]
    vprev[...] = lst["vb"]
    xr_tail[...] = lst["xr"][(GROUPS - (RNN_CONV_W - 1)) * SUBLANES:, :]
    hcar[...] = lst["h_out"]

    @pl.when(step < nblocks)
    def _state():
        kst[...] = lst["kv"][:, 0:KV_WIDTH]
        vst[...] = lst["kv"][:, KV_WIDTH:2 * KV_WIDTH]
        for t in range(RNN_CONV_W - 1):
            r = (GROUPS - (RNN_CONV_W - 1) + t) * SUBLANES + SUBLANES - 1
            convst_ref[t:t + 1, :] = lst["xr"][r:r + 1, :]
        hlast_ref[...] = lst["h_out"][0:1, :]

    @pl.when(step >= 1)
    def _store_y():
        for cp in y_copies(step - 1, fslot):
            cp.start()

    @pl.when(step == nblocks)
    def _finish():
        for cp in state_copies():
            cp.start()
        for cp in state_copies() + weight_out_copies():
            cp.wait()
        for cp in y_copies(0, lax.rem(step + 1, 3)) + y_copies(0, fslot):
            cp.wait()


def _const_spec(shape):
    nd = len(shape)
    return pl.BlockSpec(shape, lambda i: (0,) * nd)


def _prompt_call(x, mod, mod_row0, sinks, params):
    T = x.shape[0]
    TB = TOKEN_BLOCK
    assert T // TB >= 3
    (g_pre_mix, w_in, conv_w, conv_b, wa, ba, wi, bi, lam, g_attn, g_rnn, w_out,
     g_post_mix, g_pre_ffn, w_up, fconv_w, fconv_b, w_down, g_post_ffn) = params
    x4 = x.reshape(T // SUB_ROWS, SUBLANES, GROUPS, D_MODEL)
    w_out3 = w_out.reshape(-1, SUBLANES, D_MODEL)
    w_down3 = w_down.reshape(-1, SUBLANES, D_MODEL)
    ins = [x4, mod, sinks, g_pre_mix, w_in, conv_w, conv_b, wa, ba, wi, bi, lam, g_attn, g_rnn,
           w_out3, g_post_mix, g_pre_ffn, w_up, fconv_w, fconv_b, w_down3, g_post_ffn]
    assert mod_row0 % MOD_PAD_ROWS == 0
    in_specs = [pl.BlockSpec(memory_space=pl.ANY),
                pl.BlockSpec((MOD_PAD_ROWS, mod.shape[1]), lambda i: (mod_row0 // MOD_PAD_ROWS, 0)),
                pl.BlockSpec(memory_space=pltpu.SMEM)]
    big = (w_in, w_out, w_up, w_down)
    assert all(w.dtype == f32 and w.shape[0] % TOKEN_BLOCK == 0 for w in big)
    in_specs += [pl.BlockSpec(memory_space=pl.ANY)
                 if any(a is w for w in (w_in, w_out3, w_up, w_down3))
                 else _const_spec(a.shape) for a in ins[3:]]
    out_shape = (
        jax.ShapeDtypeStruct(x4.shape, f32),
        jax.ShapeDtypeStruct((SUBLANES, GROUPS, KV_WIDTH), f32),
        jax.ShapeDtypeStruct((SUBLANES, GROUPS, KV_WIDTH), f32),
        jax.ShapeDtypeStruct((1, D_RNN), f32),
        jax.ShapeDtypeStruct((RNN_CONV_W - 1, D_RNN), f32),
        jax.ShapeDtypeStruct((FFN_CONV_W - 1, 2 * D_FF), f32),
    ) + tuple(jax.ShapeDtypeStruct(w.shape, bf16) for w in big)
    out_specs = (
        pl.BlockSpec(memory_space=pl.ANY),
        pl.BlockSpec(memory_space=pl.ANY),
        pl.BlockSpec(memory_space=pl.ANY),
        _const_spec((1, D_RNN)),
        _const_spec((RNN_CONV_W - 1, D_RNN)),
        _const_spec((FFN_CONV_W - 1, 2 * D_FF)),
    ) + (pl.BlockSpec(memory_space=pl.ANY),) * len(big)
    scratch = [
        pltpu.VMEM((2, TB // SUBLANES, SUBLANES, D_MODEL), f32),
        pltpu.VMEM((3, TB // SUBLANES, SUBLANES, D_MODEL), f32),
        pltpu.SemaphoreType.DMA((2,)),
        pltpu.SemaphoreType.DMA((3,)),
        pltpu.SemaphoreType.DMA(()),
        pltpu.VMEM((SUB_ROWS, KV_WIDTH), bf16),
        pltpu.VMEM((SUB_ROWS, KV_WIDTH), bf16),
        pltpu.VMEM((SUB_ROWS, KV_WIDTH), f32),
        pltpu.VMEM((SUB_ROWS, KV_WIDTH), f32),
        pltpu.VMEM(((RNN_CONV_W - 1) * SUBLANES, D_RNN), f32),
        pltpu.VMEM(((FFN_CONV_W - 1) * SUBLANES, 2 * D_FF), f32),
        pltpu.VMEM((SUBLANES, D_RNN), f32),
        pltpu.VMEM((N_HEADS, SUB_ROWS, 2 * SUB_ROWS), f32),
        pltpu.VMEM((2, TOKEN_BLOCK, D_MODEL), bf16),
        pltpu.VMEM(w_in.shape, bf16),
        pltpu.VMEM((w_out.shape[0], PADDED_OUT_COLS), bf16),
        pltpu.VMEM(w_up.shape, bf16),
        pltpu.VMEM((w_down.shape[0], PADDED_OUT_COLS), bf16),
        pltpu.VMEM((WSTAGE_SLOTS, WSTAGE_ROWS, w_up.shape[1]), f32),
        pltpu.SemaphoreType.DMA((WSTAGE_SLOTS,)),
        pltpu.SemaphoreType.DMA(()),
    ]
    assert w_in.shape[1] <= w_up.shape[1] and w_in.shape[0] % WSTAGE_ROWS == 0
    y4, kp, vp, hp, convp, ffnp, w_in_bf, w_out_bf, w_up_bf, w_down_bf = pl.pallas_call(
        _prompt_kernel,
        grid=(T // TB + 1,),
        in_specs=in_specs,
        out_specs=out_specs,
        out_shape=out_shape,
        scratch_shapes=scratch,
        compiler_params=pltpu.CompilerParams(
            dimension_semantics=("arbitrary",), vmem_limit_bytes=VMEM_LIMIT_BYTES),
        name="prompt_layer",
    )(*ins)
    return (y4.reshape(T, D_MODEL), kp.reshape(WINDOW, KV_WIDTH), vp.reshape(WINDOW, KV_WIDTH),
            hp, convp, ffnp, (w_in_bf, w_out_bf, w_up_bf, w_down_bf))


def _sample_pre_kernel(x_ref, mod_ref, g_pre_mix_ref, w_in_hbm, q_ref, kv_ref, xr_ref, yr_ref,
                       w_in_ref, sem):
    halves = [pltpu.make_async_copy(w_in_hbm.at[:, pl.ds(c0, c1 - c0)],
                                    w_in_ref.at[:, pl.ds(c0, c1 - c0)], sem.at[i])
              for i, (c0, c1) in enumerate(((0, C_XR), (C_XR, IN_COLS)))]
    for cp in halves:
        cp.start()
    B = x_ref.shape[0]
    sh1 = mod_ref[0:B, 0 * D_MODEL:1 * D_MODEL]
    sc1 = mod_ref[0:B, 1 * D_MODEL:2 * D_MODEL]
    hmix = (_rms_norm(x_ref[:, 0, :], g_pre_mix_ref[...]) * (1.0 + sc1) + sh1).astype(bf16)
    halves[0].wait()
    q = jnp.dot(hmix, w_in_ref[:, 0:C_K], preferred_element_type=f32) * Q_SCALE
    for h in range(N_HEADS):
        q_ref[:, h, :] = q[:, h * HEAD_DIM:(h + 1) * HEAD_DIM]
    kv_ref[...] = jnp.dot(hmix, w_in_ref[:, C_K:C_XR], preferred_element_type=f32)
    halves[1].wait()
    xr_ref[...] = jnp.dot(hmix, w_in_ref[:, C_XR:C_YR], preferred_element_type=f32)
    yr_ref[...] = jnp.dot(hmix, w_in_ref[:, C_YR:IN_COLS], preferred_element_type=f32)


def _sample_pre_call(x, mod_s, g_pre_mix, w_in):
    B = x.shape[0]
    vmem = pl.BlockSpec(memory_space=pltpu.VMEM)
    return pl.pallas_call(
        _sample_pre_kernel,
        in_specs=[vmem, vmem, vmem, pl.BlockSpec(memory_space=pl.ANY)],
        out_shape=(
            jax.ShapeDtypeStruct((B, N_HEADS, HEAD_DIM), f32),
            jax.ShapeDtypeStruct((B, 2 * KV_WIDTH), f32),
            jax.ShapeDtypeStruct((B, D_RNN), f32),
            jax.ShapeDtypeStruct((B, D_RNN), f32),
        ),
        scratch_shapes=[pltpu.VMEM(w_in.shape, w_in.dtype), pltpu.SemaphoreType.DMA((2,))],
        compiler_params=pltpu.CompilerParams(vmem_limit_bytes=VMEM_LIMIT_BYTES),
        name="sample_pre",
    )(x, mod_s, g_pre_mix, w_in)


def _sample_attn_kernel(q_ref, kv_ref, ck_ref, cv_ref, sinks_ref, o_ref, kwin_ref, vwin_ref):
    R = N_KV_HEADS * WINDOW
    hrow = lax.broadcasted_iota(jnp.int32, (N_HEADS, R), 0)
    rcol = lax.broadcasted_iota(jnp.int32, (N_HEADS, R), 1)
    slope = jnp.exp2(-8.0 * (hrow + 1).astype(f32) / N_HEADS)
    own = (rcol % N_KV_HEADS) == (hrow // GQA_GROUP)
    bias = jnp.where(own, slope * (WINDOW - 1 - rcol // N_KV_HEADS).astype(f32), jnp.inf)
    wrow = lax.broadcasted_iota(jnp.int32, (R, HEAD_DIM), 0)
    sink = sinks_ref[...]

    def shifted(cache, new_rows):
        out = pltpu.roll(cache, R - N_KV_HEADS, axis=0)
        for c in range(N_KV_HEADS):
            out = jnp.where(wrow == R - N_KV_HEADS + c, new_rows[c], out)
        return out

    scores = []
    for b in range(SAMPLE_CHUNK):
        knew = [kv_ref[b:b + 1, c * HEAD_DIM:(c + 1) * HEAD_DIM] for c in range(N_KV_HEADS)]
        kw = shifted(ck_ref[b], knew)
        kwin_ref[b] = kw
        scores.append(_dot_nt(q_ref[b].astype(bf16), kw.astype(bf16)) - bias)
    probs, denoms = [], []
    for b in range(SAMPLE_CHUNK):
        s = scores[b]
        m = jnp.maximum(jnp.max(s, axis=-1, keepdims=True), sink)
        p = jnp.exp(s - m)
        denoms.append(jnp.sum(p, axis=-1, keepdims=True) + jnp.exp(sink - m))
        probs.append(p.astype(bf16))
    for b in range(SAMPLE_CHUNK):
        vnew = [kv_ref[b:b + 1, KV_WIDTH + c * HEAD_DIM:KV_WIDTH + (c + 1) * HEAD_DIM]
                for c in range(N_KV_HEADS)]
        vw = shifted(cv_ref[b], vnew)
        vwin_ref[b] = vw
        o_ref[b] = (jnp.dot(probs[b], vw.astype(bf16), preferred_element_type=f32)
                    / denoms[b])


def _sample_attn_call(q, kv, ck, cv, sinks_col):
    B = q.shape[0]
    BC = SAMPLE_CHUNK
    return pl.pallas_call(
        _sample_attn_kernel,
        grid=(B // BC,),
        in_specs=[
            pl.BlockSpec((BC, N_HEADS, HEAD_DIM), lambda i: (i, 0, 0)),
            pl.BlockSpec((BC, 2 * KV_WIDTH), lambda i: (i, 0)),
            pl.BlockSpec((BC, N_KV_HEADS * WINDOW, HEAD_DIM), lambda i: (i, 0, 0)),
            pl.BlockSpec((BC, N_KV_HEADS * WINDOW, HEAD_DIM), lambda i: (i, 0, 0)),
            pl.BlockSpec((N_HEADS, 1), lambda i: (0, 0)),
        ],
        out_specs=(
            pl.BlockSpec((BC, N_HEADS, HEAD_DIM), lambda i: (i, 0, 0)),
            pl.BlockSpec((BC, N_KV_HEADS * WINDOW, HEAD_DIM), lambda i: (i, 0, 0)),
            pl.BlockSpec((BC, N_KV_HEADS * WINDOW, HEAD_DIM), lambda i: (i, 0, 0)),
        ),
        out_shape=(
            jax.ShapeDtypeStruct((B, N_HEADS, HEAD_DIM), f32),
            jax.ShapeDtypeStruct((B, N_KV_HEADS * WINDOW, HEAD_DIM), f32),
            jax.ShapeDtypeStruct((B, N_KV_HEADS * WINDOW, HEAD_DIM), f32),
        ),
        compiler_params=pltpu.CompilerParams(
            dimension_semantics=("arbitrary",), vmem_limit_bytes=SAMPLE_ATTN_VMEM_BYTES),
        name="sample_attn",
    )(q, kv, ck, cv, sinks_col)


def _sample_post_kernel(
    x_ref, mod_ref, attn_ref, xr_ref, yr_ref, h0_ref, cbuf_ref, fbuf_hbm,
    conv_w_ref, conv_b_ref, wa_ref, ba_ref, wi_ref, bi_ref, lam_ref,
    g_attn_ref, g_rnn_ref, w_out_ref, g_post_mix_ref, g_pre_ffn_ref, w_up_hbm,
    fconv_w_ref, fconv_b_ref, w_down_hbm, g_post_ffn_ref,
    y_ref, h_ref, cst_ref, fst_ref,
    w_up_ref, w_down_ref, fbuf_ref, sem_ffn,
):
    up_copy = pltpu.make_async_copy(w_up_hbm, w_up_ref, sem_ffn.at[0])
    down_copy = pltpu.make_async_copy(w_down_hbm, w_down_ref, sem_ffn.at[1])
    fbuf_copy = pltpu.make_async_copy(fbuf_hbm, fbuf_ref, sem_ffn.at[2])
    up_copy.start()
    fbuf_copy.start()
    down_copy.start()

    B = x_ref.shape[0]
    g1 = mod_ref[0:B, 2 * D_MODEL:3 * D_MODEL]
    sh2 = mod_ref[0:B, 3 * D_MODEL:4 * D_MODEL]
    sc2 = mod_ref[0:B, 4 * D_MODEL:5 * D_MODEL]
    g2 = mod_ref[0:B, 5 * D_MODEL:6 * D_MODEL]
    x = x_ref[:, 0, :]
    xr = xr_ref[...]

    xc = conv_b_ref[...] + conv_w_ref[RNN_CONV_W - 1:RNN_CONV_W, :] * xr
    for jj in range(RNN_CONV_W - 1):
        xc = xc + conv_w_ref[jj:jj + 1, :] * cbuf_ref[jj]
    for jj in range(RNN_CONV_W - 2):
        cst_ref[jj] = cbuf_ref[jj + 1]
    cst_ref[RNN_CONV_W - 2] = xr

    sp = (-RG_C) * _softplus_neg(lam_ref[...])
    a_parts, u_parts = _rglru_gates(xc, wa_ref, ba_ref[...], wi_ref, bi_ref[...], sp)
    a = jnp.concatenate(a_parts, axis=1)
    u = jnp.concatenate(u_parts, axis=1)
    h = a * h0_ref[...] + u
    h_ref[...] = h
    rnn = h * _gelu_tanh(yr_ref[...])

    attn = jnp.concatenate([attn_ref[:, h, :] for h in range(N_HEADS)], axis=1)
    attn_n = _rms_norm(attn, g_attn_ref[...])
    rnn_n = _rms_norm(rnn, g_rnn_ref[...])
    mo = (_dot(attn_n, w_out_ref[0:ATTN_WIDTH, 0:D_MODEL])
          + _dot(rnn_n, w_out_ref[ATTN_WIDTH:ATTN_WIDTH + D_RNN, 0:D_MODEL]))
    x1 = x + g1 * _rms_norm(mo, g_post_mix_ref[...])

    hff = (_rms_norm(x1, g_pre_ffn_ref[...]) * (1.0 + sc2) + sh2).astype(bf16)
    up_copy.wait()
    up_pre = jnp.dot(hff, w_up_ref[...], preferred_element_type=f32)
    fbuf_copy.wait()
    up = fconv_b_ref[...] + fconv_w_ref[FFN_CONV_W - 1:FFN_CONV_W, :] * up_pre
    for jj in range(FFN_CONV_W - 1):
        up = up + fconv_w_ref[jj:jj + 1, :] * fbuf_ref[:, jj, :]
    for jj in range(FFN_CONV_W - 2):
        fst_ref[:, jj, :] = fbuf_ref[:, jj + 1, :]
    fst_ref[:, FFN_CONV_W - 2, :] = up_pre

    act = (_gelu_tanh(up[:, 0:D_FF]) * up[:, D_FF:2 * D_FF]).astype(bf16)
    down_copy.wait()
    f = jnp.dot(act, w_down_ref[:, 0:D_MODEL], preferred_element_type=f32)
    y_ref[:, 0, :] = x1 + g2 * _rms_norm(f, g_post_ffn_ref[...])


def _sample_post_call(x, mod_s, attn, xr, yr, h0, cbuf, fbuf, params):
    B = x.shape[0]
    (_, _, conv_w, conv_b, wa, ba, wi, bi, lam, g_attn, g_rnn, w_out,
     g_post_mix, g_pre_ffn, w_up, fconv_w, fconv_b, w_down, g_post_ffn) = params
    ins = (x, mod_s, attn, xr, yr, h0, cbuf, fbuf, conv_w, conv_b, wa, ba, wi, bi, lam,
           g_attn, g_rnn, w_out, g_post_mix, g_pre_ffn, w_up, fconv_w, fconv_b, w_down, g_post_ffn)
    return pl.pallas_call(
        _sample_post_kernel,
        in_specs=[pl.BlockSpec(memory_space=pl.ANY) if (a is w_up or a is w_down or a is fbuf)
                  else pl.BlockSpec(memory_space=pltpu.VMEM) for a in ins],
        out_shape=(
            jax.ShapeDtypeStruct((B, 1, D_MODEL), f32),
            jax.ShapeDtypeStruct((B, D_RNN), f32),
            jax.ShapeDtypeStruct((RNN_CONV_W - 1, B, D_RNN), f32),
            jax.ShapeDtypeStruct((B, FFN_CONV_W - 1, 2 * D_FF), f32),
        ),
        scratch_shapes=[pltpu.VMEM(w_up.shape, w_up.dtype), pltpu.VMEM(w_down.shape, w_down.dtype),
                        pltpu.VMEM(fbuf.shape, fbuf.dtype), pltpu.SemaphoreType.DMA((3,))],
        compiler_params=pltpu.CompilerParams(vmem_limit_bytes=VMEM_LIMIT_BYTES),
        name="sample_post",
    )(*ins)


def kernel(x_prompt, x_sample, cache_k, cache_v, state_h, state_conv, state_ffn_conv, c_prompt, c_sample, w_ada, b_ada, g_pre_mix, w_in, conv_w, conv_b, w_a, b_a, w_i, b_i, lam, sinks, g_attn_out, g_rnn_out, w_out, g_post_mix, g_pre_ffn, w_up, ffn_conv_w, ffn_conv_b, w_down, g_post_ffn):
    depth = w_in.shape[0]
    assert depth == 1 and x_prompt.shape[0] == 1 and x_sample.shape[1] == 1
    T = x_prompt.shape[1]
    B = x_sample.shape[0]
    W = cache_k.shape[2]
    assert W == WINDOW and T % TOKEN_BLOCK == 0 and B % SAMPLE_CHUNK == 0

    row = lambda a: a[0].reshape(1, -1)
    params = (
        row(g_pre_mix), w_in[0], conv_w[0], row(conv_b),
        w_a[0].astype(bf16), row(b_a), w_i[0].astype(bf16), row(b_i), row(lam),
        row(g_attn_out), row(g_rnn_out), w_out[0], row(g_post_mix),
        row(g_pre_ffn),
        w_up[0], ffn_conv_w[0], row(ffn_conv_b), w_down[0],
        row(g_post_ffn),
    )

    c_all = jnp.concatenate(
        [c_sample, jnp.broadcast_to(c_prompt, (MOD_PAD_ROWS, D_MODEL))], axis=0)
    mod = _mod_call(c_all, w_ada[0], b_ada[0].reshape(1, -1))
    mod_s = mod

    yp, kp, vp, hp, convp, ffnp, (w_in_bf, w_out_bf, w_up_bf, w_down_bf) = _prompt_call(
        x_prompt[0], mod, B, sinks[0], params)
    params = list(params)
    params[1], params[11], params[14], params[17] = w_in_bf, w_out_bf, w_up_bf, w_down_bf
    assert params[11].shape == w_out.shape[1:] and params[17].shape == w_down.shape[1:]

    xs = x_sample
    q, kv, xr, yr = _sample_pre_call(xs, mod_s, params[0], params[1])
    ck = cache_k.reshape(B, W * N_KV_HEADS, HEAD_DIM)
    cv = cache_v.reshape(B, W * N_KV_HEADS, HEAD_DIM)
    attn3, kwin, vwin = _sample_attn_call(
        q, kv, ck, cv, sinks[0].reshape(N_HEADS, 1))
    attn = attn3
    ys, hs, convs, ffns = _sample_post_call(
        xs, mod_s, attn, xr, yr, state_h[0],
        jnp.transpose(state_conv[0], (1, 0, 2)), state_ffn_conv[0],
        params)
    convs = jnp.transpose(convs, (1, 0, 2))

    kv_shape = (1, 1, W, N_KV_HEADS, HEAD_DIM)
    kvs_shape = (1, B, W, N_KV_HEADS, HEAD_DIM)
    return (
        yp[None], ys,
        kp.reshape(kv_shape), vp.reshape(kv_shape), hp[None], convp[None, None], ffnp[None, None],
        kwin.reshape(kvs_shape), vwin.reshape(kvs_shape), hs[None], convs[None], ffns[None],
    )
```

```python
import math

import jax
import jax.numpy as jnp
from jax import lax
from jax.experimental import pallas as pl
from jax.experimental.pallas import tpu as pltpu

D_MODEL = 1024
N_HEADS = 8
N_KV_HEADS = 2
HEAD_DIM = 128
GQA_GROUP = N_HEADS // N_KV_HEADS
ATTN_WIDTH = N_HEADS * HEAD_DIM
KV_WIDTH = N_KV_HEADS * HEAD_DIM
WINDOW = 128
D_RNN = D_MODEL
RNN_BLOCKS = 8
RNN_BLOCK_W = D_RNN // RNN_BLOCKS
RG_C = 8.0
RNN_CONV_W = 4
D_FF = 2816
FFN_CONV_W = 3
RMS_EPS = 1e-6

C_K = ATTN_WIDTH
C_V = C_K + KV_WIDTH
C_XR = C_V + KV_WIDTH
C_YR = C_XR + D_RNN
IN_COLS = C_YR + D_RNN

SUBLANES = 8
SUB_ROWS = WINDOW
GROUPS = SUB_ROWS // SUBLANES
TOKEN_BLOCK = 256
NSUB = TOKEN_BLOCK // SUB_ROWS
FFN_CHUNK = 256
STAGE_SKEW = 5
WSTAGE_ROWS = 64
WSTAGE_SLOTS = 4
SAMPLE_CHUNK = 16
SAMPLE_ATTN_VMEM_BYTES = (2 * 2 * 2 * SAMPLE_CHUNK * N_KV_HEADS * WINDOW * HEAD_DIM * 4
                          + 8 * 1024 * 1024)
MOD_PAD_ROWS = 8
VMEM_LIMIT_BYTES = 56 * 1024 * 1024

ALIBI_SLOPES = tuple(2.0 ** (-8.0 * (h + 1) / N_HEADS) for h in range(N_HEADS))
Q_SCALE = HEAD_DIM ** -0.5
SQRT_2_OVER_PI = math.sqrt(2.0 / math.pi)
LOG2_E = 1.0 / math.log(2.0)
PADDED_OUT_COLS = D_MODEL + 128

bf16 = jnp.bfloat16
f32 = jnp.float32


def _rms_scale(x):
    return lax.rsqrt(jnp.mean(x * x, axis=-1, keepdims=True) + RMS_EPS)


def _rms_norm(x, g):
    return x * _rms_scale(x) * g


def _gelu_tanh(x):
    k1 = -2.0 * LOG2_E * SQRT_2_OVER_PI
    k3 = k1 * 0.044715
    return x / (1.0 + jnp.exp2(x * (k1 + k3 * (x * x))))


def _sigmoid(x):
    return 1.0 / (1.0 + jnp.exp2(x * (-LOG2_E)))


def _softplus_neg(lam):
    return jnp.maximum(-lam, 0.0) + jnp.log1p(jnp.exp(-jnp.abs(lam)))


def _dot(a, b):
    return jnp.dot(a.astype(bf16), b, preferred_element_type=f32)


def _dot_nt(a, b):
    return lax.dot_general(a, b, (((1,), (1,)), ((), ())), preferred_element_type=f32)


def _rglru_gates(xc, wa_ref, ba, wi_ref, bi, neg_c_sp):
    a_parts, u_parts = [], []
    for n in range(RNN_BLOCKS):
        sl = slice(n * RNN_BLOCK_W, (n + 1) * RNN_BLOCK_W)
        xn = xc[:, sl]
        xb = xn.astype(bf16)
        r = _sigmoid(jnp.dot(xb, wa_ref[n], preferred_element_type=f32) + ba[:, sl])
        i = _sigmoid(jnp.dot(xb, wi_ref[n], preferred_element_type=f32) + bi[:, sl])
        log_a = r * neg_c_sp[:, sl]
        t = jnp.tanh(log_a)
        one_minus_a2 = (-2.0 * t) / (1.0 - t)
        a_parts.append(jnp.exp(log_a))
        u_parts.append(jnp.sqrt(one_minus_a2) * (i * xn))
    return a_parts, u_parts


def _mod_kernel(c_ref, w_hbm, b_ref, o_ref, wbuf, sem):
    nblk, _, bn = wbuf.shape
    copies = [pltpu.make_async_copy(w_hbm.at[:, pl.ds(j * bn, bn)], wbuf.at[j], sem.at[j])
              for j in range(nblk)]
    for cp in copies:
        cp.start()
    c = c_ref[...]
    s = (c * _sigmoid(c)).astype(bf16)
    for j in range(nblk):
        copies[j].wait()
        cols = slice(j * bn, (j + 1) * bn)
        o_ref[:, cols] = (jnp.dot(s, wbuf[j].astype(bf16), preferred_element_type=f32)
                          + b_ref[:, cols])


def _mod_call(c_all, w_ada, b_ada):
    rows = c_all.shape[0]
    k, ncol = w_ada.shape
    bn = D_MODEL
    vmem = pl.BlockSpec(memory_space=pltpu.VMEM)
    return pl.pallas_call(
        _mod_kernel,
        in_specs=[vmem, pl.BlockSpec(memory_space=pl.ANY), vmem],
        out_shape=jax.ShapeDtypeStruct((rows, ncol), f32),
        scratch_shapes=[pltpu.VMEM((ncol // bn, k, bn), f32),
                        pltpu.SemaphoreType.DMA((ncol // bn,))],
        compiler_params=pltpu.CompilerParams(
            vmem_limit_bytes=k * ncol * 4 + rows * ncol * 4 + 8 * 1024 * 1024),
        name="adaln_mod",
    )(c_all, w_ada, b_ada)


def _groups(v, c0=None, c1=None):
    if c0 is None:
        return [v[j * SUBLANES:(j + 1) * SUBLANES, :] for j in range(GROUPS)]
    return [v[j * SUBLANES:(j + 1) * SUBLANES, c0:c1] for j in range(GROUPS)]


def _shifted_groups(X, tail_row, sub0, max_shift):
    wrapped = {}
    for j in range(GROUPS - max_shift, GROUPS):
        wrapped[j] = jnp.where(sub0, tail_row(j), pltpu.roll(X[j], 1, axis=0))
    sh = {}
    for d in range(1, max_shift + 1):
        sh[d] = [X[j - d] if j >= d else wrapped[j - d + GROUPS] for j in range(GROUPS)]
    return sh


def _scan_sub(a, u, h_in, sub_iota):
    A = _groups(a)
    L = _groups(u)
    for j in range(1, GROUPS):
        L[j] = A[j] * L[j - 1] + L[j]
        A[j] = A[j] * A[j - 1]
    ae, le = A[GROUPS - 1], L[GROUPS - 1]
    for s_ in (1, 2, 4):
        ok = sub_iota >= s_
        a_sh = pltpu.roll(ae, s_, axis=0)
        l_sh = pltpu.roll(le, s_, axis=0)
        le = jnp.where(ok, ae * l_sh + le, le)
        ae = jnp.where(ok, ae * a_sh, ae)
    hend = le + ae * h_in
    hprev = jnp.where(sub_iota == 0, h_in, pltpu.roll(hend, 1, axis=0))
    h = jnp.concatenate([L[j] + A[j] * hprev for j in range(GROUPS)], axis=0)
    h_out = jnp.broadcast_to(hend[SUBLANES - 1:SUBLANES, :], hend.shape)
    return h, h_out


def _prompt_kernel(
    x_hbm, mod_ref, sinks_ref,
    g_pre_mix_ref, w_in_hbm, conv_w_ref, conv_b_ref, wa_ref, ba_ref, wi_ref, bi_ref, lam_ref,
    g_attn_ref, g_rnn_ref, w_out_hbm, g_post_mix_ref, g_pre_ffn_ref, w_up_hbm,
    fconv_w_ref, fconv_b_ref, w_down_hbm, g_post_ffn_ref,
    y_hbm, kwin_hbm, vwin_hbm, hlast_ref, convst_ref, ffnst_ref,
    w_in_bf_hbm, w_out_bf_hbm, w_up_bf_hbm, w_down_bf_hbm,
    xbuf, ybuf, sem_in, sem_out, sem_st, kprev, vprev, kst, vst, xr_tail, up_tail, hcar, bias_s,
    hff_s, w_in_ref, w_out_ref, w_up_ref, w_down_ref, wstage, sem_w, sem_wout,
):
    step = pl.program_id(0)
    nblocks = pl.num_programs(0) - 1
    slot = lax.rem(step, 2)
    yslot = lax.rem(step, 3)
    fslot = lax.rem(step + 2, 3)

    def x_copies(st, sl):
        return [pltpu.make_async_copy(
            x_hbm.at[st * NSUB + s, c, :, :],
            xbuf.at[sl, pl.ds(s * GROUPS, GROUPS), c, :],
            sem_in.at[sl]) for s in range(NSUB) for c in range(SUBLANES)]

    def y_copies(st, sl):
        return [pltpu.make_async_copy(
            ybuf.at[sl, pl.ds(s * GROUPS, GROUPS), c, :],
            y_hbm.at[st * NSUB + s, c, :, :],
            sem_out.at[sl]) for s in range(NSUB) for c in range(SUBLANES)]

    def state_copies():
        cps = []
        for src, dst in ((kst, kwin_hbm), (vst, vwin_hbm)):
            cps += [pltpu.make_async_copy(src.at[pl.ds(j * SUBLANES, SUBLANES), :],
                                          dst.at[:, j, :], sem_st) for j in range(GROUPS)]
        return cps

    def weight_out_copies():
        return [pltpu.make_async_copy(w_in_ref, w_in_bf_hbm, sem_wout),
                pltpu.make_async_copy(w_up_ref, w_up_bf_hbm, sem_wout),
                pltpu.make_async_copy(w_out_ref.at[:, pl.ds(0, D_MODEL)], w_out_bf_hbm, sem_wout),
                pltpu.make_async_copy(w_down_ref.at[:, pl.ds(0, D_MODEL)], w_down_bf_hbm, sem_wout)]

    def load_cast(w_hbm, w_s, stage, chunk_rows):
        tiled = len(stage.shape) == 4
        if tiled:
            n_rows, n_cols = w_hbm.shape[0] * SUBLANES, w_hbm.shape[2]
            assert stage.shape[1:] == (chunk_rows // SUBLANES, SUBLANES, n_cols)
        else:
            n_rows, n_cols = w_hbm.shape
        n = n_rows // chunk_rows
        depth = stage.shape[0]
        assert n >= depth

        def chunk(i, sl):
            if tiled:
                g = chunk_rows // SUBLANES
                return pltpu.make_async_copy(w_hbm.at[pl.ds(i * g, g), :, :], stage.at[sl],
                                             sem_w.at[sl])
            return pltpu.make_async_copy(
                w_hbm.at[pl.ds(i * chunk_rows, chunk_rows), :],
                stage.at[sl, pl.ds(0, chunk_rows), pl.ds(0, n_cols)], sem_w.at[sl])

        def staged(sl):
            if tiled:
                return stage[sl].reshape(chunk_rows, n_cols)
            return stage[sl, 0:chunk_rows, 0:n_cols]

        for i in range(depth):
            chunk(i, i).start()

        def body(i, carry):
            sl = lax.rem(i, depth)
            chunk(i, sl).wait()
            r0 = pl.multiple_of(i * chunk_rows, chunk_rows)
            w_s[pl.ds(r0, chunk_rows), 0:n_cols] = staged(sl).astype(bf16)

            @pl.when(i + depth < n)
            def _next():
                chunk(i + depth, sl).start()

            return carry

        lax.fori_loop(0, n, body, 0)

    @pl.when(step == 0)
    def _init():
        for cp in x_copies(0, 0):
            cp.start()
        load_cast(w_in_hbm, w_in_ref, wstage, WSTAGE_ROWS)
        load_cast(w_out_hbm, w_out_ref, ybuf, TOKEN_BLOCK)
        load_cast(w_up_hbm, w_up_ref, wstage, WSTAGE_ROWS)
        load_cast(w_down_hbm, w_down_ref, ybuf, TOKEN_BLOCK)
        for cp in weight_out_copies():
            cp.start()
        kprev[...] = jnp.zeros(kprev.shape, bf16)
        vprev[...] = jnp.zeros(vprev.shape, bf16)
        xr_tail[...] = jnp.zeros(xr_tail.shape, f32)
        up_tail[...] = jnp.zeros(up_tail.shape, f32)
        hcar[...] = jnp.zeros(hcar.shape, f32)
        hff_s[...] = jnp.zeros(hff_s.shape, bf16)
        ybuf[2] = jnp.zeros(ybuf.shape[1:], f32)
        rq = lax.broadcasted_iota(jnp.int32, (SUB_ROWS, 2 * SUB_ROWS), 0)
        ck = lax.broadcasted_iota(jnp.int32, (SUB_ROWS, 2 * SUB_ROWS), 1)
        rk = ck & (SUB_ROWS - 1)
        tq = (rq % SUBLANES) * GROUPS + rq // SUBLANES
        tk = (rk % SUBLANES) * GROUPS + rk // SUBLANES
        dist = tq - tk + jnp.where(ck < SUB_ROWS, WINDOW, 0)
        base = jnp.where((dist >= 0) & (dist < WINDOW), dist.astype(f32), jnp.inf)
        for h in range(N_HEADS):
            bias_s[h] = (ALIBI_SLOPES[h] * LOG2_E) * base

    @pl.when(step + 1 < nblocks)
    def _prefetch():
        for cp in x_copies(step + 1, 1 - slot):
            cp.start()

    @pl.when(step >= 3)
    def _free_ybuf():
        for cp in y_copies(0, yslot):
            cp.wait()

    @pl.when(step < nblocks)
    def _wait_x():
        for cp in x_copies(step, slot):
            cp.wait()

    sh1 = mod_ref[0:1, 0 * D_MODEL:1 * D_MODEL]
    sc1 = mod_ref[0:1, 1 * D_MODEL:2 * D_MODEL]
    g1 = mod_ref[0:1, 2 * D_MODEL:3 * D_MODEL]
    sh2 = mod_ref[0:1, 3 * D_MODEL:4 * D_MODEL]
    sc2 = mod_ref[0:1, 4 * D_MODEL:5 * D_MODEL]
    g2 = mod_ref[0:1, 5 * D_MODEL:6 * D_MODEL]
    gs1 = g_pre_mix_ref[...] * (1.0 + sc1)
    gs2 = g_pre_ffn_ref[...] * (1.0 + sc2)
    gg1 = g1 * g_post_mix_ref[...]
    gg2 = g2 * g_post_ffn_ref[...]
    sp = (-RG_C) * _softplus_neg(lam_ref[...])
    first_pen = jnp.where(step == 0, jnp.inf, 0.0)
    sub_iota = lax.broadcasted_iota(jnp.int32, (SUBLANES, D_RNN), 0)
    sub0_rnn = sub_iota == 0
    sub0_ffn = lax.broadcasted_iota(jnp.int32, (SUBLANES, FFN_CHUNK), 0) == 0
    conv_w = [conv_w_ref.at[jj:jj + 1, :] for jj in range(RNN_CONV_W)]

    sts = [dict() for _ in range(NSUB)]

    def rows(s):
        return pl.ds(s * SUB_ROWS, SUB_ROWS)

    def groups(s):
        return pl.ds(s * GROUPS, GROUPS)

    def stage_pre(s):
        x = xbuf[slot, groups(s)].reshape(SUB_ROWS, D_MODEL)
        sts[s]["hmix"] = (x * _rms_scale(x) * gs1 + sh1).astype(bf16)

    def stage_inproj(s):
        st = sts[s]
        hm = st["hmix"]
        st["q"] = (jnp.dot(hm, w_in_ref[:, 0:C_K], preferred_element_type=f32)
                   * (Q_SCALE * LOG2_E)).astype(bf16)
        kv = jnp.dot(hm, w_in_ref[:, C_K:C_XR], preferred_element_type=f32)
        st["kb"] = kv[:, 0:KV_WIDTH].astype(bf16)
        st["vb"] = kv[:, KV_WIDTH:2 * KV_WIDTH].astype(bf16)
        st["kv"] = kv
        st["xr"] = jnp.dot(hm, w_in_ref[:, C_XR:C_YR], preferred_element_type=f32)

    def stage_attn(s):
        st = sts[s]
        if s == 0:
            kp, vp = kprev[...], vprev[...]
        else:
            kp, vp = sts[s - 1]["kb"], sts[s - 1]["vb"]
        kw = jnp.concatenate([kp, st["kb"]], axis=0)
        vw = jnp.concatenate([vp, st["vb"]], axis=0)
        outs = []
        for h in range(N_HEADS):
            c = h // GQA_GROUP
            hs = slice(h * HEAD_DIM, (h + 1) * HEAD_DIM)
            cs = slice(c * HEAD_DIM, (c + 1) * HEAD_DIM)
            sc = _dot_nt(st["q"][:, hs], kw[:, cs])
            if s == 0:
                sc = jnp.concatenate([sc[:, 0:SUB_ROWS] - first_pen, sc[:, SUB_ROWS:]], axis=1)
            sc = sc - bias_s[h]
            sink = sinks_ref[h] * LOG2_E
            m = jnp.maximum(jnp.max(sc, axis=-1, keepdims=True), sink)
            p = jnp.exp2(sc - m)
            denom = jnp.sum(p, axis=-1, keepdims=True) + jnp.exp2(sink - m)
            outs.append(jnp.dot(p.astype(bf16), vw[:, cs], preferred_element_type=f32) / denom)
        attn = jnp.concatenate(outs, axis=1)
        st["attn_n"] = _rms_norm(attn, g_attn_ref[...]).astype(bf16)

    def stage_rnn(s):
        st = sts[s]
        xr = st["xr"]
        X = _groups(xr)
        if s == 0:
            tail = lambda j: xr_tail[(j - (GROUPS - RNN_CONV_W + 1)) * SUBLANES + SUBLANES - 1:
                                     (j - (GROUPS - RNN_CONV_W + 1)) * SUBLANES + SUBLANES, :]
        else:
            pxr = sts[s - 1]["xr"]
            tail = lambda j: pxr[j * SUBLANES + SUBLANES - 1:(j + 1) * SUBLANES, :]
        sh = _shifted_groups(X, tail, sub0_rnn, RNN_CONV_W - 1)
        xcs = []
        for j in range(GROUPS):
            acc = conv_b_ref[...] + conv_w[RNN_CONV_W - 1][...] * X[j]
            for d in range(1, RNN_CONV_W):
                acc = acc + conv_w[RNN_CONV_W - 1 - d][...] * sh[d][j]
            xcs.append(acc)
        xc = jnp.concatenate(xcs, axis=0)
        a_parts, u_parts = _rglru_gates(xc, wa_ref, ba_ref[...], wi_ref, bi_ref[...], sp)
        a = jnp.concatenate(a_parts, axis=1)
        u = jnp.concatenate(u_parts, axis=1)
        h_in = hcar[...] if s == 0 else sts[s - 1]["h_out"]
        h, st["h_out"] = _scan_sub(a, u, h_in, sub_iota)
        yr = jnp.dot(st["hmix"], w_in_ref[:, C_YR:IN_COLS], preferred_element_type=f32)
        rnn = h * _gelu_tanh(yr)
        st["rnn_n"] = _rms_norm(rnn, g_rnn_ref[...]).astype(bf16)

    def stage_out(s):
        st = sts[s]
        mo = (jnp.dot(st["attn_n"], w_out_ref[0:ATTN_WIDTH, 0:D_MODEL], preferred_element_type=f32)
              + jnp.dot(st["rnn_n"], w_out_ref[ATTN_WIDTH:ATTN_WIDTH + D_RNN, 0:D_MODEL],
                        preferred_element_type=f32))
        x1 = (xbuf[slot, groups(s)].reshape(SUB_ROWS, D_MODEL)
              + mo * _rms_scale(mo) * gg1)
        ybuf[yslot, groups(s)] = x1.reshape(GROUPS, SUBLANES, D_MODEL)
        hff_s[slot, rows(s), :] = (x1 * _rms_scale(x1) * gs2 + sh2).astype(bf16)

    tail_g0 = GROUPS - (FFN_CONV_W - 1)
    hff = hff_s[1 - slot]
    acts, tails = [], {}

    def conv_chunk(c0):
        c1 = c0 + FFN_CHUNK
        up = jnp.dot(hff, w_up_ref[:, c0:c1], preferred_element_type=f32)
        w = [fconv_w_ref.at[jj:jj + 1, c0:c1] for jj in range(FFN_CONV_W)]
        outs = []
        for s in range(NSUB):
            r0 = s * SUB_ROWS
            X = [up[r0 + j * SUBLANES:r0 + (j + 1) * SUBLANES, :] for j in range(GROUPS)]
            if s == 0:
                tail = lambda j: up_tail[(j - tail_g0) * SUBLANES + SUBLANES - 1:
                                         (j - tail_g0 + 1) * SUBLANES, c0:c1]
            else:
                tail = lambda j, p0=r0 - SUB_ROWS: up[p0 + j * SUBLANES + SUBLANES - 1:
                                                      p0 + (j + 1) * SUBLANES, :]
            sh = _shifted_groups(X, tail, sub0_ffn, FFN_CONV_W - 1)
            for j in range(GROUPS):
                acc = fconv_b_ref[:, c0:c1] + w[FFN_CONV_W - 1][...] * X[j]
                for d in range(1, FFN_CONV_W):
                    acc = acc + w[FFN_CONV_W - 1 - d][...] * sh[d][j]
                outs.append(acc)
        last0 = (NSUB - 1) * SUB_ROWS
        for t in range(FFN_CONV_W - 1):
            r = last0 + (tail_g0 + t) * SUBLANES + SUBLANES - 1
            ffnst_ref[t:t + 1, c0:c1] = up[r:r + 1, :]
        tails[c0] = up[last0 + tail_g0 * SUBLANES:, :]
        return jnp.concatenate(outs, axis=0)

    def ffn_chunk(cc):
        gate = conv_chunk(cc * FFN_CHUNK)
        val = conv_chunk(D_FF + cc * FFN_CHUNK)
        acts.append((_gelu_tanh(gate) * val).astype(bf16))

    def ffn_finish():
        for c0, tl in tails.items():
            up_tail[:, c0:c0 + FFN_CHUNK] = tl
        act = jnp.concatenate(acts, axis=1)
        f = jnp.dot(act, w_down_ref[:, 0:D_MODEL], preferred_element_type=f32)
        y = ybuf[fslot].reshape(TOKEN_BLOCK, D_MODEL) + f * _rms_scale(f) * gg2
        ybuf[fslot] = y.reshape(NSUB * GROUPS, SUBLANES, D_MODEL)

    stages = (stage_pre, stage_inproj, stage_attn, stage_rnn, stage_out)
    n_times = len(stages) + (NSUB - 1) * STAGE_SKEW
    n_chunks = D_FF // FFN_CHUNK
    ffn_chunk(0)
    nxt = 1
    for t in range(n_times):
        for s in range(NSUB):
            k = t - s * STAGE_SKEW
            if 0 <= k < len(stages):
                stages[k](s)
                if nxt < n_chunks:
                    ffn_chunk(nxt)
                    nxt += 1
    while nxt < n_chunks:
        ffn_chunk(nxt)
        nxt += 1
    ffn_finish()

    lst = sts[NSUB - 1]
    kprev[...] = lst["kb"]
    vprev[...] = lst["vb"]
    xr_tail[...] = lst["xr"][(GROUPS - (RNN_CONV_W - 1)) * SUBLANES:, :]
    hcar[...] = lst["h_out"]

    @pl.when(step < nblocks)
    def _state():
        kst[...] = lst["kv"][:, 0:KV_WIDTH]
        vst[...] = lst["kv"][:, KV_WIDTH:2 * KV_WIDTH]
        for t in range(RNN_CONV_W - 1):
            r = (GROUPS - (RNN_CONV_W - 1) + t) * SUBLANES + SUBLANES - 1
            convst_ref[t:t + 1, :] = lst["xr"][r:r + 1, :]
        hlast_ref[...] = lst["h_out"][0:1, :]

    @pl.when(step >= 1)
    def _store_y():
        for cp in y_copies(step - 1, fslot):
            cp.start()

    @pl.when(step == nblocks)
    def _finish():
        for cp in state_copies():
            cp.start()
        for cp in state_copies() + weight_out_copies():
            cp.wait()
        for cp in y_copies(0, lax.rem(step + 1, 3)) + y_copies(0, fslot):
            cp.wait()


def _const_spec(shape):
    nd = len(shape)
    return pl.BlockSpec(shape, lambda i: (0,) * nd)


def _prompt_call(x, mod, mod_row0, sinks, params):
    T = x.shape[0]
    TB = TOKEN_BLOCK
    assert T // TB >= 3
    (g_pre_mix, w_in, conv_w, conv_b, wa, ba, wi, bi, lam, g_attn, g_rnn, w_out,
     g_post_mix, g_pre_ffn, w_up, fconv_w, fconv_b, w_down, g_post_ffn) = params
    x4 = x.reshape(T // SUB_ROWS, SUBLANES, GROUPS, D_MODEL)
    w_out3 = w_out.reshape(-1, SUBLANES, D_MODEL)
    w_down3 = w_down.reshape(-1, SUBLANES, D_MODEL)
    ins = [x4, mod, sinks, g_pre_mix, w_in, conv_w, conv_b, wa, ba, wi, bi, lam, g_attn, g_rnn,
           w_out3, g_post_mix, g_pre_ffn, w_up, fconv_w, fconv_b, w_down3, g_post_ffn]
    assert mod_row0 % MOD_PAD_ROWS == 0
    in_specs = [pl.BlockSpec(memory_space=pl.ANY),
                pl.BlockSpec((MOD_PAD_ROWS, mod.shape[1]), lambda i: (mod_row0 // MOD_PAD_ROWS, 0)),
                pl.BlockSpec(memory_space=pltpu.SMEM)]
    big = (w_in, w_out, w_up, w_down)
    assert all(w.dtype == f32 and w.shape[0] % TOKEN_BLOCK == 0 for w in big)
    in_specs += [pl.BlockSpec(memory_space=pl.ANY)
                 if any(a is w for w in (w_in, w_out3, w_up, w_down3))
                 else _const_spec(a.shape) for a in ins[3:]]
    out_shape = (
        jax.ShapeDtypeStruct(x4.shape, f32),
        jax.ShapeDtypeStruct((SUBLANES, GROUPS, KV_WIDTH), f32),
        jax.ShapeDtypeStruct((SUBLANES, GROUPS, KV_WIDTH), f32),
        jax.ShapeDtypeStruct((1, D_RNN), f32),
        jax.ShapeDtypeStruct((RNN_CONV_W - 1, D_RNN), f32),
        jax.ShapeDtypeStruct((FFN_CONV_W - 1, 2 * D_FF), f32),
    ) + tuple(jax.ShapeDtypeStruct(w.shape, bf16) for w in big)
    out_specs = (
        pl.BlockSpec(memory_space=pl.ANY),
        pl.BlockSpec(memory_space=pl.ANY),
        pl.BlockSpec(memory_space=pl.ANY),
        _const_spec((1, D_RNN)),
        _const_spec((RNN_CONV_W - 1, D_RNN)),
        _const_spec((FFN_CONV_W - 1, 2 * D_FF)),
    ) + (pl.BlockSpec(memory_space=pl.ANY),) * len(big)
    scratch = [
        pltpu.VMEM((2, TB // SUBLANES, SUBLANES, D_MODEL), f32),
        pltpu.VMEM((3, TB // SUBLANES, SUBLANES, D_MODEL), f32),
        pltpu.SemaphoreType.DMA((2,)),
        pltpu.SemaphoreType.DMA((3,)),
        pltpu.SemaphoreType.DMA(()),
        pltpu.VMEM((SUB_ROWS, KV_WIDTH), bf16),
        pltpu.VMEM((SUB_ROWS, KV_WIDTH), bf16),
        pltpu.VMEM((SUB_ROWS, KV_WIDTH), f32),
        pltpu.VMEM((SUB_ROWS, KV_WIDTH), f32),
        pltpu.VMEM(((RNN_CONV_W - 1) * SUBLANES, D_RNN), f32),
        pltpu.VMEM(((FFN_CONV_W - 1) * SUBLANES, 2 * D_FF), f32),
        pltpu.VMEM((SUBLANES, D_RNN), f32),
        pltpu.VMEM((N_HEADS, SUB_ROWS, 2 * SUB_ROWS), f32),
        pltpu.VMEM((2, TOKEN_BLOCK, D_MODEL), bf16),
        pltpu.VMEM(w_in.shape, bf16),
        pltpu.VMEM((w_out.shape[0], PADDED_OUT_COLS), bf16),
        pltpu.VMEM(w_up.shape, bf16),
        pltpu.VMEM((w_down.shape[0], PADDED_OUT_COLS), bf16),
        pltpu.VMEM((WSTAGE_SLOTS, WSTAGE_ROWS, w_up.shape[1]), f32),
        pltpu.SemaphoreType.DMA((WSTAGE_SLOTS,)),
        pltpu.SemaphoreType.DMA(()),
    ]
    assert w_in.shape[1] <= w_up.shape[1] and w_in.shape[0] % WSTAGE_ROWS == 0
    y4, kp, vp, hp, convp, ffnp, w_in_bf, w_out_bf, w_up_bf, w_down_bf = pl.pallas_call(
        _prompt_kernel,
        grid=(T // TB + 1,),
        in_specs=in_specs,
        out_specs=out_specs,
        out_shape=out_shape,
        scratch_shapes=scratch,
        compiler_params=pltpu.CompilerParams(
            dimension_semantics=("arbitrary",), vmem_limit_bytes=VMEM_LIMIT_BYTES),
        name="prompt_layer",
    )(*ins)
    return (y4.reshape(T, D_MODEL), kp.reshape(WINDOW, KV_WIDTH), vp.reshape(WINDOW, KV_WIDTH),
            hp, convp, ffnp, (w_in_bf, w_out_bf, w_up_bf, w_down_bf))


def _sample_pre_kernel(x_ref, mod_ref, g_pre_mix_ref, w_in_hbm, q_ref, kv_ref, xr_ref, yr_ref,
                       w_in_ref, sem):
    halves = [pltpu.make_async_copy(w_in_hbm.at[:, pl.ds(c0, c1 - c0)],
                                    w_in_ref.at[:, pl.ds(c0, c1 - c0)], sem.at[i])
              for i, (c0, c1) in enumerate(((0, C_XR), (C_XR, IN_COLS)))]
    for cp in halves:
        cp.start()
    B = x_ref.shape[0]
    sh1 = mod_ref[0:B, 0 * D_MODEL:1 * D_MODEL]
    sc1 = mod_ref[0:B, 1 * D_MODEL:2 * D_MODEL]
    hmix = (_rms_norm(x_ref[:, 0, :], g_pre_mix_ref[...]) * (1.0 + sc1) + sh1).astype(bf16)
    halves[0].wait()
    q = jnp.dot(hmix, w_in_ref[:, 0:C_K], preferred_element_type=f32) * Q_SCALE
    for h in range(N_HEADS):
        q_ref[:, h, :] = q[:, h * HEAD_DIM:(h + 1) * HEAD_DIM]
    kv_ref[...] = jnp.dot(hmix, w_in_ref[:, C_K:C_XR], preferred_element_type=f32)
    halves[1].wait()
    xr_ref[...] = jnp.dot(hmix, w_in_ref[:, C_XR:C_YR], preferred_element_type=f32)
    yr_ref[...] = jnp.dot(hmix, w_in_ref[:, C_YR:IN_COLS], preferred_element_type=f32)


def _sample_pre_call(x, mod_s, g_pre_mix, w_in):
    B = x.shape[0]
    vmem = pl.BlockSpec(memory_space=pltpu.VMEM)
    return pl.pallas_call(
        _sample_pre_kernel,
        in_specs=[vmem, vmem, vmem, pl.BlockSpec(memory_space=pl.ANY)],
        out_shape=(
            jax.ShapeDtypeStruct((B, N_HEADS, HEAD_DIM), f32),
            jax.ShapeDtypeStruct((B, 2 * KV_WIDTH), f32),
            jax.ShapeDtypeStruct((B, D_RNN), f32),
            jax.ShapeDtypeStruct((B, D_RNN), f32),
        ),
        scratch_shapes=[pltpu.VMEM(w_in.shape, w_in.dtype), pltpu.SemaphoreType.DMA((2,))],
        compiler_params=pltpu.CompilerParams(vmem_limit_bytes=VMEM_LIMIT_BYTES),
        name="sample_pre",
    )(x, mod_s, g_pre_mix, w_in)


def _sample_attn_kernel(q_ref, kv_ref, ck_ref, cv_ref, sinks_ref, o_ref, kwin_ref, vwin_ref):
    R = N_KV_HEADS * WINDOW
    hrow = lax.broadcasted_iota(jnp.int32, (N_HEADS, R), 0)
    rcol = lax.broadcasted_iota(jnp.int32, (N_HEADS, R), 1)
    slope = jnp.exp2(-8.0 * (hrow + 1).astype(f32) / N_HEADS)
    own = (rcol % N_KV_HEADS) == (hrow // GQA_GROUP)
    bias = jnp.where(own, slope * (WINDOW - 1 - rcol // N_KV_HEADS).astype(f32), jnp.inf)
    wrow = lax.broadcasted_iota(jnp.int32, (R, HEAD_DIM), 0)
    sink = sinks_ref[...]

    def shifted(cache, new_rows):
        out = pltpu.roll(cache, R - N_KV_HEADS, axis=0)
        for c in range(N_KV_HEADS):
            out = jnp.where(wrow == R - N_KV_HEADS + c, new_rows[c], out)
        return out

    scores = []
    for b in range(SAMPLE_CHUNK):
        knew = [kv_ref[b:b + 1, c * HEAD_DIM:(c + 1) * HEAD_DIM] for c in range(N_KV_HEADS)]
        kw = shifted(ck_ref[b], knew)
        kwin_ref[b] = kw
        scores.append(_dot_nt(q_ref[b].astype(bf16), kw.astype(bf16)) - bias)
    probs, denoms = [], []
    for b in range(SAMPLE_CHUNK):
        s = scores[b]
        m = jnp.maximum(jnp.max(s, axis=-1, keepdims=True), sink)
        p = jnp.exp(s - m)
        denoms.append(jnp.sum(p, axis=-1, keepdims=True) + jnp.exp(sink - m))
        probs.append(p.astype(bf16))
    for b in range(SAMPLE_CHUNK):
        vnew = [kv_ref[b:b + 1, KV_WIDTH + c * HEAD_DIM:KV_WIDTH + (c + 1) * HEAD_DIM]
                for c in range(N_KV_HEADS)]
        vw = shifted(cv_ref[b], vnew)
        vwin_ref[b] = vw
        o_ref[b] = (jnp.dot(probs[b], vw.astype(bf16), preferred_element_type=f32)
                    / denoms[b])


def _sample_attn_call(q, kv, ck, cv, sinks_col):
    B = q.shape[0]
    BC = SAMPLE_CHUNK
    return pl.pallas_call(
        _sample_attn_kernel,
        grid=(B // BC,),
        in_specs=[
            pl.BlockSpec((BC, N_HEADS, HEAD_DIM), lambda i: (i, 0, 0)),
            pl.BlockSpec((BC, 2 * KV_WIDTH), lambda i: (i, 0)),
            pl.BlockSpec((BC, N_KV_HEADS * WINDOW, HEAD_DIM), lambda i: (i, 0, 0)),
            pl.BlockSpec((BC, N_KV_HEADS * WINDOW, HEAD_DIM), lambda i: (i, 0, 0)),
            pl.BlockSpec((N_HEADS, 1), lambda i: (0, 0)),
        ],
        out_specs=(
            pl.BlockSpec((BC, N_HEADS, HEAD_DIM), lambda i: (i, 0, 0)),
            pl.BlockSpec((BC, N_KV_HEADS * WINDOW, HEAD_DIM), lambda i: (i, 0, 0)),
            pl.BlockSpec((BC, N_KV_HEADS * WINDOW, HEAD_DIM), lambda i: (i, 0, 0)),
        ),
        out_shape=(
            jax.ShapeDtypeStruct((B, N_HEADS, HEAD_DIM), f32),
            jax.ShapeDtypeStruct((B, N_KV_HEADS * WINDOW, HEAD_DIM), f32),
            jax.ShapeDtypeStruct((B, N_KV_HEADS * WINDOW, HEAD_DIM), f32),
        ),
        compiler_params=pltpu.CompilerParams(
            dimension_semantics=("arbitrary",), vmem_limit_bytes=SAMPLE_ATTN_VMEM_BYTES),
        name="sample_attn",
    )(q, kv, ck, cv, sinks_col)


def _sample_post_kernel(
    x_ref, mod_ref, attn_ref, xr_ref, yr_ref, h0_ref, cbuf_ref, fbuf_hbm,
    conv_w_ref, conv_b_ref, wa_ref, ba_ref, wi_ref, bi_ref, lam_ref,
    g_attn_ref, g_rnn_ref, w_out_ref, g_post_mix_ref, g_pre_ffn_ref, w_up_hbm,
    fconv_w_ref, fconv_b_ref, w_down_hbm, g_post_ffn_ref,
    y_ref, h_ref, cst_ref, fst_ref,
    w_up_ref, w_down_ref, fbuf_ref, sem_ffn,
):
    up_copy = pltpu.make_async_copy(w_up_hbm, w_up_ref, sem_ffn.at[0])
    down_copy = pltpu.make_async_copy(w_down_hbm, w_down_ref, sem_ffn.at[1])
    fbuf_copy = pltpu.make_async_copy(fbuf_hbm, fbuf_ref, sem_ffn.at[2])
    up_copy.start()
    fbuf_copy.start()
    down_copy.start()

    B = x_ref.shape[0]
    g1 = mod_ref[0:B, 2 * D_MODEL:3 * D_MODEL]
    sh2 = mod_ref[0:B, 3 * D_MODEL:4 * D_MODEL]
    sc2 = mod_ref[0:B, 4 * D_MODEL:5 * D_MODEL]
    g2 = mod_ref[0:B, 5 * D_MODEL:6 * D_MODEL]
    x = x_ref[:, 0, :]
    xr = xr_ref[...]

    xc = conv_b_ref[...] + conv_w_ref[RNN_CONV_W - 1:RNN_CONV_W, :] * xr
    for jj in range(RNN_CONV_W - 1):
        xc = xc + conv_w_ref[jj:jj + 1, :] * cbuf_ref[jj]
    for jj in range(RNN_CONV_W - 2):
        cst_ref[jj] = cbuf_ref[jj + 1]
    cst_ref[RNN_CONV_W - 2] = xr

    sp = (-RG_C) * _softplus_neg(lam_ref[...])
    a_parts, u_parts = _rglru_gates(xc, wa_ref, ba_ref[...], wi_ref, bi_ref[...], sp)
    a = jnp.concatenate(a_parts, axis=1)
    u = jnp.concatenate(u_parts, axis=1)
    h = a * h0_ref[...] + u
    h_ref[...] = h
    rnn = h * _gelu_tanh(yr_ref[...])

    attn = jnp.concatenate([attn_ref[:, h, :] for h in range(N_HEADS)], axis=1)
    attn_n = _rms_norm(attn, g_attn_ref[...])
    rnn_n = _rms_norm(rnn, g_rnn_ref[...])
    mo = (_dot(attn_n, w_out_ref[0:ATTN_WIDTH, 0:D_MODEL])
          + _dot(rnn_n, w_out_ref[ATTN_WIDTH:ATTN_WIDTH + D_RNN, 0:D_MODEL]))
    x1 = x + g1 * _rms_norm(mo, g_post_mix_ref[...])

    hff = (_rms_norm(x1, g_pre_ffn_ref[...]) * (1.0 + sc2) + sh2).astype(bf16)
    up_copy.wait()
    up_pre = jnp.dot(hff, w_up_ref[...], preferred_element_type=f32)
    fbuf_copy.wait()
    up = fconv_b_ref[...] + fconv_w_ref[FFN_CONV_W - 1:FFN_CONV_W, :] * up_pre
    for jj in range(FFN_CONV_W - 1):
        up = up + fconv_w_ref[jj:jj + 1, :] * fbuf_ref[:, jj, :]
    for jj in range(FFN_CONV_W - 2):
        fst_ref[:, jj, :] = fbuf_ref[:, jj + 1, :]
    fst_ref[:, FFN_CONV_W - 2, :] = up_pre

    act = (_gelu_tanh(up[:, 0:D_FF]) * up[:, D_FF:2 * D_FF]).astype(bf16)
    down_copy.wait()
    f = jnp.dot(act, w_down_ref[:, 0:D_MODEL], preferred_element_type=f32)
    y_ref[:, 0, :] = x1 + g2 * _rms_norm(f, g_post_ffn_ref[...])


def _sample_post_call(x, mod_s, attn, xr, yr, h0, cbuf, fbuf, params):
    B = x.shape[0]
    (_, _, conv_w, conv_b, wa, ba, wi, bi, lam, g_attn, g_rnn, w_out,
     g_post_mix, g_pre_ffn, w_up, fconv_w, fconv_b, w_down, g_post_ffn) = params
    ins = (x, mod_s, attn, xr, yr, h0, cbuf, fbuf, conv_w, conv_b, wa, ba, wi, bi, lam,
           g_attn, g_rnn, w_out, g_post_mix, g_pre_ffn, w_up, fconv_w, fconv_b, w_down, g_post_ffn)
    return pl.pallas_call(
        _sample_post_kernel,
        in_specs=[pl.BlockSpec(memory_space=pl.ANY) if (a is w_up or a is w_down or a is fbuf)
                  else pl.BlockSpec(memory_space=pltpu.VMEM) for a in ins],
        out_shape=(
            jax.ShapeDtypeStruct((B, 1, D_MODEL), f32),
            jax.ShapeDtypeStruct((B, D_RNN), f32),
            jax.ShapeDtypeStruct((RNN_CONV_W - 1, B, D_RNN), f32),
            jax.ShapeDtypeStruct((B, FFN_CONV_W - 1, 2 * D_FF), f32),
        ),
        scratch_shapes=[pltpu.VMEM(w_up.shape, w_up.dtype), pltpu.VMEM(w_down.shape, w_down.dtype),
                        pltpu.VMEM(fbuf.shape, fbuf.dtype), pltpu.SemaphoreType.DMA((3,))],
        compiler_params=pltpu.CompilerParams(vmem_limit_bytes=VMEM_LIMIT_BYTES),
        name="sample_post",
    )(*ins)


def kernel(x_prompt, x_sample, cache_k, cache_v, state_h, state_conv, state_ffn_conv, c_prompt, c_sample, w_ada, b_ada, g_pre_mix, w_in, conv_w, conv_b, w_a, b_a, w_i, b_i, lam, sinks, g_attn_out, g_rnn_out, w_out, g_post_mix, g_pre_ffn, w_up, ffn_conv_w, ffn_conv_b, w_down, g_post_ffn):
    depth = w_in.shape[0]
    assert depth == 1 and x_prompt.shape[0] == 1 and x_sample.shape[1] == 1
    T = x_prompt.shape[1]
    B = x_sample.shape[0]
    W = cache_k.shape[2]
    assert W == WINDOW and T % TOKEN_BLOCK == 0 and B % SAMPLE_CHUNK == 0

    row = lambda a: a[0].reshape(1, -1)
    params = (
        row(g_pre_mix), w_in[0], conv_w[0], row(conv_b),
        w_a[0].astype(bf16), row(b_a), w_i[0].astype(bf16), row(b_i), row(lam),
        row(g_attn_out), row(g_rnn_out), w_out[0], row(g_post_mix),
        row(g_pre_ffn),
        w_up[0], ffn_conv_w[0], row(ffn_conv_b), w_down[0],
        row(g_post_ffn),
    )

    c_all = jnp.concatenate(
        [c_sample, jnp.broadcast_to(c_prompt, (MOD_PAD_ROWS, D_MODEL))], axis=0)
    mod = _mod_call(c_all, w_ada[0], b_ada[0].reshape(1, -1))
    mod_s = mod

    yp, kp, vp, hp, convp, ffnp, (w_in_bf, w_out_bf, w_up_bf, w_down_bf) = _prompt_call(
        x_prompt[0], mod, B, sinks[0], params)
    params = list(params)
    params[1], params[11], params[14], params[17] = w_in_bf, w_out_bf, w_up_bf, w_down_bf
    assert params[11].shape == w_out.shape[1:] and params[17].shape == w_down.shape[1:]

    xs = x_sample
    q, kv, xr, yr = _sample_pre_call(xs, mod_s, params[0], params[1])
    ck = cache_k.reshape(B, W * N_KV_HEADS, HEAD_DIM)
    cv = cache_v.reshape(B, W * N_KV_HEADS, HEAD_DIM)
    attn3, kwin, vwin = _sample_attn_call(
        q, kv, ck, cv, sinks[0].reshape(N_HEADS, 1))
    attn = attn3
    ys, hs, convs, ffns = _sample_post_call(
        xs, mod_s, attn, xr, yr, state_h[0],
        jnp.transpose(state_conv[0], (1, 0, 2)), state_ffn_conv[0],
        params)
    convs = jnp.transpose(convs, (1, 0, 2))

    kv_shape = (1, 1, W, N_KV_HEADS, HEAD_DIM)
    kvs_shape = (1, B, W, N_KV_HEADS, HEAD_DIM)
    return (
        yp[None], ys,
        kp.reshape(kv_shape), vp.reshape(kv_shape), hp[None], convp[None, None], ffnp[None, None],
        kwin.reshape(kvs_shape), vwin.reshape(kvs_shape), hs[None], convs[None], ffns[None],
    )
```
